```python
import jax, jax.numpy as jnp
from jax import lax
import numpy as np

D_MODEL = 4096
BATCH = 2
SEQ = 8192
DEPTH = 2

CTX_LEN = 256
GRID_W = 64
HEAD_DIM = 128
CONV_WIDTH = D_MODEL // 4
NA_HEADS = D_MODEL // (4 * HEAD_DIM)
NA_WIDTH = NA_HEADS * HEAD_DIM
MLA_NOPE = 128
MLA_ROPE = 64
MLA_V = 128
MLA_HEADS = D_MODEL // (2 * MLA_V)
MLA_WIDTH = MLA_HEADS * MLA_V
D_MIX = CONV_WIDTH + NA_WIDTH + MLA_WIDTH
MLA_Q_RANK = D_MODEL // 4
MLA_KV_RANK = D_MODEL // 8
CONV_K = 3
NA_WIN_H = 8
NA_WIN_W = 16
NA_QBLK_W = 16
NA_KBLK_W = NA_QBLK_W + NA_WIN_W
N_EXPERTS = 64
N_GROUPS = 8
TOPK_GROUPS = 4
TOP_K = 8
D_EXPERT = D_MODEL // 16
ROUTE_SCALE = 2.5
ROPE_BASE = 10000.0
Q_BLOCK = 128
EPS = 1e-6
IN_WIDTHS = (CONV_WIDTH, CONV_WIDTH, CONV_WIDTH, NA_WIDTH, NA_WIDTH, NA_WIDTH,
             MLA_Q_RANK, MLA_KV_RANK, MLA_ROPE)
IN_COLS = sum(IN_WIDTHS)

kernel_name = 'hybrid_conv_natten_mla_moe_dit'


def rms_norm(x, g):
    xf = x.astype(jnp.float32)
    y = xf * lax.rsqrt(jnp.mean(jnp.square(xf), axis=-1, keepdims=True) + EPS)
    return (y * g.astype(jnp.float32)).astype(x.dtype)


def swiglu(h, wg, wu, wd):
    return (jax.nn.silu(h @ wg) * (h @ wu)) @ wd


def split_projection(z):
    idx = np.cumsum(IN_WIDTHS)[:-1].tolist()
    return jnp.split(z, idx, axis=-1)


def short_conv_mixer(b_gate, c_gate, u, conv_w):
    v = c_gate * u
    n = v.shape[1]
    pad = CONV_K // 2
    vp = jnp.pad(v, ((0, 0), (pad, pad), (0, 0)))
    y = sum(conv_w[i] * vp[:, i:i + n] for i in range(CONV_K))
    return b_gate * y


def axial_rope_tables(n_tok):
    t = jnp.arange(n_tok)
    row = (t // GRID_W).astype(jnp.float32)
    col = (t % GRID_W).astype(jnp.float32)
    n_freq = MLA_ROPE // 4
    inv = ROPE_BASE ** (-jnp.arange(n_freq, dtype=jnp.float32) / n_freq)
    ang = jnp.concatenate([row[:, None] * inv, col[:, None] * inv], axis=-1)
    return jnp.cos(ang), jnp.sin(ang)


def apply_axial_rope(x, cos, sin):
    xf = x.astype(jnp.float32)
    half = x.shape[-1] // 2
    x1, x2 = xf[..., :half], xf[..., half:]
    cs, sn = cos[:, None, :], sin[:, None, :]
    return jnp.concatenate([x1 * cs - x2 * sn, x2 * cs + x1 * sn], axis=-1).astype(x.dtype)


def attention_heads(parts, na_q_norm, na_k_norm, mla_q_lat_norm, mla_kv_lat_norm,
                    w_uq, w_ukv, mla_q_norm, mla_k_norm):
    _, _, _, nq, nk, nv, cq, ckv, kr = parts
    b, l, _ = nq.shape
    na_q = rms_norm(nq.reshape(b, l, NA_HEADS, HEAD_DIM), na_q_norm)
    na_k = rms_norm(nk.reshape(b, l, NA_HEADS, HEAD_DIM), na_k_norm)
    na_v = nv.reshape(b, l, NA_HEADS, HEAD_DIM)
    qh = (rms_norm(cq, mla_q_lat_norm) @ w_uq).reshape(b, l, MLA_HEADS, MLA_NOPE + MLA_ROPE)
    kvh = (rms_norm(ckv, mla_kv_lat_norm) @ w_ukv).reshape(b, l, MLA_HEADS, MLA_NOPE + MLA_V)
    q = jnp.concatenate([rms_norm(qh[..., :MLA_NOPE], mla_q_norm[:MLA_NOPE]),
                         rms_norm(qh[..., MLA_NOPE:], mla_q_norm[MLA_NOPE:])], axis=-1)
    k_rope = rms_norm(kr, mla_k_norm[MLA_NOPE:])
    k = jnp.concatenate([rms_norm(kvh[..., :MLA_NOPE], mla_k_norm[:MLA_NOPE]),
                         jnp.broadcast_to(k_rope[:, :, None, :], (b, l, MLA_HEADS, MLA_ROPE))], axis=-1)
    v = kvh[..., MLA_NOPE:]
    return na_q, na_k, na_v, q, k, v


def rope_tail(x, cos, sin):
    return jnp.concatenate([x[..., :MLA_NOPE], apply_axial_rope(x[..., MLA_NOPE:], cos, sin)], axis=-1)


def dense_attention(q, k, v):
    s = jnp.einsum('bqhd,bkhd->bhqk', q, k).astype(jnp.float32) * (q.shape[-1] ** -0.5)
    p = jax.nn.softmax(s, axis=-1).astype(v.dtype)
    return jnp.einsum('bhqk,bkhd->bqhd', p, v)


def blocked_attention(q, k, v):
    b, s, h, dq = q.shape
    nb = s // Q_BLOCK
    qb = q.reshape(b, nb, Q_BLOCK, h, dq).transpose(1, 0, 2, 3, 4)
    out = lax.map(lambda qi: dense_attention(qi, k, v), qb)
    return out.transpose(1, 0, 2, 3, 4).reshape(b, s, h, v.shape[-1])


def neighborhood_attention(q, k, v, k_ctx, v_ctx, rpb):
    b, s, h, dh = q.shape
    rows = s // GRID_W
    wh = min(NA_WIN_H, rows)
    nqb = GRID_W // NA_QBLK_W
    scale = dh ** -0.5
    qg = q.reshape(b, rows, nqb, NA_QBLK_W, h, dh)
    qcol = jnp.arange(GRID_W).reshape(nqb, NA_QBLK_W)
    kb0 = jnp.clip(jnp.arange(nqb) * NA_QBLK_W - NA_WIN_W // 2, 0, GRID_W - NA_KBLK_W)
    kcol = kb0[:, None] + jnp.arange(NA_KBLK_W)
    cstart = jnp.clip(qcol - NA_WIN_W // 2, 0, GRID_W - NA_WIN_W)
    col_ok = (kcol[:, None, :] >= cstart[..., None]) & (kcol[:, None, :] < cstart[..., None] + NA_WIN_W)
    dc_idx = jnp.clip(kcol[:, None, :] - qcol[..., None] + NA_WIN_W - 1, 0, 2 * NA_WIN_W - 2)
    kg = k.reshape(b, rows, GRID_W, h, dh)[:, :, kcol]
    vg = v.reshape(b, rows, GRID_W, h, dh)[:, :, kcol]
    n_loc = wh * NA_KBLK_W

    def row_block(r):
        r0 = jnp.clip(r - wh // 2, 0, rows - wh)
        kr = lax.dynamic_slice_in_dim(kg, r0, wh, axis=1)
        vr = lax.dynamic_slice_in_dim(vg, r0, wh, axis=1)
        qr = lax.dynamic_index_in_dim(qg, r, axis=1, keepdims=False)
        dr_idx = r0 + jnp.arange(wh) - r + NA_WIN_H - 1
        bias = rpb[:, dr_idx[None, None, :, None], dc_idx[:, :, None, :]]
        s_loc = jnp.einsum('bjqhd,bwjkhd->bjhqwk', qr, kr).astype(jnp.float32) * scale
        s_loc = s_loc + bias.transpose(1, 0, 2, 3, 4).astype(jnp.float32)[None]
        s_loc = jnp.where(col_ok[None, :, None, :, None, :], s_loc, -jnp.inf)
        s_ctx = jnp.einsum('bjqhd,bchd->bjhqc', qr, k_ctx).astype(jnp.float32) * scale
        s_all = jnp.concatenate([s_loc.reshape(b, nqb, h, NA_QBLK_W, n_loc), s_ctx], axis=-1)
        p = jax.nn.softmax(s_all, axis=-1).astype(v.dtype)
        p_loc = p[..., :n_loc].reshape(b, nqb, h, NA_QBLK_W, wh, NA_KBLK_W)
        o = jnp.einsum('bjhqwk,bwjkhd->bjqhd', p_loc, vr)
        return o + jnp.einsum('bjhqc,bchd->bjqhd', p[..., n_loc:], v_ctx)

    out = lax.map(row_block, jnp.arange(rows))
    return out.transpose(1, 0, 2, 3, 4, 5).reshape(b, s, h, dh)


def merge_groups(y_conv, y_na, y_mla, out_g, w_o):
    b, l = y_conv.shape[:2]
    y = jnp.concatenate([
        rms_norm(y_conv, out_g[:CONV_WIDTH]),
        rms_norm(y_na.reshape(b, l, NA_WIDTH), out_g[CONV_WIDTH:CONV_WIDTH + NA_WIDTH]),
        rms_norm(y_mla.reshape(b, l, MLA_WIDTH), out_g[CONV_WIDTH + NA_WIDTH:])], axis=-1)
    return y @ w_o


def moe_ffn(h, w_router, router_bias, w_gate, w_up, w_down, ws_gate, ws_up, ws_down):
    n = h.shape[0]
    scores = jax.nn.sigmoid((h @ w_router).astype(jnp.float32))
    sel = scores + router_bias.astype(jnp.float32)
    grp_score = lax.top_k(sel.reshape(n, N_GROUPS, N_EXPERTS // N_GROUPS), 2)[0].sum(-1)
    _, gidx = lax.top_k(grp_score, TOPK_GROUPS)
    gmask = jax.nn.one_hot(gidx, N_GROUPS, dtype=jnp.float32).sum(1) > 0
    sel = jnp.where(jnp.repeat(gmask, N_EXPERTS // N_GROUPS, axis=-1), sel, -jnp.inf)
    _, eidx = lax.top_k(sel, TOP_K)
    w_sel = jnp.take_along_axis(scores, eidx, axis=-1)
    w_sel = w_sel / jnp.sum(w_sel, axis=-1, keepdims=True) * ROUTE_SCALE
    gates = (jax.nn.one_hot(eidx, N_EXPERTS, dtype=jnp.float32) * w_sel[..., None]).sum(1).astype(h.dtype)
    out = swiglu(h, ws_gate, ws_up, ws_down)
    for e in range(N_EXPERTS):
        out = out + gates[:, e:e + 1] * swiglu(h, w_gate[e], w_up[e], w_down[e])
    return out


def setup_inputs(seed: int = 0) -> dict:
    key = jax.random.key(seed)
    ks = jax.random.split(key, 32)
    f32 = jnp.float32
    L, D = DEPTH, D_MODEL

    def nrm(k, shape, scale):
        return jax.random.normal(k, shape, f32) * scale

    def gain(k, shape):
        return 1.0 + 0.05 * jax.random.normal(k, shape, f32)

    return {
        'x': nrm(ks[0], (BATCH, SEQ, D), 1.0),
        'c': nrm(ks[1], (BATCH, D), 1.0),
        'ctx': nrm(ks[2], (BATCH, CTX_LEN, D), 1.0),
        'c_ctx': nrm(ks[3], (D,), 1.0),
        'ada_w': nrm(ks[4], (L, D, 6 * D), 0.5 * D ** -0.5),
        'ada_b': nrm(ks[5], (L, 6 * D), 0.01),
        'norm1_g': gain(ks[6], (L, D)),
        'w_in': nrm(ks[7], (L, D, IN_COLS), D ** -0.5),
        'conv_w': nrm(ks[8], (L, CONV_K, CONV_WIDTH), CONV_K ** -0.5),
        'na_q_norm': gain(ks[9], (L, HEAD_DIM)),
        'na_k_norm': gain(ks[10], (L, HEAD_DIM)),
        'na_rpb': nrm(ks[11], (L, NA_HEADS, 2 * NA_WIN_H - 1, 2 * NA_WIN_W - 1), 0.1),
        'mla_q_lat_norm': gain(ks[12], (L, MLA_Q_RANK)),
        'mla_kv_lat_norm': gain(ks[13], (L, MLA_KV_RANK)),
        'w_uq': nrm(ks[14], (L, MLA_Q_RANK, MLA_HEADS * (MLA_NOPE + MLA_ROPE)), MLA_Q_RANK ** -0.5),
        'w_ukv': nrm(ks[15], (L, MLA_KV_RANK, MLA_HEADS * (MLA_NOPE + MLA_V)), MLA_KV_RANK ** -0.5),
        'mla_q_norm': gain(ks[16], (L, MLA_NOPE + MLA_ROPE)),
        'mla_k_norm': gain(ks[17], (L, MLA_NOPE + MLA_ROPE)),
        'out_norm_g': gain(ks[18], (L, D_MIX)),
        'w_o': nrm(ks[19], (L, D_MIX, D), D_MIX ** -0.5),
        'norm2_g': gain(ks[20], (L, D)),
        'w_router': nrm(ks[21], (L, D, N_EXPERTS), D ** -0.5),
        'router_bias': nrm(ks[22], (L, N_EXPERTS), 0.01),
        'w_gate': nrm(ks[23], (L, N_EXPERTS, D, D_EXPERT), D ** -0.5),
        'w_up': nrm(ks[24], (L, N_EXPERTS, D, D_EXPERT), D ** -0.5),
        'w_down': nrm(ks[25], (L, N_EXPERTS, D_EXPERT, D), D_EXPERT ** -0.5),
        'ws_gate': nrm(ks[26], (L, D, D_EXPERT), D ** -0.5),
        'ws_up': nrm(ks[27], (L, D, D_EXPERT), D ** -0.5),
        'ws_down': nrm(ks[28], (L, D_EXPERT, D), D_EXPERT ** -0.5),
    }


def reference(x, c, ctx, c_ctx, ada_w, ada_b, norm1_g, w_in, conv_w, na_q_norm, na_k_norm, na_rpb,
              mla_q_lat_norm, mla_kv_lat_norm, w_uq, w_ukv, mla_q_norm, mla_k_norm, out_norm_g, w_o,
              norm2_g, w_router, router_bias, w_gate, w_up, w_down, ws_gate, ws_up, ws_down):
    b, s, d = x.shape
    rope_cos, rope_sin = axial_rope_tables(s)
    silu_c = jax.nn.silu(c)
    silu_cc = jax.nn.silu(c_ctx)
    xc = ctx
    for layer in range(DEPTH):
        last = layer == DEPTH - 1
        sh1, sc1, g1, sh2, sc2, g2 = [m[:, None, :] for m in jnp.split(silu_c @ ada_w[layer] + ada_b[layer], 6, axis=-1)]
        sh1c, sc1c, g1c, sh2c, sc2c, g2c = jnp.split(silu_cc @ ada_w[layer] + ada_b[layer], 6, axis=-1)
        head_w = (na_q_norm[layer], na_k_norm[layer], mla_q_lat_norm[layer], mla_kv_lat_norm[layer],
                  w_uq[layer], w_ukv[layer], mla_q_norm[layer], mla_k_norm[layer])

        h = rms_norm(x, norm1_g[layer]) * (1 + sc1) + sh1
        hc = rms_norm(xc, norm1_g[layer]) * (1 + sc1c) + sh1c
        parts = split_projection(h @ w_in[layer])
        parts_c = split_projection(hc @ w_in[layer])
        na_q, na_k, na_v, mq, mk, mv = attention_heads(parts, *head_w)
        na_qc, na_kc, na_vc, mqc, mkc, mvc = attention_heads(parts_c, *head_w)
        mq = rope_tail(mq, rope_cos, rope_sin)
        mk = rope_tail(mk, rope_cos, rope_sin)

        y_conv = short_conv_mixer(parts[0], parts[1], parts[2], conv_w[layer])
        y_na = neighborhood_attention(na_q, na_k, na_v, na_kc, na_vc, na_rpb[layer])
        y_mla = blocked_attention(mq, jnp.concatenate([mk, mkc], axis=1), jnp.concatenate([mv, mvc], axis=1))
        x = x + g1 * merge_groups(y_conv, y_na, y_mla, out_norm_g[layer], w_o[layer])
        if not last:
            yc_conv = short_conv_mixer(parts_c[0], parts_c[1], parts_c[2], conv_w[layer])
            yc_na = dense_attention(na_qc, na_kc, na_vc)
            yc_mla = dense_attention(mqc, mkc, mvc)
            xc = xc + g1c * merge_groups(yc_conv, yc_na, yc_mla, out_norm_g[layer], w_o[layer])

        moe_w = (w_router[layer], router_bias[layer], w_gate[layer], w_up[layer], w_down[layer],
                 ws_gate[layer], ws_up[layer], ws_down[layer])
        h2 = rms_norm(x, norm2_g[layer]) * (1 + sc2) + sh2
        x = x + g2 * moe_ffn(h2.reshape(b * s, d), *moe_w).reshape(b, s, d)
        if not last:
            h2c = rms_norm(xc, norm2_g[layer]) * (1 + sc2c) + sh2c
            xc = xc + g2c * moe_ffn(h2c.reshape(-1, d), *moe_w).reshape(xc.shape)
    return x
```

```python
import functools

import numpy as np
import jax
import jax.numpy as jnp
from jax import lax
from jax.experimental import pallas as pl
from jax.experimental.pallas import tpu as pltpu

F32 = jnp.float32
BF16 = jnp.bfloat16

GRID_W = 64
HEAD_DIM = 128
MLA_NOPE = 128
MLA_ROPE = 64
MLA_V = 128
MLA_QPAD = 256
NA_WIN_H = 8
NA_WIN_W = 16
N_GROUPS = 8
TOPK_GROUPS = 4
TOP_K = 8
ROUTE_SCALE = 2.5
ROPE_BASE = 10000.0
EPS = 1e-6
NEG = -1e30

LANE = 128
ROW_TILE = 512
CONV_TILE = 256
MOE_TILE = 256
VMEM_LIMIT = 56 * 1024 * 1024


def _cparams(n_axes):
    return pltpu.CompilerParams(dimension_semantics=("arbitrary",) * n_axes,
                                vmem_limit_bytes=VMEM_LIMIT)


def _rms(x, n):
    return x * lax.rsqrt(jnp.sum(x * x, axis=-1, keepdims=True) * (1.0 / n) + EPS)


def _mod_kernel(sb_ref, w_ref, b_ref, o_ref):
    tn = w_ref.shape[1]
    o_ref[...] = jnp.zeros(o_ref.shape, F32)
    for cblk in range(tn // LANE):
        wc = w_ref[:, cblk * LANE:(cblk + 1) * LANE]
        for r in range(3):
            acc = jnp.sum(wc * sb_ref[r], axis=0, keepdims=True)
            o_ref[r:r + 1, cblk * LANE:(cblk + 1) * LANE] = acc + b_ref[:, cblk * LANE:(cblk + 1) * LANE]


def _modulation(cvecs, ada_w, ada_b):
    L, D, D6 = ada_w.shape
    tn = 512
    s = cvecs * jax.nn.sigmoid(cvecs)
    sb = jnp.broadcast_to(s[:, :, None], (3, D, LANE))
    return pl.pallas_call(
        _mod_kernel,
        grid=(L, D6 // tn),
        in_specs=[pl.BlockSpec((3, D, LANE), lambda l, j: (0, 0, 0)),
                  pl.BlockSpec((None, D, tn), lambda l, j: (l, 0, j)),
                  pl.BlockSpec((None, 1, tn), lambda l, j: (l, 0, j))],
        out_specs=pl.BlockSpec((None, 8, tn), lambda l, j: (l, 0, j)),
        out_shape=jax.ShapeDtypeStruct((L, 8, D6), F32),
        compiler_params=_cparams(2),
        name="modulation",
    )(sb, ada_w, ada_b.reshape(L, 1, D6))


def _mod_row(i, tm, n_lat, seq):
    r0 = i * tm
    return jnp.where(r0 < n_lat, r0 // seq, n_lat // seq)


def _norm_mod_kernel(x_ref, g_ref, sc_ref, sh_ref, o_ref):
    x = x_ref[...]
    y = _rms(x, x.shape[-1]) * g_ref[...]
    o_ref[...] = (y * (1.0 + sc_ref[...]) + sh_ref[...]).astype(o_ref.dtype)


def _norm_mod(x, g, mod_l, which_sc, which_sh, n_lat, seq):
    R, D = x.shape
    tm = ROW_TILE
    mrow = functools.partial(_mod_row, tm=tm, n_lat=n_lat, seq=seq)
    return pl.pallas_call(
        _norm_mod_kernel,
        grid=(R // tm,),
        in_specs=[pl.BlockSpec((tm, D), lambda i: (i, 0)),
                  pl.BlockSpec((1, D), lambda i: (0, 0)),
                  pl.BlockSpec((None, 1, D), lambda i: (mrow(i) * 6 + which_sc, 0, 0)),
                  pl.BlockSpec((None, 1, D), lambda i: (mrow(i) * 6 + which_sh, 0, 0))],
        out_specs=pl.BlockSpec((tm, D), lambda i: (i, 0)),
        out_shape=jax.ShapeDtypeStruct((R, D), BF16),
        compiler_params=_cparams(1),
        name="norm_mod",
    )(x, g.reshape(1, D), mod_l, mod_l)


def _mm_kernel(x_ref, w_ref, o_ref):
    o_ref[...] = jnp.dot(x_ref[...], w_ref[...], preferred_element_type=F32).astype(o_ref.dtype)


def _matmul(x, w, out_dtype, tn, rows=None):
    R, K = x.shape
    rows = R if rows is None else rows
    N = w.shape[1]
    tm = ROW_TILE
    return pl.pallas_call(
        _mm_kernel,
        grid=(rows // tm, N // tn),
        in_specs=[pl.BlockSpec((tm, K), lambda i, j: (i, 0)),
                  pl.BlockSpec((K, tn), lambda i, j: (0, j))],
        out_specs=pl.BlockSpec((tm, tn), lambda i, j: (i, j)),
        out_shape=jax.ShapeDtypeStruct((rows, N), out_dtype),
        compiler_params=_cparams(2),
        name="matmul",
    )(x, w)


def _rope_rotate(t, c_ref, s1_ref, s2_ref):
    return (t * c_ref[...] + pltpu.roll(t, LANE - MLA_ROPE // 2, axis=1) * s1_ref[...]
            + pltpu.roll(t, MLA_ROPE // 2, axis=1) * s2_ref[...])


def _uq_kernel(x_ref, w_ref, gn_ref, gr_ref, c_ref, s1_ref, s2_ref, o_ref):
    acc = jnp.dot(x_ref[...], w_ref[...], preferred_element_type=F32)
    scale = (MLA_NOPE + MLA_ROPE) ** -0.5
    for h in range(acc.shape[1] // MLA_QPAD):
        o = h * MLA_QPAD
        a = _rms(acc[:, o:o + MLA_NOPE], MLA_NOPE) * gn_ref[...]
        r = _rms(acc[:, o + MLA_NOPE:o + MLA_QPAD], MLA_ROPE) * gr_ref[...]
        r = _rope_rotate(r, c_ref, s1_ref, s2_ref)
        o_ref[:, o:o + MLA_NOPE] = (a * scale).astype(o_ref.dtype)
        o_ref[:, o + MLA_NOPE:o + MLA_QPAD] = (r * scale).astype(o_ref.dtype)


def _rope_block(i, tm, n_lat, seq):
    return jnp.where(i * tm < n_lat, i % (seq // tm), seq // tm)


def _uq_matmul(x, w, gn, gr, tabs, n_lat, seq):
    R, K = x.shape
    N = w.shape[1]
    tm, tn = ROW_TILE, 1024
    rb = functools.partial(_rope_block, tm=tm, n_lat=n_lat, seq=seq)
    tab_spec = pl.BlockSpec((tm, LANE), lambda i, j: (rb(i), 0))
    vec_spec = pl.BlockSpec((1, LANE), lambda i, j: (0, 0))
    return pl.pallas_call(
        _uq_kernel,
        grid=(R // tm, N // tn),
        in_specs=[pl.BlockSpec((tm, K), lambda i, j: (i, 0)),
                  pl.BlockSpec((K, tn), lambda i, j: (0, j)),
                  vec_spec, vec_spec, tab_spec, tab_spec, tab_spec],
        out_specs=pl.BlockSpec((tm, tn), lambda i, j: (i, j)),
        out_shape=jax.ShapeDtypeStruct((R, N), BF16),
        compiler_params=_cparams(2),
        name="uq_matmul",
    )(x, w, gn, gr, *tabs)


def _uk_kernel(x_ref, w_ref, gk_ref, kr_ref, o_ref):
    acc = jnp.dot(x_ref[...], w_ref[...], preferred_element_type=F32)
    for h in range(acc.shape[1] // MLA_NOPE):
        a = _rms(acc[:, h * MLA_NOPE:(h + 1) * MLA_NOPE], MLA_NOPE) * gk_ref[...]
        o_ref[:, h * MLA_QPAD:h * MLA_QPAD + MLA_NOPE] = a.astype(o_ref.dtype)
        o_ref[:, h * MLA_QPAD + MLA_NOPE:(h + 1) * MLA_QPAD] = kr_ref[...]


def _uk_matmul(x, w, gk, kr_pad):
    R, K = x.shape
    N = w.shape[1]
    tm, tn = ROW_TILE, 512
    return pl.pallas_call(
        _uk_kernel,
        grid=(R // tm, N // tn),
        in_specs=[pl.BlockSpec((tm, K), lambda i, j: (i, 0)),
                  pl.BlockSpec((K, tn), lambda i, j: (0, j)),
                  pl.BlockSpec((1, LANE), lambda i, j: (0, 0)),
                  pl.BlockSpec((tm, LANE), lambda i, j: (i, 0))],
        out_specs=pl.BlockSpec((tm, 2 * tn), lambda i, j: (i, j)),
        out_shape=jax.ShapeDtypeStruct((R, 2 * N), BF16),
        compiler_params=_cparams(2),
        name="uk_matmul",
    )(x, w, gk, kr_pad)


def _wo_kernel(y_ref, w_ref, x_ref, g_ref, o_ref):
    acc = jnp.dot(y_ref[...], w_ref[...], preferred_element_type=F32)
    o_ref[...] = x_ref[...] + g_ref[...] * acc


def _wo_matmul(y, w, x, mod_l, rows, n_lat, seq):
    K = y.shape[1]
    N = w.shape[1]
    tm, tn = ROW_TILE, 512
    mrow = functools.partial(_mod_row, tm=tm, n_lat=n_lat, seq=seq)
    return pl.pallas_call(
        _wo_kernel,
        grid=(rows // tm, N // tn),
        in_specs=[pl.BlockSpec((tm, K), lambda i, j: (i, 0)),
                  pl.BlockSpec((K, tn), lambda i, j: (0, j)),
                  pl.BlockSpec((tm, tn), lambda i, j: (i, j)),
                  pl.BlockSpec((None, 1, tn), lambda i, j: (mrow(i) * 6 + 2, 0, j))],
        out_specs=pl.BlockSpec((tm, tn), lambda i, j: (i, j)),
        out_shape=jax.ShapeDtypeStruct((rows, N), F32),
        compiler_params=_cparams(2),
        name="wo_matmul",
    )(y, w, x, mod_l)


def _post_in_kernel(nq_ref, nk_ref, cq_ref, tail_ref, gq_ref, gk_ref, gql_ref, gkvl_ref, gkr_ref,
                    c_ref, s1_ref, s2_ref, nq_o, nk_o, cq_o, ckv_o, kr_o):
    nh = nq_ref.shape[1] // HEAD_DIM
    scale = HEAD_DIM ** -0.5
    for h in range(nh):
        sl = slice(h * HEAD_DIM, (h + 1) * HEAD_DIM)
        q = nq_ref[:, sl].astype(F32)
        nq_o[:, sl] = (_rms(q, HEAD_DIM) * gq_ref[...] * scale).astype(nq_o.dtype)
        k = nk_ref[:, sl].astype(F32)
        nk_o[:, sl] = (_rms(k, HEAD_DIM) * gk_ref[...]).astype(nk_o.dtype)
    cq = cq_ref[...].astype(F32)
    cq_o[...] = (_rms(cq, cq.shape[1]) * gql_ref[...]).astype(cq_o.dtype)
    kvr = ckv_o.shape[1]
    ckv = tail_ref[:, :kvr].astype(F32)
    ckv_o[...] = (_rms(ckv, kvr) * gkvl_ref[...]).astype(ckv_o.dtype)
    kr = tail_ref[:, kvr:kvr + LANE].astype(F32)
    kr = _rms(kr, MLA_ROPE) * gkr_ref[...]
    kr_o[...] = _rope_rotate(kr, c_ref, s1_ref, s2_ref).astype(kr_o.dtype)


def _post_in(z_main, z_tail, gq, gk, gql, gkvl, gkr, tabs, n_lat, seq):
    R = z_main.shape[0]
    W = z_main.shape[1] // 7
    kvr = z_tail.shape[1] - LANE
    tm = ROW_TILE
    rb = functools.partial(_rope_block, tm=tm, n_lat=n_lat, seq=seq)
    tab_spec = pl.BlockSpec((tm, LANE), lambda i: (rb(i), 0))

    def vec(n):
        return pl.BlockSpec((1, n), lambda i: (0, 0))

    def col(j):
        return pl.BlockSpec((tm, W), lambda i: (i, j))

    return pl.pallas_call(
        _post_in_kernel,
        grid=(R // tm,),
        in_specs=[col(3), col(4), col(6), pl.BlockSpec((tm, kvr + LANE), lambda i: (i, 0)),
                  vec(LANE), vec(LANE), vec(W), vec(kvr), vec(LANE), tab_spec, tab_spec, tab_spec],
        out_specs=[pl.BlockSpec((tm, W), lambda i: (i, 0)), pl.BlockSpec((tm, W), lambda i: (i, 0)),
                   pl.BlockSpec((tm, W), lambda i: (i, 0)), pl.BlockSpec((tm, kvr), lambda i: (i, 0)),
                   pl.BlockSpec((tm, LANE), lambda i: (i, 0))],
        out_shape=[jax.ShapeDtypeStruct((R, W), BF16), jax.ShapeDtypeStruct((R, W), BF16),
                   jax.ShapeDtypeStruct((R, W), BF16), jax.ShapeDtypeStruct((R, kvr), BF16),
                   jax.ShapeDtypeStruct((R, LANE), BF16)],
        compiler_params=_cparams(1),
        name="post_in",
    )(z_main, z_main, z_main, z_tail, gq, gk, gql, gkvl, gkr, *tabs)


def _conv_kernel(b_ref, c_ref, u_ref, cp_ref, up_ref, cn_ref, un_ref, w_ref, g_ref, o_ref,
                 *, n_lat, seq, ctx_len):
    tm, W = c_ref.shape
    row0 = pl.program_id(0) * tm
    lat = row0 < n_lat
    pos = jnp.where(lat, row0 % seq, (row0 - n_lat) % ctx_len)
    slen = jnp.where(lat, seq, ctx_len)
    has_prev = (pos > 0).astype(F32)
    has_next = (pos + tm < slen).astype(F32)
    v = c_ref[...].astype(F32) * u_ref[...].astype(F32)
    vprev = cp_ref[7:8, :].astype(F32) * up_ref[7:8, :].astype(F32) * has_prev
    vnext = cn_ref[0:1, :].astype(F32) * un_ref[0:1, :].astype(F32) * has_next
    rid = lax.broadcasted_iota(jnp.int32, (tm, W), 0)
    v_dn = jnp.where(rid == 0, vprev, pltpu.roll(v, 1, axis=0))
    v_up = jnp.where(rid == tm - 1, vnext, pltpu.roll(v, tm - 1, axis=0))
    y = w_ref[0:1, :] * v_dn + w_ref[1:2, :] * v + w_ref[2:3, :] * v_up
    y = b_ref[...].astype(F32) * y
    o_ref[...] = (_rms(y, W) * g_ref[...]).astype(o_ref.dtype)


def _conv_mixer(z_main, conv_w, g, rows, n_lat, seq, ctx_len):
    R = z_main.shape[0]
    W = z_main.shape[1] // 7
    tm = CONV_TILE
    nb8 = R // 8

    def cur(j):
        return pl.BlockSpec((tm, W), lambda i: (i, j))

    def prev(j):
        return pl.BlockSpec((8, W), lambda i: (jnp.maximum(i * (tm // 8) - 1, 0), j))

    def nxt(j):
        return pl.BlockSpec((8, W), lambda i: (jnp.minimum((i + 1) * (tm // 8), nb8 - 1), j))

    return pl.pallas_call(
        functools.partial(_conv_kernel, n_lat=n_lat, seq=seq, ctx_len=ctx_len),
        grid=(rows // tm,),
        in_specs=[cur(0), cur(1), cur(2), prev(1), prev(2), nxt(1), nxt(2),
                  pl.BlockSpec((3, W), lambda i: (0, 0)), pl.BlockSpec((1, W), lambda i: (0, 0))],
        out_specs=pl.BlockSpec((tm, W), lambda i: (i, 0)),
        out_shape=jax.ShapeDtypeStruct((rows, W), BF16),
        compiler_params=_cparams(1),
        name="conv_mixer",
    )(z_main, z_main, z_main, z_main, z_main, z_main, z_main, conv_w, g)


def _na_bias_table(rpb):
    qc = np.arange(GRID_W)[:, None]
    kc = np.arange(GRID_W)[None, :]
    cstart = np.clip(qc - NA_WIN_W // 2, 0, GRID_W - NA_WIN_W)
    ok = (kc >= cstart) & (kc < cstart + NA_WIN_W)
    dc = np.clip(kc - qc + NA_WIN_W - 1, 0, 2 * NA_WIN_W - 2)
    t = np.arange(NA_WIN_H)[:, None]
    w = np.arange(NA_WIN_H)[None, :]
    g = rpb[:, (t + w)[:, :, None, None], dc[None, None]]
    g = jnp.where(ok[None, None, None], g, NEG)
    H = rpb.shape[0]
    return g.transpose(0, 1, 3, 2, 4).reshape(H, NA_WIN_H, GRID_W, NA_WIN_H * GRID_W).astype(F32)


def _na_kernel(q_ref, k_ref, v_ref, kc_ref, vc_ref, bias_ref, o_ref):
    rows = q_ref.shape[0] // GRID_W
    nk = NA_WIN_H * GRID_W
    dn = (((1,), (1,)), ((), ()))
    kc = kc_ref[...]
    vc = vc_ref[...]

    def body(r, carry):
        r0 = jnp.clip(r - NA_WIN_H // 2, 0, rows - NA_WIN_H)
        q = q_ref[pl.ds(pl.multiple_of(r * GRID_W, GRID_W), GRID_W), :]
        koff = pl.multiple_of(r0 * GRID_W, GRID_W)
        k = k_ref[pl.ds(koff, nk), :]
        v = v_ref[pl.ds(koff, nk), :]
        s = lax.dot_general(q, k, dn, preferred_element_type=F32) + bias_ref[r0 - r + NA_WIN_H - 1]
        sc = lax.dot_general(q, kc, dn, preferred_element_type=F32)
        m = jnp.maximum(jnp.max(s, axis=-1, keepdims=True), jnp.max(sc, axis=-1, keepdims=True))
        p = jnp.exp(s - m)
        pc = jnp.exp(sc - m)
        l = jnp.sum(p, axis=-1, keepdims=True) + jnp.sum(pc, axis=-1, keepdims=True)
        o = (jnp.dot(p.astype(BF16), v, preferred_element_type=F32)
             + jnp.dot(pc.astype(BF16), vc, preferred_element_type=F32))
        o_ref[pl.ds(pl.multiple_of(r * GRID_W, GRID_W), GRID_W), :] = (o / l).astype(o_ref.dtype)
        return carry

    lax.fori_loop(0, rows, body, 0)


def _na_attention(nq, nk, z_main, bias_tab, B, seq, ctx_len, n_lat):
    R, W = nq.shape
    nh = W // HEAD_DIM
    cb = n_lat // ctx_len
    voff = 5 * nh
    return pl.pallas_call(
        _na_kernel,
        grid=(B, nh),
        in_specs=[pl.BlockSpec((seq, HEAD_DIM), lambda b, h: (b, h)),
                  pl.BlockSpec((seq, HEAD_DIM), lambda b, h: (b, h)),
                  pl.BlockSpec((seq, HEAD_DIM), lambda b, h: (b, voff + h)),
                  pl.BlockSpec((ctx_len, HEAD_DIM), lambda b, h: (cb + b, h)),
                  pl.BlockSpec((ctx_len, HEAD_DIM), lambda b, h: (cb + b, voff + h)),
                  pl.BlockSpec((None, NA_WIN_H, GRID_W, NA_WIN_H * GRID_W), lambda b, h: (h, 0, 0, 0))],
        out_specs=pl.BlockSpec((seq, HEAD_DIM), lambda b, h: (b, h)),
        out_shape=jax.ShapeDtypeStruct((R, W), BF16),
        compiler_params=_cparams(2),
        name="na_attention",
    )(nq, nk, z_main, nk, z_main, bias_tab)


def _flash_kernel(q_ref, k_ref, v_ref, kc_ref, vc_ref, o_ref, *, tk):
    tq = q_ref.shape[0]
    dv = v_ref.shape[1]
    dn = (((1,), (1,)), ((), ()))
    q = q_ref[...]

    def step(k, v, m, l, acc):
        s = lax.dot_general(q, k, dn, preferred_element_type=F32)
        m_new = jnp.maximum(m, jnp.max(s, axis=-1, keepdims=True))
        alpha = jnp.exp(m - m_new)
        p = jnp.exp(s - m_new)
        l = alpha * l + jnp.sum(p, axis=-1, keepdims=True)
        acc = alpha * acc + jnp.dot(p.astype(BF16), v, preferred_element_type=F32)
        return m_new, l, acc

    def body(j, carry):
        off = pl.multiple_of(j * tk, tk)
        return step(k_ref[pl.ds(off, tk), :], v_ref[pl.ds(off, tk), :], *carry)

    init = (jnp.full((tq, 1), NEG, F32), jnp.zeros((tq, 1), F32), jnp.zeros((tq, dv), F32))
    m, l, acc = lax.fori_loop(0, k_ref.shape[0] // tk, body, init)
    m, l, acc = step(kc_ref[...], vc_ref[...], m, l, acc)
    o_ref[...] = (acc / l).astype(o_ref.dtype)


def _mla_attention(qf, kf, vf, B, seq, ctx_len, n_lat):
    R = qf.shape[0]
    nh = qf.shape[1] // MLA_QPAD
    tq, tk = 512, 512
    cb = n_lat // ctx_len
    nqb = seq // tq
    return pl.pallas_call(
        functools.partial(_flash_kernel, tk=tk),
        grid=(B, nh, nqb),
        in_specs=[pl.BlockSpec((tq, MLA_QPAD), lambda b, h, i: (b * nqb + i, h)),
                  pl.BlockSpec((seq, MLA_QPAD), lambda b, h, i: (b, h)),
                  pl.BlockSpec((seq, MLA_V), lambda b, h, i: (b, h)),
                  pl.BlockSpec((ctx_len, MLA_QPAD), lambda b, h, i: (cb + b, h)),
                  pl.BlockSpec((ctx_len, MLA_V), lambda b, h, i: (cb + b, h))],
        out_specs=pl.BlockSpec((tq, MLA_V), lambda b, h, i: (b * nqb + i, h)),
        out_shape=jax.ShapeDtypeStruct((R, nh * MLA_V), BF16),
        compiler_params=_cparams(3),
        name="mla_attention",
    )(qf, kf, vf, kf, vf)


def _ctx_attn_kernel(q_ref, k_ref, v_ref, y_hbm, o_ref):
    del y_hbm
    s = lax.dot_general(q_ref[...], k_ref[...], (((1,), (1,)), ((), ())), preferred_element_type=F32)
    p = jnp.exp(s - jnp.max(s, axis=-1, keepdims=True))
    o = jnp.dot(p.astype(BF16), v_ref[...], preferred_element_type=F32)
    o_ref[...] = (o / jnp.sum(p, axis=-1, keepdims=True)).astype(o_ref.dtype)


def _ctx_attention(q, k, v, y, dq, dv, qoff, koff, voff, B, ctx_len, n_lat):
    nh = y.shape[1] // dv
    cb = n_lat // ctx_len
    return pl.pallas_call(
        _ctx_attn_kernel,
        grid=(B, nh),
        in_specs=[pl.BlockSpec((ctx_len, dq), lambda b, h: (cb + b, qoff + h)),
                  pl.BlockSpec((ctx_len, dq), lambda b, h: (cb + b, koff + h)),
                  pl.BlockSpec((ctx_len, dv), lambda b, h: (cb + b, voff + h)),
                  pl.BlockSpec(memory_space=pl.ANY)],
        out_specs=pl.BlockSpec((ctx_len, dv), lambda b, h: (cb + b, h)),
        out_shape=jax.ShapeDtypeStruct(y.shape, y.dtype),
        input_output_aliases={3: 0},
        compiler_params=_cparams(2),
        name="ctx_attention",
    )(q, k, v, y)


def _merge_kernel(yc_ref, yn_ref, ym_ref, gn_ref, gm_ref, o_ref):
    cw = yc_ref.shape[1]
    nw = yn_ref.shape[1]
    o_ref[:, :cw] = yc_ref[...]
    yn = yn_ref[...].astype(F32)
    o_ref[:, cw:cw + nw] = (_rms(yn, nw) * gn_ref[...]).astype(o_ref.dtype)
    ym = ym_ref[...].astype(F32)
    o_ref[:, cw + nw:] = (_rms(ym, ym.shape[1]) * gm_ref[...]).astype(o_ref.dtype)


def _merge(y_conv, y_na, y_mla, gn, gm, rows):
    cw, nw, mw = y_conv.shape[1], y_na.shape[1], y_mla.shape[1]
    tm = ROW_TILE
    return pl.pallas_call(
        _merge_kernel,
        grid=(rows // tm,),
        in_specs=[pl.BlockSpec((tm, cw), lambda i: (i, 0)), pl.BlockSpec((tm, nw), lambda i: (i, 0)),
                  pl.BlockSpec((tm, mw), lambda i: (i, 0)),
                  pl.BlockSpec((1, nw), lambda i: (0, 0)), pl.BlockSpec((1, mw), lambda i: (0, 0))],
        out_specs=pl.BlockSpec((tm, cw + nw + mw), lambda i: (i, 0)),
        out_shape=jax.ShapeDtypeStruct((rows, cw + nw + mw), BF16),
        compiler_params=_cparams(1),
        name="merge_norm",
    )(y_conv, y_na, y_mla, gn, gm)


def _router_kernel(x_ref, g_ref, sc_ref, sh_ref, whi_ref, wlo_ref, rb_ref, h_ref, idx_ref, wgt_ref):
    x = x_ref[...]
    h = _rms(x, x.shape[-1]) * g_ref[...] * (1.0 + sc_ref[...]) + sh_ref[...]
    h_hi = h.astype(BF16)
    h_ref[...] = h_hi
    h_lo = (h - h_hi.astype(F32)).astype(BF16)
    dn = (((1,), (1,)), ((), ()))
    logits = (lax.dot_general(whi_ref[...], h_hi, dn, preferred_element_type=F32)
              + lax.dot_general(whi_ref[...], h_lo, dn, preferred_element_type=F32)
              + lax.dot_general(wlo_ref[...], h_hi, dn, preferred_element_type=F32))
    E, tm = logits.shape
    gs = E // N_GROUPS
    scores = jax.nn.sigmoid(logits)
    sel = scores + rb_ref[...]
    sel3 = sel.reshape(N_GROUPS, gs, tm)
    io3 = lax.broadcasted_iota(jnp.int32, sel3.shape, 1)
    m1 = jnp.max(sel3, axis=1, keepdims=True)
    i1 = jnp.min(jnp.where(sel3 == m1, io3, gs), axis=1, keepdims=True)
    m2 = jnp.max(jnp.where(io3 == i1, -jnp.inf, sel3), axis=1, keepdims=True)
    grp = m1 + m2
    iog = lax.broadcasted_iota(jnp.int32, grp.shape, 0)
    keep = jnp.zeros(grp.shape, jnp.bool_)
    for _ in range(TOPK_GROUPS):
        gm = jnp.max(grp, axis=0, keepdims=True)
        gi = jnp.min(jnp.where(grp == gm, iog, N_GROUPS), axis=0, keepdims=True)
        hit = iog == gi
        keep = jnp.logical_or(keep, hit)
        grp = jnp.where(hit, -jnp.inf, grp)
    selm = jnp.where(keep, sel3, -jnp.inf).reshape(E, tm)
    ioe = lax.broadcasted_iota(jnp.int32, (E, tm), 0)
    idxs, wgts = [], []
    for _ in range(TOP_K):
        m = jnp.max(selm, axis=0, keepdims=True)
        ei = jnp.min(jnp.where(selm == m, ioe, E), axis=0, keepdims=True)
        hit = ioe == ei
        idxs.append(ei)
        wgts.append(jnp.sum(jnp.where(hit, scores, 0.0), axis=0, keepdims=True))
        selm = jnp.where(hit, -jnp.inf, selm)
    wsum = wgts[0]
    for w in wgts[1:]:
        wsum = wsum + w
    for k in range(TOP_K):
        idx_ref[k:k + 1, :] = idxs[k]
        wgt_ref[k:k + 1, :] = wgts[k] / wsum * ROUTE_SCALE


def _router(x, g, mod_l, w_router, router_bias, rows, n_lat, seq):
    D = x.shape[1]
    E = w_router.shape[1]
    tm = ROW_TILE
    mrow = functools.partial(_mod_row, tm=tm, n_lat=n_lat, seq=seq)
    wt = w_router.T
    whi = wt.astype(BF16)
    wlo = (wt - whi.astype(F32)).astype(BF16)
    return pl.pallas_call(
        _router_kernel,
        grid=(rows // tm,),
        in_specs=[pl.BlockSpec((tm, D), lambda i: (i, 0)),
                  pl.BlockSpec((1, D), lambda i: (0, 0)),
                  pl.BlockSpec((None, 1, D), lambda i: (mrow(i) * 6 + 4, 0, 0)),
                  pl.BlockSpec((None, 1, D), lambda i: (mrow(i) * 6 + 3, 0, 0)),
                  pl.BlockSpec((E, D), lambda i: (0, 0)),
                  pl.BlockSpec((E, D), lambda i: (0, 0)),
                  pl.BlockSpec((E, 1), lambda i: (0, 0))],
        out_specs=[pl.BlockSpec((tm, D), lambda i: (i, 0)),
                   pl.BlockSpec((TOP_K, tm), lambda i: (0, i)),
                   pl.BlockSpec((TOP_K, tm), lambda i: (0, i))],
        out_shape=[jax.ShapeDtypeStruct((rows, D), BF16),
                   jax.ShapeDtypeStruct((TOP_K, rows), jnp.int32),
                   jax.ShapeDtypeStruct((TOP_K, rows), F32)],
        compiler_params=_cparams(1),
        name="router",
    )(x, g.reshape(1, D), mod_l, mod_l, whi, wlo, router_bias.reshape(E, 1))


def _expert_kernel(te_ref, nt_ref, xs_ref, wg_ref, wu_ref, wd_ref, gate_ref, o_ref):
    del te_ref
    t = pl.program_id(0)

    @pl.when(t < nt_ref[0])
    def _():
        xs = xs_ref[...]
        g = jnp.dot(xs, wg_ref[...], preferred_element_type=F32)
        u = jnp.dot(xs, wu_ref[...], preferred_element_type=F32)
        a = g * jax.nn.sigmoid(g) * u * gate_ref[...]
        o_ref[...] = jnp.dot(a.astype(BF16), wd_ref[...], preferred_element_type=F32).astype(o_ref.dtype)

    @pl.when(t >= nt_ref[0])
    def _():
        o_ref[...] = jnp.zeros(o_ref.shape, o_ref.dtype)


def _experts(xs, row_gate, tile_expert, n_tiles, wg, wu, wd):
    P, D = xs.shape
    E, _, DE = wg.shape
    T = MOE_TILE
    grid_spec = pltpu.PrefetchScalarGridSpec(
        num_scalar_prefetch=2,
        grid=(P // T,),
        in_specs=[pl.BlockSpec((T, D), lambda t, te, nt: (t, 0)),
                  pl.BlockSpec((None, D, DE), lambda t, te, nt: (te[t], 0, 0)),
                  pl.BlockSpec((None, D, DE), lambda t, te, nt: (te[t], 0, 0)),
                  pl.BlockSpec((None, DE, D), lambda t, te, nt: (te[t], 0, 0)),
                  pl.BlockSpec((T, 1), lambda t, te, nt: (t, 0))],
        out_specs=pl.BlockSpec((T, D), lambda t, te, nt: (t, 0)),
    )
    return pl.pallas_call(
        _expert_kernel,
        grid_spec=grid_spec,
        out_shape=jax.ShapeDtypeStruct((P, D), BF16),
        compiler_params=_cparams(1),
        name="experts",
    )(tile_expert, n_tiles, xs, wg, wu, wd, row_gate)


def _dispatch_plan(eidx, wgt, n_experts):
    K, rows = eidx.shape
    T = MOE_TILE
    P0 = K * rows
    max_tiles = -(-P0 // T) + n_experts
    e_flat = eidx.reshape(-1)
    order = jnp.argsort(e_flat, stable=True).astype(jnp.int32)
    e_sorted = e_flat[order]
    counts = jnp.zeros((n_experts,), jnp.int32).at[e_flat].add(1)
    tiles_per = (counts + T - 1) // T
    tile_end = jnp.cumsum(tiles_per)
    tile_start = tile_end - tiles_per
    start = jnp.cumsum(counts) - counts
    dest_sorted = tile_start[e_sorted] * T + (jnp.arange(P0, dtype=jnp.int32) - start[e_sorted])
    row_token = jnp.zeros((max_tiles * T,), jnp.int32).at[dest_sorted].set(order % rows)
    row_gate = jnp.zeros((max_tiles * T,), F32).at[dest_sorted].set(wgt.reshape(-1)[order])
    pos = jnp.zeros((P0,), jnp.int32).at[order].set(dest_sorted).reshape(K, rows)
    tile_expert = jnp.minimum(
        jnp.searchsorted(tile_end, jnp.arange(max_tiles, dtype=jnp.int32), side="right"),
        n_experts - 1).astype(jnp.int32)
    n_tiles = tile_end[-1:].astype(jnp.int32)
    return row_token, row_gate.reshape(-1, 1), pos, tile_expert, n_tiles


def _shared_kernel(h_ref, wg_ref, wu_ref, wd_ref, moe_ref, x_ref, g2_ref, o_ref):
    h = h_ref[...]
    g = jnp.dot(h, wg_ref[...], preferred_element_type=F32)
    u = jnp.dot(h, wu_ref[...], preferred_element_type=F32)
    a = (g * jax.nn.sigmoid(g) * u).astype(BF16)
    y = jnp.dot(a, wd_ref[...], preferred_element_type=F32) + moe_ref[...].astype(F32)
    o_ref[...] = x_ref[...] + g2_ref[...] * y


def _shared_combine(h2, wsg, wsu, wsd, moe_sum, x, mod_l, rows, n_lat, seq):
    D = h2.shape[1]
    DE = wsg.shape[1]
    tm = ROW_TILE
    mrow = functools.partial(_mod_row, tm=tm, n_lat=n_lat, seq=seq)
    return pl.pallas_call(
        _shared_kernel,
        grid=(rows // tm,),
        in_specs=[pl.BlockSpec((tm, D), lambda i: (i, 0)),
                  pl.BlockSpec((D, DE), lambda i: (0, 0)),
                  pl.BlockSpec((D, DE), lambda i: (0, 0)),
                  pl.BlockSpec((DE, D), lambda i: (0, 0)),
                  pl.BlockSpec((tm, D), lambda i: (i, 0)),
                  pl.BlockSpec((tm, D), lambda i: (i, 0)),
                  pl.BlockSpec((None, 1, D), lambda i: (mrow(i) * 6 + 5, 0, 0))],
        out_specs=pl.BlockSpec((tm, D), lambda i: (i, 0)),
        out_shape=jax.ShapeDtypeStruct((rows, D), F32),
        compiler_params=_cparams(1),
        name="shared_combine",
    )(h2, wsg, wsu, wsd, moe_sum, x, mod_l)


def _rope_tables(seq, pad_rows):
    t = np.arange(seq)
    row = (t // GRID_W).astype(np.float32)
    col = (t % GRID_W).astype(np.float32)
    n_freq = MLA_ROPE // 4
    inv = (ROPE_BASE ** (-jnp.arange(n_freq, dtype=F32) / n_freq))
    ang = jnp.concatenate([jnp.asarray(row)[:, None] * inv, jnp.asarray(col)[:, None] * inv], axis=-1)
    cos, sin = jnp.cos(ang), jnp.sin(ang)
    half = MLA_ROPE // 2
    z = jnp.zeros((seq, LANE - MLA_ROPE), F32)
    zh = jnp.zeros((seq, half), F32)
    c = jnp.concatenate([cos, cos, z], axis=-1)
    s1 = jnp.concatenate([-sin, zh, z], axis=-1)
    s2 = jnp.concatenate([zh, sin, z], axis=-1)
    ident = jnp.concatenate([jnp.ones((pad_rows, MLA_ROPE), F32), jnp.zeros((pad_rows, LANE - MLA_ROPE), F32)], -1)
    zero = jnp.zeros((pad_rows, LANE), F32)
    return (jnp.concatenate([c, ident], 0), jnp.concatenate([s1, zero], 0), jnp.concatenate([s2, zero], 0))


def _pad_lanes(v, n):
    return jnp.concatenate([v, jnp.zeros((n - v.shape[0],), v.dtype)]).reshape(1, n)


def kernel(x, c, ctx, c_ctx, ada_w, ada_b, norm1_g, w_in, conv_w, na_q_norm, na_k_norm, na_rpb,
           mla_q_lat_norm, mla_kv_lat_norm, w_uq, w_ukv, mla_q_norm, mla_k_norm, out_norm_g, w_o,
           norm2_g, w_router, router_bias, w_gate, w_up, w_down, ws_gate, ws_up, ws_down):
    B, S, D = x.shape
    CTX = ctx.shape[1]
    L = ada_w.shape[0]
    n_lat = B * S
    R = n_lat + B * CTX
    W = D // 4
    kvr = w_ukv.shape[1]
    nh_mla = w_uq.shape[2] // (MLA_NOPE + MLA_ROPE)
    E = w_router.shape[2]
    assert B == 2 and S % ROW_TILE == 0 and (B * CTX) % ROW_TILE == 0 and CTX % CONV_TILE == 0
    assert S % GRID_W == 0 and S // GRID_W >= NA_WIN_H and W % HEAD_DIM == 0 and kvr % LANE == 0

    xs = jnp.concatenate([x.reshape(n_lat, D), ctx.reshape(B * CTX, D)], axis=0)
    cvecs = jnp.concatenate([c, c_ctx[None]], axis=0)
    mod = _modulation(cvecs, ada_w, ada_b)
    tabs = _rope_tables(S, ROW_TILE)

    for l in range(L):
        last = l == L - 1
        rows = n_lat if last else R
        mod_l = mod[l].reshape(8 * 6, 1, D)

        w_in_l = w_in[l].astype(BF16)
        w_main = w_in_l[:, :7 * W]
        w_tail = jnp.concatenate([w_in_l[:, 7 * W:], jnp.zeros((D, LANE - MLA_ROPE), BF16)], axis=1)
        wq = w_uq[l].reshape(W, nh_mla, MLA_NOPE + MLA_ROPE)
        wq = jnp.concatenate([wq, jnp.zeros((W, nh_mla, MLA_QPAD - MLA_NOPE - MLA_ROPE), F32)], axis=2)
        wq = wq.reshape(W, nh_mla * MLA_QPAD).astype(BF16)
        wkv = w_ukv[l].reshape(kvr, nh_mla, MLA_NOPE + MLA_V)
        wk = wkv[:, :, :MLA_NOPE].reshape(kvr, nh_mla * MLA_NOPE).astype(BF16)
        wv = wkv[:, :, MLA_NOPE:].reshape(kvr, nh_mla * MLA_V).astype(BF16)
        gq_n = mla_q_norm[l][:MLA_NOPE].reshape(1, LANE)
        gq_r = _pad_lanes(mla_q_norm[l][MLA_NOPE:], LANE)
        gk_n = mla_k_norm[l][:MLA_NOPE].reshape(1, LANE)
        gk_r = _pad_lanes(mla_k_norm[l][MLA_NOPE:], LANE)
        og = out_norm_g[l]

        h = _norm_mod(xs, norm1_g[l], mod_l, 1, 0, n_lat, S)
        z_main = _matmul(h, w_main, BF16, W)
        z_tail = _matmul(h, w_tail, BF16, w_tail.shape[1])
        nq, nk, cq, ckv, kr = _post_in(z_main, z_tail, na_q_norm[l].reshape(1, LANE),
                                       na_k_norm[l].reshape(1, LANE), mla_q_lat_norm[l].reshape(1, W),
                                       mla_kv_lat_norm[l].reshape(1, kvr), gk_r, tabs, n_lat, S)
        qf = _uq_matmul(cq, wq, gq_n, gq_r, tabs, n_lat, S)
        kf = _uk_matmul(ckv, wk, gk_n, kr)
        vf = _matmul(ckv, wv, BF16, 512)

        y_conv = _conv_mixer(z_main, conv_w[l], og[:W].reshape(1, W), rows, n_lat, S, CTX)
        y_na = _na_attention(nq, nk, z_main, _na_bias_table(na_rpb[l]), B, S, CTX, n_lat)
        y_mla = _mla_attention(qf, kf, vf, B, S, CTX, n_lat)
        if not last:
            nh_na = W // HEAD_DIM
            y_na = _ctx_attention(nq, nk, z_main, y_na, HEAD_DIM, HEAD_DIM, 0, 0, 5 * nh_na, B, CTX, n_lat)
            y_mla = _ctx_attention(qf, kf, vf, y_mla, MLA_QPAD, MLA_V, 0, 0, 0, B, CTX, n_lat)
        y = _merge(y_conv, y_na, y_mla, og[W:2 * W].reshape(1, W), og[2 * W:].reshape(1, D - 2 * W), rows)
        xs = _wo_matmul(y, w_o[l].astype(BF16), xs, mod_l, rows, n_lat, S)

        h2, eidx, wgt = _router(xs, norm2_g[l], mod_l, w_router[l], router_bias[l], rows, n_lat, S)
        row_token, row_gate, pos, tile_expert, n_tiles = _dispatch_plan(eidx, wgt, E)
        gathered = jnp.take(h2, row_token, axis=0)
        y_rows = _experts(gathered, row_gate, tile_expert, n_tiles,
                          w_gate[l].astype(BF16), w_up[l].astype(BF16), w_down[l].astype(BF16))
        moe_sum = jnp.take(y_rows, pos.T.reshape(-1), axis=0).reshape(rows, TOP_K, D)
        moe_sum = jnp.sum(moe_sum.astype(F32), axis=1).astype(BF16)
        xs = _shared_combine(h2, ws_gate[l].astype(BF16), ws_up[l].astype(BF16), ws_down[l].astype(BF16),
                             moe_sum, xs, mod_l, rows, n_lat, S)

    return xs[:n_lat].reshape(B, S, D)
```

```python
import functools

import numpy as np
import jax
import jax.numpy as jnp
from jax import lax
from jax.experimental import pallas as pl
from jax.experimental.pallas import tpu as pltpu

F32 = jnp.float32
BF16 = jnp.bfloat16

GRID_W = 64
HEAD_DIM = 128
MLA_NOPE = 128
MLA_ROPE = 64
MLA_V = 128
MLA_QPAD = 256
NA_WIN_H = 8
NA_WIN_W = 16
N_GROUPS = 8
TOPK_GROUPS = 4
TOP_K = 8
ROUTE_SCALE = 2.5
ROPE_BASE = 10000.0
EPS = 1e-6
NEG = -1e30
LOG2E = 1.4426950408889634

LANE = 128
ROW_TILE = 512
CONV_TILE = 256
MOE_TILE = 256
COMBINE_TILE = 128
VMEM_LIMIT = 56 * 1024 * 1024


def _cparams(n_axes):
    return pltpu.CompilerParams(dimension_semantics=("arbitrary",) * n_axes,
                                vmem_limit_bytes=VMEM_LIMIT)


def _rms(x, n):
    return x * lax.rsqrt(jnp.sum(x * x, axis=-1, keepdims=True) * (1.0 / n) + EPS)


def _mod_kernel(sb_ref, w_ref, b_ref, o_ref):
    tn = w_ref.shape[1]
    o_ref[...] = jnp.zeros(o_ref.shape, F32)
    for cblk in range(tn // LANE):
        wc = w_ref[:, cblk * LANE:(cblk + 1) * LANE]
        for r in range(3):
            acc = jnp.sum(wc * sb_ref[r], axis=0, keepdims=True)
            o_ref[r:r + 1, cblk * LANE:(cblk + 1) * LANE] = acc + b_ref[:, cblk * LANE:(cblk + 1) * LANE]


def _modulation(cvecs, ada_w, ada_b):
    L, D, D6 = ada_w.shape
    tn = 512
    s = cvecs * jax.nn.sigmoid(cvecs)
    sb = jnp.broadcast_to(s[:, :, None], (3, D, LANE))
    return pl.pallas_call(
        _mod_kernel,
        grid=(L, D6 // tn),
        in_specs=[pl.BlockSpec((3, D, LANE), lambda l, j: (0, 0, 0)),
                  pl.BlockSpec((None, D, tn), lambda l, j: (l, 0, j)),
                  pl.BlockSpec((None, 1, tn), lambda l, j: (l, 0, j))],
        out_specs=pl.BlockSpec((None, 8, tn), lambda l, j: (l, 0, j)),
        out_shape=jax.ShapeDtypeStruct((L, 8, D6), F32),
        compiler_params=_cparams(2),
        name="modulation",
    )(sb, ada_w, ada_b.reshape(L, 1, D6))


def _mod_row(i, tm, n_lat, seq):
    r0 = i * tm
    return jnp.where(r0 < n_lat, r0 // seq, n_lat // seq)


def _norm_mod_kernel(x_ref, g_ref, sc_ref, sh_ref, o_ref):
    x = x_ref[...]
    y = _rms(x, x.shape[-1]) * g_ref[...]
    o_ref[...] = (y * (1.0 + sc_ref[...]) + sh_ref[...]).astype(o_ref.dtype)


def _norm_mod(x, g, mod_l, which_sc, which_sh, n_lat, seq):
    R, D = x.shape
    tm = ROW_TILE
    mrow = functools.partial(_mod_row, tm=tm, n_lat=n_lat, seq=seq)
    return pl.pallas_call(
        _norm_mod_kernel,
        grid=(R // tm,),
        in_specs=[pl.BlockSpec((tm, D), lambda i: (i, 0)),
                  pl.BlockSpec((1, D), lambda i: (0, 0)),
                  pl.BlockSpec((None, 1, D), lambda i: (mrow(i) * 6 + which_sc, 0, 0)),
                  pl.BlockSpec((None, 1, D), lambda i: (mrow(i) * 6 + which_sh, 0, 0))],
        out_specs=pl.BlockSpec((tm, D), lambda i: (i, 0)),
        out_shape=jax.ShapeDtypeStruct((R, D), BF16),
        compiler_params=_cparams(1),
        name="norm_mod",
    )(x, g.reshape(1, D), mod_l, mod_l)


def _mm_kernel(x_ref, w_ref, o_ref):
    o_ref[...] = jnp.dot(x_ref[...], w_ref[...], preferred_element_type=F32).astype(o_ref.dtype)


def _matmul(x, w, out_dtype, tn, rows=None):
    R, K = x.shape
    rows = R if rows is None else rows
    N = w.shape[1]
    tm = ROW_TILE
    return pl.pallas_call(
        _mm_kernel,
        grid=(rows // tm, N // tn),
        in_specs=[pl.BlockSpec((tm, K), lambda i, j: (i, 0)),
                  pl.BlockSpec((K, tn), lambda i, j: (0, j))],
        out_specs=pl.BlockSpec((tm, tn), lambda i, j: (i, j)),
        out_shape=jax.ShapeDtypeStruct((rows, N), out_dtype),
        compiler_params=_cparams(2),
        name="matmul",
    )(x, w)


def _mm_ws_kernel(x_ref, w_ref, o_ref, wb_ref):
    @pl.when(pl.program_id(1) == 0)
    def _():
        wb_ref[...] = w_ref[...].astype(BF16)

    o_ref[...] = jnp.dot(x_ref[...], wb_ref[...], preferred_element_type=F32).astype(o_ref.dtype)


def _matmul_ws(x, w, layer, n_cols, tn, out_dtype):
    R, K = x.shape
    tm = ROW_TILE
    if layer is None:
        w_spec = pl.BlockSpec((K, tn), lambda j, i: (0, j))
    else:
        w_spec = pl.BlockSpec((None, K, tn), lambda j, i: (layer, 0, j))
    return pl.pallas_call(
        _mm_ws_kernel,
        grid=(n_cols // tn, R // tm),
        in_specs=[pl.BlockSpec((tm, K), lambda j, i: (i, 0)), w_spec],
        out_specs=pl.BlockSpec((tm, tn), lambda j, i: (i, j)),
        out_shape=jax.ShapeDtypeStruct((R, n_cols), out_dtype),
        scratch_shapes=[pltpu.VMEM((K, tn), BF16)],
        compiler_params=_cparams(2),
        name="matmul_ws",
    )(x, w)


def _rope_rotate(t, c_ref, s1_ref, s2_ref):
    return (t * c_ref[...] + pltpu.roll(t, LANE - MLA_ROPE // 2, axis=1) * s1_ref[...]
            + pltpu.roll(t, MLA_ROPE // 2, axis=1) * s2_ref[...])


def _uq_kernel(x_ref, w_ref, gn_ref, gr_ref, c_ref, s1_ref, s2_ref, o_ref):
    acc = jnp.dot(x_ref[...], w_ref[...], preferred_element_type=F32)
    scale = (MLA_NOPE + MLA_ROPE) ** -0.5 * LOG2E
    for h in range(acc.shape[1] // MLA_QPAD):
        o = h * MLA_QPAD
        a = _rms(acc[:, o:o + MLA_NOPE], MLA_NOPE) * gn_ref[...]
        r = _rms(acc[:, o + MLA_NOPE:o + MLA_QPAD], MLA_ROPE) * gr_ref[...]
        r = _rope_rotate(r, c_ref, s1_ref, s2_ref)
        o_ref[:, o:o + MLA_NOPE] = (a * scale).astype(o_ref.dtype)
        o_ref[:, o + MLA_NOPE:o + MLA_QPAD] = (r * scale).astype(o_ref.dtype)


def _rope_block(i, tm, n_lat, seq):
    return jnp.where(i * tm < n_lat, i % (seq // tm), seq // tm)


def _uq_matmul(x, w, gn, gr, tabs, n_lat, seq):
    R, K = x.shape
    N = w.shape[1]
    tm, tn = ROW_TILE, 1024
    rb = functools.partial(_rope_block, tm=tm, n_lat=n_lat, seq=seq)
    tab_spec = pl.BlockSpec((tm, LANE), lambda i, j: (rb(i), 0))
    vec_spec = pl.BlockSpec((1, LANE), lambda i, j: (0, 0))
    return pl.pallas_call(
        _uq_kernel,
        grid=(R // tm, N // tn),
        in_specs=[pl.BlockSpec((tm, K), lambda i, j: (i, 0)),
                  pl.BlockSpec((K, tn), lambda i, j: (0, j)),
                  vec_spec, vec_spec, tab_spec, tab_spec, tab_spec],
        out_specs=pl.BlockSpec((tm, tn), lambda i, j: (i, j)),
        out_shape=jax.ShapeDtypeStruct((R, N), BF16),
        compiler_params=_cparams(2),
        name="uq_matmul",
    )(x, w, gn, gr, *tabs)


def _uk_kernel(x_ref, w_ref, gk_ref, kr_ref, o_ref):
    acc = jnp.dot(x_ref[...], w_ref[...], preferred_element_type=F32)
    for h in range(acc.shape[1] // MLA_NOPE):
        a = _rms(acc[:, h * MLA_NOPE:(h + 1) * MLA_NOPE], MLA_NOPE) * gk_ref[...]
        o_ref[:, h * MLA_QPAD:h * MLA_QPAD + MLA_NOPE] = a.astype(o_ref.dtype)
        o_ref[:, h * MLA_QPAD + MLA_NOPE:(h + 1) * MLA_QPAD] = kr_ref[...]


def _uk_matmul(x, w, gk, kr_pad):
    R, K = x.shape
    N = w.shape[1]
    tm, tn = ROW_TILE, 512
    return pl.pallas_call(
        _uk_kernel,
        grid=(R // tm, N // tn),
        in_specs=[pl.BlockSpec((tm, K), lambda i, j: (i, 0)),
                  pl.BlockSpec((K, tn), lambda i, j: (0, j)),
                  pl.BlockSpec((1, LANE), lambda i, j: (0, 0)),
                  pl.BlockSpec((tm, LANE), lambda i, j: (i, 0))],
        out_specs=pl.BlockSpec((tm, 2 * tn), lambda i, j: (i, j)),
        out_shape=jax.ShapeDtypeStruct((R, 2 * N), BF16),
        compiler_params=_cparams(2),
        name="uk_matmul",
    )(x, w, gk, kr_pad)


def _wo_kernel(y_ref, w_ref, x_ref, g_ref, o_ref, wb_ref):
    @pl.when(pl.program_id(1) == 0)
    def _():
        wb_ref[...] = w_ref[...].astype(BF16)

    acc = jnp.dot(y_ref[...], wb_ref[...], preferred_element_type=F32)
    o_ref[...] = x_ref[...] + g_ref[...] * acc


def _wo_matmul(y, w_o, layer, x, mod_l, rows, n_lat, seq):
    K = y.shape[1]
    N = w_o.shape[2]
    tm, tn = ROW_TILE, 512
    mrow = functools.partial(_mod_row, tm=tm, n_lat=n_lat, seq=seq)
    return pl.pallas_call(
        _wo_kernel,
        grid=(N // tn, rows // tm),
        in_specs=[pl.BlockSpec((tm, K), lambda j, i: (i, 0)),
                  pl.BlockSpec((None, K, tn), lambda j, i: (layer, 0, j)),
                  pl.BlockSpec((tm, tn), lambda j, i: (i, j)),
                  pl.BlockSpec((None, 1, tn), lambda j, i: (mrow(i) * 6 + 2, 0, j))],
        out_specs=pl.BlockSpec((tm, tn), lambda j, i: (i, j)),
        out_shape=jax.ShapeDtypeStruct((rows, N), F32),
        scratch_shapes=[pltpu.VMEM((K, tn), BF16)],
        compiler_params=_cparams(2),
        name="wo_matmul",
    )(y, w_o, x, mod_l)


def _post_in_kernel(nq_ref, nk_ref, cq_ref, tail_ref, gq_ref, gk_ref, gql_ref, gkvl_ref, gkr_ref,
                    c_ref, s1_ref, s2_ref, nq_o, nk_o, cq_o, ckv_o, kr_o):
    nh = nq_ref.shape[1] // HEAD_DIM
    scale = HEAD_DIM ** -0.5 * LOG2E
    for h in range(nh):
        sl = slice(h * HEAD_DIM, (h + 1) * HEAD_DIM)
        q = nq_ref[:, sl].astype(F32)
        nq_o[:, sl] = (_rms(q, HEAD_DIM) * gq_ref[...] * scale).astype(nq_o.dtype)
        k = nk_ref[:, sl].astype(F32)
        nk_o[:, sl] = (_rms(k, HEAD_DIM) * gk_ref[...]).astype(nk_o.dtype)
    cq = cq_ref[...].astype(F32)
    cq_o[...] = (_rms(cq, cq.shape[1]) * gql_ref[...]).astype(cq_o.dtype)
    kvr = ckv_o.shape[1]
    ckv = tail_ref[:, :kvr].astype(F32)
    ckv_o[...] = (_rms(ckv, kvr) * gkvl_ref[...]).astype(ckv_o.dtype)
    kr = tail_ref[:, kvr:kvr + LANE].astype(F32)
    kr = _rms(kr, MLA_ROPE) * gkr_ref[...]
    kr_o[...] = _rope_rotate(kr, c_ref, s1_ref, s2_ref).astype(kr_o.dtype)


def _post_in(z_main, z_tail, gq, gk, gql, gkvl, gkr, tabs, n_lat, seq):
    R = z_main.shape[0]
    W = z_main.shape[1] // 7
    kvr = z_tail.shape[1] - LANE
    tm = ROW_TILE
    rb = functools.partial(_rope_block, tm=tm, n_lat=n_lat, seq=seq)
    tab_spec = pl.BlockSpec((tm, LANE), lambda i: (rb(i), 0))

    def vec(n):
        return pl.BlockSpec((1, n), lambda i: (0, 0))

    def col(j):
        return pl.BlockSpec((tm, W), lambda i: (i, j))

    return pl.pallas_call(
        _post_in_kernel,
        grid=(R // tm,),
        in_specs=[col(3), col(4), col(6), pl.BlockSpec((tm, kvr + LANE), lambda i: (i, 0)),
                  vec(LANE), vec(LANE), vec(W), vec(kvr), vec(LANE), tab_spec, tab_spec, tab_spec],
        out_specs=[pl.BlockSpec((tm, W), lambda i: (i, 0)), pl.BlockSpec((tm, W), lambda i: (i, 0)),
                   pl.BlockSpec((tm, W), lambda i: (i, 0)), pl.BlockSpec((tm, kvr), lambda i: (i, 0)),
                   pl.BlockSpec((tm, LANE), lambda i: (i, 0))],
        out_shape=[jax.ShapeDtypeStruct((R, W), BF16), jax.ShapeDtypeStruct((R, W), BF16),
                   jax.ShapeDtypeStruct((R, W), BF16), jax.ShapeDtypeStruct((R, kvr), BF16),
                   jax.ShapeDtypeStruct((R, LANE), BF16)],
        compiler_params=_cparams(1),
        name="post_in",
    )(z_main, z_main, z_main, z_tail, gq, gk, gql, gkvl, gkr, *tabs)


def _conv_kernel(b_ref, c_ref, u_ref, cp_ref, up_ref, cn_ref, un_ref, w_ref, g_ref, o_ref,
                 *, n_lat, seq, ctx_len):
    tm, W = c_ref.shape
    row0 = pl.program_id(0) * tm
    lat = row0 < n_lat
    pos = jnp.where(lat, row0 % seq, (row0 - n_lat) % ctx_len)
    slen = jnp.where(lat, seq, ctx_len)
    has_prev = (pos > 0).astype(F32)
    has_next = (pos + tm < slen).astype(F32)
    v = c_ref[...].astype(F32) * u_ref[...].astype(F32)
    vprev = cp_ref[7:8, :].astype(F32) * up_ref[7:8, :].astype(F32) * has_prev
    vnext = cn_ref[0:1, :].astype(F32) * un_ref[0:1, :].astype(F32) * has_next
    rid = lax.broadcasted_iota(jnp.int32, (tm, W), 0)
    v_dn = jnp.where(rid == 0, vprev, pltpu.roll(v, 1, axis=0))
    v_up = jnp.where(rid == tm - 1, vnext, pltpu.roll(v, tm - 1, axis=0))
    y = w_ref[0:1, :] * v_dn + w_ref[1:2, :] * v + w_ref[2:3, :] * v_up
    y = b_ref[...].astype(F32) * y
    o_ref[...] = (_rms(y, W) * g_ref[...]).astype(o_ref.dtype)


def _conv_mixer(z_main, conv_w, g, rows, n_lat, seq, ctx_len):
    R = z_main.shape[0]
    W = z_main.shape[1] // 7
    tm = CONV_TILE
    nb8 = R // 8

    def cur(j):
        return pl.BlockSpec((tm, W), lambda i: (i, j))

    def prev(j):
        return pl.BlockSpec((8, W), lambda i: (jnp.maximum(i * (tm // 8) - 1, 0), j))

    def nxt(j):
        return pl.BlockSpec((8, W), lambda i: (jnp.minimum((i + 1) * (tm // 8), nb8 - 1), j))

    return pl.pallas_call(
        functools.partial(_conv_kernel, n_lat=n_lat, seq=seq, ctx_len=ctx_len),
        grid=(rows // tm,),
        in_specs=[cur(0), cur(1), cur(2), prev(1), prev(2), nxt(1), nxt(2),
                  pl.BlockSpec((3, W), lambda i: (0, 0)), pl.BlockSpec((1, W), lambda i: (0, 0))],
        out_specs=pl.BlockSpec((tm, W), lambda i: (i, 0)),
        out_shape=jax.ShapeDtypeStruct((rows, W), BF16),
        compiler_params=_cparams(1),
        name="conv_mixer",
    )(z_main, z_main, z_main, z_main, z_main, z_main, z_main, conv_w, g)


def _na_bias_table(rpb):
    qc = np.arange(GRID_W)[:, None]
    kc = np.arange(GRID_W)[None, :]
    cstart = np.clip(qc - NA_WIN_W // 2, 0, GRID_W - NA_WIN_W)
    ok = (kc >= cstart) & (kc < cstart + NA_WIN_W)
    dc = np.clip(kc - qc + NA_WIN_W - 1, 0, 2 * NA_WIN_W - 2)
    t = np.arange(NA_WIN_H)[:, None]
    w = np.arange(NA_WIN_H)[None, :]
    g = rpb[:, (t + w)[:, :, None, None], dc[None, None]]
    g = jnp.where(ok[None, None, None], g * LOG2E, NEG)
    H = rpb.shape[0]
    return g.transpose(0, 1, 3, 2, 4).reshape(H, NA_WIN_H, GRID_W, NA_WIN_H * GRID_W).astype(F32)


def _na_kernel(q_ref, k_ref, v_ref, kc_ref, vc_ref, bias_ref, o_ref):
    rows = q_ref.shape[0] // GRID_W
    nk = NA_WIN_H * GRID_W
    dn = (((1,), (1,)), ((), ()))
    kc = kc_ref[...]
    vc = vc_ref[...]

    def body(r, carry):
        r0 = jnp.clip(r - NA_WIN_H // 2, 0, rows - NA_WIN_H)
        q = q_ref[pl.ds(pl.multiple_of(r * GRID_W, GRID_W), GRID_W), :]
        koff = pl.multiple_of(r0 * GRID_W, GRID_W)
        k = k_ref[pl.ds(koff, nk), :]
        v = v_ref[pl.ds(koff, nk), :]
        s = lax.dot_general(q, k, dn, preferred_element_type=F32) + bias_ref[r0 - r + NA_WIN_H - 1]
        sc = lax.dot_general(q, kc, dn, preferred_element_type=F32)
        m = jnp.maximum(jnp.max(s, axis=-1, keepdims=True), jnp.max(sc, axis=-1, keepdims=True))
        p = jnp.exp2(s - m)
        pc = jnp.exp2(sc - m)
        l = jnp.sum(p, axis=-1, keepdims=True) + jnp.sum(pc, axis=-1, keepdims=True)
        o = (jnp.dot(p.astype(BF16), v, preferred_element_type=F32)
             + jnp.dot(pc.astype(BF16), vc, preferred_element_type=F32))
        o_ref[pl.ds(pl.multiple_of(r * GRID_W, GRID_W), GRID_W), :] = (o / l).astype(o_ref.dtype)
        return carry

    lax.fori_loop(0, rows, body, 0)


def _na_attention(nq, nk, z_main, bias_tab, B, seq, ctx_len, n_lat):
    R, W = nq.shape
    nh = W // HEAD_DIM
    cb = n_lat // ctx_len
    voff = 5 * nh
    return pl.pallas_call(
        _na_kernel,
        grid=(B, nh),
        in_specs=[pl.BlockSpec((seq, HEAD_DIM), lambda b, h: (b, h)),
                  pl.BlockSpec((seq, HEAD_DIM), lambda b, h: (b, h)),
                  pl.BlockSpec((seq, HEAD_DIM), lambda b, h: (b, voff + h)),
                  pl.BlockSpec((ctx_len, HEAD_DIM), lambda b, h: (cb + b, h)),
                  pl.BlockSpec((ctx_len, HEAD_DIM), lambda b, h: (cb + b, voff + h)),
                  pl.BlockSpec((None, NA_WIN_H, GRID_W, NA_WIN_H * GRID_W), lambda b, h: (h, 0, 0, 0))],
        out_specs=pl.BlockSpec((seq, HEAD_DIM), lambda b, h: (b, h)),
        out_shape=jax.ShapeDtypeStruct((R, W), BF16),
        compiler_params=_cparams(2),
        name="na_attention",
    )(nq, nk, z_main, nk, z_main, bias_tab)


def _flash_kernel(q_ref, k_ref, v_ref, kc_ref, vc_ref, o_ref, *, tk):
    tq = q_ref.shape[0]
    dv = v_ref.shape[1]
    dn = (((1,), (1,)), ((), ()))
    q = q_ref[...]

    def step(k, v, m, l, acc):
        s = lax.dot_general(q, k, dn, preferred_element_type=F32)
        m_new = jnp.maximum(m, jnp.max(s, axis=-1, keepdims=True))
        alpha = jnp.exp2(m - m_new)
        p = jnp.exp2(s - m_new)
        l = alpha * l + jnp.sum(p, axis=-1, keepdims=True)
        acc = alpha * acc + jnp.dot(p.astype(BF16), v, preferred_element_type=F32)
        return m_new, l, acc

    def body(j, carry):
        off = pl.multiple_of(j * tk, tk)
        return step(k_ref[pl.ds(off, tk), :], v_ref[pl.ds(off, tk), :], *carry)

    init = (jnp.full((tq, 1), NEG, F32), jnp.zeros((tq, 1), F32), jnp.zeros((tq, dv), F32))
    m, l, acc = lax.fori_loop(0, k_ref.shape[0] // tk, body, init, unroll=2)
    m, l, acc = step(kc_ref[...], vc_ref[...], m, l, acc)
    o_ref[...] = (acc / l).astype(o_ref.dtype)


def _mla_attention(qf, kf, vf, B, seq, ctx_len, n_lat):
    R = qf.shape[0]
    nh = qf.shape[1] // MLA_QPAD
    tq, tk = 512, 512
    cb = n_lat // ctx_len
    nqb = seq // tq
    return pl.pallas_call(
        functools.partial(_flash_kernel, tk=tk),
        grid=(B, nh, nqb),
        in_specs=[pl.BlockSpec((tq, MLA_QPAD), lambda b, h, i: (b * nqb + i, h)),
                  pl.BlockSpec((seq, MLA_QPAD), lambda b, h, i: (b, h)),
                  pl.BlockSpec((seq, MLA_V), lambda b, h, i: (b, h)),
                  pl.BlockSpec((ctx_len, MLA_QPAD), lambda b, h, i: (cb + b, h)),
                  pl.BlockSpec((ctx_len, MLA_V), lambda b, h, i: (cb + b, h))],
        out_specs=pl.BlockSpec((tq, MLA_V), lambda b, h, i: (b * nqb + i, h)),
        out_shape=jax.ShapeDtypeStruct((R, nh * MLA_V), BF16),
        compiler_params=_cparams(3),
        name="mla_attention",
    )(qf, kf, vf, kf, vf)


def _ctx_attn_kernel(q_ref, k_ref, v_ref, y_hbm, o_ref):
    del y_hbm
    s = lax.dot_general(q_ref[...], k_ref[...], (((1,), (1,)), ((), ())), preferred_element_type=F32)
    p = jnp.exp2(s - jnp.max(s, axis=-1, keepdims=True))
    o = jnp.dot(p.astype(BF16), v_ref[...], preferred_element_type=F32)
    o_ref[...] = (o / jnp.sum(p, axis=-1, keepdims=True)).astype(o_ref.dtype)


def _ctx_attention(q, k, v, y, dq, dv, qoff, koff, voff, B, ctx_len, n_lat):
    nh = y.shape[1] // dv
    cb = n_lat // ctx_len
    return pl.pallas_call(
        _ctx_attn_kernel,
        grid=(B, nh),
        in_specs=[pl.BlockSpec((ctx_len, dq), lambda b, h: (cb + b, qoff + h)),
                  pl.BlockSpec((ctx_len, dq), lambda b, h: (cb + b, koff + h)),
                  pl.BlockSpec((ctx_len, dv), lambda b, h: (cb + b, voff + h)),
                  pl.BlockSpec(memory_space=pl.ANY)],
        out_specs=pl.BlockSpec((ctx_len, dv), lambda b, h: (cb + b, h)),
        out_shape=jax.ShapeDtypeStruct(y.shape, y.dtype),
        input_output_aliases={3: 0},
        compiler_params=_cparams(2),
        name="ctx_attention",
    )(q, k, v, y)


def _merge_kernel(yc_ref, yn_ref, ym_ref, gn_ref, gm_ref, o_ref):
    cw = yc_ref.shape[1]
    nw = yn_ref.shape[1]
    o_ref[:, :cw] = yc_ref[...]
    yn = yn_ref[...].astype(F32)
    o_ref[:, cw:cw + nw] = (_rms(yn, nw) * gn_ref[...]).astype(o_ref.dtype)
    ym = ym_ref[...].astype(F32)
    o_ref[:, cw + nw:] = (_rms(ym, ym.shape[1]) * gm_ref[...]).astype(o_ref.dtype)


def _merge(y_conv, y_na, y_mla, gn, gm, rows):
    cw, nw, mw = y_conv.shape[1], y_na.shape[1], y_mla.shape[1]
    tm = ROW_TILE
    return pl.pallas_call(
        _merge_kernel,
        grid=(rows // tm,),
        in_specs=[pl.BlockSpec((tm, cw), lambda i: (i, 0)), pl.BlockSpec((tm, nw), lambda i: (i, 0)),
                  pl.BlockSpec((tm, mw), lambda i: (i, 0)),
                  pl.BlockSpec((1, nw), lambda i: (0, 0)), pl.BlockSpec((1, mw), lambda i: (0, 0))],
        out_specs=pl.BlockSpec((tm, cw + nw + mw), lambda i: (i, 0)),
        out_shape=jax.ShapeDtypeStruct((rows, cw + nw + mw), BF16),
        compiler_params=_cparams(1),
        name="merge_norm",
    )(y_conv, y_na, y_mla, gn, gm)


def _router_kernel(x_ref, g_ref, sc_ref, sh_ref, whi_ref, wlo_ref, rb_ref, tri_ref,
                   h_ref, idx_ref, wgt_ref, rank_ref, cnt_ref, base_ref):
    @pl.when(pl.program_id(0) == 0)
    def _():
        base_ref[...] = jnp.zeros(base_ref.shape, F32)

    x = x_ref[...]
    h = _rms(x, x.shape[-1]) * g_ref[...] * (1.0 + sc_ref[...]) + sh_ref[...]
    h_hi = h.astype(BF16)
    h_ref[...] = h_hi
    h_lo = (h - h_hi.astype(F32)).astype(BF16)
    dn = (((1,), (1,)), ((), ()))
    logits = (lax.dot_general(whi_ref[...], h_hi, dn, preferred_element_type=F32)
              + lax.dot_general(whi_ref[...], h_lo, dn, preferred_element_type=F32)
              + lax.dot_general(wlo_ref[...], h_hi, dn, preferred_element_type=F32))
    E, tm = logits.shape
    gs = E // N_GROUPS
    scores = jax.nn.sigmoid(logits)
    sel = scores + rb_ref[...]
    sel3 = sel.reshape(N_GROUPS, gs, tm)
    io3 = lax.broadcasted_iota(jnp.int32, sel3.shape, 1)
    m1 = jnp.max(sel3, axis=1, keepdims=True)
    i1 = jnp.min(jnp.where(sel3 == m1, io3, gs), axis=1, keepdims=True)
    m2 = jnp.max(jnp.where(io3 == i1, -jnp.inf, sel3), axis=1, keepdims=True)
    grp = m1 + m2
    iog = lax.broadcasted_iota(jnp.int32, grp.shape, 0)
    keep = jnp.zeros(grp.shape, jnp.bool_)
    for _ in range(TOPK_GROUPS):
        gm = jnp.max(grp, axis=0, keepdims=True)
        gi = jnp.min(jnp.where(grp == gm, iog, N_GROUPS), axis=0, keepdims=True)
        hit = iog == gi
        keep = jnp.logical_or(keep, hit)
        grp = jnp.where(hit, -jnp.inf, grp)
    selm = jnp.where(keep, sel3, -jnp.inf).reshape(E, tm)
    ioe = lax.broadcasted_iota(jnp.int32, (E, tm), 0)
    idxs, wgts, hits = [], [], []
    for _ in range(TOP_K):
        m = jnp.max(selm, axis=0, keepdims=True)
        ei = jnp.min(jnp.where(selm == m, ioe, E), axis=0, keepdims=True)
        hit = ioe == ei
        idxs.append(ei)
        hits.append(hit)
        wgts.append(jnp.sum(jnp.where(hit, scores, 0.0), axis=0, keepdims=True))
        selm = jnp.where(hit, -jnp.inf, selm)
    wsum = wgts[0]
    for w in wgts[1:]:
        wsum = wsum + w
    chosen = jnp.zeros((E, tm), F32)
    for hit in hits:
        chosen = chosen + jnp.where(hit, 1.0, 0.0)
    before = base_ref[:, 0:1] + jnp.dot(chosen.astype(BF16), tri_ref[...], preferred_element_type=F32)
    for k in range(TOP_K):
        idx_ref[k:k + 1, :] = idxs[k]
        wgt_ref[k:k + 1, :] = wgts[k] / wsum * ROUTE_SCALE
        rank_ref[k:k + 1, :] = jnp.sum(jnp.where(hits[k], before, 0.0), axis=0, keepdims=True).astype(jnp.int32)
    base_ref[...] = base_ref[...] + jnp.sum(chosen, axis=1, keepdims=True)
    cnt_ref[...] = base_ref[...]


def _router(x, g, mod_l, w_router, router_bias, rows, n_lat, seq):
    D = x.shape[1]
    E = w_router.shape[1]
    tm = ROW_TILE
    mrow = functools.partial(_mod_row, tm=tm, n_lat=n_lat, seq=seq)
    wt = w_router.T
    whi = wt.astype(BF16)
    wlo = (wt - whi.astype(F32)).astype(BF16)
    tri = (np.arange(tm)[:, None] < np.arange(tm)[None, :]).astype(np.float32)
    return pl.pallas_call(
        _router_kernel,
        grid=(rows // tm,),
        in_specs=[pl.BlockSpec((tm, D), lambda i: (i, 0)),
                  pl.BlockSpec((1, D), lambda i: (0, 0)),
                  pl.BlockSpec((None, 1, D), lambda i: (mrow(i) * 6 + 4, 0, 0)),
                  pl.BlockSpec((None, 1, D), lambda i: (mrow(i) * 6 + 3, 0, 0)),
                  pl.BlockSpec((E, D), lambda i: (0, 0)),
                  pl.BlockSpec((E, D), lambda i: (0, 0)),
                  pl.BlockSpec((E, 1), lambda i: (0, 0)),
                  pl.BlockSpec((tm, tm), lambda i: (0, 0))],
        out_specs=[pl.BlockSpec((tm, D), lambda i: (i, 0)),
                   pl.BlockSpec((TOP_K, tm), lambda i: (0, i)),
                   pl.BlockSpec((TOP_K, tm), lambda i: (0, i)),
                   pl.BlockSpec((TOP_K, tm), lambda i: (0, i)),
                   pl.BlockSpec((E, LANE), lambda i: (0, 0))],
        out_shape=[jax.ShapeDtypeStruct((rows, D), BF16),
                   jax.ShapeDtypeStruct((TOP_K, rows), jnp.int32),
                   jax.ShapeDtypeStruct((TOP_K, rows), F32),
                   jax.ShapeDtypeStruct((TOP_K, rows), jnp.int32),
                   jax.ShapeDtypeStruct((E, LANE), F32)],
        scratch_shapes=[pltpu.VMEM((E, LANE), F32)],
        compiler_params=_cparams(1),
        name="router",
    )(x, g.reshape(1, D), mod_l, mod_l, whi, wlo, router_bias.reshape(E, 1), jnp.asarray(tri, BF16))


def _expert_kernel(vt_ref, ve_ref, nv_ref, gs_ref, xs_ref, wg_ref, wu_ref, wd_ref, o_ref,
                   wgb_ref, wub_ref, wdb_ref):
    v = pl.program_id(0)
    T = xs_ref.shape[0]

    @pl.when(v < nv_ref[0])
    def _():
        e = ve_ref[v]
        tile = vt_ref[v]
        prev = jnp.maximum(v - 1, 0)
        new_expert = jnp.logical_or(v == 0, ve_ref[prev] != e)
        first_visit = jnp.logical_or(v == 0, vt_ref[prev] != tile)

        @pl.when(new_expert)
        def _():
            wgb_ref[...] = wg_ref[...].astype(BF16)
            wub_ref[...] = wu_ref[...].astype(BF16)
            wdb_ref[...] = wd_ref[...].astype(BF16)

        xs = xs_ref[...]
        g = jnp.dot(xs, wgb_ref[...], preferred_element_type=F32)
        u = jnp.dot(xs, wub_ref[...], preferred_element_type=F32)
        a = (g * jax.nn.sigmoid(g) * u).astype(BF16)
        y = jnp.dot(a, wdb_ref[...], preferred_element_type=F32).astype(o_ref.dtype)

        @pl.when(first_visit)
        def _():
            o_ref[...] = y

        @pl.when(jnp.logical_not(first_visit))
        def _():
            row = tile * T + lax.broadcasted_iota(jnp.int32, (T, 1), 0)
            mine = jnp.logical_and(row >= gs_ref[e], row < gs_ref[e + 1])
            o_ref[...] = jnp.where(mine, y, o_ref[...])


def _experts(xs, visit_tile, visit_expert, n_visits, group_start, w_gate, w_up, w_down, layer):
    P, D = xs.shape
    DE = w_gate.shape[3]
    T = MOE_TILE
    grid_spec = pltpu.PrefetchScalarGridSpec(
        num_scalar_prefetch=4,
        grid=(visit_tile.shape[0],),
        in_specs=[pl.BlockSpec((T, D), lambda v, vt, ve, nv, gs: (vt[v], 0)),
                  pl.BlockSpec((None, None, D, DE), lambda v, vt, ve, nv, gs: (layer, ve[v], 0, 0)),
                  pl.BlockSpec((None, None, D, DE), lambda v, vt, ve, nv, gs: (layer, ve[v], 0, 0)),
                  pl.BlockSpec((None, None, DE, D), lambda v, vt, ve, nv, gs: (layer, ve[v], 0, 0))],
        out_specs=pl.BlockSpec((T, D), lambda v, vt, ve, nv, gs: (vt[v], 0)),
        scratch_shapes=[pltpu.VMEM((D, DE), BF16), pltpu.VMEM((D, DE), BF16), pltpu.VMEM((DE, D), BF16)],
    )
    return pl.pallas_call(
        _expert_kernel,
        grid_spec=grid_spec,
        out_shape=jax.ShapeDtypeStruct((P, D), BF16),
        compiler_params=_cparams(1),
        name="experts",
    )(visit_tile, visit_expert, n_visits, group_start, xs, w_gate, w_up, w_down)


def _dispatch_plan(eidx, rank, counts):
    K, rows = eidx.shape
    E = counts.shape[0]
    T = MOE_TILE
    P0 = K * rows
    n_tiles = P0 // T
    gend = jnp.cumsum(counts)
    gstart = gend - counts
    eids = jnp.arange(E, dtype=jnp.int32)[:, None, None]
    dest = rank + jnp.sum(jnp.where(eidx[None] == eids, gstart[:, None, None], 0), axis=0)
    token = jnp.broadcast_to(jnp.arange(rows, dtype=jnp.int32)[None], (K, rows))
    _, row_token = lax.sort((dest.reshape(-1), token.reshape(-1)), num_keys=1)
    first_tile = gstart // T
    n_vis_e = jnp.where(counts > 0, (gend - 1) // T - first_tile + 1, 0)
    vend = jnp.cumsum(n_vis_e)
    voff = vend - n_vis_e
    n_visits = vend[-1]
    v = jnp.arange(n_tiles + E, dtype=jnp.int32)
    ve = jnp.minimum(jnp.sum((v[:, None] >= vend[None, :]).astype(jnp.int32), axis=1), E - 1)
    onehot = (ve[:, None] == jnp.arange(E, dtype=jnp.int32)[None, :]).astype(jnp.int32)
    vt = jnp.sum(onehot * (first_tile - voff)[None, :], axis=1) + v
    vt = jnp.where(v < n_visits, vt, n_tiles - 1).astype(jnp.int32)
    group_start = jnp.concatenate([gstart, gend[-1:]]).astype(jnp.int32)
    return dest, row_token, vt, ve.astype(jnp.int32), n_visits.reshape(1).astype(jnp.int32), group_start


def _shared_kernel(h_ref, wg_ref, wu_ref, wd_ref, y8_ref, gate_ref, x_ref, g2_ref, o_ref):
    h = h_ref[...]
    g = jnp.dot(h, wg_ref[...], preferred_element_type=F32)
    u = jnp.dot(h, wu_ref[...], preferred_element_type=F32)
    a = (g * jax.nn.sigmoid(g) * u).astype(BF16)
    y = jnp.dot(a, wd_ref[...], preferred_element_type=F32)
    for k in range(y8_ref.shape[0]):
        y = y + gate_ref[:, k:k + 1] * y8_ref[k].astype(F32)
    o_ref[...] = x_ref[...] + g2_ref[...] * y


def _shared_combine(h2, wsg, wsu, wsd, y8, gates, x, mod_l, rows, n_lat, seq):
    D = h2.shape[1]
    DE = wsg.shape[1]
    K = y8.shape[0]
    tm = COMBINE_TILE
    mrow = functools.partial(_mod_row, tm=tm, n_lat=n_lat, seq=seq)
    return pl.pallas_call(
        _shared_kernel,
        grid=(rows // tm,),
        in_specs=[pl.BlockSpec((tm, D), lambda i: (i, 0)),
                  pl.BlockSpec((D, DE), lambda i: (0, 0)),
                  pl.BlockSpec((D, DE), lambda i: (0, 0)),
                  pl.BlockSpec((DE, D), lambda i: (0, 0)),
                  pl.BlockSpec((K, tm, D), lambda i: (0, i, 0)),
                  pl.BlockSpec((tm, K), lambda i: (i, 0)),
                  pl.BlockSpec((tm, D), lambda i: (i, 0)),
                  pl.BlockSpec((None, 1, D), lambda i: (mrow(i) * 6 + 5, 0, 0))],
        out_specs=pl.BlockSpec((tm, D), lambda i: (i, 0)),
        out_shape=jax.ShapeDtypeStruct((rows, D), F32),
        compiler_params=_cparams(1),
        name="shared_combine",
    )(h2, wsg, wsu, wsd, y8, gates, x, mod_l)


def _rope_tables(seq, pad_rows):
    t = np.arange(seq)
    row = (t // GRID_W).astype(np.float32)
    col = (t % GRID_W).astype(np.float32)
    n_freq = MLA_ROPE // 4
    inv = (ROPE_BASE ** (-jnp.arange(n_freq, dtype=F32) / n_freq))
    ang = jnp.concatenate([jnp.asarray(row)[:, None] * inv, jnp.asarray(col)[:, None] * inv], axis=-1)
    cos, sin = jnp.cos(ang), jnp.sin(ang)
    half = MLA_ROPE // 2
    z = jnp.zeros((seq, LANE - MLA_ROPE), F32)
    zh = jnp.zeros((seq, half), F32)
    c = jnp.concatenate([cos, cos, z], axis=-1)
    s1 = jnp.concatenate([-sin, zh, z], axis=-1)
    s2 = jnp.concatenate([zh, sin, z], axis=-1)
    ident = jnp.concatenate([jnp.ones((pad_rows, MLA_ROPE), F32), jnp.zeros((pad_rows, LANE - MLA_ROPE), F32)], -1)
    zero = jnp.zeros((pad_rows, LANE), F32)
    return (jnp.concatenate([c, ident], 0), jnp.concatenate([s1, zero], 0), jnp.concatenate([s2, zero], 0))


def _pad_lanes(v, n):
    return jnp.concatenate([v, jnp.zeros((n - v.shape[0],), v.dtype)]).reshape(1, n)


def kernel(x, c, ctx, c_ctx, ada_w, ada_b, norm1_g, w_in, conv_w, na_q_norm, na_k_norm, na_rpb,
           mla_q_lat_norm, mla_kv_lat_norm, w_uq, w_ukv, mla_q_norm, mla_k_norm, out_norm_g, w_o,
           norm2_g, w_router, router_bias, w_gate, w_up, w_down, ws_gate, ws_up, ws_down):
    B, S, D = x.shape
    CTX = ctx.shape[1]
    L = ada_w.shape[0]
    n_lat = B * S
    R = n_lat + B * CTX
    W = D // 4
    kvr = w_ukv.shape[1]
    nh_mla = w_uq.shape[2] // (MLA_NOPE + MLA_ROPE)
    E = w_router.shape[2]
    assert B == 2 and S % ROW_TILE == 0 and (B * CTX) % ROW_TILE == 0 and CTX % CONV_TILE == 0
    assert S % GRID_W == 0 and S // GRID_W >= NA_WIN_H and W % HEAD_DIM == 0 and kvr % LANE == 0

    xs = jnp.concatenate([x.reshape(n_lat, D), ctx.reshape(B * CTX, D)], axis=0)
    cvecs = jnp.concatenate([c, c_ctx[None]], axis=0)
    mod = _modulation(cvecs, ada_w, ada_b)
    tabs = _rope_tables(S, ROW_TILE)

    for l in range(L):
        last = l == L - 1
        rows = n_lat if last else R
        mod_l = mod[l].reshape(8 * 6, 1, D)

        w_tail = jnp.concatenate([w_in[l, :, 7 * W:], jnp.zeros((D, LANE - MLA_ROPE), F32)], axis=1)
        wq = w_uq[l].reshape(W, nh_mla, MLA_NOPE + MLA_ROPE)
        wq = jnp.concatenate([wq, jnp.zeros((W, nh_mla, MLA_QPAD - MLA_NOPE - MLA_ROPE), F32)], axis=2)
        wq = wq.reshape(W, nh_mla * MLA_QPAD).astype(BF16)
        wkv = w_ukv[l].reshape(kvr, nh_mla, MLA_NOPE + MLA_V)
        wk = wkv[:, :, :MLA_NOPE].reshape(kvr, nh_mla * MLA_NOPE).astype(BF16)
        wv = wkv[:, :, MLA_NOPE:].reshape(kvr, nh_mla * MLA_V).astype(BF16)
        gq_n = mla_q_norm[l][:MLA_NOPE].reshape(1, LANE)
        gq_r = _pad_lanes(mla_q_norm[l][MLA_NOPE:], LANE)
        gk_n = mla_k_norm[l][:MLA_NOPE].reshape(1, LANE)
        gk_r = _pad_lanes(mla_k_norm[l][MLA_NOPE:], LANE)
        og = out_norm_g[l]

        h = _norm_mod(xs, norm1_g[l], mod_l, 1, 0, n_lat, S)
        z_main = _matmul_ws(h, w_in, l, 7 * W, min(512, W), BF16)
        z_tail = _matmul_ws(h, w_tail, None, w_tail.shape[1], w_tail.shape[1], BF16)
        nq, nk, cq, ckv, kr = _post_in(z_main, z_tail, na_q_norm[l].reshape(1, LANE),
                                       na_k_norm[l].reshape(1, LANE), mla_q_lat_norm[l].reshape(1, W),
                                       mla_kv_lat_norm[l].reshape(1, kvr), gk_r, tabs, n_lat, S)
        qf = _uq_matmul(cq, wq, gq_n, gq_r, tabs, n_lat, S)
        kf = _uk_matmul(ckv, wk, gk_n, kr)
        vf = _matmul(ckv, wv, BF16, 512)

        y_conv = _conv_mixer(z_main, conv_w[l], og[:W].reshape(1, W), rows, n_lat, S, CTX)
        y_na = _na_attention(nq, nk, z_main, _na_bias_table(na_rpb[l]), B, S, CTX, n_lat)
        y_mla = _mla_attention(qf, kf, vf, B, S, CTX, n_lat)
        if not last:
            nh_na = W // HEAD_DIM
            y_na = _ctx_attention(nq, nk, z_main, y_na, HEAD_DIM, HEAD_DIM, 0, 0, 5 * nh_na, B, CTX, n_lat)
            y_mla = _ctx_attention(qf, kf, vf, y_mla, MLA_QPAD, MLA_V, 0, 0, 0, B, CTX, n_lat)
        y = _merge(y_conv, y_na, y_mla, og[W:2 * W].reshape(1, W), og[2 * W:].reshape(1, D - 2 * W), rows)
        xs = _wo_matmul(y, w_o, l, xs, mod_l, rows, n_lat, S)

        h2, eidx, wgt, rank, cnt = _router(xs, norm2_g[l], mod_l, w_router[l], router_bias[l], rows, n_lat, S)
        dest, row_token, vis_tile, vis_exp, n_vis, gstart = _dispatch_plan(
            eidx, rank, cnt[:, 0].astype(jnp.int32))
        gathered = jnp.take(h2, row_token, axis=0)
        y_rows = _experts(gathered, vis_tile, vis_exp, n_vis, gstart, w_gate, w_up, w_down, l)
        y8 = jnp.take(y_rows, dest.reshape(-1), axis=0).reshape(TOP_K, rows, D)
        xs = _shared_combine(h2, ws_gate[l].astype(BF16), ws_up[l].astype(BF16), ws_down[l].astype(BF16),
                             y8, wgt.T, xs, mod_l, rows, n_lat, S)

    return xs[:n_lat].reshape(B, S, D)
```

```python
import functools

import numpy as np
import jax
import jax.numpy as jnp
from jax import lax
from jax.experimental import pallas as pl
from jax.experimental.pallas import tpu as pltpu

F32 = jnp.float32
BF16 = jnp.bfloat16

GRID_W = 64
HEAD_DIM = 128
MLA_NOPE = 128
MLA_ROPE = 64
MLA_V = 128
MLA_QPAD = 256
NA_WIN_H = 8
NA_WIN_W = 16
N_GROUPS = 8
TOPK_GROUPS = 4
TOP_K = 8
ROUTE_SCALE = 2.5
ROPE_BASE = 10000.0
EPS = 1e-6
NEG = -1e30
LOG2E = 1.4426950408889634

LANE = 128
ROW_TILE = 512
CONV_TILE = 256
MOE_TILE = 256
COMBINE_TILE = 128
VMEM_LIMIT = 56 * 1024 * 1024


def _cparams(n_axes):
    return pltpu.CompilerParams(dimension_semantics=("arbitrary",) * n_axes,
                                vmem_limit_bytes=VMEM_LIMIT)


def _rms(x, n):
    return x * lax.rsqrt(jnp.sum(x * x, axis=-1, keepdims=True) * (1.0 / n) + EPS)


def _mod_kernel(sb_ref, w_ref, b_ref, o_ref):
    tn = w_ref.shape[1]
    o_ref[...] = jnp.zeros(o_ref.shape, F32)
    for cblk in range(tn // LANE):
        wc = w_ref[:, cblk * LANE:(cblk + 1) * LANE]
        for r in range(3):
            acc = jnp.sum(wc * sb_ref[r], axis=0, keepdims=True)
            o_ref[r:r + 1, cblk * LANE:(cblk + 1) * LANE] = acc + b_ref[:, cblk * LANE:(cblk + 1) * LANE]


def _modulation(cvecs, ada_w, ada_b):
    L, D, D6 = ada_w.shape
    tn = 512
    s = cvecs * jax.nn.sigmoid(cvecs)
    sb = jnp.broadcast_to(s[:, :, None], (3, D, LANE))
    return pl.pallas_call(
        _mod_kernel,
        grid=(L, D6 // tn),
        in_specs=[pl.BlockSpec((3, D, LANE), lambda l, j: (0, 0, 0)),
                  pl.BlockSpec((None, D, tn), lambda l, j: (l, 0, j)),
                  pl.BlockSpec((None, 1, tn), lambda l, j: (l, 0, j))],
        out_specs=pl.BlockSpec((None, 8, tn), lambda l, j: (l, 0, j)),
        out_shape=jax.ShapeDtypeStruct((L, 8, D6), F32),
        compiler_params=_cparams(2),
        name="modulation",
    )(sb, ada_w, ada_b.reshape(L, 1, D6))


def _mod_row(i, tm, n_lat, seq):
    r0 = i * tm
    return jnp.where(r0 < n_lat, r0 // seq, n_lat // seq)


def _norm_mod_kernel(x_ref, g_ref, sc_ref, sh_ref, o_ref):
    x = x_ref[...]
    y = _rms(x, x.shape[-1]) * g_ref[...]
    o_ref[...] = (y * (1.0 + sc_ref[...]) + sh_ref[...]).astype(o_ref.dtype)


def _norm_mod(x, g, mod_l, which_sc, which_sh, n_lat, seq):
    R, D = x.shape
    tm = ROW_TILE
    mrow = functools.partial(_mod_row, tm=tm, n_lat=n_lat, seq=seq)
    return pl.pallas_call(
        _norm_mod_kernel,
        grid=(R // tm,),
        in_specs=[pl.BlockSpec((tm, D), lambda i: (i, 0)),
                  pl.BlockSpec((1, D), lambda i: (0, 0)),
                  pl.BlockSpec((None, 1, D), lambda i: (mrow(i) * 6 + which_sc, 0, 0)),
                  pl.BlockSpec((None, 1, D), lambda i: (mrow(i) * 6 + which_sh, 0, 0))],
        out_specs=pl.BlockSpec((tm, D), lambda i: (i, 0)),
        out_shape=jax.ShapeDtypeStruct((R, D), BF16),
        compiler_params=_cparams(1),
        name="norm_mod",
    )(x, g.reshape(1, D), mod_l, mod_l)


def _mm_kernel(x_ref, w_ref, o_ref):
    o_ref[...] = jnp.dot(x_ref[...], w_ref[...], preferred_element_type=F32).astype(o_ref.dtype)


def _matmul(x, w, out_dtype, tn, rows=None):
    R, K = x.shape
    rows = R if rows is None else rows
    N = w.shape[1]
    tm = ROW_TILE
    return pl.pallas_call(
        _mm_kernel,
        grid=(rows // tm, N // tn),
        in_specs=[pl.BlockSpec((tm, K), lambda i, j: (i, 0)),
                  pl.BlockSpec((K, tn), lambda i, j: (0, j))],
        out_specs=pl.BlockSpec((tm, tn), lambda i, j: (i, j)),
        out_shape=jax.ShapeDtypeStruct((rows, N), out_dtype),
        compiler_params=_cparams(2),
        name="matmul",
    )(x, w)


def _mm_ws_kernel(x_ref, w_ref, o_ref, wb_ref):
    @pl.when(pl.program_id(1) == 0)
    def _():
        wb_ref[...] = w_ref[...].astype(BF16)

    o_ref[...] = jnp.dot(x_ref[...], wb_ref[...], preferred_element_type=F32).astype(o_ref.dtype)


def _matmul_ws(x, w, layer, n_cols, tn, out_dtype):
    R, K = x.shape
    tm = ROW_TILE
    if layer is None:
        w_spec = pl.BlockSpec((K, tn), lambda j, i: (0, j))
    else:
        w_spec = pl.BlockSpec((None, K, tn), lambda j, i: (layer, 0, j))
    return pl.pallas_call(
        _mm_ws_kernel,
        grid=(n_cols // tn, R // tm),
        in_specs=[pl.BlockSpec((tm, K), lambda j, i: (i, 0)), w_spec],
        out_specs=pl.BlockSpec((tm, tn), lambda j, i: (i, j)),
        out_shape=jax.ShapeDtypeStruct((R, n_cols), out_dtype),
        scratch_shapes=[pltpu.VMEM((K, tn), BF16)],
        compiler_params=_cparams(2),
        name="matmul_ws",
    )(x, w)


def _rope_rotate(t, c_ref, s1_ref, s2_ref):
    return (t * c_ref[...] + pltpu.roll(t, LANE - MLA_ROPE // 2, axis=1) * s1_ref[...]
            + pltpu.roll(t, MLA_ROPE // 2, axis=1) * s2_ref[...])


def _uq_kernel(x_ref, w_ref, gn_ref, gr_ref, c_ref, s1_ref, s2_ref, o_ref):
    acc = jnp.dot(x_ref[...], w_ref[...], preferred_element_type=F32)
    scale = (MLA_NOPE + MLA_ROPE) ** -0.5 * LOG2E
    for h in range(acc.shape[1] // MLA_QPAD):
        o = h * MLA_QPAD
        a = _rms(acc[:, o:o + MLA_NOPE], MLA_NOPE) * gn_ref[...]
        r = _rms(acc[:, o + MLA_NOPE:o + MLA_QPAD], MLA_ROPE) * gr_ref[...]
        r = _rope_rotate(r, c_ref, s1_ref, s2_ref)
        o_ref[:, o:o + MLA_NOPE] = (a * scale).astype(o_ref.dtype)
        o_ref[:, o + MLA_NOPE:o + MLA_QPAD] = (r * scale).astype(o_ref.dtype)


def _rope_block(i, tm, n_lat, seq):
    return jnp.where(i * tm < n_lat, i % (seq // tm), seq // tm)


def _uq_matmul(x, w, gn, gr, tabs, n_lat, seq):
    R, K = x.shape
    N = w.shape[1]
    tm, tn = ROW_TILE, 1024
    rb = functools.partial(_rope_block, tm=tm, n_lat=n_lat, seq=seq)
    tab_spec = pl.BlockSpec((tm, LANE), lambda i, j: (rb(i), 0))
    vec_spec = pl.BlockSpec((1, LANE), lambda i, j: (0, 0))
    return pl.pallas_call(
        _uq_kernel,
        grid=(R // tm, N // tn),
        in_specs=[pl.BlockSpec((tm, K), lambda i, j: (i, 0)),
                  pl.BlockSpec((K, tn), lambda i, j: (0, j)),
                  vec_spec, vec_spec, tab_spec, tab_spec, tab_spec],
        out_specs=pl.BlockSpec((tm, tn), lambda i, j: (i, j)),
        out_shape=jax.ShapeDtypeStruct((R, N), BF16),
        compiler_params=_cparams(2),
        name="uq_matmul",
    )(x, w, gn, gr, *tabs)


def _uk_kernel(x_ref, w_ref, gk_ref, kr_ref, o_ref):
    acc = jnp.dot(x_ref[...], w_ref[...], preferred_element_type=F32)
    for h in range(acc.shape[1] // MLA_NOPE):
        a = _rms(acc[:, h * MLA_NOPE:(h + 1) * MLA_NOPE], MLA_NOPE) * gk_ref[...]
        o_ref[:, h * MLA_QPAD:h * MLA_QPAD + MLA_NOPE] = a.astype(o_ref.dtype)
        o_ref[:, h * MLA_QPAD + MLA_NOPE:(h + 1) * MLA_QPAD] = kr_ref[...]


def _uk_matmul(x, w, gk, kr_pad):
    R, K = x.shape
    N = w.shape[1]
    tm, tn = ROW_TILE, 512
    return pl.pallas_call(
        _uk_kernel,
        grid=(R // tm, N // tn),
        in_specs=[pl.BlockSpec((tm, K), lambda i, j: (i, 0)),
                  pl.BlockSpec((K, tn), lambda i, j: (0, j)),
                  pl.BlockSpec((1, LANE), lambda i, j: (0, 0)),
                  pl.BlockSpec((tm, LANE), lambda i, j: (i, 0))],
        out_specs=pl.BlockSpec((tm, 2 * tn), lambda i, j: (i, j)),
        out_shape=jax.ShapeDtypeStruct((R, 2 * N), BF16),
        compiler_params=_cparams(2),
        name="uk_matmul",
    )(x, w, gk, kr_pad)


def _wo_kernel(y_ref, w_ref, x_ref, g_ref, o_ref, wb_ref):
    @pl.when(pl.program_id(1) == 0)
    def _():
        wb_ref[...] = w_ref[...].astype(BF16)

    acc = jnp.dot(y_ref[...], wb_ref[...], preferred_element_type=F32)
    o_ref[...] = x_ref[...] + g_ref[...] * acc


def _wo_matmul(y, w_o, layer, x, mod_l, rows, n_lat, seq):
    K = y.shape[1]
    N = w_o.shape[2]
    tm, tn = ROW_TILE, 512
    mrow = functools.partial(_mod_row, tm=tm, n_lat=n_lat, seq=seq)
    return pl.pallas_call(
        _wo_kernel,
        grid=(N // tn, rows // tm),
        in_specs=[pl.BlockSpec((tm, K), lambda j, i: (i, 0)),
                  pl.BlockSpec((None, K, tn), lambda j, i: (layer, 0, j)),
                  pl.BlockSpec((tm, tn), lambda j, i: (i, j)),
                  pl.BlockSpec((None, 1, tn), lambda j, i: (mrow(i) * 6 + 2, 0, j))],
        out_specs=pl.BlockSpec((tm, tn), lambda j, i: (i, j)),
        out_shape=jax.ShapeDtypeStruct((rows, N), F32),
        scratch_shapes=[pltpu.VMEM((K, tn), BF16)],
        compiler_params=_cparams(2),
        name="wo_matmul",
    )(y, w_o, x, mod_l)


def _post_in_kernel(nq_ref, nk_ref, cq_ref, tail_ref, gq_ref, gk_ref, gql_ref, gkvl_ref, gkr_ref,
                    c_ref, s1_ref, s2_ref, nq_o, nk_o, cq_o, ckv_o, kr_o):
    nh = nq_ref.shape[1] // HEAD_DIM
    scale = HEAD_DIM ** -0.5 * LOG2E
    for h in range(nh):
        sl = slice(h * HEAD_DIM, (h + 1) * HEAD_DIM)
        q = nq_ref[:, sl].astype(F32)
        nq_o[:, sl] = (_rms(q, HEAD_DIM) * gq_ref[...] * scale).astype(nq_o.dtype)
        k = nk_ref[:, sl].astype(F32)
        nk_o[:, sl] = (_rms(k, HEAD_DIM) * gk_ref[...]).astype(nk_o.dtype)
    cq = cq_ref[...].astype(F32)
    cq_o[...] = (_rms(cq, cq.shape[1]) * gql_ref[...]).astype(cq_o.dtype)
    kvr = ckv_o.shape[1]
    ckv = tail_ref[:, :kvr].astype(F32)
    ckv_o[...] = (_rms(ckv, kvr) * gkvl_ref[...]).astype(ckv_o.dtype)
    kr = tail_ref[:, kvr:kvr + LANE].astype(F32)
    kr = _rms(kr, MLA_ROPE) * gkr_ref[...]
    kr_o[...] = _rope_rotate(kr, c_ref, s1_ref, s2_ref).astype(kr_o.dtype)


def _post_in(z_main, z_tail, gq, gk, gql, gkvl, gkr, tabs, n_lat, seq):
    R = z_main.shape[0]
    W = z_main.shape[1] // 7
    kvr = z_tail.shape[1] - LANE
    tm = ROW_TILE
    rb = functools.partial(_rope_block, tm=tm, n_lat=n_lat, seq=seq)
    tab_spec = pl.BlockSpec((tm, LANE), lambda i: (rb(i), 0))

    def vec(n):
        return pl.BlockSpec((1, n), lambda i: (0, 0))

    def col(j):
        return pl.BlockSpec((tm, W), lambda i: (i, j))

    return pl.pallas_call(
        _post_in_kernel,
        grid=(R // tm,),
        in_specs=[col(3), col(4), col(6), pl.BlockSpec((tm, kvr + LANE), lambda i: (i, 0)),
                  vec(LANE), vec(LANE), vec(W), vec(kvr), vec(LANE), tab_spec, tab_spec, tab_spec],
        out_specs=[pl.BlockSpec((tm, W), lambda i: (i, 0)), pl.BlockSpec((tm, W), lambda i: (i, 0)),
                   pl.BlockSpec((tm, W), lambda i: (i, 0)), pl.BlockSpec((tm, kvr), lambda i: (i, 0)),
                   pl.BlockSpec((tm, LANE), lambda i: (i, 0))],
        out_shape=[jax.ShapeDtypeStruct((R, W), BF16), jax.ShapeDtypeStruct((R, W), BF16),
                   jax.ShapeDtypeStruct((R, W), BF16), jax.ShapeDtypeStruct((R, kvr), BF16),
                   jax.ShapeDtypeStruct((R, LANE), BF16)],
        compiler_params=_cparams(1),
        name="post_in",
    )(z_main, z_main, z_main, z_tail, gq, gk, gql, gkvl, gkr, *tabs)


def _conv_kernel(b_ref, c_ref, u_ref, cp_ref, up_ref, cn_ref, un_ref, w_ref, g_ref, o_ref,
                 *, n_lat, seq, ctx_len):
    tm, W = c_ref.shape
    row0 = pl.program_id(0) * tm
    lat = row0 < n_lat
    pos = jnp.where(lat, row0 % seq, (row0 - n_lat) % ctx_len)
    slen = jnp.where(lat, seq, ctx_len)
    has_prev = (pos > 0).astype(F32)
    has_next = (pos + tm < slen).astype(F32)
    v = c_ref[...].astype(F32) * u_ref[...].astype(F32)
    vprev = cp_ref[7:8, :].astype(F32) * up_ref[7:8, :].astype(F32) * has_prev
    vnext = cn_ref[0:1, :].astype(F32) * un_ref[0:1, :].astype(F32) * has_next
    rid = lax.broadcasted_iota(jnp.int32, (tm, W), 0)
    v_dn = jnp.where(rid == 0, vprev, pltpu.roll(v, 1, axis=0))
    v_up = jnp.where(rid == tm - 1, vnext, pltpu.roll(v, tm - 1, axis=0))
    y = w_ref[0:1, :] * v_dn + w_ref[1:2, :] * v + w_ref[2:3, :] * v_up
    y = b_ref[...].astype(F32) * y
    o_ref[...] = (_rms(y, W) * g_ref[...]).astype(o_ref.dtype)


def _conv_mixer(z_main, conv_w, g, rows, n_lat, seq, ctx_len):
    R = z_main.shape[0]
    W = z_main.shape[1] // 7
    tm = CONV_TILE
    nb8 = R // 8

    def cur(j):
        return pl.BlockSpec((tm, W), lambda i: (i, j))

    def prev(j):
        return pl.BlockSpec((8, W), lambda i: (jnp.maximum(i * (tm // 8) - 1, 0), j))

    def nxt(j):
        return pl.BlockSpec((8, W), lambda i: (jnp.minimum((i + 1) * (tm // 8), nb8 - 1), j))

    return pl.pallas_call(
        functools.partial(_conv_kernel, n_lat=n_lat, seq=seq, ctx_len=ctx_len),
        grid=(rows // tm,),
        in_specs=[cur(0), cur(1), cur(2), prev(1), prev(2), nxt(1), nxt(2),
                  pl.BlockSpec((3, W), lambda i: (0, 0)), pl.BlockSpec((1, W), lambda i: (0, 0))],
        out_specs=pl.BlockSpec((tm, W), lambda i: (i, 0)),
        out_shape=jax.ShapeDtypeStruct((rows, W), BF16),
        compiler_params=_cparams(1),
        name="conv_mixer",
    )(z_main, z_main, z_main, z_main, z_main, z_main, z_main, conv_w, g)


def _na_bias_table(rpb):
    qc = np.arange(GRID_W)[:, None]
    kc = np.arange(GRID_W)[None, :]
    cstart = np.clip(qc - NA_WIN_W // 2, 0, GRID_W - NA_WIN_W)
    ok = (kc >= cstart) & (kc < cstart + NA_WIN_W)
    dc = np.clip(kc - qc + NA_WIN_W - 1, 0, 2 * NA_WIN_W - 2)
    H = rpb.shape[0]
    onehot = (dc[None] == np.arange(2 * NA_WIN_W - 1)[:, None, None]).astype(np.float32)
    cols = jnp.einsum("hdc,cqk->hdqk", rpb.astype(F32), onehot, precision=lax.Precision.HIGHEST)
    cols = jnp.where(ok[None, None], cols * LOG2E, NEG)
    g = jnp.stack([cols[:, t:t + NA_WIN_H] for t in range(NA_WIN_H)], axis=1)
    return g.transpose(0, 1, 3, 2, 4).reshape(H, NA_WIN_H, GRID_W, NA_WIN_H * GRID_W)


def _na_kernel(q_ref, k_ref, v_ref, kc_ref, vc_ref, bias_ref, o_ref):
    rows = q_ref.shape[0] // GRID_W
    nk = NA_WIN_H * GRID_W
    dn = (((1,), (1,)), ((), ()))
    kc = kc_ref[...]
    vc = vc_ref[...]

    def body(r, carry):
        r0 = jnp.clip(r - NA_WIN_H // 2, 0, rows - NA_WIN_H)
        q = q_ref[pl.ds(pl.multiple_of(r * GRID_W, GRID_W), GRID_W), :]
        koff = pl.multiple_of(r0 * GRID_W, GRID_W)
        k = k_ref[pl.ds(koff, nk), :]
        v = v_ref[pl.ds(koff, nk), :]
        s = lax.dot_general(q, k, dn, preferred_element_type=F32) + bias_ref[r0 - r + NA_WIN_H - 1]
        sc = lax.dot_general(q, kc, dn, preferred_element_type=F32)
        m = jnp.maximum(jnp.max(s, axis=-1, keepdims=True), jnp.max(sc, axis=-1, keepdims=True))
        p = jnp.exp2(s - m)
        pc = jnp.exp2(sc - m)
        l = jnp.sum(p, axis=-1, keepdims=True) + jnp.sum(pc, axis=-1, keepdims=True)
        o = (jnp.dot(p.astype(BF16), v, preferred_element_type=F32)
             + jnp.dot(pc.astype(BF16), vc, preferred_element_type=F32))
        o_ref[pl.ds(pl.multiple_of(r * GRID_W, GRID_W), GRID_W), :] = (o / l).astype(o_ref.dtype)
        return carry

    lax.fori_loop(0, rows, body, 0)


def _na_attention(nq, nk, z_main, bias_tab, B, seq, ctx_len, n_lat):
    R, W = nq.shape
    nh = W // HEAD_DIM
    cb = n_lat // ctx_len
    voff = 5 * nh
    return pl.pallas_call(
        _na_kernel,
        grid=(B, nh),
        in_specs=[pl.BlockSpec((seq, HEAD_DIM), lambda b, h: (b, h)),
                  pl.BlockSpec((seq, HEAD_DIM), lambda b, h: (b, h)),
                  pl.BlockSpec((seq, HEAD_DIM), lambda b, h: (b, voff + h)),
                  pl.BlockSpec((ctx_len, HEAD_DIM), lambda b, h: (cb + b, h)),
                  pl.BlockSpec((ctx_len, HEAD_DIM), lambda b, h: (cb + b, voff + h)),
                  pl.BlockSpec((None, NA_WIN_H, GRID_W, NA_WIN_H * GRID_W), lambda b, h: (h, 0, 0, 0))],
        out_specs=pl.BlockSpec((seq, HEAD_DIM), lambda b, h: (b, h)),
        out_shape=jax.ShapeDtypeStruct((n_lat, W), BF16),
        compiler_params=_cparams(2),
        name="na_attention",
    )(nq, nk, z_main, nk, z_main, bias_tab)


def _flash_kernel(q_ref, k_ref, v_ref, kc_ref, vc_ref, o_ref, *, tk):
    tq = q_ref.shape[0]
    dv = v_ref.shape[1]
    dn = (((1,), (1,)), ((), ()))
    q = q_ref[...]

    def step(k, v, m, l, acc):
        s = lax.dot_general(q, k, dn, preferred_element_type=F32)
        m_new = jnp.maximum(m, jnp.max(s, axis=-1, keepdims=True))
        alpha = jnp.exp2(m - m_new)
        p = jnp.exp2(s - m_new)
        l = alpha * l + jnp.sum(p, axis=-1, keepdims=True)
        acc = alpha * acc + jnp.dot(p.astype(BF16), v, preferred_element_type=F32)
        return m_new, l, acc

    def body(j, carry):
        off = pl.multiple_of(j * tk, tk)
        return step(k_ref[pl.ds(off, tk), :], v_ref[pl.ds(off, tk), :], *carry)

    init = (jnp.full((tq, 1), NEG, F32), jnp.zeros((tq, 1), F32), jnp.zeros((tq, dv), F32))
    n_chunks = k_ref.shape[0] // tk
    m, l, acc = lax.fori_loop(0, n_chunks, body, init, unroll=min(n_chunks, 8))
    m, l, acc = step(kc_ref[...], vc_ref[...], m, l, acc)
    o_ref[...] = (acc / l).astype(o_ref.dtype)


def _mla_attention(qf, kf, vf, B, seq, ctx_len, n_lat):
    R = qf.shape[0]
    nh = qf.shape[1] // MLA_QPAD
    tq, tk = 1024, 1024
    cb = n_lat // ctx_len
    nqb = seq // tq
    return pl.pallas_call(
        functools.partial(_flash_kernel, tk=tk),
        grid=(B, nh, nqb),
        in_specs=[pl.BlockSpec((tq, MLA_QPAD), lambda b, h, i: (b * nqb + i, h)),
                  pl.BlockSpec((seq, MLA_QPAD), lambda b, h, i: (b, h)),
                  pl.BlockSpec((seq, MLA_V), lambda b, h, i: (b, h)),
                  pl.BlockSpec((ctx_len, MLA_QPAD), lambda b, h, i: (cb + b, h)),
                  pl.BlockSpec((ctx_len, MLA_V), lambda b, h, i: (cb + b, h))],
        out_specs=pl.BlockSpec((tq, MLA_V), lambda b, h, i: (b * nqb + i, h)),
        out_shape=jax.ShapeDtypeStruct((n_lat, nh * MLA_V), BF16),
        compiler_params=_cparams(3),
        name="mla_attention",
    )(qf, kf, vf, kf, vf)


def _ctx_attn_kernel(q_ref, k_ref, v_ref, o_ref):
    s = lax.dot_general(q_ref[...], k_ref[...], (((1,), (1,)), ((), ())), preferred_element_type=F32)
    p = jnp.exp2(s - jnp.max(s, axis=-1, keepdims=True))
    o = jnp.dot(p.astype(BF16), v_ref[...], preferred_element_type=F32)
    o_ref[...] = (o / jnp.sum(p, axis=-1, keepdims=True)).astype(o_ref.dtype)


def _ctx_attention(q, k, v, nh, dq, dv, voff, B, ctx_len, n_lat):
    cb = n_lat // ctx_len
    return pl.pallas_call(
        _ctx_attn_kernel,
        grid=(B, nh),
        in_specs=[pl.BlockSpec((ctx_len, dq), lambda b, h: (cb + b, h)),
                  pl.BlockSpec((ctx_len, dq), lambda b, h: (cb + b, h)),
                  pl.BlockSpec((ctx_len, dv), lambda b, h: (cb + b, voff + h))],
        out_specs=pl.BlockSpec((ctx_len, dv), lambda b, h: (b, h)),
        out_shape=jax.ShapeDtypeStruct((B * ctx_len, nh * dv), BF16),
        compiler_params=_cparams(2),
        name="ctx_attention",
    )(q, k, v)


def _merge_kernel(yc_ref, yn_ref, ym_ref, ync_ref, ymc_ref, gn_ref, gm_ref, o_ref, *, lat_blocks):
    cw = yc_ref.shape[1]
    nw = yn_ref.shape[1]
    lat = pl.program_id(0) < lat_blocks
    o_ref[:, :cw] = yc_ref[...]
    yn = jnp.where(lat, yn_ref[...], ync_ref[...]).astype(F32)
    o_ref[:, cw:cw + nw] = (_rms(yn, nw) * gn_ref[...]).astype(o_ref.dtype)
    ym = jnp.where(lat, ym_ref[...], ymc_ref[...]).astype(F32)
    o_ref[:, cw + nw:] = (_rms(ym, ym.shape[1]) * gm_ref[...]).astype(o_ref.dtype)


def _merge(y_conv, y_na, y_mla, y_na_ctx, y_mla_ctx, gn, gm, rows):
    cw, nw, mw = y_conv.shape[1], y_na.shape[1], y_mla.shape[1]
    tm = ROW_TILE
    lat_blocks = y_na.shape[0] // tm
    ctx_last = y_na_ctx.shape[0] // tm - 1

    def lat_map(i):
        return (jnp.minimum(i, lat_blocks - 1), 0)

    def ctx_map(i):
        return (jnp.clip(i - lat_blocks, 0, ctx_last), 0)

    return pl.pallas_call(
        functools.partial(_merge_kernel, lat_blocks=lat_blocks),
        grid=(rows // tm,),
        in_specs=[pl.BlockSpec((tm, cw), lambda i: (i, 0)),
                  pl.BlockSpec((tm, nw), lat_map), pl.BlockSpec((tm, mw), lat_map),
                  pl.BlockSpec((tm, nw), ctx_map), pl.BlockSpec((tm, mw), ctx_map),
                  pl.BlockSpec((1, nw), lambda i: (0, 0)), pl.BlockSpec((1, mw), lambda i: (0, 0))],
        out_specs=pl.BlockSpec((tm, cw + nw + mw), lambda i: (i, 0)),
        out_shape=jax.ShapeDtypeStruct((rows, cw + nw + mw), BF16),
        compiler_params=_cparams(1),
        name="merge_norm",
    )(y_conv, y_na, y_mla, y_na_ctx, y_mla_ctx, gn, gm)


def _router_kernel(x_ref, g_ref, sc_ref, sh_ref, whi_ref, wlo_ref, rb_ref, tri_ref,
                   h_ref, idx_ref, wgt_ref, rank_ref, cnt_ref, base_ref):
    @pl.when(pl.program_id(0) == 0)
    def _():
        base_ref[...] = jnp.zeros(base_ref.shape, F32)

    x = x_ref[...]
    h = _rms(x, x.shape[-1]) * g_ref[...] * (1.0 + sc_ref[...]) + sh_ref[...]
    h_hi = h.astype(BF16)
    h_ref[...] = h_hi
    h_lo = (h - h_hi.astype(F32)).astype(BF16)
    dn = (((1,), (1,)), ((), ()))
    logits = (lax.dot_general(whi_ref[...], h_hi, dn, preferred_element_type=F32)
              + lax.dot_general(whi_ref[...], h_lo, dn, preferred_element_type=F32)
              + lax.dot_general(wlo_ref[...], h_hi, dn, preferred_element_type=F32))
    E, tm = logits.shape
    gs = E // N_GROUPS
    scores = jax.nn.sigmoid(logits)
    sel = scores + rb_ref[...]
    sel3 = sel.reshape(N_GROUPS, gs, tm)
    io3 = lax.broadcasted_iota(jnp.int32, sel3.shape, 1)
    m1 = jnp.max(sel3, axis=1, keepdims=True)
    i1 = jnp.min(jnp.where(sel3 == m1, io3, gs), axis=1, keepdims=True)
    m2 = jnp.max(jnp.where(io3 == i1, -jnp.inf, sel3), axis=1, keepdims=True)
    grp = m1 + m2
    iog = lax.broadcasted_iota(jnp.int32, grp.shape, 0)
    keep = jnp.zeros(grp.shape, jnp.bool_)
    for _ in range(TOPK_GROUPS):
        gm = jnp.max(grp, axis=0, keepdims=True)
        gi = jnp.min(jnp.where(grp == gm, iog, N_GROUPS), axis=0, keepdims=True)
        hit = iog == gi
        keep = jnp.logical_or(keep, hit)
        grp = jnp.where(hit, -jnp.inf, grp)
    selm = jnp.where(keep, sel3, -jnp.inf).reshape(E, tm)
    ioe = lax.broadcasted_iota(jnp.int32, (E, tm), 0)
    idxs, wgts, hits = [], [], []
    for _ in range(TOP_K):
        m = jnp.max(selm, axis=0, keepdims=True)
        ei = jnp.min(jnp.where(selm == m, ioe, E), axis=0, keepdims=True)
        hit = ioe == ei
        idxs.append(ei)
        hits.append(hit)
        wgts.append(jnp.sum(jnp.where(hit, scores, 0.0), axis=0, keepdims=True))
        selm = jnp.where(hit, -jnp.inf, selm)
    wsum = wgts[0]
    for w in wgts[1:]:
        wsum = wsum + w
    chosen = jnp.zeros((E, tm), F32)
    for hit in hits:
        chosen = chosen + jnp.where(hit, 1.0, 0.0)
    before = base_ref[:, 0:1] + jnp.dot(chosen.astype(BF16), tri_ref[...], preferred_element_type=F32)
    for k in range(TOP_K):
        idx_ref[k:k + 1, :] = idxs[k]
        wgt_ref[k:k + 1, :] = wgts[k] / wsum * ROUTE_SCALE
        rank_ref[k:k + 1, :] = jnp.sum(jnp.where(hits[k], before, 0.0), axis=0, keepdims=True).astype(jnp.int32)
    base_ref[...] = base_ref[...] + jnp.sum(chosen, axis=1, keepdims=True)
    cnt_ref[...] = base_ref[...]


def _router(x, g, mod_l, w_router, router_bias, rows, n_lat, seq):
    D = x.shape[1]
    E = w_router.shape[1]
    tm = ROW_TILE
    mrow = functools.partial(_mod_row, tm=tm, n_lat=n_lat, seq=seq)
    wt = w_router.T
    whi = wt.astype(BF16)
    wlo = (wt - whi.astype(F32)).astype(BF16)
    tri = (np.arange(tm)[:, None] < np.arange(tm)[None, :]).astype(np.float32)
    return pl.pallas_call(
        _router_kernel,
        grid=(rows // tm,),
        in_specs=[pl.BlockSpec((tm, D), lambda i: (i, 0)),
                  pl.BlockSpec((1, D), lambda i: (0, 0)),
                  pl.BlockSpec((None, 1, D), lambda i: (mrow(i) * 6 + 4, 0, 0)),
                  pl.BlockSpec((None, 1, D), lambda i: (mrow(i) * 6 + 3, 0, 0)),
                  pl.BlockSpec((E, D), lambda i: (0, 0)),
                  pl.BlockSpec((E, D), lambda i: (0, 0)),
                  pl.BlockSpec((E, 1), lambda i: (0, 0)),
                  pl.BlockSpec((tm, tm), lambda i: (0, 0))],
        out_specs=[pl.BlockSpec((tm, D), lambda i: (i, 0)),
                   pl.BlockSpec((TOP_K, tm), lambda i: (0, i)),
                   pl.BlockSpec((TOP_K, tm), lambda i: (0, i)),
                   pl.BlockSpec((TOP_K, tm), lambda i: (0, i)),
                   pl.BlockSpec((E, LANE), lambda i: (0, 0))],
        out_shape=[jax.ShapeDtypeStruct((rows, D), BF16),
                   jax.ShapeDtypeStruct((TOP_K, rows), jnp.int32),
                   jax.ShapeDtypeStruct((TOP_K, rows), F32),
                   jax.ShapeDtypeStruct((TOP_K, rows), jnp.int32),
                   jax.ShapeDtypeStruct((E, LANE), F32)],
        scratch_shapes=[pltpu.VMEM((E, LANE), F32)],
        compiler_params=_cparams(1),
        name="router",
    )(x, g.reshape(1, D), mod_l, mod_l, whi, wlo, router_bias.reshape(E, 1), jnp.asarray(tri, BF16))


def _expert_kernel(vt_ref, ve_ref, nv_ref, gs_ref, xs_ref, wg_ref, wu_ref, wd_ref, o_ref,
                   wgb_ref, wub_ref, wdb_ref):
    v = pl.program_id(0)
    T = xs_ref.shape[0]

    @pl.when(v < nv_ref[0])
    def _():
        e = ve_ref[v]
        tile = vt_ref[v]
        prev = jnp.maximum(v - 1, 0)
        new_expert = jnp.logical_or(v == 0, ve_ref[prev] != e)
        first_visit = jnp.logical_or(v == 0, vt_ref[prev] != tile)

        @pl.when(new_expert)
        def _():
            wgb_ref[...] = wg_ref[...].astype(BF16)
            wub_ref[...] = wu_ref[...].astype(BF16)
            wdb_ref[...] = wd_ref[...].astype(BF16)

        xs = xs_ref[...]
        g = jnp.dot(xs, wgb_ref[...], preferred_element_type=F32)
        u = jnp.dot(xs, wub_ref[...], preferred_element_type=F32)
        a = (g * jax.nn.sigmoid(g) * u).astype(BF16)
        y = jnp.dot(a, wdb_ref[...], preferred_element_type=F32).astype(o_ref.dtype)

        @pl.when(first_visit)
        def _():
            o_ref[...] = y

        @pl.when(jnp.logical_not(first_visit))
        def _():
            row = tile * T + lax.broadcasted_iota(jnp.int32, (T, 1), 0)
            mine = jnp.logical_and(row >= gs_ref[e], row < gs_ref[e + 1])
            o_ref[...] = jnp.where(mine, y, o_ref[...])


def _experts(xs, visit_tile, visit_expert, n_visits, group_start, w_gate, w_up, w_down, layer):
    P, D = xs.shape
    DE = w_gate.shape[3]
    T = MOE_TILE
    grid_spec = pltpu.PrefetchScalarGridSpec(
        num_scalar_prefetch=4,
        grid=(visit_tile.shape[0],),
        in_specs=[pl.BlockSpec((T, D), lambda v, vt, ve, nv, gs: (vt[v], 0)),
                  pl.BlockSpec((None, None, D, DE), lambda v, vt, ve, nv, gs: (layer, ve[v], 0, 0)),
                  pl.BlockSpec((None, None, D, DE), lambda v, vt, ve, nv, gs: (layer, ve[v], 0, 0)),
                  pl.BlockSpec((None, None, DE, D), lambda v, vt, ve, nv, gs: (layer, ve[v], 0, 0))],
        out_specs=pl.BlockSpec((T, D), lambda v, vt, ve, nv, gs: (vt[v], 0)),
        scratch_shapes=[pltpu.VMEM((D, DE), BF16), pltpu.VMEM((D, DE), BF16), pltpu.VMEM((DE, D), BF16)],
    )
    return pl.pallas_call(
        _expert_kernel,
        grid_spec=grid_spec,
        out_shape=jax.ShapeDtypeStruct((P, D), BF16),
        compiler_params=_cparams(1),
        name="experts",
    )(visit_tile, visit_expert, n_visits, group_start, xs, w_gate, w_up, w_down)


def _dispatch_plan(eidx, rank, counts):
    K, rows = eidx.shape
    E = counts.shape[0]
    T = MOE_TILE
    P0 = K * rows
    n_tiles = P0 // T
    gend = jnp.cumsum(counts)
    gstart = gend - counts
    eids = jnp.arange(E, dtype=jnp.int32)[:, None, None]
    dest = rank + jnp.sum(jnp.where(eidx[None] == eids, gstart[:, None, None], 0), axis=0)
    token = jnp.broadcast_to(jnp.arange(rows, dtype=jnp.int32)[None], (K, rows))
    _, row_token = lax.sort((dest.reshape(-1), token.reshape(-1)), num_keys=1)
    first_tile = gstart // T
    n_vis_e = jnp.where(counts > 0, (gend - 1) // T - first_tile + 1, 0)
    vend = jnp.cumsum(n_vis_e)
    voff = vend - n_vis_e
    n_visits = vend[-1]
    v = jnp.arange(n_tiles + E, dtype=jnp.int32)
    ve = jnp.minimum(jnp.sum((v[:, None] >= vend[None, :]).astype(jnp.int32), axis=1), E - 1)
    onehot = (ve[:, None] == jnp.arange(E, dtype=jnp.int32)[None, :]).astype(jnp.int32)
    vt = jnp.sum(onehot * (first_tile - voff)[None, :], axis=1) + v
    vt = jnp.where(v < n_visits, vt, n_tiles - 1).astype(jnp.int32)
    group_start = jnp.concatenate([gstart, gend[-1:]]).astype(jnp.int32)
    return dest, row_token, vt, ve.astype(jnp.int32), n_visits.reshape(1).astype(jnp.int32), group_start


def _shared_kernel(h_ref, wg_ref, wu_ref, wd_ref, y8_ref, gate_ref, x_ref, g2_ref, o_ref):
    h = h_ref[...]
    g = jnp.dot(h, wg_ref[...], preferred_element_type=F32)
    u = jnp.dot(h, wu_ref[...], preferred_element_type=F32)
    a = (g * jax.nn.sigmoid(g) * u).astype(BF16)
    y = jnp.dot(a, wd_ref[...], preferred_element_type=F32)
    for k in range(y8_ref.shape[0]):
        y = y + gate_ref[:, k:k + 1] * y8_ref[k].astype(F32)
    o_ref[...] = x_ref[...] + g2_ref[...] * y


def _shared_combine(h2, wsg, wsu, wsd, y8, gates, x, mod_l, rows, n_lat, seq):
    D = h2.shape[1]
    DE = wsg.shape[1]
    K = y8.shape[0]
    tm = COMBINE_TILE
    mrow = functools.partial(_mod_row, tm=tm, n_lat=n_lat, seq=seq)
    return pl.pallas_call(
        _shared_kernel,
        grid=(rows // tm,),
        in_specs=[pl.BlockSpec((tm, D), lambda i: (i, 0)),
                  pl.BlockSpec((D, DE), lambda i: (0, 0)),
                  pl.BlockSpec((D, DE), lambda i: (0, 0)),
                  pl.BlockSpec((DE, D), lambda i: (0, 0)),
                  pl.BlockSpec((K, tm, D), lambda i: (0, i, 0)),
                  pl.BlockSpec((tm, K), lambda i: (i, 0)),
                  pl.BlockSpec((tm, D), lambda i: (i, 0)),
                  pl.BlockSpec((None, 1, D), lambda i: (mrow(i) * 6 + 5, 0, 0))],
        out_specs=pl.BlockSpec((tm, D), lambda i: (i, 0)),
        out_shape=jax.ShapeDtypeStruct((rows, D), F32),
        compiler_params=_cparams(1),
        name="shared_combine",
    )(h2, wsg, wsu, wsd, y8, gates, x, mod_l)


def _rope_tables(seq, pad_rows):
    t = np.arange(seq)
    row = (t // GRID_W).astype(np.float32)
    col = (t % GRID_W).astype(np.float32)
    n_freq = MLA_ROPE // 4
    inv = (ROPE_BASE ** (-jnp.arange(n_freq, dtype=F32) / n_freq))
    ang = jnp.concatenate([jnp.asarray(row)[:, None] * inv, jnp.asarray(col)[:, None] * inv], axis=-1)
    cos, sin = jnp.cos(ang), jnp.sin(ang)
    half = MLA_ROPE // 2
    z = jnp.zeros((seq, LANE - MLA_ROPE), F32)
    zh = jnp.zeros((seq, half), F32)
    c = jnp.concatenate([cos, cos, z], axis=-1)
    s1 = jnp.concatenate([-sin, zh, z], axis=-1)
    s2 = jnp.concatenate([zh, sin, z], axis=-1)
    ident = jnp.concatenate([jnp.ones((pad_rows, MLA_ROPE), F32), jnp.zeros((pad_rows, LANE - MLA_ROPE), F32)], -1)
    zero = jnp.zeros((pad_rows, LANE), F32)
    return (jnp.concatenate([c, ident], 0), jnp.concatenate([s1, zero], 0), jnp.concatenate([s2, zero], 0))


def _pad_lanes(v, n):
    return jnp.concatenate([v, jnp.zeros((n - v.shape[0],), v.dtype)]).reshape(1, n)


def kernel(x, c, ctx, c_ctx, ada_w, ada_b, norm1_g, w_in, conv_w, na_q_norm, na_k_norm, na_rpb,
           mla_q_lat_norm, mla_kv_lat_norm, w_uq, w_ukv, mla_q_norm, mla_k_norm, out_norm_g, w_o,
           norm2_g, w_router, router_bias, w_gate, w_up, w_down, ws_gate, ws_up, ws_down):
    B, S, D = x.shape
    CTX = ctx.shape[1]
    L = ada_w.shape[0]
    n_lat = B * S
    R = n_lat + B * CTX
    W = D // 4
    kvr = w_ukv.shape[1]
    nh_mla = w_uq.shape[2] // (MLA_NOPE + MLA_ROPE)
    E = w_router.shape[2]
    assert B == 2 and S % ROW_TILE == 0 and (B * CTX) % ROW_TILE == 0 and CTX % CONV_TILE == 0
    assert S % GRID_W == 0 and S // GRID_W >= NA_WIN_H and W % HEAD_DIM == 0 and kvr % LANE == 0
    assert S % 1024 == 0 and (TOP_K * B * CTX) % MOE_TILE == 0 and ROW_TILE % COMBINE_TILE == 0

    xs = jnp.concatenate([x.reshape(n_lat, D), ctx.reshape(B * CTX, D)], axis=0)
    cvecs = jnp.concatenate([c, c_ctx[None]], axis=0)
    mod = _modulation(cvecs, ada_w, ada_b)
    tabs = _rope_tables(S, ROW_TILE)

    for l in range(L):
        last = l == L - 1
        rows = n_lat if last else R
        mod_l = mod[l].reshape(8 * 6, 1, D)

        w_tail = jnp.concatenate([w_in[l, :, 7 * W:], jnp.zeros((D, LANE - MLA_ROPE), F32)], axis=1)
        wq = w_uq[l].reshape(W, nh_mla, MLA_NOPE + MLA_ROPE)
        wq = jnp.concatenate([wq, jnp.zeros((W, nh_mla, MLA_QPAD - MLA_NOPE - MLA_ROPE), F32)], axis=2)
        wq = wq.reshape(W, nh_mla * MLA_QPAD).astype(BF16)
        wkv = w_ukv[l].reshape(kvr, nh_mla, MLA_NOPE + MLA_V)
        wk = wkv[:, :, :MLA_NOPE].reshape(kvr, nh_mla * MLA_NOPE).astype(BF16)
        wv = wkv[:, :, MLA_NOPE:].reshape(kvr, nh_mla * MLA_V).astype(BF16)
        gq_n = mla_q_norm[l][:MLA_NOPE].reshape(1, LANE)
        gq_r = _pad_lanes(mla_q_norm[l][MLA_NOPE:], LANE)
        gk_n = mla_k_norm[l][:MLA_NOPE].reshape(1, LANE)
        gk_r = _pad_lanes(mla_k_norm[l][MLA_NOPE:], LANE)
        og = out_norm_g[l]

        h = _norm_mod(xs, norm1_g[l], mod_l, 1, 0, n_lat, S)
        z_main = _matmul_ws(h, w_in, l, 7 * W, min(512, W), BF16)
        z_tail = _matmul_ws(h, w_tail, None, w_tail.shape[1], w_tail.shape[1], BF16)
        nq, nk, cq, ckv, kr = _post_in(z_main, z_tail, na_q_norm[l].reshape(1, LANE),
                                       na_k_norm[l].reshape(1, LANE), mla_q_lat_norm[l].reshape(1, W),
                                       mla_kv_lat_norm[l].reshape(1, kvr), gk_r, tabs, n_lat, S)
        qf = _uq_matmul(cq, wq, gq_n, gq_r, tabs, n_lat, S)
        kf = _uk_matmul(ckv, wk, gk_n, kr)
        vf = _matmul(ckv, wv, BF16, 512)

        y_conv = _conv_mixer(z_main, conv_w[l], og[:W].reshape(1, W), rows, n_lat, S, CTX)
        y_na = _na_attention(nq, nk, z_main, _na_bias_table(na_rpb[l]), B, S, CTX, n_lat)
        y_mla = _mla_attention(qf, kf, vf, B, S, CTX, n_lat)
        if last:
            y_na_c, y_mla_c = y_na, y_mla
        else:
            nh_na = W // HEAD_DIM
            y_na_c = _ctx_attention(nq, nk, z_main, nh_na, HEAD_DIM, HEAD_DIM, 5 * nh_na, B, CTX, n_lat)
            y_mla_c = _ctx_attention(qf, kf, vf, nh_mla, MLA_QPAD, MLA_V, 0, B, CTX, n_lat)
        y = _merge(y_conv, y_na, y_mla, y_na_c, y_mla_c,
                   og[W:2 * W].reshape(1, W), og[2 * W:].reshape(1, D - 2 * W), rows)
        xs = _wo_matmul(y, w_o, l, xs, mod_l, rows, n_lat, S)

        h2, eidx, wgt, rank, cnt = _router(xs, norm2_g[l], mod_l, w_router[l], router_bias[l], rows, n_lat, S)
        dest, row_token, vis_tile, vis_exp, n_vis, gstart = _dispatch_plan(
            eidx, rank, cnt[:, 0].astype(jnp.int32))
        gathered = h2.at[row_token].get(mode="promise_in_bounds")
        y_rows = _experts(gathered, vis_tile, vis_exp, n_vis, gstart, w_gate, w_up, w_down, l)
        y8 = y_rows.at[dest.reshape(-1)].get(mode="promise_in_bounds").reshape(TOP_K, rows, D)
        xs = _shared_combine(h2, ws_gate[l].astype(BF16), ws_up[l].astype(BF16), ws_down[l].astype(BF16),
                             y8, wgt.T, xs, mod_l, rows, n_lat, S)

    return xs[:n_lat].reshape(B, S, D)
```

```python
import functools

import numpy as np
import jax
import jax.numpy as jnp
from jax import lax
from jax.experimental import pallas as pl
from jax.experimental.pallas import tpu as pltpu

F32 = jnp.float32
BF16 = jnp.bfloat16

GRID_W = 64
HEAD_DIM = 128
MLA_NOPE = 128
MLA_ROPE = 64
MLA_V = 128
MLA_QPAD = 256
NA_WIN_H = 8
NA_WIN_W = 16
NA_GROUP = 8
NA_KROWS = NA_GROUP + NA_WIN_H
N_GROUPS = 8
TOPK_GROUPS = 4
TOP_K = 8
ROUTE_SCALE = 2.5
ROPE_BASE = 10000.0
EPS = 1e-6
NEG = -1e30
LOG2E = 1.4426950408889634

LANE = 128
ROW_TILE = 512
CONV_TILE = 256
MOE_TILE = 256
COMBINE_TILE = 128
MOE_PARTS = 2
VMEM_LIMIT = 56 * 1024 * 1024


def _cparams(n_axes):
    return pltpu.CompilerParams(dimension_semantics=("arbitrary",) * n_axes,
                                vmem_limit_bytes=VMEM_LIMIT)


def _rms(x, n):
    return x * lax.rsqrt(jnp.sum(x * x, axis=-1, keepdims=True) * (1.0 / n) + EPS)


def _mod_kernel(sb_ref, w_ref, b_ref, o_ref):
    tn = w_ref.shape[1]
    o_ref[...] = jnp.zeros(o_ref.shape, F32)
    for cblk in range(tn // LANE):
        wc = w_ref[:, cblk * LANE:(cblk + 1) * LANE]
        for r in range(3):
            acc = jnp.sum(wc * sb_ref[r], axis=0, keepdims=True)
            o_ref[r:r + 1, cblk * LANE:(cblk + 1) * LANE] = acc + b_ref[:, cblk * LANE:(cblk + 1) * LANE]


def _modulation(cvecs, ada_w, ada_b):
    L, D, D6 = ada_w.shape
    tn = 512
    s = cvecs * jax.nn.sigmoid(cvecs)
    sb = jnp.broadcast_to(s[:, :, None], (3, D, LANE))
    return pl.pallas_call(
        _mod_kernel,
        grid=(L, D6 // tn),
        in_specs=[pl.BlockSpec((3, D, LANE), lambda l, j: (0, 0, 0)),
                  pl.BlockSpec((None, D, tn), lambda l, j: (l, 0, j)),
                  pl.BlockSpec((None, 1, tn), lambda l, j: (l, 0, j))],
        out_specs=pl.BlockSpec((None, 8, tn), lambda l, j: (l, 0, j)),
        out_shape=jax.ShapeDtypeStruct((L, 8, D6), F32),
        compiler_params=_cparams(2),
        name="modulation",
    )(sb, ada_w, ada_b.reshape(L, 1, D6))


def _mod_row(i, tm, n_lat, seq):
    r0 = i * tm
    return jnp.where(r0 < n_lat, r0 // seq, n_lat // seq)


def _norm_mod_kernel(x_ref, g_ref, sc_ref, sh_ref, o_ref):
    x = x_ref[...]
    y = _rms(x, x.shape[-1]) * g_ref[...]
    o_ref[...] = (y * (1.0 + sc_ref[...]) + sh_ref[...]).astype(o_ref.dtype)


def _norm_mod(x, g, mod_l, which_sc, which_sh, n_lat, seq):
    R, D = x.shape
    tm = ROW_TILE
    mrow = functools.partial(_mod_row, tm=tm, n_lat=n_lat, seq=seq)
    return pl.pallas_call(
        _norm_mod_kernel,
        grid=(R // tm,),
        in_specs=[pl.BlockSpec((tm, D), lambda i: (i, 0)),
                  pl.BlockSpec((1, D), lambda i: (0, 0)),
                  pl.BlockSpec((None, 1, D), lambda i: (mrow(i) * 6 + which_sc, 0, 0)),
                  pl.BlockSpec((None, 1, D), lambda i: (mrow(i) * 6 + which_sh, 0, 0))],
        out_specs=pl.BlockSpec((tm, D), lambda i: (i, 0)),
        out_shape=jax.ShapeDtypeStruct((R, D), BF16),
        compiler_params=_cparams(1),
        name="norm_mod",
    )(x, g.reshape(1, D), mod_l, mod_l)


def _mm_kernel(x_ref, w_ref, o_ref):
    o_ref[...] = jnp.dot(x_ref[...], w_ref[...], preferred_element_type=F32).astype(o_ref.dtype)


def _matmul(x, w, out_dtype, tn, rows=None):
    R, K = x.shape
    rows = R if rows is None else rows
    N = w.shape[1]
    tm = ROW_TILE
    return pl.pallas_call(
        _mm_kernel,
        grid=(rows // tm, N // tn),
        in_specs=[pl.BlockSpec((tm, K), lambda i, j: (i, 0)),
                  pl.BlockSpec((K, tn), lambda i, j: (0, j))],
        out_specs=pl.BlockSpec((tm, tn), lambda i, j: (i, j)),
        out_shape=jax.ShapeDtypeStruct((rows, N), out_dtype),
        compiler_params=_cparams(2),
        name="matmul",
    )(x, w)


def _mm_ws_kernel(x_ref, w_ref, o_ref, wb_ref):
    @pl.when(pl.program_id(1) == 0)
    def _():
        wb_ref[...] = w_ref[...].astype(BF16)

    o_ref[...] = jnp.dot(x_ref[...], wb_ref[...], preferred_element_type=F32).astype(o_ref.dtype)


def _matmul_ws(x, w, layer, n_cols, tn, out_dtype):
    R, K = x.shape
    tm = ROW_TILE
    if layer is None:
        w_spec = pl.BlockSpec((K, tn), lambda j, i: (0, j))
    else:
        w_spec = pl.BlockSpec((None, K, tn), lambda j, i: (layer, 0, j))
    return pl.pallas_call(
        _mm_ws_kernel,
        grid=(n_cols // tn, R // tm),
        in_specs=[pl.BlockSpec((tm, K), lambda j, i: (i, 0)), w_spec],
        out_specs=pl.BlockSpec((tm, tn), lambda j, i: (i, j)),
        out_shape=jax.ShapeDtypeStruct((R, n_cols), out_dtype),
        scratch_shapes=[pltpu.VMEM((K, tn), BF16)],
        compiler_params=_cparams(2),
        name="matmul_ws",
    )(x, w)


def _rope_rotate(t, c_ref, s1_ref, s2_ref):
    return (t * c_ref[...] + pltpu.roll(t, LANE - MLA_ROPE // 2, axis=1) * s1_ref[...]
            + pltpu.roll(t, MLA_ROPE // 2, axis=1) * s2_ref[...])


def _uq_kernel(x_ref, w_ref, gn_ref, gr_ref, c_ref, s1_ref, s2_ref, o_ref):
    acc = jnp.dot(x_ref[...], w_ref[...], preferred_element_type=F32)
    scale = (MLA_NOPE + MLA_ROPE) ** -0.5 * LOG2E
    for h in range(acc.shape[1] // MLA_QPAD):
        o = h * MLA_QPAD
        a = _rms(acc[:, o:o + MLA_NOPE], MLA_NOPE) * gn_ref[...]
        r = _rms(acc[:, o + MLA_NOPE:o + MLA_QPAD], MLA_ROPE) * gr_ref[...]
        r = _rope_rotate(r, c_ref, s1_ref, s2_ref)
        o_ref[:, o:o + MLA_NOPE] = (a * scale).astype(o_ref.dtype)
        o_ref[:, o + MLA_NOPE:o + MLA_QPAD] = (r * scale).astype(o_ref.dtype)


def _rope_block(i, tm, n_lat, seq):
    return jnp.where(i * tm < n_lat, i % (seq // tm), seq // tm)


def _uq_matmul(x, w, gn, gr, tabs, n_lat, seq):
    R, K = x.shape
    N = w.shape[1]
    tm, tn = ROW_TILE, 1024
    rb = functools.partial(_rope_block, tm=tm, n_lat=n_lat, seq=seq)
    tab_spec = pl.BlockSpec((tm, LANE), lambda i, j: (rb(i), 0))
    vec_spec = pl.BlockSpec((1, LANE), lambda i, j: (0, 0))
    return pl.pallas_call(
        _uq_kernel,
        grid=(R // tm, N // tn),
        in_specs=[pl.BlockSpec((tm, K), lambda i, j: (i, 0)),
                  pl.BlockSpec((K, tn), lambda i, j: (0, j)),
                  vec_spec, vec_spec, tab_spec, tab_spec, tab_spec],
        out_specs=pl.BlockSpec((tm, tn), lambda i, j: (i, j)),
        out_shape=jax.ShapeDtypeStruct((R, N), BF16),
        compiler_params=_cparams(2),
        name="uq_matmul",
    )(x, w, gn, gr, *tabs)


def _uk_kernel(x_ref, w_ref, gk_ref, kr_ref, o_ref):
    acc = jnp.dot(x_ref[...], w_ref[...], preferred_element_type=F32)
    for h in range(acc.shape[1] // MLA_NOPE):
        a = _rms(acc[:, h * MLA_NOPE:(h + 1) * MLA_NOPE], MLA_NOPE) * gk_ref[...]
        o_ref[:, h * MLA_QPAD:h * MLA_QPAD + MLA_NOPE] = a.astype(o_ref.dtype)
        o_ref[:, h * MLA_QPAD + MLA_NOPE:(h + 1) * MLA_QPAD] = kr_ref[...]


def _uk_matmul(x, w, gk, kr_pad):
    R, K = x.shape
    N = w.shape[1]
    tm, tn = ROW_TILE, 512
    return pl.pallas_call(
        _uk_kernel,
        grid=(R // tm, N // tn),
        in_specs=[pl.BlockSpec((tm, K), lambda i, j: (i, 0)),
                  pl.BlockSpec((K, tn), lambda i, j: (0, j)),
                  pl.BlockSpec((1, LANE), lambda i, j: (0, 0)),
                  pl.BlockSpec((tm, LANE), lambda i, j: (i, 0))],
        out_specs=pl.BlockSpec((tm, 2 * tn), lambda i, j: (i, j)),
        out_shape=jax.ShapeDtypeStruct((R, 2 * N), BF16),
        compiler_params=_cparams(2),
        name="uk_matmul",
    )(x, w, gk, kr_pad)


def _wo_kernel(y_ref, w_ref, x_ref, g_ref, o_ref, wb_ref):
    @pl.when(pl.program_id(1) == 0)
    def _():
        wb_ref[...] = w_ref[...].astype(BF16)

    acc = jnp.dot(y_ref[...], wb_ref[...], preferred_element_type=F32)
    o_ref[...] = x_ref[...] + g_ref[...] * acc


def _wo_matmul(y, w_o, layer, x, mod_l, rows, n_lat, seq):
    K = y.shape[1]
    N = w_o.shape[2]
    tm, tn = ROW_TILE, 512
    mrow = functools.partial(_mod_row, tm=tm, n_lat=n_lat, seq=seq)
    return pl.pallas_call(
        _wo_kernel,
        grid=(N // tn, rows // tm),
        in_specs=[pl.BlockSpec((tm, K), lambda j, i: (i, 0)),
                  pl.BlockSpec((None, K, tn), lambda j, i: (layer, 0, j)),
                  pl.BlockSpec((tm, tn), lambda j, i: (i, j)),
                  pl.BlockSpec((None, 1, tn), lambda j, i: (mrow(i) * 6 + 2, 0, j))],
        out_specs=pl.BlockSpec((tm, tn), lambda j, i: (i, j)),
        out_shape=jax.ShapeDtypeStruct((rows, N), F32),
        scratch_shapes=[pltpu.VMEM((K, tn), BF16)],
        compiler_params=_cparams(2),
        name="wo_matmul",
    )(y, w_o, x, mod_l)


def _post_in_kernel(nq_ref, nk_ref, cq_ref, tail_ref, gq_ref, gk_ref, gql_ref, gkvl_ref, gkr_ref,
                    c_ref, s1_ref, s2_ref, nq_o, nk_o, cq_o, ckv_o, kr_o):
    nh = nq_ref.shape[1] // HEAD_DIM
    scale = HEAD_DIM ** -0.5 * LOG2E
    for h in range(nh):
        sl = slice(h * HEAD_DIM, (h + 1) * HEAD_DIM)
        q = nq_ref[:, sl].astype(F32)
        nq_o[:, sl] = (_rms(q, HEAD_DIM) * gq_ref[...] * scale).astype(nq_o.dtype)
        k = nk_ref[:, sl].astype(F32)
        nk_o[:, sl] = (_rms(k, HEAD_DIM) * gk_ref[...]).astype(nk_o.dtype)
    cq = cq_ref[...].astype(F32)
    cq_o[...] = (_rms(cq, cq.shape[1]) * gql_ref[...]).astype(cq_o.dtype)
    kvr = ckv_o.shape[1]
    ckv = tail_ref[:, :kvr].astype(F32)
    ckv_o[...] = (_rms(ckv, kvr) * gkvl_ref[...]).astype(ckv_o.dtype)
    kr = tail_ref[:, kvr:kvr + LANE].astype(F32)
    kr = _rms(kr, MLA_ROPE) * gkr_ref[...]
    kr_o[...] = _rope_rotate(kr, c_ref, s1_ref, s2_ref).astype(kr_o.dtype)


def _post_in(z_main, z_tail, gq, gk, gql, gkvl, gkr, tabs, n_lat, seq):
    R = z_main.shape[0]
    W = z_main.shape[1] // 7
    kvr = z_tail.shape[1] - LANE
    tm = ROW_TILE
    rb = functools.partial(_rope_block, tm=tm, n_lat=n_lat, seq=seq)
    tab_spec = pl.BlockSpec((tm, LANE), lambda i: (rb(i), 0))

    def vec(n):
        return pl.BlockSpec((1, n), lambda i: (0, 0))

    def col(j):
        return pl.BlockSpec((tm, W), lambda i: (i, j))

    return pl.pallas_call(
        _post_in_kernel,
        grid=(R // tm,),
        in_specs=[col(3), col(4), col(6), pl.BlockSpec((tm, kvr + LANE), lambda i: (i, 0)),
                  vec(LANE), vec(LANE), vec(W), vec(kvr), vec(LANE), tab_spec, tab_spec, tab_spec],
        out_specs=[pl.BlockSpec((tm, W), lambda i: (i, 0)), pl.BlockSpec((tm, W), lambda i: (i, 0)),
                   pl.BlockSpec((tm, W), lambda i: (i, 0)), pl.BlockSpec((tm, kvr), lambda i: (i, 0)),
                   pl.BlockSpec((tm, LANE), lambda i: (i, 0))],
        out_shape=[jax.ShapeDtypeStruct((R, W), BF16), jax.ShapeDtypeStruct((R, W), BF16),
                   jax.ShapeDtypeStruct((R, W), BF16), jax.ShapeDtypeStruct((R, kvr), BF16),
                   jax.ShapeDtypeStruct((R, LANE), BF16)],
        compiler_params=_cparams(1),
        name="post_in",
    )(z_main, z_main, z_main, z_tail, gq, gk, gql, gkvl, gkr, *tabs)


def _conv_kernel(b_ref, c_ref, u_ref, cp_ref, up_ref, cn_ref, un_ref, w_ref, g_ref, o_ref,
                 *, n_lat, seq, ctx_len):
    tm, W = c_ref.shape
    row0 = pl.program_id(0) * tm
    lat = row0 < n_lat
    pos = jnp.where(lat, row0 % seq, (row0 - n_lat) % ctx_len)
    slen = jnp.where(lat, seq, ctx_len)
    has_prev = (pos > 0).astype(F32)
    has_next = (pos + tm < slen).astype(F32)
    v = c_ref[...].astype(F32) * u_ref[...].astype(F32)
    vprev = cp_ref[7:8, :].astype(F32) * up_ref[7:8, :].astype(F32) * has_prev
    vnext = cn_ref[0:1, :].astype(F32) * un_ref[0:1, :].astype(F32) * has_next
    rid = lax.broadcasted_iota(jnp.int32, (tm, W), 0)
    v_dn = jnp.where(rid == 0, vprev, pltpu.roll(v, 1, axis=0))
    v_up = jnp.where(rid == tm - 1, vnext, pltpu.roll(v, tm - 1, axis=0))
    y = w_ref[0:1, :] * v_dn + w_ref[1:2, :] * v + w_ref[2:3, :] * v_up
    y = b_ref[...].astype(F32) * y
    o_ref[...] = (_rms(y, W) * g_ref[...]).astype(o_ref.dtype)


def _conv_mixer(z_main, conv_w, g, rows, n_lat, seq, ctx_len):
    R = z_main.shape[0]
    W = z_main.shape[1] // 7
    tm = CONV_TILE
    nb8 = R // 8

    def cur(j):
        return pl.BlockSpec((tm, W), lambda i: (i, j))

    def prev(j):
        return pl.BlockSpec((8, W), lambda i: (jnp.maximum(i * (tm // 8) - 1, 0), j))

    def nxt(j):
        return pl.BlockSpec((8, W), lambda i: (jnp.minimum((i + 1) * (tm // 8), nb8 - 1), j))

    return pl.pallas_call(
        functools.partial(_conv_kernel, n_lat=n_lat, seq=seq, ctx_len=ctx_len),
        grid=(rows // tm,),
        in_specs=[cur(0), cur(1), cur(2), prev(1), prev(2), nxt(1), nxt(2),
                  pl.BlockSpec((3, W), lambda i: (0, 0)), pl.BlockSpec((1, W), lambda i: (0, 0))],
        out_specs=pl.BlockSpec((tm, W), lambda i: (i, 0)),
        out_shape=jax.ShapeDtypeStruct((rows, W), BF16),
        compiler_params=_cparams(1),
        name="conv_mixer",
    )(z_main, z_main, z_main, z_main, z_main, z_main, z_main, conv_w, g)


def _na_group_plan(rows):
    n_groups = rows // NA_GROUP
    kinds = [(0, 0), (NA_GROUP, NA_GROUP - NA_WIN_H // 2), ((n_groups - 1) * NA_GROUP, rows - NA_KROWS)]
    plane = np.full((3, NA_GROUP, NA_KROWS), 2 * NA_WIN_H - 1, np.int64)
    for t, (r_first, k_first) in enumerate(kinds):
        for qr in range(NA_GROUP):
            r = r_first + qr
            r0 = min(max(r - NA_WIN_H // 2, 0), rows - NA_WIN_H)
            for kr in range(NA_KROWS):
                kk = k_first + kr
                if r0 <= kk < r0 + NA_WIN_H:
                    plane[t, qr, kr] = kk - r + NA_WIN_H - 1
    return plane


def _na_bias_table(rpb, rows):
    qc = np.arange(GRID_W)[:, None]
    kc = np.arange(GRID_W)[None, :]
    cstart = np.clip(qc - NA_WIN_W // 2, 0, GRID_W - NA_WIN_W)
    ok = (kc >= cstart) & (kc < cstart + NA_WIN_W)
    dc = np.clip(kc - qc + NA_WIN_W - 1, 0, 2 * NA_WIN_W - 2)
    H = rpb.shape[0]
    onehot = (dc[None] == np.arange(2 * NA_WIN_W - 1)[:, None, None]).astype(np.float32)
    cols = jnp.einsum("hdc,cqk->hdqk", rpb.astype(F32), onehot, precision=lax.Precision.HIGHEST)
    cols = jnp.where(ok[None, None], cols * LOG2E, NEG)
    masked = jnp.full((H, GRID_W, GRID_W), NEG, F32)
    plane = _na_group_plan(rows)
    blocks = [cols[:, p] if p < 2 * NA_WIN_H - 1 else masked for p in plane.reshape(-1)]
    g = jnp.stack(blocks, axis=1).reshape(H, 3, NA_GROUP, NA_KROWS, GRID_W, GRID_W)
    return g.transpose(0, 1, 2, 4, 3, 5).reshape(H, 3, NA_GROUP * GRID_W, NA_KROWS * GRID_W)


def _na_kernel(q_ref, k_ref, v_ref, kc_ref, vc_ref, bias_ref, o_ref):
    rows = q_ref.shape[0] // GRID_W
    n_groups = rows // NA_GROUP
    nq = NA_GROUP * GRID_W
    nk = NA_KROWS * GRID_W
    dn = (((1,), (1,)), ((), ()))
    kc = kc_ref[...]
    vc = vc_ref[...]

    def body(g, carry):
        r_first = g * NA_GROUP
        k_first = jnp.clip(r_first - NA_WIN_H // 2, 0, rows - NA_KROWS)
        kind = jnp.where(g == 0, 0, jnp.where(g == n_groups - 1, 2, 1))
        qoff = pl.multiple_of(r_first * GRID_W, GRID_W)
        koff = pl.multiple_of(k_first * GRID_W, GRID_W)
        q = q_ref[pl.ds(qoff, nq), :]
        k = k_ref[pl.ds(koff, nk), :]
        v = v_ref[pl.ds(koff, nk), :]
        s = lax.dot_general(q, k, dn, preferred_element_type=F32) + bias_ref[kind]
        sc = lax.dot_general(q, kc, dn, preferred_element_type=F32)
        m = jnp.maximum(jnp.max(s, axis=-1, keepdims=True), jnp.max(sc, axis=-1, keepdims=True))
        p = jnp.exp2(s - m)
        pc = jnp.exp2(sc - m)
        l = jnp.sum(p, axis=-1, keepdims=True) + jnp.sum(pc, axis=-1, keepdims=True)
        o = (jnp.dot(p.astype(BF16), v, preferred_element_type=F32)
             + jnp.dot(pc.astype(BF16), vc, preferred_element_type=F32))
        o_ref[pl.ds(qoff, nq), :] = (o / l).astype(o_ref.dtype)
        return carry

    lax.fori_loop(0, n_groups, body, 0, unroll=2)


def _na_attention(nq, nk, z_main, bias_tab, B, seq, ctx_len, n_lat):
    W = nq.shape[1]
    nh = W // HEAD_DIM
    cb = n_lat // ctx_len
    voff = 5 * nh
    return pl.pallas_call(
        _na_kernel,
        grid=(nh, B),
        in_specs=[pl.BlockSpec((seq, HEAD_DIM), lambda h, b: (b, h)),
                  pl.BlockSpec((seq, HEAD_DIM), lambda h, b: (b, h)),
                  pl.BlockSpec((seq, HEAD_DIM), lambda h, b: (b, voff + h)),
                  pl.BlockSpec((ctx_len, HEAD_DIM), lambda h, b: (cb + b, h)),
                  pl.BlockSpec((ctx_len, HEAD_DIM), lambda h, b: (cb + b, voff + h)),
                  pl.BlockSpec((None, 3, NA_GROUP * GRID_W, NA_KROWS * GRID_W), lambda h, b: (h, 0, 0, 0))],
        out_specs=pl.BlockSpec((seq, HEAD_DIM), lambda h, b: (b, h)),
        out_shape=jax.ShapeDtypeStruct((n_lat, W), BF16),
        compiler_params=_cparams(2),
        name="na_attention",
    )(nq, nk, z_main, nk, z_main, bias_tab)


def _flash_kernel(q_ref, k_ref, v_ref, kc_ref, vc_ref, o_ref, *, tk):
    tq = q_ref.shape[0]
    dv = v_ref.shape[1]
    dn = (((1,), (1,)), ((), ()))
    q = q_ref[...]

    def step(k, v, m, l, acc):
        s = lax.dot_general(q, k, dn, preferred_element_type=F32)
        m_new = jnp.maximum(m, jnp.max(s, axis=-1, keepdims=True))
        alpha = jnp.exp2(m - m_new)
        p = jnp.exp2(s - m_new)
        l = alpha * l + jnp.sum(p, axis=-1, keepdims=True)
        acc = alpha * acc + jnp.dot(p.astype(BF16), v, preferred_element_type=F32)
        return m_new, l, acc

    def body(j, carry):
        off = pl.multiple_of(j * tk, tk)
        return step(k_ref[pl.ds(off, tk), :], v_ref[pl.ds(off, tk), :], *carry)

    init = (jnp.full((tq, 1), NEG, F32), jnp.zeros((tq, 1), F32), jnp.zeros((tq, dv), F32))
    n_chunks = k_ref.shape[0] // tk
    m, l, acc = lax.fori_loop(0, n_chunks, body, init, unroll=min(n_chunks, 8))
    m, l, acc = step(kc_ref[...], vc_ref[...], m, l, acc)
    o_ref[...] = (acc / l).astype(o_ref.dtype)


def _mla_attention(qf, kf, vf, B, seq, ctx_len, n_lat):
    R = qf.shape[0]
    nh = qf.shape[1] // MLA_QPAD
    tq, tk = 1024, 1024
    cb = n_lat // ctx_len
    nqb = seq // tq
    return pl.pallas_call(
        functools.partial(_flash_kernel, tk=tk),
        grid=(B, nh, nqb),
        in_specs=[pl.BlockSpec((tq, MLA_QPAD), lambda b, h, i: (b * nqb + i, h)),
                  pl.BlockSpec((seq, MLA_QPAD), lambda b, h, i: (b, h)),
                  pl.BlockSpec((seq, MLA_V), lambda b, h, i: (b, h)),
                  pl.BlockSpec((ctx_len, MLA_QPAD), lambda b, h, i: (cb + b, h)),
                  pl.BlockSpec((ctx_len, MLA_V), lambda b, h, i: (cb + b, h))],
        out_specs=pl.BlockSpec((tq, MLA_V), lambda b, h, i: (b * nqb + i, h)),
        out_shape=jax.ShapeDtypeStruct((n_lat, nh * MLA_V), BF16),
        compiler_params=_cparams(3),
        name="mla_attention",
    )(qf, kf, vf, kf, vf)


def _ctx_attn_kernel(q_ref, k_ref, v_ref, o_ref):
    s = lax.dot_general(q_ref[...], k_ref[...], (((1,), (1,)), ((), ())), preferred_element_type=F32)
    p = jnp.exp2(s - jnp.max(s, axis=-1, keepdims=True))
    o = jnp.dot(p.astype(BF16), v_ref[...], preferred_element_type=F32)
    o_ref[...] = (o / jnp.sum(p, axis=-1, keepdims=True)).astype(o_ref.dtype)


def _ctx_attention(q, k, v, nh, dq, dv, voff, B, ctx_len, n_lat):
    cb = n_lat // ctx_len
    return pl.pallas_call(
        _ctx_attn_kernel,
        grid=(B, nh),
        in_specs=[pl.BlockSpec((ctx_len, dq), lambda b, h: (cb + b, h)),
                  pl.BlockSpec((ctx_len, dq), lambda b, h: (cb + b, h)),
                  pl.BlockSpec((ctx_len, dv), lambda b, h: (cb + b, voff + h))],
        out_specs=pl.BlockSpec((ctx_len, dv), lambda b, h: (b, h)),
        out_shape=jax.ShapeDtypeStruct((B * ctx_len, nh * dv), BF16),
        compiler_params=_cparams(2),
        name="ctx_attention",
    )(q, k, v)


def _merge_kernel(yc_ref, yn_ref, ym_ref, ync_ref, ymc_ref, gn_ref, gm_ref, o_ref, *, lat_blocks):
    cw = yc_ref.shape[1]
    nw = yn_ref.shape[1]
    lat = pl.program_id(0) < lat_blocks
    o_ref[:, :cw] = yc_ref[...]
    yn = jnp.where(lat, yn_ref[...], ync_ref[...]).astype(F32)
    o_ref[:, cw:cw + nw] = (_rms(yn, nw) * gn_ref[...]).astype(o_ref.dtype)
    ym = jnp.where(lat, ym_ref[...], ymc_ref[...]).astype(F32)
    o_ref[:, cw + nw:] = (_rms(ym, ym.shape[1]) * gm_ref[...]).astype(o_ref.dtype)


def _merge(y_conv, y_na, y_mla, y_na_ctx, y_mla_ctx, gn, gm, rows):
    cw, nw, mw = y_conv.shape[1], y_na.shape[1], y_mla.shape[1]
    tm = ROW_TILE
    lat_blocks = y_na.shape[0] // tm
    ctx_last = y_na_ctx.shape[0] // tm - 1

    def lat_map(i):
        return (jnp.minimum(i, lat_blocks - 1), 0)

    def ctx_map(i):
        return (jnp.clip(i - lat_blocks, 0, ctx_last), 0)

    return pl.pallas_call(
        functools.partial(_merge_kernel, lat_blocks=lat_blocks),
        grid=(rows // tm,),
        in_specs=[pl.BlockSpec((tm, cw), lambda i: (i, 0)),
                  pl.BlockSpec((tm, nw), lat_map), pl.BlockSpec((tm, mw), lat_map),
                  pl.BlockSpec((tm, nw), ctx_map), pl.BlockSpec((tm, mw), ctx_map),
                  pl.BlockSpec((1, nw), lambda i: (0, 0)), pl.BlockSpec((1, mw), lambda i: (0, 0))],
        out_specs=pl.BlockSpec((tm, cw + nw + mw), lambda i: (i, 0)),
        out_shape=jax.ShapeDtypeStruct((rows, cw + nw + mw), BF16),
        compiler_params=_cparams(1),
        name="merge_norm",
    )(y_conv, y_na, y_mla, y_na_ctx, y_mla_ctx, gn, gm)


def _router_kernel(x_ref, g_ref, sc_ref, sh_ref, whi_ref, wlo_ref, rb_ref, tri_ref,
                   h_ref, idx_ref, wgt_ref, rank_ref, cnt_ref, base_ref):
    @pl.when(pl.program_id(0) == 0)
    def _():
        base_ref[...] = jnp.zeros(base_ref.shape, F32)

    x = x_ref[...]
    h = _rms(x, x.shape[-1]) * g_ref[...] * (1.0 + sc_ref[...]) + sh_ref[...]
    h_hi = h.astype(BF16)
    h_ref[...] = h_hi
    h_lo = (h - h_hi.astype(F32)).astype(BF16)
    dn = (((1,), (1,)), ((), ()))
    logits = (lax.dot_general(whi_ref[...], h_hi, dn, preferred_element_type=F32)
              + lax.dot_general(whi_ref[...], h_lo, dn, preferred_element_type=F32)
              + lax.dot_general(wlo_ref[...], h_hi, dn, preferred_element_type=F32))
    E, tm = logits.shape
    gs = E // N_GROUPS
    scores = jax.nn.sigmoid(logits)
    sel = scores + rb_ref[...]
    sel3 = sel.reshape(N_GROUPS, gs, tm)
    io3 = lax.broadcasted_iota(jnp.int32, sel3.shape, 1)
    m1 = jnp.max(sel3, axis=1, keepdims=True)
    i1 = jnp.min(jnp.where(sel3 == m1, io3, gs), axis=1, keepdims=True)
    m2 = jnp.max(jnp.where(io3 == i1, -jnp.inf, sel3), axis=1, keepdims=True)
    grp = m1 + m2
    iog = lax.broadcasted_iota(jnp.int32, grp.shape, 0)
    keep = jnp.zeros(grp.shape, jnp.bool_)
    for _ in range(TOPK_GROUPS):
        gm = jnp.max(grp, axis=0, keepdims=True)
        gi = jnp.min(jnp.where(grp == gm, iog, N_GROUPS), axis=0, keepdims=True)
        hit = iog == gi
        keep = jnp.logical_or(keep, hit)
        grp = jnp.where(hit, -jnp.inf, grp)
    selm = jnp.where(keep, sel3, -jnp.inf).reshape(E, tm)
    ioe = lax.broadcasted_iota(jnp.int32, (E, tm), 0)
    idxs, wgts, hits = [], [], []
    for _ in range(TOP_K):
        m = jnp.max(selm, axis=0, keepdims=True)
        ei = jnp.min(jnp.where(selm == m, ioe, E), axis=0, keepdims=True)
        hit = ioe == ei
        idxs.append(ei)
        hits.append(hit)
        wgts.append(jnp.sum(jnp.where(hit, scores, 0.0), axis=0, keepdims=True))
        selm = jnp.where(hit, -jnp.inf, selm)
    wsum = wgts[0]
    for w in wgts[1:]:
        wsum = wsum + w
    chosen = jnp.zeros((E, tm), F32)
    for hit in hits:
        chosen = chosen + jnp.where(hit, 1.0, 0.0)
    before = base_ref[:, 0:1] + jnp.dot(chosen.astype(BF16), tri_ref[...], preferred_element_type=F32)
    for k in range(TOP_K):
        idx_ref[k:k + 1, :] = idxs[k]
        wgt_ref[k:k + 1, :] = wgts[k] / wsum * ROUTE_SCALE
        rank_ref[k:k + 1, :] = jnp.sum(jnp.where(hits[k], before, 0.0), axis=0, keepdims=True).astype(jnp.int32)
    base_ref[...] = base_ref[...] + jnp.sum(chosen, axis=1, keepdims=True)
    cnt_ref[...] = base_ref[...]


def _router(x, g, mod_l, w_router, router_bias, row0, rows, n_lat, seq):
    D = x.shape[1]
    E = w_router.shape[1]
    tm = ROW_TILE
    b0 = row0 // tm
    mrow = functools.partial(_mod_row, tm=tm, n_lat=n_lat, seq=seq)
    wt = w_router.T
    whi = wt.astype(BF16)
    wlo = (wt - whi.astype(F32)).astype(BF16)
    tri = (np.arange(tm)[:, None] < np.arange(tm)[None, :]).astype(np.float32)
    return pl.pallas_call(
        _router_kernel,
        grid=(rows // tm,),
        in_specs=[pl.BlockSpec((tm, D), lambda i: (i + b0, 0)),
                  pl.BlockSpec((1, D), lambda i: (0, 0)),
                  pl.BlockSpec((None, 1, D), lambda i: (mrow(i + b0) * 6 + 4, 0, 0)),
                  pl.BlockSpec((None, 1, D), lambda i: (mrow(i + b0) * 6 + 3, 0, 0)),
                  pl.BlockSpec((E, D), lambda i: (0, 0)),
                  pl.BlockSpec((E, D), lambda i: (0, 0)),
                  pl.BlockSpec((E, 1), lambda i: (0, 0)),
                  pl.BlockSpec((tm, tm), lambda i: (0, 0))],
        out_specs=[pl.BlockSpec((tm, D), lambda i: (i, 0)),
                   pl.BlockSpec((TOP_K, tm), lambda i: (0, i)),
                   pl.BlockSpec((TOP_K, tm), lambda i: (0, i)),
                   pl.BlockSpec((TOP_K, tm), lambda i: (0, i)),
                   pl.BlockSpec((E, LANE), lambda i: (0, 0))],
        out_shape=[jax.ShapeDtypeStruct((rows, D), BF16),
                   jax.ShapeDtypeStruct((TOP_K, rows), jnp.int32),
                   jax.ShapeDtypeStruct((TOP_K, rows), F32),
                   jax.ShapeDtypeStruct((TOP_K, rows), jnp.int32),
                   jax.ShapeDtypeStruct((E, LANE), F32)],
        scratch_shapes=[pltpu.VMEM((E, LANE), F32)],
        compiler_params=_cparams(1),
        name="router",
    )(x, g.reshape(1, D), mod_l, mod_l, whi, wlo, router_bias.reshape(E, 1), jnp.asarray(tri, BF16))


def _expert_kernel(vt_ref, ve_ref, nv_ref, gs_ref, xs_ref, wg_ref, wu_ref, wd_ref, o_ref,
                   wgb_ref, wub_ref, wdb_ref):
    v = pl.program_id(0)
    T = xs_ref.shape[0]

    @pl.when(v < nv_ref[0])
    def _():
        e = ve_ref[v]
        tile = vt_ref[v]
        prev = jnp.maximum(v - 1, 0)
        new_expert = jnp.logical_or(v == 0, ve_ref[prev] != e)
        first_visit = jnp.logical_or(v == 0, vt_ref[prev] != tile)

        @pl.when(new_expert)
        def _():
            wgb_ref[...] = wg_ref[...].astype(BF16)
            wub_ref[...] = wu_ref[...].astype(BF16)
            wdb_ref[...] = wd_ref[...].astype(BF16)

        xs = xs_ref[...]
        g = jnp.dot(xs, wgb_ref[...], preferred_element_type=F32)
        u = jnp.dot(xs, wub_ref[...], preferred_element_type=F32)
        a = (g * jax.nn.sigmoid(g) * u).astype(BF16)
        y = jnp.dot(a, wdb_ref[...], preferred_element_type=F32).astype(o_ref.dtype)

        @pl.when(first_visit)
        def _():
            o_ref[...] = y

        @pl.when(jnp.logical_not(first_visit))
        def _():
            row = tile * T + lax.broadcasted_iota(jnp.int32, (T, 1), 0)
            mine = jnp.logical_and(row >= gs_ref[e], row < gs_ref[e + 1])
            o_ref[...] = jnp.where(mine, y, o_ref[...])


def _experts(xs, visit_tile, visit_expert, n_visits, group_start, w_gate, w_up, w_down, layer):
    P, D = xs.shape
    DE = w_gate.shape[3]
    T = MOE_TILE
    grid_spec = pltpu.PrefetchScalarGridSpec(
        num_scalar_prefetch=4,
        grid=(visit_tile.shape[0],),
        in_specs=[pl.BlockSpec((T, D), lambda v, vt, ve, nv, gs: (vt[v], 0)),
                  pl.BlockSpec((None, None, D, DE), lambda v, vt, ve, nv, gs: (layer, ve[v], 0, 0)),
                  pl.BlockSpec((None, None, D, DE), lambda v, vt, ve, nv, gs: (layer, ve[v], 0, 0)),
                  pl.BlockSpec((None, None, DE, D), lambda v, vt, ve, nv, gs: (layer, ve[v], 0, 0))],
        out_specs=pl.BlockSpec((T, D), lambda v, vt, ve, nv, gs: (vt[v], 0)),
        scratch_shapes=[pltpu.VMEM((D, DE), BF16), pltpu.VMEM((D, DE), BF16), pltpu.VMEM((DE, D), BF16)],
    )
    return pl.pallas_call(
        _expert_kernel,
        grid_spec=grid_spec,
        out_shape=jax.ShapeDtypeStruct((P, D), BF16),
        compiler_params=_cparams(1),
        name="experts",
    )(visit_tile, visit_expert, n_visits, group_start, xs, w_gate, w_up, w_down)


def _dispatch_plan(eidx, rank, counts):
    K, rows = eidx.shape
    E = counts.shape[0]
    T = MOE_TILE
    P0 = K * rows
    n_tiles = P0 // T
    gend = jnp.cumsum(counts)
    gstart = gend - counts
    eids = jnp.arange(E, dtype=jnp.int32)[:, None, None]
    dest = rank + jnp.sum(jnp.where(eidx[None] == eids, gstart[:, None, None], 0), axis=0)
    token = jnp.broadcast_to(jnp.arange(rows, dtype=jnp.int32)[None], (K, rows))
    _, row_token = lax.sort((dest.reshape(-1), token.reshape(-1)), num_keys=1)
    first_tile = gstart // T
    n_vis_e = jnp.where(counts > 0, (gend - 1) // T - first_tile + 1, 0)
    vend = jnp.cumsum(n_vis_e)
    voff = vend - n_vis_e
    n_visits = vend[-1]
    v = jnp.arange(n_tiles + E, dtype=jnp.int32)
    ve = jnp.minimum(jnp.sum((v[:, None] >= vend[None, :]).astype(jnp.int32), axis=1), E - 1)
    onehot = (ve[:, None] == jnp.arange(E, dtype=jnp.int32)[None, :]).astype(jnp.int32)
    vt = jnp.sum(onehot * (first_tile - voff)[None, :], axis=1) + v
    vt = jnp.where(v < n_visits, vt, n_tiles - 1).astype(jnp.int32)
    group_start = jnp.concatenate([gstart, gend[-1:]]).astype(jnp.int32)
    return dest, row_token, vt, ve.astype(jnp.int32), n_visits.reshape(1).astype(jnp.int32), group_start


def _shared_kernel(h_ref, wg_ref, wu_ref, wd_ref, y8_ref, gate_ref, x_ref, g2_ref, o_ref):
    h = h_ref[...]
    g = jnp.dot(h, wg_ref[...], preferred_element_type=F32)
    u = jnp.dot(h, wu_ref[...], preferred_element_type=F32)
    a = (g * jax.nn.sigmoid(g) * u).astype(BF16)
    y = jnp.dot(a, wd_ref[...], preferred_element_type=F32)
    for k in range(y8_ref.shape[0]):
        y = y + gate_ref[:, k:k + 1] * y8_ref[k].astype(F32)
    o_ref[...] = x_ref[...] + g2_ref[...] * y


def _shared_combine(h2, wsg, wsu, wsd, y8, gates, x, mod_l, row0, rows, n_lat, seq):
    D = h2.shape[1]
    DE = wsg.shape[1]
    K = y8.shape[0]
    tm = COMBINE_TILE
    b0 = row0 // tm
    mrow = functools.partial(_mod_row, tm=tm, n_lat=n_lat, seq=seq)
    return pl.pallas_call(
        _shared_kernel,
        grid=(rows // tm,),
        in_specs=[pl.BlockSpec((tm, D), lambda i: (i, 0)),
                  pl.BlockSpec((D, DE), lambda i: (0, 0)),
                  pl.BlockSpec((D, DE), lambda i: (0, 0)),
                  pl.BlockSpec((DE, D), lambda i: (0, 0)),
                  pl.BlockSpec((K, tm, D), lambda i: (0, i, 0)),
                  pl.BlockSpec((tm, K), lambda i: (i, 0)),
                  pl.BlockSpec((tm, D), lambda i: (i + b0, 0)),
                  pl.BlockSpec((None, 1, D), lambda i: (mrow(i + b0) * 6 + 5, 0, 0))],
        out_specs=pl.BlockSpec((tm, D), lambda i: (i, 0)),
        out_shape=jax.ShapeDtypeStruct((rows, D), F32),
        compiler_params=_cparams(1),
        name="shared_combine",
    )(h2, wsg, wsu, wsd, y8, gates, x, mod_l)


def _rope_tables(seq, pad_rows):
    t = np.arange(seq)
    row = (t // GRID_W).astype(np.float32)
    col = (t % GRID_W).astype(np.float32)
    n_freq = MLA_ROPE // 4
    inv = (ROPE_BASE ** (-jnp.arange(n_freq, dtype=F32) / n_freq))
    ang = jnp.concatenate([jnp.asarray(row)[:, None] * inv, jnp.asarray(col)[:, None] * inv], axis=-1)
    cos, sin = jnp.cos(ang), jnp.sin(ang)
    half = MLA_ROPE // 2
    z = jnp.zeros((seq, LANE - MLA_ROPE), F32)
    zh = jnp.zeros((seq, half), F32)
    c = jnp.concatenate([cos, cos, z], axis=-1)
    s1 = jnp.concatenate([-sin, zh, z], axis=-1)
    s2 = jnp.concatenate([zh, sin, z], axis=-1)
    ident = jnp.concatenate([jnp.ones((pad_rows, MLA_ROPE), F32), jnp.zeros((pad_rows, LANE - MLA_ROPE), F32)], -1)
    zero = jnp.zeros((pad_rows, LANE), F32)
    return (jnp.concatenate([c, ident], 0), jnp.concatenate([s1, zero], 0), jnp.concatenate([s2, zero], 0))


def _pad_lanes(v, n):
    return jnp.concatenate([v, jnp.zeros((n - v.shape[0],), v.dtype)]).reshape(1, n)


def kernel(x, c, ctx, c_ctx, ada_w, ada_b, norm1_g, w_in, conv_w, na_q_norm, na_k_norm, na_rpb,
           mla_q_lat_norm, mla_kv_lat_norm, w_uq, w_ukv, mla_q_norm, mla_k_norm, out_norm_g, w_o,
           norm2_g, w_router, router_bias, w_gate, w_up, w_down, ws_gate, ws_up, ws_down):
    B, S, D = x.shape
    CTX = ctx.shape[1]
    L = ada_w.shape[0]
    n_lat = B * S
    R = n_lat + B * CTX
    W = D // 4
    kvr = w_ukv.shape[1]
    nh_mla = w_uq.shape[2] // (MLA_NOPE + MLA_ROPE)
    E = w_router.shape[2]
    assert B == 2 and S % ROW_TILE == 0 and (B * CTX) % ROW_TILE == 0 and CTX % CONV_TILE == 0
    assert S % GRID_W == 0 and S // GRID_W >= NA_WIN_H and W % HEAD_DIM == 0 and kvr % LANE == 0
    assert S % 1024 == 0 and (TOP_K * B * CTX) % MOE_TILE == 0 and ROW_TILE % COMBINE_TILE == 0
    assert (S // GRID_W) % NA_GROUP == 0 and S // GRID_W >= NA_KROWS

    xs = jnp.concatenate([x.reshape(n_lat, D), ctx.reshape(B * CTX, D)], axis=0)
    cvecs = jnp.concatenate([c, c_ctx[None]], axis=0)
    mod = _modulation(cvecs, ada_w, ada_b)
    tabs = _rope_tables(S, ROW_TILE)

    for l in range(L):
        last = l == L - 1
        rows = n_lat if last else R
        mod_l = mod[l].reshape(8 * 6, 1, D)

        w_tail = jnp.concatenate([w_in[l, :, 7 * W:], jnp.zeros((D, LANE - MLA_ROPE), F32)], axis=1)
        wq = w_uq[l].reshape(W, nh_mla, MLA_NOPE + MLA_ROPE)
        wq = jnp.concatenate([wq, jnp.zeros((W, nh_mla, MLA_QPAD - MLA_NOPE - MLA_ROPE), F32)], axis=2)
        wq = wq.reshape(W, nh_mla * MLA_QPAD).astype(BF16)
        wkv = w_ukv[l].reshape(kvr, nh_mla, MLA_NOPE + MLA_V)
        wk = wkv[:, :, :MLA_NOPE].reshape(kvr, nh_mla * MLA_NOPE).astype(BF16)
        wv = wkv[:, :, MLA_NOPE:].reshape(kvr, nh_mla * MLA_V).astype(BF16)
        gq_n = mla_q_norm[l][:MLA_NOPE].reshape(1, LANE)
        gq_r = _pad_lanes(mla_q_norm[l][MLA_NOPE:], LANE)
        gk_n = mla_k_norm[l][:MLA_NOPE].reshape(1, LANE)
        gk_r = _pad_lanes(mla_k_norm[l][MLA_NOPE:], LANE)
        og = out_norm_g[l]

        h = _norm_mod(xs, norm1_g[l], mod_l, 1, 0, n_lat, S)
        z_main = _matmul_ws(h, w_in, l, 7 * W, min(512, W), BF16)
        z_tail = _matmul_ws(h, w_tail, None, w_tail.shape[1], w_tail.shape[1], BF16)
        nq, nk, cq, ckv, kr = _post_in(z_main, z_tail, na_q_norm[l].reshape(1, LANE),
                                       na_k_norm[l].reshape(1, LANE), mla_q_lat_norm[l].reshape(1, W),
                                       mla_kv_lat_norm[l].reshape(1, kvr), gk_r, tabs, n_lat, S)
        qf = _uq_matmul(cq, wq, gq_n, gq_r, tabs, n_lat, S)
        kf = _uk_matmul(ckv, wk, gk_n, kr)
        vf = _matmul(ckv, wv, BF16, 512)

        y_conv = _conv_mixer(z_main, conv_w[l], og[:W].reshape(1, W), rows, n_lat, S, CTX)
        y_na = _na_attention(nq, nk, z_main, _na_bias_table(na_rpb[l], S // GRID_W), B, S, CTX, n_lat)
        y_mla = _mla_attention(qf, kf, vf, B, S, CTX, n_lat)
        if last:
            y_na_c, y_mla_c = y_na, y_mla
        else:
            nh_na = W // HEAD_DIM
            y_na_c = _ctx_attention(nq, nk, z_main, nh_na, HEAD_DIM, HEAD_DIM, 5 * nh_na, B, CTX, n_lat)
            y_mla_c = _ctx_attention(qf, kf, vf, nh_mla, MLA_QPAD, MLA_V, 0, B, CTX, n_lat)
        y = _merge(y_conv, y_na, y_mla, y_na_c, y_mla_c,
                   og[W:2 * W].reshape(1, W), og[2 * W:].reshape(1, D - 2 * W), rows)
        xs = _wo_matmul(y, w_o, l, xs, mod_l, rows, n_lat, S)

        wsg, wsu, wsd = ws_gate[l].astype(BF16), ws_up[l].astype(BF16), ws_down[l].astype(BF16)
        n_blocks = rows // ROW_TILE
        bounds = [(n_blocks * p // MOE_PARTS) * ROW_TILE for p in range(MOE_PARTS + 1)]
        parts = []
        for row0, row1 in zip(bounds[:-1], bounds[1:]):
            prow = row1 - row0
            h2, eidx, wgt, rank, cnt = _router(xs, norm2_g[l], mod_l, w_router[l], router_bias[l],
                                               row0, prow, n_lat, S)
            dest, row_token, vis_tile, vis_exp, n_vis, gstart = _dispatch_plan(
                eidx, rank, cnt[:, 0].astype(jnp.int32))
            gathered = h2.at[row_token].get(mode="promise_in_bounds")
            y_rows = _experts(gathered, vis_tile, vis_exp, n_vis, gstart, w_gate, w_up, w_down, l)
            y8 = y_rows.at[dest.reshape(-1)].get(mode="promise_in_bounds").reshape(TOP_K, prow, D)
            parts.append(_shared_combine(h2, wsg, wsu, wsd, y8, wgt.T, xs, mod_l, row0, prow, n_lat, S))
        xs = jnp.concatenate(parts, axis=0)

    return xs[:n_lat].reshape(B, S, D)
```

```python
import functools

import numpy as np
import jax
import jax.numpy as jnp
from jax import lax
from jax.experimental import pallas as pl
from jax.experimental.pallas import tpu as pltpu

F32 = jnp.float32
BF16 = jnp.bfloat16

GRID_W = 64
HEAD_DIM = 128
MLA_NOPE = 128
MLA_ROPE = 64
MLA_V = 128
MLA_QPAD = 256
NA_WIN_H = 8
NA_WIN_W = 16
NA_GROUP = 8
NA_KROWS = NA_GROUP + NA_WIN_H
N_GROUPS = 8
TOPK_GROUPS = 4
TOP_K = 8
ROUTE_SCALE = 2.5
ROPE_BASE = 10000.0
EPS = 1e-6
NEG = -1e30
LOG2E = 1.4426950408889634

LANE = 128
ROW_TILE = 512
CONV_TILE = 256
MOE_TILE = 256
COMBINE_TILE = 128
MOE_PARTS = 2
VMEM_LIMIT = 56 * 1024 * 1024


def _cparams(n_axes):
    return pltpu.CompilerParams(dimension_semantics=("arbitrary",) * n_axes,
                                vmem_limit_bytes=VMEM_LIMIT)


def _rms(x, n):
    return x * lax.rsqrt(jnp.sum(x * x, axis=-1, keepdims=True) * (1.0 / n) + EPS)


def _mod_kernel(sb_ref, w_ref, b_ref, o_ref):
    tn = w_ref.shape[1]
    o_ref[...] = jnp.zeros(o_ref.shape, F32)
    for cblk in range(tn // LANE):
        wc = w_ref[:, cblk * LANE:(cblk + 1) * LANE]
        for r in range(3):
            acc = jnp.sum(wc * sb_ref[r], axis=0, keepdims=True)
            o_ref[r:r + 1, cblk * LANE:(cblk + 1) * LANE] = acc + b_ref[:, cblk * LANE:(cblk + 1) * LANE]


def _modulation(cvecs, ada_w, ada_b):
    L, D, D6 = ada_w.shape
    tn = 512
    s = cvecs * jax.nn.sigmoid(cvecs)
    sb = jnp.broadcast_to(s[:, :, None], (3, D, LANE))
    return pl.pallas_call(
        _mod_kernel,
        grid=(L, D6 // tn),
        in_specs=[pl.BlockSpec((3, D, LANE), lambda l, j: (0, 0, 0)),
                  pl.BlockSpec((None, D, tn), lambda l, j: (l, 0, j)),
                  pl.BlockSpec((None, 1, tn), lambda l, j: (l, 0, j))],
        out_specs=pl.BlockSpec((None, 8, tn), lambda l, j: (l, 0, j)),
        out_shape=jax.ShapeDtypeStruct((L, 8, D6), F32),
        compiler_params=_cparams(2),
        name="modulation",
    )(sb, ada_w, ada_b.reshape(L, 1, D6))


def _mod_row(i, tm, n_lat, seq):
    r0 = i * tm
    return jnp.where(r0 < n_lat, r0 // seq, n_lat // seq)


def _norm_mod_kernel(x_ref, g_ref, sc_ref, sh_ref, o_ref):
    x = x_ref[...]
    y = _rms(x, x.shape[-1]) * g_ref[...]
    o_ref[...] = (y * (1.0 + sc_ref[...]) + sh_ref[...]).astype(o_ref.dtype)


def _norm_mod(x, g, mod_l, which_sc, which_sh, n_lat, seq):
    R, D = x.shape
    tm = ROW_TILE
    mrow = functools.partial(_mod_row, tm=tm, n_lat=n_lat, seq=seq)
    return pl.pallas_call(
        _norm_mod_kernel,
        grid=(R // tm,),
        in_specs=[pl.BlockSpec((tm, D), lambda i: (i, 0)),
                  pl.BlockSpec((1, D), lambda i: (0, 0)),
                  pl.BlockSpec((None, 1, D), lambda i: (mrow(i) * 6 + which_sc, 0, 0)),
                  pl.BlockSpec((None, 1, D), lambda i: (mrow(i) * 6 + which_sh, 0, 0))],
        out_specs=pl.BlockSpec((tm, D), lambda i: (i, 0)),
        out_shape=jax.ShapeDtypeStruct((R, D), BF16),
        compiler_params=_cparams(1),
        name="norm_mod",
    )(x, g.reshape(1, D), mod_l, mod_l)


def _mm_kernel(x_ref, w_ref, o_ref):
    o_ref[...] = jnp.dot(x_ref[...], w_ref[...], preferred_element_type=F32).astype(o_ref.dtype)


def _matmul(x, w, out_dtype, tn, rows=None):
    R, K = x.shape
    rows = R if rows is None else rows
    N = w.shape[1]
    tm = ROW_TILE
    return pl.pallas_call(
        _mm_kernel,
        grid=(rows // tm, N // tn),
        in_specs=[pl.BlockSpec((tm, K), lambda i, j: (i, 0)),
                  pl.BlockSpec((K, tn), lambda i, j: (0, j))],
        out_specs=pl.BlockSpec((tm, tn), lambda i, j: (i, j)),
        out_shape=jax.ShapeDtypeStruct((rows, N), out_dtype),
        compiler_params=_cparams(2),
        name="matmul",
    )(x, w)


def _mm_ws_kernel(x_ref, w_ref, o_ref, wb_ref):
    @pl.when(pl.program_id(1) == 0)
    def _():
        wb_ref[...] = w_ref[...].astype(BF16)

    o_ref[...] = jnp.dot(x_ref[...], wb_ref[...], preferred_element_type=F32).astype(o_ref.dtype)


def _matmul_ws(x, w, layer, n_cols, tn, out_dtype):
    R, K = x.shape
    tm = ROW_TILE
    if layer is None:
        w_spec = pl.BlockSpec((K, tn), lambda j, i: (0, j))
    else:
        w_spec = pl.BlockSpec((None, K, tn), lambda j, i: (layer, 0, j))
    return pl.pallas_call(
        _mm_ws_kernel,
        grid=(n_cols // tn, R // tm),
        in_specs=[pl.BlockSpec((tm, K), lambda j, i: (i, 0)), w_spec],
        out_specs=pl.BlockSpec((tm, tn), lambda j, i: (i, j)),
        out_shape=jax.ShapeDtypeStruct((R, n_cols), out_dtype),
        scratch_shapes=[pltpu.VMEM((K, tn), BF16)],
        compiler_params=_cparams(2),
        name="matmul_ws",
    )(x, w)


def _rope_rotate(t, c_ref, s1_ref, s2_ref):
    return (t * c_ref[...] + pltpu.roll(t, LANE - MLA_ROPE // 2, axis=1) * s1_ref[...]
            + pltpu.roll(t, MLA_ROPE // 2, axis=1) * s2_ref[...])


def _uq_kernel(x_ref, w_ref, gn_ref, gr_ref, c_ref, s1_ref, s2_ref, o_ref):
    acc = jnp.dot(x_ref[...], w_ref[...], preferred_element_type=F32)
    scale = (MLA_NOPE + MLA_ROPE) ** -0.5 * LOG2E
    for h in range(acc.shape[1] // MLA_QPAD):
        o = h * MLA_QPAD
        a = _rms(acc[:, o:o + MLA_NOPE], MLA_NOPE) * gn_ref[...]
        r = _rms(acc[:, o + MLA_NOPE:o + MLA_QPAD], MLA_ROPE) * gr_ref[...]
        r = _rope_rotate(r, c_ref, s1_ref, s2_ref)
        o_ref[:, o:o + MLA_NOPE] = (a * scale).astype(o_ref.dtype)
        o_ref[:, o + MLA_NOPE:o + MLA_QPAD] = (r * scale).astype(o_ref.dtype)


def _rope_block(i, tm, n_lat, seq):
    return jnp.where(i * tm < n_lat, i % (seq // tm), seq // tm)


def _uq_matmul(x, w, gn, gr, tabs, n_lat, seq):
    R, K = x.shape
    N = w.shape[1]
    tm, tn = ROW_TILE, 1024
    rb = functools.partial(_rope_block, tm=tm, n_lat=n_lat, seq=seq)
    tab_spec = pl.BlockSpec((tm, LANE), lambda i, j: (rb(i), 0))
    vec_spec = pl.BlockSpec((1, LANE), lambda i, j: (0, 0))
    return pl.pallas_call(
        _uq_kernel,
        grid=(R // tm, N // tn),
        in_specs=[pl.BlockSpec((tm, K), lambda i, j: (i, 0)),
                  pl.BlockSpec((K, tn), lambda i, j: (0, j)),
                  vec_spec, vec_spec, tab_spec, tab_spec, tab_spec],
        out_specs=pl.BlockSpec((tm, tn), lambda i, j: (i, j)),
        out_shape=jax.ShapeDtypeStruct((R, N), BF16),
        compiler_params=_cparams(2),
        name="uq_matmul",
    )(x, w, gn, gr, *tabs)


def _uk_kernel(x_ref, w_ref, gk_ref, kr_ref, o_ref):
    acc = jnp.dot(x_ref[...], w_ref[...], preferred_element_type=F32)
    for h in range(acc.shape[1] // MLA_NOPE):
        a = _rms(acc[:, h * MLA_NOPE:(h + 1) * MLA_NOPE], MLA_NOPE) * gk_ref[...]
        o_ref[:, h * MLA_QPAD:h * MLA_QPAD + MLA_NOPE] = a.astype(o_ref.dtype)
        o_ref[:, h * MLA_QPAD + MLA_NOPE:(h + 1) * MLA_QPAD] = kr_ref[...]


def _uk_matmul(x, w, gk, kr_pad):
    R, K = x.shape
    N = w.shape[1]
    tm, tn = ROW_TILE, 512
    return pl.pallas_call(
        _uk_kernel,
        grid=(R // tm, N // tn),
        in_specs=[pl.BlockSpec((tm, K), lambda i, j: (i, 0)),
                  pl.BlockSpec((K, tn), lambda i, j: (0, j)),
                  pl.BlockSpec((1, LANE), lambda i, j: (0, 0)),
                  pl.BlockSpec((tm, LANE), lambda i, j: (i, 0))],
        out_specs=pl.BlockSpec((tm, 2 * tn), lambda i, j: (i, j)),
        out_shape=jax.ShapeDtypeStruct((R, 2 * N), BF16),
        compiler_params=_cparams(2),
        name="uk_matmul",
    )(x, w, gk, kr_pad)


def _wo_kernel(y_ref, w_ref, x_ref, g_ref, o_ref, wb_ref):
    @pl.when(pl.program_id(1) == 0)
    def _():
        wb_ref[...] = w_ref[...].astype(BF16)

    acc = jnp.dot(y_ref[...], wb_ref[...], preferred_element_type=F32)
    o_ref[...] = x_ref[...] + g_ref[...] * acc


def _wo_matmul(y, w_o, layer, x, mod_l, rows, n_lat, seq):
    K = y.shape[1]
    N = w_o.shape[2]
    tm, tn = ROW_TILE, 512
    mrow = functools.partial(_mod_row, tm=tm, n_lat=n_lat, seq=seq)
    return pl.pallas_call(
        _wo_kernel,
        grid=(N // tn, rows // tm),
        in_specs=[pl.BlockSpec((tm, K), lambda j, i: (i, 0)),
                  pl.BlockSpec((None, K, tn), lambda j, i: (layer, 0, j)),
                  pl.BlockSpec((tm, tn), lambda j, i: (i, j)),
                  pl.BlockSpec((None, 1, tn), lambda j, i: (mrow(i) * 6 + 2, 0, j))],
        out_specs=pl.BlockSpec((tm, tn), lambda j, i: (i, j)),
        out_shape=jax.ShapeDtypeStruct((rows, N), F32),
        scratch_shapes=[pltpu.VMEM((K, tn), BF16)],
        compiler_params=_cparams(2),
        name="wo_matmul",
    )(y, w_o, x, mod_l)


def _post_in_kernel(nq_ref, nk_ref, cq_ref, tail_ref, gq_ref, gk_ref, gql_ref, gkvl_ref, gkr_ref,
                    c_ref, s1_ref, s2_ref, nq_o, nk_o, cq_o, ckv_o, kr_o):
    nh = nq_ref.shape[1] // HEAD_DIM
    scale = HEAD_DIM ** -0.5 * LOG2E
    for h in range(nh):
        sl = slice(h * HEAD_DIM, (h + 1) * HEAD_DIM)
        q = nq_ref[:, sl].astype(F32)
        nq_o[:, sl] = (_rms(q, HEAD_DIM) * gq_ref[...] * scale).astype(nq_o.dtype)
        k = nk_ref[:, sl].astype(F32)
        nk_o[:, sl] = (_rms(k, HEAD_DIM) * gk_ref[...]).astype(nk_o.dtype)
    cq = cq_ref[...].astype(F32)
    cq_o[...] = (_rms(cq, cq.shape[1]) * gql_ref[...]).astype(cq_o.dtype)
    kvr = ckv_o.shape[1]
    ckv = tail_ref[:, :kvr].astype(F32)
    ckv_o[...] = (_rms(ckv, kvr) * gkvl_ref[...]).astype(ckv_o.dtype)
    kr = tail_ref[:, kvr:kvr + LANE].astype(F32)
    kr = _rms(kr, MLA_ROPE) * gkr_ref[...]
    kr_o[...] = _rope_rotate(kr, c_ref, s1_ref, s2_ref).astype(kr_o.dtype)


def _post_in(z_main, z_tail, gq, gk, gql, gkvl, gkr, tabs, n_lat, seq):
    R = z_main.shape[0]
    W = z_main.shape[1] // 7
    kvr = z_tail.shape[1] - LANE
    tm = ROW_TILE
    rb = functools.partial(_rope_block, tm=tm, n_lat=n_lat, seq=seq)
    tab_spec = pl.BlockSpec((tm, LANE), lambda i: (rb(i), 0))

    def vec(n):
        return pl.BlockSpec((1, n), lambda i: (0, 0))

    def col(j):
        return pl.BlockSpec((tm, W), lambda i: (i, j))

    return pl.pallas_call(
        _post_in_kernel,
        grid=(R // tm,),
        in_specs=[col(3), col(4), col(6), pl.BlockSpec((tm, kvr + LANE), lambda i: (i, 0)),
                  vec(LANE), vec(LANE), vec(W), vec(kvr), vec(LANE), tab_spec, tab_spec, tab_spec],
        out_specs=[pl.BlockSpec((tm, W), lambda i: (i, 0)), pl.BlockSpec((tm, W), lambda i: (i, 0)),
                   pl.BlockSpec((tm, W), lambda i: (i, 0)), pl.BlockSpec((tm, kvr), lambda i: (i, 0)),
                   pl.BlockSpec((tm, LANE), lambda i: (i, 0))],
        out_shape=[jax.ShapeDtypeStruct((R, W), BF16), jax.ShapeDtypeStruct((R, W), BF16),
                   jax.ShapeDtypeStruct((R, W), BF16), jax.ShapeDtypeStruct((R, kvr), BF16),
                   jax.ShapeDtypeStruct((R, LANE), BF16)],
        compiler_params=_cparams(1),
        name="post_in",
    )(z_main, z_main, z_main, z_tail, gq, gk, gql, gkvl, gkr, *tabs)


def _conv_kernel(b_ref, c_ref, u_ref, cp_ref, up_ref, cn_ref, un_ref, w_ref, g_ref, o_ref,
                 *, n_lat, seq, ctx_len):
    tm, W = c_ref.shape
    row0 = pl.program_id(0) * tm
    lat = row0 < n_lat
    pos = jnp.where(lat, row0 % seq, (row0 - n_lat) % ctx_len)
    slen = jnp.where(lat, seq, ctx_len)
    has_prev = (pos > 0).astype(F32)
    has_next = (pos + tm < slen).astype(F32)
    v = c_ref[...].astype(F32) * u_ref[...].astype(F32)
    vprev = cp_ref[7:8, :].astype(F32) * up_ref[7:8, :].astype(F32) * has_prev
    vnext = cn_ref[0:1, :].astype(F32) * un_ref[0:1, :].astype(F32) * has_next
    rid = lax.broadcasted_iota(jnp.int32, (tm, W), 0)
    v_dn = jnp.where(rid == 0, vprev, pltpu.roll(v, 1, axis=0))
    v_up = jnp.where(rid == tm - 1, vnext, pltpu.roll(v, tm - 1, axis=0))
    y = w_ref[0:1, :] * v_dn + w_ref[1:2, :] * v + w_ref[2:3, :] * v_up
    y = b_ref[...].astype(F32) * y
    o_ref[...] = (_rms(y, W) * g_ref[...]).astype(o_ref.dtype)


def _conv_mixer(z_main, conv_w, g, rows, n_lat, seq, ctx_len):
    R = z_main.shape[0]
    W = z_main.shape[1] // 7
    tm = CONV_TILE
    nb8 = R // 8

    def cur(j):
        return pl.BlockSpec((tm, W), lambda i: (i, j))

    def prev(j):
        return pl.BlockSpec((8, W), lambda i: (jnp.maximum(i * (tm // 8) - 1, 0), j))

    def nxt(j):
        return pl.BlockSpec((8, W), lambda i: (jnp.minimum((i + 1) * (tm // 8), nb8 - 1), j))

    return pl.pallas_call(
        functools.partial(_conv_kernel, n_lat=n_lat, seq=seq, ctx_len=ctx_len),
        grid=(rows // tm,),
        in_specs=[cur(0), cur(1), cur(2), prev(1), prev(2), nxt(1), nxt(2),
                  pl.BlockSpec((3, W), lambda i: (0, 0)), pl.BlockSpec((1, W), lambda i: (0, 0))],
        out_specs=pl.BlockSpec((tm, W), lambda i: (i, 0)),
        out_shape=jax.ShapeDtypeStruct((rows, W), BF16),
        compiler_params=_cparams(1),
        name="conv_mixer",
    )(z_main, z_main, z_main, z_main, z_main, z_main, z_main, conv_w, g)


def _na_group_plan(rows):
    n_groups = rows // NA_GROUP
    kinds = [(0, 0), (NA_GROUP, NA_GROUP - NA_WIN_H // 2), ((n_groups - 1) * NA_GROUP, rows - NA_KROWS)]
    plane = np.full((3, NA_GROUP, NA_KROWS), 2 * NA_WIN_H - 1, np.int64)
    for t, (r_first, k_first) in enumerate(kinds):
        for qr in range(NA_GROUP):
            r = r_first + qr
            r0 = min(max(r - NA_WIN_H // 2, 0), rows - NA_WIN_H)
            for kr in range(NA_KROWS):
                kk = k_first + kr
                if r0 <= kk < r0 + NA_WIN_H:
                    plane[t, qr, kr] = kk - r + NA_WIN_H - 1
    return plane


def _na_bias_table(rpb, rows):
    qc = np.arange(GRID_W)[:, None]
    kc = np.arange(GRID_W)[None, :]
    cstart = np.clip(qc - NA_WIN_W // 2, 0, GRID_W - NA_WIN_W)
    ok = (kc >= cstart) & (kc < cstart + NA_WIN_W)
    dc = np.clip(kc - qc + NA_WIN_W - 1, 0, 2 * NA_WIN_W - 2)
    H = rpb.shape[0]
    onehot = (dc[None] == np.arange(2 * NA_WIN_W - 1)[:, None, None]).astype(np.float32)
    cols = jnp.einsum("hdc,cqk->hdqk", rpb.astype(F32), onehot, precision=lax.Precision.HIGHEST)
    cols = jnp.where(ok[None, None], cols * LOG2E, NEG)
    planes = jnp.concatenate([cols, jnp.full((H, 1, GRID_W, GRID_W), NEG, F32)], axis=1)
    pick = (_na_group_plan(rows)[..., None] == np.arange(2 * NA_WIN_H)).astype(np.float32)
    g = jnp.einsum("tqkp,hpxy->htqxky", pick, planes, precision=lax.Precision.HIGHEST)
    return g.reshape(H, 3, NA_GROUP * GRID_W, NA_KROWS * GRID_W)


def _na_kernel(q_ref, k_ref, v_ref, kc_ref, vc_ref, bias_ref, o_ref):
    rows = q_ref.shape[0] // GRID_W
    n_groups = rows // NA_GROUP
    nq = NA_GROUP * GRID_W
    nk = NA_KROWS * GRID_W
    dn = (((1,), (1,)), ((), ()))
    kc = kc_ref[...]
    vc = vc_ref[...]

    def body(g, carry):
        r_first = g * NA_GROUP
        k_first = jnp.clip(r_first - NA_WIN_H // 2, 0, rows - NA_KROWS)
        kind = jnp.where(g == 0, 0, jnp.where(g == n_groups - 1, 2, 1))
        qoff = pl.multiple_of(r_first * GRID_W, GRID_W)
        koff = pl.multiple_of(k_first * GRID_W, GRID_W)
        q = q_ref[pl.ds(qoff, nq), :]
        k = k_ref[pl.ds(koff, nk), :]
        v = v_ref[pl.ds(koff, nk), :]
        s = lax.dot_general(q, k, dn, preferred_element_type=F32) + bias_ref[kind]
        sc = lax.dot_general(q, kc, dn, preferred_element_type=F32)
        m = jnp.maximum(jnp.max(s, axis=-1, keepdims=True), jnp.max(sc, axis=-1, keepdims=True))
        p = jnp.exp2(s - m)
        pc = jnp.exp2(sc - m)
        l = jnp.sum(p, axis=-1, keepdims=True) + jnp.sum(pc, axis=-1, keepdims=True)
        o = (jnp.dot(p.astype(BF16), v, preferred_element_type=F32)
             + jnp.dot(pc.astype(BF16), vc, preferred_element_type=F32))
        o_ref[pl.ds(qoff, nq), :] = (o / l).astype(o_ref.dtype)
        return carry

    lax.fori_loop(0, n_groups, body, 0, unroll=2)


def _na_attention(nq, nk, z_main, bias_tab, B, seq, ctx_len, n_lat):
    W = nq.shape[1]
    nh = W // HEAD_DIM
    cb = n_lat // ctx_len
    voff = 5 * nh
    return pl.pallas_call(
        _na_kernel,
        grid=(nh, B),
        in_specs=[pl.BlockSpec((seq, HEAD_DIM), lambda h, b: (b, h)),
                  pl.BlockSpec((seq, HEAD_DIM), lambda h, b: (b, h)),
                  pl.BlockSpec((seq, HEAD_DIM), lambda h, b: (b, voff + h)),
                  pl.BlockSpec((ctx_len, HEAD_DIM), lambda h, b: (cb + b, h)),
                  pl.BlockSpec((ctx_len, HEAD_DIM), lambda h, b: (cb + b, voff + h)),
                  pl.BlockSpec((None, 3, NA_GROUP * GRID_W, NA_KROWS * GRID_W), lambda h, b: (h, 0, 0, 0))],
        out_specs=pl.BlockSpec((seq, HEAD_DIM), lambda h, b: (b, h)),
        out_shape=jax.ShapeDtypeStruct((n_lat, W), BF16),
        compiler_params=_cparams(2),
        name="na_attention",
    )(nq, nk, z_main, nk, z_main, bias_tab)


def _flash_kernel(q_ref, k_ref, v_ref, kc_ref, vc_ref, o_ref, *, tk):
    tq = q_ref.shape[0]
    dv = v_ref.shape[1]
    dn = (((1,), (1,)), ((), ()))
    q = q_ref[...]

    def step(k, v, m, l, acc):
        s = lax.dot_general(q, k, dn, preferred_element_type=F32)
        m_new = jnp.maximum(m, jnp.max(s, axis=-1, keepdims=True))
        alpha = jnp.exp2(m - m_new)
        p = jnp.exp2(s - m_new)
        l = alpha * l + jnp.sum(p, axis=-1, keepdims=True)
        acc = alpha * acc + jnp.dot(p.astype(BF16), v, preferred_element_type=F32)
        return m_new, l, acc

    def body(j, carry):
        off = pl.multiple_of(j * tk, tk)
        return step(k_ref[pl.ds(off, tk), :], v_ref[pl.ds(off, tk), :], *carry)

    init = (jnp.full((tq, 1), NEG, F32), jnp.zeros((tq, 1), F32), jnp.zeros((tq, dv), F32))
    n_chunks = k_ref.shape[0] // tk
    m, l, acc = lax.fori_loop(0, n_chunks, body, init, unroll=min(n_chunks, 8))
    m, l, acc = step(kc_ref[...], vc_ref[...], m, l, acc)
    o_ref[...] = (acc / l).astype(o_ref.dtype)


def _mla_attention(qf, kf, vf, B, seq, ctx_len, n_lat):
    R = qf.shape[0]
    nh = qf.shape[1] // MLA_QPAD
    tq, tk = 1024, 1024
    cb = n_lat // ctx_len
    nqb = seq // tq
    return pl.pallas_call(
        functools.partial(_flash_kernel, tk=tk),
        grid=(B, nh, nqb),
        in_specs=[pl.BlockSpec((tq, MLA_QPAD), lambda b, h, i: (b * nqb + i, h)),
                  pl.BlockSpec((seq, MLA_QPAD), lambda b, h, i: (b, h)),
                  pl.BlockSpec((seq, MLA_V), lambda b, h, i: (b, h)),
                  pl.BlockSpec((ctx_len, MLA_QPAD), lambda b, h, i: (cb + b, h)),
                  pl.BlockSpec((ctx_len, MLA_V), lambda b, h, i: (cb + b, h))],
        out_specs=pl.BlockSpec((tq, MLA_V), lambda b, h, i: (b * nqb + i, h)),
        out_shape=jax.ShapeDtypeStruct((n_lat, nh * MLA_V), BF16),
        compiler_params=_cparams(3),
        name="mla_attention",
    )(qf, kf, vf, kf, vf)


def _ctx_attn_kernel(q_ref, k_ref, v_ref, o_ref):
    s = lax.dot_general(q_ref[...], k_ref[...], (((1,), (1,)), ((), ())), preferred_element_type=F32)
    p = jnp.exp2(s - jnp.max(s, axis=-1, keepdims=True))
    o = jnp.dot(p.astype(BF16), v_ref[...], preferred_element_type=F32)
    o_ref[...] = (o / jnp.sum(p, axis=-1, keepdims=True)).astype(o_ref.dtype)


def _ctx_attention(q, k, v, nh, dq, dv, voff, B, ctx_len, n_lat):
    cb = n_lat // ctx_len
    return pl.pallas_call(
        _ctx_attn_kernel,
        grid=(B, nh),
        in_specs=[pl.BlockSpec((ctx_len, dq), lambda b, h: (cb + b, h)),
                  pl.BlockSpec((ctx_len, dq), lambda b, h: (cb + b, h)),
                  pl.BlockSpec((ctx_len, dv), lambda b, h: (cb + b, voff + h))],
        out_specs=pl.BlockSpec((ctx_len, dv), lambda b, h: (b, h)),
        out_shape=jax.ShapeDtypeStruct((B * ctx_len, nh * dv), BF16),
        compiler_params=_cparams(2),
        name="ctx_attention",
    )(q, k, v)


def _merge_kernel(yc_ref, yn_ref, ym_ref, ync_ref, ymc_ref, gn_ref, gm_ref, o_ref, *, lat_blocks):
    cw = yc_ref.shape[1]
    nw = yn_ref.shape[1]
    lat = pl.program_id(0) < lat_blocks
    o_ref[:, :cw] = yc_ref[...]
    yn = jnp.where(lat, yn_ref[...], ync_ref[...]).astype(F32)
    o_ref[:, cw:cw + nw] = (_rms(yn, nw) * gn_ref[...]).astype(o_ref.dtype)
    ym = jnp.where(lat, ym_ref[...], ymc_ref[...]).astype(F32)
    o_ref[:, cw + nw:] = (_rms(ym, ym.shape[1]) * gm_ref[...]).astype(o_ref.dtype)


def _merge(y_conv, y_na, y_mla, y_na_ctx, y_mla_ctx, gn, gm, rows):
    cw, nw, mw = y_conv.shape[1], y_na.shape[1], y_mla.shape[1]
    tm = ROW_TILE
    lat_blocks = y_na.shape[0] // tm
    ctx_last = y_na_ctx.shape[0] // tm - 1

    def lat_map(i):
        return (jnp.minimum(i, lat_blocks - 1), 0)

    def ctx_map(i):
        return (jnp.clip(i - lat_blocks, 0, ctx_last), 0)

    return pl.pallas_call(
        functools.partial(_merge_kernel, lat_blocks=lat_blocks),
        grid=(rows // tm,),
        in_specs=[pl.BlockSpec((tm, cw), lambda i: (i, 0)),
                  pl.BlockSpec((tm, nw), lat_map), pl.BlockSpec((tm, mw), lat_map),
                  pl.BlockSpec((tm, nw), ctx_map), pl.BlockSpec((tm, mw), ctx_map),
                  pl.BlockSpec((1, nw), lambda i: (0, 0)), pl.BlockSpec((1, mw), lambda i: (0, 0))],
        out_specs=pl.BlockSpec((tm, cw + nw + mw), lambda i: (i, 0)),
        out_shape=jax.ShapeDtypeStruct((rows, cw + nw + mw), BF16),
        compiler_params=_cparams(1),
        name="merge_norm",
    )(y_conv, y_na, y_mla, y_na_ctx, y_mla_ctx, gn, gm)


def _router_kernel(x_ref, g_ref, sc_ref, sh_ref, whi_ref, wlo_ref, rb_ref, tri_ref,
                   h_ref, idx_ref, wgt_ref, rank_ref, cnt_ref, base_ref):
    @pl.when(pl.program_id(0) == 0)
    def _():
        base_ref[...] = jnp.zeros(base_ref.shape, F32)

    x = x_ref[...]
    h = _rms(x, x.shape[-1]) * g_ref[...] * (1.0 + sc_ref[...]) + sh_ref[...]
    h_hi = h.astype(BF16)
    h_ref[...] = h_hi
    h_lo = (h - h_hi.astype(F32)).astype(BF16)
    dn = (((1,), (1,)), ((), ()))
    logits = (lax.dot_general(whi_ref[...], h_hi, dn, preferred_element_type=F32)
              + lax.dot_general(whi_ref[...], h_lo, dn, preferred_element_type=F32)
              + lax.dot_general(wlo_ref[...], h_hi, dn, preferred_element_type=F32))
    E, tm = logits.shape
    gs = E // N_GROUPS
    scores = jax.nn.sigmoid(logits)
    sel = scores + rb_ref[...]
    sel3 = sel.reshape(N_GROUPS, gs, tm)
    io3 = lax.broadcasted_iota(jnp.int32, sel3.shape, 1)
    m1 = jnp.max(sel3, axis=1, keepdims=True)
    i1 = jnp.min(jnp.where(sel3 == m1, io3, gs), axis=1, keepdims=True)
    m2 = jnp.max(jnp.where(io3 == i1, -jnp.inf, sel3), axis=1, keepdims=True)
    grp = m1 + m2
    iog = lax.broadcasted_iota(jnp.int32, grp.shape, 0)
    keep = jnp.zeros(grp.shape, jnp.bool_)
    for _ in range(TOPK_GROUPS):
        gm = jnp.max(grp, axis=0, keepdims=True)
        gi = jnp.min(jnp.where(grp == gm, iog, N_GROUPS), axis=0, keepdims=True)
        hit = iog == gi
        keep = jnp.logical_or(keep, hit)
        grp = jnp.where(hit, -jnp.inf, grp)
    selm = jnp.where(keep, sel3, -jnp.inf).reshape(E, tm)
    ioe = lax.broadcasted_iota(jnp.int32, (E, tm), 0)
    idxs, wgts, hits = [], [], []
    for _ in range(TOP_K):
        m = jnp.max(selm, axis=0, keepdims=True)
        ei = jnp.min(jnp.where(selm == m, ioe, E), axis=0, keepdims=True)
        hit = ioe == ei
        idxs.append(ei)
        hits.append(hit)
        wgts.append(jnp.sum(jnp.where(hit, scores, 0.0), axis=0, keepdims=True))
        selm = jnp.where(hit, -jnp.inf, selm)
    wsum = wgts[0]
    for w in wgts[1:]:
        wsum = wsum + w
    chosen = jnp.zeros((E, tm), F32)
    for hit in hits:
        chosen = chosen + jnp.where(hit, 1.0, 0.0)
    before = base_ref[:, 0:1] + jnp.dot(chosen.astype(BF16), tri_ref[...], preferred_element_type=F32)
    for k in range(TOP_K):
        idx_ref[k:k + 1, :] = idxs[k]
        wgt_ref[k:k + 1, :] = wgts[k] / wsum * ROUTE_SCALE
        rank_ref[k:k + 1, :] = jnp.sum(jnp.where(hits[k], before, 0.0), axis=0, keepdims=True).astype(jnp.int32)
    base_ref[...] = base_ref[...] + jnp.sum(chosen, axis=1, keepdims=True)
    cnt_ref[...] = base_ref[...]


def _router(x, g, mod_l, w_router, router_bias, row0, rows, n_lat, seq):
    D = x.shape[1]
    E = w_router.shape[1]
    tm = ROW_TILE
    b0 = row0 // tm
    mrow = functools.partial(_mod_row, tm=tm, n_lat=n_lat, seq=seq)
    wt = w_router.T
    whi = wt.astype(BF16)
    wlo = (wt - whi.astype(F32)).astype(BF16)
    tri = (np.arange(tm)[:, None] < np.arange(tm)[None, :]).astype(np.float32)
    return pl.pallas_call(
        _router_kernel,
        grid=(rows // tm,),
        in_specs=[pl.BlockSpec((tm, D), lambda i: (i + b0, 0)),
                  pl.BlockSpec((1, D), lambda i: (0, 0)),
                  pl.BlockSpec((None, 1, D), lambda i: (mrow(i + b0) * 6 + 4, 0, 0)),
                  pl.BlockSpec((None, 1, D), lambda i: (mrow(i + b0) * 6 + 3, 0, 0)),
                  pl.BlockSpec((E, D), lambda i: (0, 0)),
                  pl.BlockSpec((E, D), lambda i: (0, 0)),
                  pl.BlockSpec((E, 1), lambda i: (0, 0)),
                  pl.BlockSpec((tm, tm), lambda i: (0, 0))],
        out_specs=[pl.BlockSpec((tm, D), lambda i: (i, 0)),
                   pl.BlockSpec((TOP_K, tm), lambda i: (0, i)),
                   pl.BlockSpec((TOP_K, tm), lambda i: (0, i)),
                   pl.BlockSpec((TOP_K, tm), lambda i: (0, i)),
                   pl.BlockSpec((E, LANE), lambda i: (0, 0))],
        out_shape=[jax.ShapeDtypeStruct((rows, D), BF16),
                   jax.ShapeDtypeStruct((TOP_K, rows), jnp.int32),
                   jax.ShapeDtypeStruct((TOP_K, rows), F32),
                   jax.ShapeDtypeStruct((TOP_K, rows), jnp.int32),
                   jax.ShapeDtypeStruct((E, LANE), F32)],
        scratch_shapes=[pltpu.VMEM((E, LANE), F32)],
        compiler_params=_cparams(1),
        name="router",
    )(x, g.reshape(1, D), mod_l, mod_l, whi, wlo, router_bias.reshape(E, 1), jnp.asarray(tri, BF16))


def _expert_kernel(vt_ref, ve_ref, nv_ref, gs_ref, xs_ref, wg_ref, wu_ref, wd_ref, o_ref,
                   wgb_ref, wub_ref, wdb_ref):
    v = pl.program_id(0)
    T = xs_ref.shape[0]

    @pl.when(v < nv_ref[0])
    def _():
        e = ve_ref[v]
        tile = vt_ref[v]
        prev = jnp.maximum(v - 1, 0)
        new_expert = jnp.logical_or(v == 0, ve_ref[prev] != e)
        first_visit = jnp.logical_or(v == 0, vt_ref[prev] != tile)

        @pl.when(new_expert)
        def _():
            wgb_ref[...] = wg_ref[...].astype(BF16)
            wub_ref[...] = wu_ref[...].astype(BF16)
            wdb_ref[...] = wd_ref[...].astype(BF16)

        xs = xs_ref[...]
        g = jnp.dot(xs, wgb_ref[...], preferred_element_type=F32)
        u = jnp.dot(xs, wub_ref[...], preferred_element_type=F32)
        a = (g * jax.nn.sigmoid(g) * u).astype(BF16)
        y = jnp.dot(a, wdb_ref[...], preferred_element_type=F32).astype(o_ref.dtype)

        @pl.when(first_visit)
        def _():
            o_ref[...] = y

        @pl.when(jnp.logical_not(first_visit))
        def _():
            row = tile * T + lax.broadcasted_iota(jnp.int32, (T, 1), 0)
            mine = jnp.logical_and(row >= gs_ref[e], row < gs_ref[e + 1])
            o_ref[...] = jnp.where(mine, y, o_ref[...])


def _experts(xs, visit_tile, visit_expert, n_visits, group_start, w_gate, w_up, w_down, layer):
    P, D = xs.shape
    DE = w_gate.shape[3]
    T = MOE_TILE
    grid_spec = pltpu.PrefetchScalarGridSpec(
        num_scalar_prefetch=4,
        grid=(visit_tile.shape[0],),
        in_specs=[pl.BlockSpec((T, D), lambda v, vt, ve, nv, gs: (vt[v], 0)),
                  pl.BlockSpec((None, None, D, DE), lambda v, vt, ve, nv, gs: (layer, ve[v], 0, 0)),
                  pl.BlockSpec((None, None, D, DE), lambda v, vt, ve, nv, gs: (layer, ve[v], 0, 0)),
                  pl.BlockSpec((None, None, DE, D), lambda v, vt, ve, nv, gs: (layer, ve[v], 0, 0))],
        out_specs=pl.BlockSpec((T, D), lambda v, vt, ve, nv, gs: (vt[v], 0)),
        scratch_shapes=[pltpu.VMEM((D, DE), BF16), pltpu.VMEM((D, DE), BF16), pltpu.VMEM((DE, D), BF16)],
    )
    return pl.pallas_call(
        _expert_kernel,
        grid_spec=grid_spec,
        out_shape=jax.ShapeDtypeStruct((P, D), BF16),
        compiler_params=_cparams(1),
        name="experts",
    )(visit_tile, visit_expert, n_visits, group_start, xs, w_gate, w_up, w_down)


def _dispatch_plan(eidx, rank, counts):
    K, rows = eidx.shape
    E = counts.shape[0]
    T = MOE_TILE
    P0 = K * rows
    n_tiles = P0 // T
    gend = jnp.cumsum(counts)
    gstart = gend - counts
    eids = jnp.arange(E, dtype=jnp.int32)[:, None, None]
    dest = rank + jnp.sum(jnp.where(eidx[None] == eids, gstart[:, None, None], 0), axis=0)
    token = jnp.broadcast_to(jnp.arange(rows, dtype=jnp.int32)[None], (K, rows))
    _, row_token = lax.sort((dest.reshape(-1), token.reshape(-1)), num_keys=1)
    first_tile = gstart // T
    n_vis_e = jnp.where(counts > 0, (gend - 1) // T - first_tile + 1, 0)
    vend = jnp.cumsum(n_vis_e)
    voff = vend - n_vis_e
    n_visits = vend[-1]
    v = jnp.arange(n_tiles + E, dtype=jnp.int32)
    ve = jnp.minimum(jnp.sum((v[:, None] >= vend[None, :]).astype(jnp.int32), axis=1), E - 1)
    onehot = (ve[:, None] == jnp.arange(E, dtype=jnp.int32)[None, :]).astype(jnp.int32)
    vt = jnp.sum(onehot * (first_tile - voff)[None, :], axis=1) + v
    vt = jnp.where(v < n_visits, vt, n_tiles - 1).astype(jnp.int32)
    group_start = jnp.concatenate([gstart, gend[-1:]]).astype(jnp.int32)
    return dest, row_token, vt, ve.astype(jnp.int32), n_visits.reshape(1).astype(jnp.int32), group_start


def _shared_kernel(h_ref, wg_ref, wu_ref, wd_ref, y8_ref, gate_ref, x_ref, g2_ref, o_ref):
    h = h_ref[...]
    g = jnp.dot(h, wg_ref[...], preferred_element_type=F32)
    u = jnp.dot(h, wu_ref[...], preferred_element_type=F32)
    a = (g * jax.nn.sigmoid(g) * u).astype(BF16)
    y = jnp.dot(a, wd_ref[...], preferred_element_type=F32)
    for k in range(y8_ref.shape[0]):
        y = y + gate_ref[:, k:k + 1] * y8_ref[k].astype(F32)
    o_ref[...] = x_ref[...] + g2_ref[...] * y


def _shared_combine(h2, wsg, wsu, wsd, y8, gates, x, mod_l, row0, rows, n_lat, seq):
    D = h2.shape[1]
    DE = wsg.shape[1]
    K = y8.shape[0]
    tm = COMBINE_TILE
    b0 = row0 // tm
    mrow = functools.partial(_mod_row, tm=tm, n_lat=n_lat, seq=seq)
    return pl.pallas_call(
        _shared_kernel,
        grid=(rows // tm,),
        in_specs=[pl.BlockSpec((tm, D), lambda i: (i, 0)),
                  pl.BlockSpec((D, DE), lambda i: (0, 0)),
                  pl.BlockSpec((D, DE), lambda i: (0, 0)),
                  pl.BlockSpec((DE, D), lambda i: (0, 0)),
                  pl.BlockSpec((K, tm, D), lambda i: (0, i, 0)),
                  pl.BlockSpec((tm, K), lambda i: (i, 0)),
                  pl.BlockSpec((tm, D), lambda i: (i + b0, 0)),
                  pl.BlockSpec((None, 1, D), lambda i: (mrow(i + b0) * 6 + 5, 0, 0))],
        out_specs=pl.BlockSpec((tm, D), lambda i: (i + b0, 0)),
        out_shape=jax.ShapeDtypeStruct(x.shape, F32),
        input_output_aliases={6: 0},
        compiler_params=_cparams(1),
        name="shared_combine",
    )(h2, wsg, wsu, wsd, y8, gates, x, mod_l)


def _rope_tables(seq, pad_rows):
    t = np.arange(seq)
    row = (t // GRID_W).astype(np.float32)
    col = (t % GRID_W).astype(np.float32)
    n_freq = MLA_ROPE // 4
    inv = (ROPE_BASE ** (-jnp.arange(n_freq, dtype=F32) / n_freq))
    ang = jnp.concatenate([jnp.asarray(row)[:, None] * inv, jnp.asarray(col)[:, None] * inv], axis=-1)
    cos, sin = jnp.cos(ang), jnp.sin(ang)
    half = MLA_ROPE // 2
    z = jnp.zeros((seq, LANE - MLA_ROPE), F32)
    zh = jnp.zeros((seq, half), F32)
    c = jnp.concatenate([cos, cos, z], axis=-1)
    s1 = jnp.concatenate([-sin, zh, z], axis=-1)
    s2 = jnp.concatenate([zh, sin, z], axis=-1)
    ident = jnp.concatenate([jnp.ones((pad_rows, MLA_ROPE), F32), jnp.zeros((pad_rows, LANE - MLA_ROPE), F32)], -1)
    zero = jnp.zeros((pad_rows, LANE), F32)
    return (jnp.concatenate([c, ident], 0), jnp.concatenate([s1, zero], 0), jnp.concatenate([s2, zero], 0))


def _pad_lanes(v, n):
    return jnp.concatenate([v, jnp.zeros((n - v.shape[0],), v.dtype)]).reshape(1, n)


def kernel(x, c, ctx, c_ctx, ada_w, ada_b, norm1_g, w_in, conv_w, na_q_norm, na_k_norm, na_rpb,
           mla_q_lat_norm, mla_kv_lat_norm, w_uq, w_ukv, mla_q_norm, mla_k_norm, out_norm_g, w_o,
           norm2_g, w_router, router_bias, w_gate, w_up, w_down, ws_gate, ws_up, ws_down):
    B, S, D = x.shape
    CTX = ctx.shape[1]
    L = ada_w.shape[0]
    n_lat = B * S
    R = n_lat + B * CTX
    W = D // 4
    kvr = w_ukv.shape[1]
    nh_mla = w_uq.shape[2] // (MLA_NOPE + MLA_ROPE)
    E = w_router.shape[2]
    assert B == 2 and S % ROW_TILE == 0 and (B * CTX) % ROW_TILE == 0 and CTX % CONV_TILE == 0
    assert S % GRID_W == 0 and S // GRID_W >= NA_WIN_H and W % HEAD_DIM == 0 and kvr % LANE == 0
    assert S % 1024 == 0 and (TOP_K * B * CTX) % MOE_TILE == 0 and ROW_TILE % COMBINE_TILE == 0
    assert (S // GRID_W) % NA_GROUP == 0 and S // GRID_W >= NA_KROWS

    xs = jnp.concatenate([x.reshape(n_lat, D), ctx.reshape(B * CTX, D)], axis=0)
    cvecs = jnp.concatenate([c, c_ctx[None]], axis=0)
    mod = _modulation(cvecs, ada_w, ada_b)
    tabs = _rope_tables(S, ROW_TILE)

    for l in range(L):
        last = l == L - 1
        rows = n_lat if last else R
        mod_l = mod[l].reshape(8 * 6, 1, D)

        w_tail = jnp.concatenate([w_in[l, :, 7 * W:], jnp.zeros((D, LANE - MLA_ROPE), F32)], axis=1)
        wq = w_uq[l].reshape(W, nh_mla, MLA_NOPE + MLA_ROPE)
        wq = jnp.concatenate([wq, jnp.zeros((W, nh_mla, MLA_QPAD - MLA_NOPE - MLA_ROPE), F32)], axis=2)
        wq = wq.reshape(W, nh_mla * MLA_QPAD).astype(BF16)
        wkv = w_ukv[l].reshape(kvr, nh_mla, MLA_NOPE + MLA_V)
        wk = wkv[:, :, :MLA_NOPE].reshape(kvr, nh_mla * MLA_NOPE).astype(BF16)
        wv = wkv[:, :, MLA_NOPE:].reshape(kvr, nh_mla * MLA_V).astype(BF16)
        gq_n = mla_q_norm[l][:MLA_NOPE].reshape(1, LANE)
        gq_r = _pad_lanes(mla_q_norm[l][MLA_NOPE:], LANE)
        gk_n = mla_k_norm[l][:MLA_NOPE].reshape(1, LANE)
        gk_r = _pad_lanes(mla_k_norm[l][MLA_NOPE:], LANE)
        og = out_norm_g[l]

        h = _norm_mod(xs, norm1_g[l], mod_l, 1, 0, n_lat, S)
        z_main = _matmul_ws(h, w_in, l, 7 * W, min(512, W), BF16)
        z_tail = _matmul_ws(h, w_tail, None, w_tail.shape[1], w_tail.shape[1], BF16)
        nq, nk, cq, ckv, kr = _post_in(z_main, z_tail, na_q_norm[l].reshape(1, LANE),
                                       na_k_norm[l].reshape(1, LANE), mla_q_lat_norm[l].reshape(1, W),
                                       mla_kv_lat_norm[l].reshape(1, kvr), gk_r, tabs, n_lat, S)
        qf = _uq_matmul(cq, wq, gq_n, gq_r, tabs, n_lat, S)
        kf = _uk_matmul(ckv, wk, gk_n, kr)
        vf = _matmul(ckv, wv, BF16, 512)

        y_conv = _conv_mixer(z_main, conv_w[l], og[:W].reshape(1, W), rows, n_lat, S, CTX)
        y_na = _na_attention(nq, nk, z_main, _na_bias_table(na_rpb[l], S // GRID_W), B, S, CTX, n_lat)
        y_mla = _mla_attention(qf, kf, vf, B, S, CTX, n_lat)
        if last:
            y_na_c, y_mla_c = y_na, y_mla
        else:
            nh_na = W // HEAD_DIM
            y_na_c = _ctx_attention(nq, nk, z_main, nh_na, HEAD_DIM, HEAD_DIM, 5 * nh_na, B, CTX, n_lat)
            y_mla_c = _ctx_attention(qf, kf, vf, nh_mla, MLA_QPAD, MLA_V, 0, B, CTX, n_lat)
        y = _merge(y_conv, y_na, y_mla, y_na_c, y_mla_c,
                   og[W:2 * W].reshape(1, W), og[2 * W:].reshape(1, D - 2 * W), rows)
        xs = _wo_matmul(y, w_o, l, xs, mod_l, rows, n_lat, S)

        wsg, wsu, wsd = ws_gate[l].astype(BF16), ws_up[l].astype(BF16), ws_down[l].astype(BF16)
        n_blocks = rows // ROW_TILE
        bounds = [(n_blocks * p // MOE_PARTS) * ROW_TILE for p in range(MOE_PARTS + 1)]
        x_mid = xs
        for row0, row1 in zip(bounds[:-1], bounds[1:]):
            prow = row1 - row0
            h2, eidx, wgt, rank, cnt = _router(x_mid, norm2_g[l], mod_l, w_router[l], router_bias[l],
                                               row0, prow, n_lat, S)
            dest, row_token, vis_tile, vis_exp, n_vis, gstart = _dispatch_plan(
                eidx, rank, cnt[:, 0].astype(jnp.int32))
            gathered = h2.at[row_token].get(mode="promise_in_bounds")
            y_rows = _experts(gathered, vis_tile, vis_exp, n_vis, gstart, w_gate, w_up, w_down, l)
            y8 = y_rows.at[dest.reshape(-1)].get(mode="promise_in_bounds").reshape(TOP_K, prow, D)
            xs = _shared_combine(h2, wsg, wsu, wsd, y8, wgt.T, xs, mod_l, row0, prow, n_lat, S)

    return xs[:n_lat].reshape(B, S, D)
```

```python
import functools

import numpy as np
import jax
import jax.numpy as jnp
from jax import lax
from jax.experimental import pallas as pl
from jax.experimental.pallas import tpu as pltpu

F32 = jnp.float32
BF16 = jnp.bfloat16

GRID_W = 64
HEAD_DIM = 128
MLA_NOPE = 128
MLA_ROPE = 64
MLA_V = 128
MLA_QPAD = 256
NA_WIN_H = 8
NA_WIN_W = 16
NA_GROUP = 8
NA_KROWS = NA_GROUP + NA_WIN_H
N_GROUPS = 8
TOPK_GROUPS = 4
TOP_K = 8
ROUTE_SCALE = 2.5
ROPE_BASE = 10000.0
EPS = 1e-6
NEG = -1e30
LOG2E = 1.4426950408889634

LANE = 128
ROW_TILE = 512
CONV_TILE = 256
MOE_TILE = 256
COMBINE_TILE = 128
MOE_PARTS = 2
VMEM_LIMIT = 56 * 1024 * 1024


def _cparams(n_axes):
    return pltpu.CompilerParams(dimension_semantics=("arbitrary",) * n_axes,
                                vmem_limit_bytes=VMEM_LIMIT)


def _rms(x, n):
    return x * lax.rsqrt(jnp.sum(x * x, axis=-1, keepdims=True) * (1.0 / n) + EPS)


def _mod_kernel(sb_ref, w_ref, b_ref, o_ref):
    tn = w_ref.shape[1]
    o_ref[...] = jnp.zeros(o_ref.shape, F32)
    for cblk in range(tn // LANE):
        wc = w_ref[:, cblk * LANE:(cblk + 1) * LANE]
        for r in range(3):
            acc = jnp.sum(wc * sb_ref[r], axis=0, keepdims=True)
            o_ref[r:r + 1, cblk * LANE:(cblk + 1) * LANE] = acc + b_ref[:, cblk * LANE:(cblk + 1) * LANE]


def _modulation(cvecs, ada_w, ada_b):
    L, D, D6 = ada_w.shape
    tn = 512
    s = cvecs * jax.nn.sigmoid(cvecs)
    sb = jnp.broadcast_to(s[:, :, None], (3, D, LANE))
    return pl.pallas_call(
        _mod_kernel,
        grid=(L, D6 // tn),
        in_specs=[pl.BlockSpec((3, D, LANE), lambda l, j: (0, 0, 0)),
                  pl.BlockSpec((None, D, tn), lambda l, j: (l, 0, j)),
                  pl.BlockSpec((None, 1, tn), lambda l, j: (l, 0, j))],
        out_specs=pl.BlockSpec((None, 8, tn), lambda l, j: (l, 0, j)),
        out_shape=jax.ShapeDtypeStruct((L, 8, D6), F32),
        compiler_params=_cparams(2),
        name="modulation",
    )(sb, ada_w, ada_b.reshape(L, 1, D6))


def _mod_row(i, tm, n_lat, seq):
    r0 = i * tm
    return jnp.where(r0 < n_lat, r0 // seq, n_lat // seq)


def _norm_mod_kernel(x_ref, g_ref, sc_ref, sh_ref, o_ref):
    x = x_ref[...]
    y = _rms(x, x.shape[-1]) * g_ref[...]
    o_ref[...] = (y * (1.0 + sc_ref[...]) + sh_ref[...]).astype(o_ref.dtype)


def _norm_mod(x, g, mod_l, which_sc, which_sh, n_lat, seq):
    R, D = x.shape
    tm = ROW_TILE
    mrow = functools.partial(_mod_row, tm=tm, n_lat=n_lat, seq=seq)
    return pl.pallas_call(
        _norm_mod_kernel,
        grid=(R // tm,),
        in_specs=[pl.BlockSpec((tm, D), lambda i: (i, 0)),
                  pl.BlockSpec((1, D), lambda i: (0, 0)),
                  pl.BlockSpec((None, 1, D), lambda i: (mrow(i) * 6 + which_sc, 0, 0)),
                  pl.BlockSpec((None, 1, D), lambda i: (mrow(i) * 6 + which_sh, 0, 0))],
        out_specs=pl.BlockSpec((tm, D), lambda i: (i, 0)),
        out_shape=jax.ShapeDtypeStruct((R, D), BF16),
        compiler_params=_cparams(1),
        name="norm_mod",
    )(x, g.reshape(1, D), mod_l, mod_l)


def _mm_kernel(x_ref, w_ref, o_ref):
    o_ref[...] = jnp.dot(x_ref[...], w_ref[...], preferred_element_type=F32).astype(o_ref.dtype)


def _matmul(x, w, out_dtype, tn, rows=None):
    R, K = x.shape
    rows = R if rows is None else rows
    N = w.shape[1]
    tm = ROW_TILE
    return pl.pallas_call(
        _mm_kernel,
        grid=(rows // tm, N // tn),
        in_specs=[pl.BlockSpec((tm, K), lambda i, j: (i, 0)),
                  pl.BlockSpec((K, tn), lambda i, j: (0, j))],
        out_specs=pl.BlockSpec((tm, tn), lambda i, j: (i, j)),
        out_shape=jax.ShapeDtypeStruct((rows, N), out_dtype),
        compiler_params=_cparams(2),
        name="matmul",
    )(x, w)


def _mm_ws_kernel(x_ref, w_ref, o_ref, wb_ref):
    @pl.when(pl.program_id(1) == 0)
    def _():
        wb_ref[...] = w_ref[...].astype(BF16)

    o_ref[...] = jnp.dot(x_ref[...], wb_ref[...], preferred_element_type=F32).astype(o_ref.dtype)


def _matmul_ws(x, w, layer, n_cols, tn, out_dtype):
    R, K = x.shape
    tm = ROW_TILE
    if layer is None:
        w_spec = pl.BlockSpec((K, tn), lambda j, i: (0, j))
    else:
        w_spec = pl.BlockSpec((None, K, tn), lambda j, i: (layer, 0, j))
    return pl.pallas_call(
        _mm_ws_kernel,
        grid=(n_cols // tn, R // tm),
        in_specs=[pl.BlockSpec((tm, K), lambda j, i: (i, 0)), w_spec],
        out_specs=pl.BlockSpec((tm, tn), lambda j, i: (i, j)),
        out_shape=jax.ShapeDtypeStruct((R, n_cols), out_dtype),
        scratch_shapes=[pltpu.VMEM((K, tn), BF16)],
        compiler_params=_cparams(2),
        name="matmul_ws",
    )(x, w)


def _rope_rotate(t, c_ref, s1_ref, s2_ref):
    return (t * c_ref[...] + pltpu.roll(t, LANE - MLA_ROPE // 2, axis=1) * s1_ref[...]
            + pltpu.roll(t, MLA_ROPE // 2, axis=1) * s2_ref[...])


def _uq_kernel(x_ref, w_ref, gn_ref, gr_ref, c_ref, s1_ref, s2_ref, o_ref):
    acc = jnp.dot(x_ref[...], w_ref[...], preferred_element_type=F32)
    scale = (MLA_NOPE + MLA_ROPE) ** -0.5 * LOG2E
    for h in range(acc.shape[1] // MLA_QPAD):
        o = h * MLA_QPAD
        a = _rms(acc[:, o:o + MLA_NOPE], MLA_NOPE) * gn_ref[...]
        r = _rms(acc[:, o + MLA_NOPE:o + MLA_QPAD], MLA_ROPE) * gr_ref[...]
        r = _rope_rotate(r, c_ref, s1_ref, s2_ref)
        o_ref[:, o:o + MLA_NOPE] = (a * scale).astype(o_ref.dtype)
        o_ref[:, o + MLA_NOPE:o + MLA_QPAD] = (r * scale).astype(o_ref.dtype)


def _rope_block(i, tm, n_lat, seq):
    return jnp.where(i * tm < n_lat, i % (seq // tm), seq // tm)


def _uq_matmul(x, w, gn, gr, tabs, n_lat, seq):
    R, K = x.shape
    N = w.shape[1]
    tm, tn = ROW_TILE, 1024
    rb = functools.partial(_rope_block, tm=tm, n_lat=n_lat, seq=seq)
    tab_spec = pl.BlockSpec((tm, LANE), lambda i, j: (rb(i), 0))
    vec_spec = pl.BlockSpec((1, LANE), lambda i, j: (0, 0))
    return pl.pallas_call(
        _uq_kernel,
        grid=(R // tm, N // tn),
        in_specs=[pl.BlockSpec((tm, K), lambda i, j: (i, 0)),
                  pl.BlockSpec((K, tn), lambda i, j: (0, j)),
                  vec_spec, vec_spec, tab_spec, tab_spec, tab_spec],
        out_specs=pl.BlockSpec((tm, tn), lambda i, j: (i, j)),
        out_shape=jax.ShapeDtypeStruct((R, N), BF16),
        compiler_params=_cparams(2),
        name="uq_matmul",
    )(x, w, gn, gr, *tabs)


def _uk_kernel(x_ref, w_ref, gk_ref, kr_ref, o_ref):
    acc = jnp.dot(x_ref[...], w_ref[...], preferred_element_type=F32)
    for h in range(acc.shape[1] // MLA_NOPE):
        a = _rms(acc[:, h * MLA_NOPE:(h + 1) * MLA_NOPE], MLA_NOPE) * gk_ref[...]
        o_ref[:, h * MLA_QPAD:h * MLA_QPAD + MLA_NOPE] = a.astype(o_ref.dtype)
        o_ref[:, h * MLA_QPAD + MLA_NOPE:(h + 1) * MLA_QPAD] = kr_ref[...]


def _uk_matmul(x, w, gk, kr_pad):
    R, K = x.shape
    N = w.shape[1]
    tm, tn = ROW_TILE, 512
    return pl.pallas_call(
        _uk_kernel,
        grid=(R // tm, N // tn),
        in_specs=[pl.BlockSpec((tm, K), lambda i, j: (i, 0)),
                  pl.BlockSpec((K, tn), lambda i, j: (0, j)),
                  pl.BlockSpec((1, LANE), lambda i, j: (0, 0)),
                  pl.BlockSpec((tm, LANE), lambda i, j: (i, 0))],
        out_specs=pl.BlockSpec((tm, 2 * tn), lambda i, j: (i, j)),
        out_shape=jax.ShapeDtypeStruct((R, 2 * N), BF16),
        compiler_params=_cparams(2),
        name="uk_matmul",
    )(x, w, gk, kr_pad)


def _wo_kernel(y_ref, w_ref, x_ref, g_ref, o_ref, wb_ref):
    @pl.when(pl.program_id(1) == 0)
    def _():
        wb_ref[...] = w_ref[...].astype(BF16)

    acc = jnp.dot(y_ref[...], wb_ref[...], preferred_element_type=F32)
    o_ref[...] = x_ref[...] + g_ref[...] * acc


def _wo_matmul(y, w_o, layer, x, mod_l, rows, n_lat, seq):
    K = y.shape[1]
    N = w_o.shape[2]
    tm, tn = ROW_TILE, 512
    mrow = functools.partial(_mod_row, tm=tm, n_lat=n_lat, seq=seq)
    return pl.pallas_call(
        _wo_kernel,
        grid=(N // tn, rows // tm),
        in_specs=[pl.BlockSpec((tm, K), lambda j, i: (i, 0)),
                  pl.BlockSpec((None, K, tn), lambda j, i: (layer, 0, j)),
                  pl.BlockSpec((tm, tn), lambda j, i: (i, j)),
                  pl.BlockSpec((None, 1, tn), lambda j, i: (mrow(i) * 6 + 2, 0, j))],
        out_specs=pl.BlockSpec((tm, tn), lambda j, i: (i, j)),
        out_shape=jax.ShapeDtypeStruct((rows, N), F32),
        scratch_shapes=[pltpu.VMEM((K, tn), BF16)],
        compiler_params=_cparams(2),
        name="wo_matmul",
    )(y, w_o, x, mod_l)


def _post_in_kernel(nq_ref, nk_ref, cq_ref, tail_ref, gq_ref, gk_ref, gql_ref, gkvl_ref, gkr_ref,
                    c_ref, s1_ref, s2_ref, nq_o, nk_o, cq_o, ckv_o, kr_o):
    nh = nq_ref.shape[1] // HEAD_DIM
    scale = HEAD_DIM ** -0.5 * LOG2E
    for h in range(nh):
        sl = slice(h * HEAD_DIM, (h + 1) * HEAD_DIM)
        q = nq_ref[:, sl].astype(F32)
        nq_o[:, sl] = (_rms(q, HEAD_DIM) * gq_ref[...] * scale).astype(nq_o.dtype)
        k = nk_ref[:, sl].astype(F32)
        nk_o[:, sl] = (_rms(k, HEAD_DIM) * gk_ref[...]).astype(nk_o.dtype)
    cq = cq_ref[...].astype(F32)
    cq_o[...] = (_rms(cq, cq.shape[1]) * gql_ref[...]).astype(cq_o.dtype)
    kvr = ckv_o.shape[1]
    ckv = tail_ref[:, :kvr].astype(F32)
    ckv_o[...] = (_rms(ckv, kvr) * gkvl_ref[...]).astype(ckv_o.dtype)
    kr = tail_ref[:, kvr:kvr + LANE].astype(F32)
    kr = _rms(kr, MLA_ROPE) * gkr_ref[...]
    kr_o[...] = _rope_rotate(kr, c_ref, s1_ref, s2_ref).astype(kr_o.dtype)


def _post_in(z_main, z_tail, gq, gk, gql, gkvl, gkr, tabs, n_lat, seq):
    R = z_main.shape[0]
    W = z_main.shape[1] // 7
    kvr = z_tail.shape[1] - LANE
    tm = ROW_TILE
    rb = functools.partial(_rope_block, tm=tm, n_lat=n_lat, seq=seq)
    tab_spec = pl.BlockSpec((tm, LANE), lambda i: (rb(i), 0))

    def vec(n):
        return pl.BlockSpec((1, n), lambda i: (0, 0))

    def col(j):
        return pl.BlockSpec((tm, W), lambda i: (i, j))

    return pl.pallas_call(
        _post_in_kernel,
        grid=(R // tm,),
        in_specs=[col(3), col(4), col(6), pl.BlockSpec((tm, kvr + LANE), lambda i: (i, 0)),
                  vec(LANE), vec(LANE), vec(W), vec(kvr), vec(LANE), tab_spec, tab_spec, tab_spec],
        out_specs=[pl.BlockSpec((tm, W), lambda i: (i, 0)), pl.BlockSpec((tm, W), lambda i: (i, 0)),
                   pl.BlockSpec((tm, W), lambda i: (i, 0)), pl.BlockSpec((tm, kvr), lambda i: (i, 0)),
                   pl.BlockSpec((tm, LANE), lambda i: (i, 0))],
        out_shape=[jax.ShapeDtypeStruct((R, W), BF16), jax.ShapeDtypeStruct((R, W), BF16),
                   jax.ShapeDtypeStruct((R, W), BF16), jax.ShapeDtypeStruct((R, kvr), BF16),
                   jax.ShapeDtypeStruct((R, LANE), BF16)],
        compiler_params=_cparams(1),
        name="post_in",
    )(z_main, z_main, z_main, z_tail, gq, gk, gql, gkvl, gkr, *tabs)


def _conv_kernel(b_ref, c_ref, u_ref, cp_ref, up_ref, cn_ref, un_ref, w_ref, g_ref, o_ref,
                 *, n_lat, seq, ctx_len):
    tm, W = c_ref.shape
    row0 = pl.program_id(0) * tm
    lat = row0 < n_lat
    pos = jnp.where(lat, row0 % seq, (row0 - n_lat) % ctx_len)
    slen = jnp.where(lat, seq, ctx_len)
    has_prev = (pos > 0).astype(F32)
    has_next = (pos + tm < slen).astype(F32)
    v = c_ref[...].astype(F32) * u_ref[...].astype(F32)
    vprev = cp_ref[7:8, :].astype(F32) * up_ref[7:8, :].astype(F32) * has_prev
    vnext = cn_ref[0:1, :].astype(F32) * un_ref[0:1, :].astype(F32) * has_next
    rid = lax.broadcasted_iota(jnp.int32, (tm, W), 0)
    v_dn = jnp.where(rid == 0, vprev, pltpu.roll(v, 1, axis=0))
    v_up = jnp.where(rid == tm - 1, vnext, pltpu.roll(v, tm - 1, axis=0))
    y = w_ref[0:1, :] * v_dn + w_ref[1:2, :] * v + w_ref[2:3, :] * v_up
    y = b_ref[...].astype(F32) * y
    o_ref[...] = (_rms(y, W) * g_ref[...]).astype(o_ref.dtype)


def _conv_mixer(z_main, conv_w, g, rows, n_lat, seq, ctx_len):
    R = z_main.shape[0]
    W = z_main.shape[1] // 7
    tm = CONV_TILE
    nb8 = R // 8

    def cur(j):
        return pl.BlockSpec((tm, W), lambda i: (i, j))

    def prev(j):
        return pl.BlockSpec((8, W), lambda i: (jnp.maximum(i * (tm // 8) - 1, 0), j))

    def nxt(j):
        return pl.BlockSpec((8, W), lambda i: (jnp.minimum((i + 1) * (tm // 8), nb8 - 1), j))

    return pl.pallas_call(
        functools.partial(_conv_kernel, n_lat=n_lat, seq=seq, ctx_len=ctx_len),
        grid=(rows // tm,),
        in_specs=[cur(0), cur(1), cur(2), prev(1), prev(2), nxt(1), nxt(2),
                  pl.BlockSpec((3, W), lambda i: (0, 0)), pl.BlockSpec((1, W), lambda i: (0, 0))],
        out_specs=pl.BlockSpec((tm, W), lambda i: (i, 0)),
        out_shape=jax.ShapeDtypeStruct((rows, W), BF16),
        compiler_params=_cparams(1),
        name="conv_mixer",
    )(z_main, z_main, z_main, z_main, z_main, z_main, z_main, conv_w, g)


def _na_group_plan(rows):
    n_groups = rows // NA_GROUP
    kinds = [(0, 0), (NA_GROUP, NA_GROUP - NA_WIN_H // 2), ((n_groups - 1) * NA_GROUP, rows - NA_KROWS)]
    plane = np.full((3, NA_GROUP, NA_KROWS), 2 * NA_WIN_H - 1, np.int64)
    for t, (r_first, k_first) in enumerate(kinds):
        for qr in range(NA_GROUP):
            r = r_first + qr
            r0 = min(max(r - NA_WIN_H // 2, 0), rows - NA_WIN_H)
            for kr in range(NA_KROWS):
                kk = k_first + kr
                if r0 <= kk < r0 + NA_WIN_H:
                    plane[t, qr, kr] = kk - r + NA_WIN_H - 1
    return plane


def _na_bias_table(rpb, rows):
    qc = np.arange(GRID_W)[:, None]
    kc = np.arange(GRID_W)[None, :]
    cstart = np.clip(qc - NA_WIN_W // 2, 0, GRID_W - NA_WIN_W)
    ok = (kc >= cstart) & (kc < cstart + NA_WIN_W)
    dc = np.clip(kc - qc + NA_WIN_W - 1, 0, 2 * NA_WIN_W - 2)
    H = rpb.shape[0]
    onehot = (dc[None] == np.arange(2 * NA_WIN_W - 1)[:, None, None]).astype(np.float32)
    cols = jnp.einsum("hdc,cqk->hdqk", rpb.astype(F32), onehot, precision=lax.Precision.HIGHEST)
    cols = jnp.where(ok[None, None], cols * LOG2E, NEG)
    planes = jnp.concatenate([cols, jnp.full((H, 1, GRID_W, GRID_W), NEG, F32)], axis=1)
    pick = (_na_group_plan(rows)[..., None] == np.arange(2 * NA_WIN_H)).astype(np.float32)
    g = jnp.einsum("tqkp,hpxy->htqxky", pick, planes, precision=lax.Precision.HIGHEST)
    return g.reshape(H, 3, NA_GROUP * GRID_W, NA_KROWS * GRID_W)


def _na_kernel(q_ref, k_ref, v_ref, kc_ref, vc_ref, bias_ref, o_ref):
    rows = q_ref.shape[0] // GRID_W
    n_groups = rows // NA_GROUP
    nq = NA_GROUP * GRID_W
    nk = NA_KROWS * GRID_W
    dn = (((1,), (1,)), ((), ()))
    kc = kc_ref[...]
    vc = vc_ref[...]

    def body(g, carry):
        r_first = g * NA_GROUP
        k_first = jnp.clip(r_first - NA_WIN_H // 2, 0, rows - NA_KROWS)
        kind = jnp.where(g == 0, 0, jnp.where(g == n_groups - 1, 2, 1))
        qoff = pl.multiple_of(r_first * GRID_W, GRID_W)
        koff = pl.multiple_of(k_first * GRID_W, GRID_W)
        q = q_ref[pl.ds(qoff, nq), :]
        k = k_ref[pl.ds(koff, nk), :]
        v = v_ref[pl.ds(koff, nk), :]
        s = lax.dot_general(q, k, dn, preferred_element_type=F32) + bias_ref[kind]
        sc = lax.dot_general(q, kc, dn, preferred_element_type=F32)
        m = jnp.maximum(jnp.max(s, axis=-1, keepdims=True), jnp.max(sc, axis=-1, keepdims=True))
        p = jnp.exp2(s - m)
        pc = jnp.exp2(sc - m)
        l = jnp.sum(p, axis=-1, keepdims=True) + jnp.sum(pc, axis=-1, keepdims=True)
        o = (jnp.dot(p.astype(BF16), v, preferred_element_type=F32)
             + jnp.dot(pc.astype(BF16), vc, preferred_element_type=F32))
        o_ref[pl.ds(qoff, nq), :] = (o / l).astype(o_ref.dtype)
        return carry

    lax.fori_loop(0, n_groups, body, 0, unroll=2)


def _na_attention(nq, nk, z_main, bias_tab, B, seq, ctx_len, n_lat):
    W = nq.shape[1]
    nh = W // HEAD_DIM
    cb = n_lat // ctx_len
    voff = 5 * nh
    return pl.pallas_call(
        _na_kernel,
        grid=(nh, B),
        in_specs=[pl.BlockSpec((seq, HEAD_DIM), lambda h, b: (b, h)),
                  pl.BlockSpec((seq, HEAD_DIM), lambda h, b: (b, h)),
                  pl.BlockSpec((seq, HEAD_DIM), lambda h, b: (b, voff + h)),
                  pl.BlockSpec((ctx_len, HEAD_DIM), lambda h, b: (cb + b, h)),
                  pl.BlockSpec((ctx_len, HEAD_DIM), lambda h, b: (cb + b, voff + h)),
                  pl.BlockSpec((None, 3, NA_GROUP * GRID_W, NA_KROWS * GRID_W), lambda h, b: (h, 0, 0, 0))],
        out_specs=pl.BlockSpec((seq, HEAD_DIM), lambda h, b: (b, h)),
        out_shape=jax.ShapeDtypeStruct((n_lat, W), BF16),
        compiler_params=_cparams(2),
        name="na_attention",
    )(nq, nk, z_main, nk, z_main, bias_tab)


def _flash_kernel(q_ref, k_ref, v_ref, kc_ref, vc_ref, o_ref, *, tk):
    tq = q_ref.shape[0]
    dv = v_ref.shape[1]
    dn = (((1,), (1,)), ((), ()))
    q = q_ref[...]

    def step(k, v, m, l, acc):
        s = lax.dot_general(q, k, dn, preferred_element_type=F32)
        m_new = jnp.maximum(m, jnp.max(s, axis=-1, keepdims=True))
        alpha = jnp.exp2(m - m_new)
        p = jnp.exp2(s - m_new)
        l = alpha * l + jnp.sum(p, axis=-1, keepdims=True)
        acc = alpha * acc + jnp.dot(p.astype(BF16), v, preferred_element_type=F32)
        return m_new, l, acc

    def body(j, carry):
        off = pl.multiple_of(j * tk, tk)
        return step(k_ref[pl.ds(off, tk), :], v_ref[pl.ds(off, tk), :], *carry)

    init = (jnp.full((tq, 1), NEG, F32), jnp.zeros((tq, 1), F32), jnp.zeros((tq, dv), F32))
    n_chunks = k_ref.shape[0] // tk
    m, l, acc = lax.fori_loop(0, n_chunks, body, init, unroll=min(n_chunks, 8))
    m, l, acc = step(kc_ref[...], vc_ref[...], m, l, acc)
    o_ref[...] = (acc / l).astype(o_ref.dtype)


def _mla_attention(qf, kf, vf, B, seq, ctx_len, n_lat):
    R = qf.shape[0]
    nh = qf.shape[1] // MLA_QPAD
    tq, tk = 1024, 1024
    cb = n_lat // ctx_len
    nqb = seq // tq
    return pl.pallas_call(
        functools.partial(_flash_kernel, tk=tk),
        grid=(B, nh, nqb),
        in_specs=[pl.BlockSpec((tq, MLA_QPAD), lambda b, h, i: (b * nqb + i, h)),
                  pl.BlockSpec((seq, MLA_QPAD), lambda b, h, i: (b, h)),
                  pl.BlockSpec((seq, MLA_V), lambda b, h, i: (b, h)),
                  pl.BlockSpec((ctx_len, MLA_QPAD), lambda b, h, i: (cb + b, h)),
                  pl.BlockSpec((ctx_len, MLA_V), lambda b, h, i: (cb + b, h))],
        out_specs=pl.BlockSpec((tq, MLA_V), lambda b, h, i: (b * nqb + i, h)),
        out_shape=jax.ShapeDtypeStruct((n_lat, nh * MLA_V), BF16),
        compiler_params=_cparams(3),
        name="mla_attention",
    )(qf, kf, vf, kf, vf)


def _ctx_attn_kernel(q_ref, k_ref, v_ref, o_ref):
    s = lax.dot_general(q_ref[...], k_ref[...], (((1,), (1,)), ((), ())), preferred_element_type=F32)
    p = jnp.exp2(s - jnp.max(s, axis=-1, keepdims=True))
    o = jnp.dot(p.astype(BF16), v_ref[...], preferred_element_type=F32)
    o_ref[...] = (o / jnp.sum(p, axis=-1, keepdims=True)).astype(o_ref.dtype)


def _ctx_attention(q, k, v, nh, dq, dv, voff, B, ctx_len, n_lat):
    cb = n_lat // ctx_len
    return pl.pallas_call(
        _ctx_attn_kernel,
        grid=(B, nh),
        in_specs=[pl.BlockSpec((ctx_len, dq), lambda b, h: (cb + b, h)),
                  pl.BlockSpec((ctx_len, dq), lambda b, h: (cb + b, h)),
                  pl.BlockSpec((ctx_len, dv), lambda b, h: (cb + b, voff + h))],
        out_specs=pl.BlockSpec((ctx_len, dv), lambda b, h: (b, h)),
        out_shape=jax.ShapeDtypeStruct((B * ctx_len, nh * dv), BF16),
        compiler_params=_cparams(2),
        name="ctx_attention",
    )(q, k, v)


def _merge_kernel(yc_ref, yn_ref, ym_ref, ync_ref, ymc_ref, gn_ref, gm_ref, o_ref, *, lat_blocks):
    cw = yc_ref.shape[1]
    nw = yn_ref.shape[1]
    lat = pl.program_id(0) < lat_blocks
    o_ref[:, :cw] = yc_ref[...]
    yn = jnp.where(lat, yn_ref[...], ync_ref[...]).astype(F32)
    o_ref[:, cw:cw + nw] = (_rms(yn, nw) * gn_ref[...]).astype(o_ref.dtype)
    ym = jnp.where(lat, ym_ref[...], ymc_ref[...]).astype(F32)
    o_ref[:, cw + nw:] = (_rms(ym, ym.shape[1]) * gm_ref[...]).astype(o_ref.dtype)


def _merge(y_conv, y_na, y_mla, y_na_ctx, y_mla_ctx, gn, gm, rows):
    cw, nw, mw = y_conv.shape[1], y_na.shape[1], y_mla.shape[1]
    tm = ROW_TILE
    lat_blocks = y_na.shape[0] // tm
    ctx_last = y_na_ctx.shape[0] // tm - 1

    def lat_map(i):
        return (jnp.minimum(i, lat_blocks - 1), 0)

    def ctx_map(i):
        return (jnp.clip(i - lat_blocks, 0, ctx_last), 0)

    return pl.pallas_call(
        functools.partial(_merge_kernel, lat_blocks=lat_blocks),
        grid=(rows // tm,),
        in_specs=[pl.BlockSpec((tm, cw), lambda i: (i, 0)),
                  pl.BlockSpec((tm, nw), lat_map), pl.BlockSpec((tm, mw), lat_map),
                  pl.BlockSpec((tm, nw), ctx_map), pl.BlockSpec((tm, mw), ctx_map),
                  pl.BlockSpec((1, nw), lambda i: (0, 0)), pl.BlockSpec((1, mw), lambda i: (0, 0))],
        out_specs=pl.BlockSpec((tm, cw + nw + mw), lambda i: (i, 0)),
        out_shape=jax.ShapeDtypeStruct((rows, cw + nw + mw), BF16),
        compiler_params=_cparams(1),
        name="merge_norm",
    )(y_conv, y_na, y_mla, y_na_ctx, y_mla_ctx, gn, gm)


def _router_kernel(x_ref, g_ref, sc_ref, sh_ref, whi_ref, wlo_ref, rb_ref, tri_ref,
                   h_ref, idx_ref, wgt_ref, rank_ref, cnt_ref, base_ref):
    @pl.when(pl.program_id(0) == 0)
    def _():
        base_ref[...] = jnp.zeros(base_ref.shape, F32)

    x = x_ref[...]
    h = _rms(x, x.shape[-1]) * g_ref[...] * (1.0 + sc_ref[...]) + sh_ref[...]
    h_hi = h.astype(BF16)
    h_ref[...] = h_hi
    h_lo = (h - h_hi.astype(F32)).astype(BF16)
    dn = (((1,), (1,)), ((), ()))
    logits = (lax.dot_general(whi_ref[...], h_hi, dn, preferred_element_type=F32)
              + lax.dot_general(whi_ref[...], h_lo, dn, preferred_element_type=F32)
              + lax.dot_general(wlo_ref[...], h_hi, dn, preferred_element_type=F32))
    E, tm = logits.shape
    gs = E // N_GROUPS
    scores = jax.nn.sigmoid(logits)
    sel = scores + rb_ref[...]
    sel3 = sel.reshape(N_GROUPS, gs, tm)
    io3 = lax.broadcasted_iota(jnp.int32, sel3.shape, 1)
    m1 = jnp.max(sel3, axis=1, keepdims=True)
    i1 = jnp.min(jnp.where(sel3 == m1, io3, gs), axis=1, keepdims=True)
    m2 = jnp.max(jnp.where(io3 == i1, -jnp.inf, sel3), axis=1, keepdims=True)
    grp = m1 + m2
    iog = lax.broadcasted_iota(jnp.int32, grp.shape, 0)
    keep = jnp.zeros(grp.shape, jnp.bool_)
    for _ in range(TOPK_GROUPS):
        gm = jnp.max(grp, axis=0, keepdims=True)
        gi = jnp.min(jnp.where(grp == gm, iog, N_GROUPS), axis=0, keepdims=True)
        hit = iog == gi
        keep = jnp.logical_or(keep, hit)
        grp = jnp.where(hit, -jnp.inf, grp)
    selm = jnp.where(keep, sel3, -jnp.inf).reshape(E, tm)
    ioe = lax.broadcasted_iota(jnp.int32, (E, tm), 0)
    idxs, wgts, hits = [], [], []
    for _ in range(TOP_K):
        m = jnp.max(selm, axis=0, keepdims=True)
        ei = jnp.min(jnp.where(selm == m, ioe, E), axis=0, keepdims=True)
        hit = ioe == ei
        idxs.append(ei)
        hits.append(hit)
        wgts.append(jnp.sum(jnp.where(hit, scores, 0.0), axis=0, keepdims=True))
        selm = jnp.where(hit, -jnp.inf, selm)
    wsum = wgts[0]
    for w in wgts[1:]:
        wsum = wsum + w
    chosen = jnp.zeros((E, tm), F32)
    for hit in hits:
        chosen = chosen + jnp.where(hit, 1.0, 0.0)
    before = base_ref[:, 0:1] + jnp.dot(chosen.astype(BF16), tri_ref[...], preferred_element_type=F32)
    for k in range(TOP_K):
        idx_ref[k:k + 1, :] = idxs[k]
        wgt_ref[k:k + 1, :] = wgts[k] / wsum * ROUTE_SCALE
        rank_ref[k:k + 1, :] = jnp.sum(jnp.where(hits[k], before, 0.0), axis=0, keepdims=True).astype(jnp.int32)
    base_ref[...] = base_ref[...] + jnp.sum(chosen, axis=1, keepdims=True)
    cnt_ref[...] = base_ref[...]


def _router(x, g, mod_l, w_router, router_bias, row0, rows, n_lat, seq):
    D = x.shape[1]
    E = w_router.shape[1]
    tm = ROW_TILE
    b0 = row0 // tm
    mrow = functools.partial(_mod_row, tm=tm, n_lat=n_lat, seq=seq)
    wt = w_router.T
    whi = wt.astype(BF16)
    wlo = (wt - whi.astype(F32)).astype(BF16)
    tri = (np.arange(tm)[:, None] < np.arange(tm)[None, :]).astype(np.float32)
    return pl.pallas_call(
        _router_kernel,
        grid=(rows // tm,),
        in_specs=[pl.BlockSpec((tm, D), lambda i: (i + b0, 0)),
                  pl.BlockSpec((1, D), lambda i: (0, 0)),
                  pl.BlockSpec((None, 1, D), lambda i: (mrow(i + b0) * 6 + 4, 0, 0)),
                  pl.BlockSpec((None, 1, D), lambda i: (mrow(i + b0) * 6 + 3, 0, 0)),
                  pl.BlockSpec((E, D), lambda i: (0, 0)),
                  pl.BlockSpec((E, D), lambda i: (0, 0)),
                  pl.BlockSpec((E, 1), lambda i: (0, 0)),
                  pl.BlockSpec((tm, tm), lambda i: (0, 0))],
        out_specs=[pl.BlockSpec((tm, D), lambda i: (i, 0)),
                   pl.BlockSpec((TOP_K, tm), lambda i: (0, i)),
                   pl.BlockSpec((TOP_K, tm), lambda i: (0, i)),
                   pl.BlockSpec((TOP_K, tm), lambda i: (0, i)),
                   pl.BlockSpec((E, LANE), lambda i: (0, 0))],
        out_shape=[jax.ShapeDtypeStruct((rows, D), BF16),
                   jax.ShapeDtypeStruct((TOP_K, rows), jnp.int32),
                   jax.ShapeDtypeStruct((TOP_K, rows), F32),
                   jax.ShapeDtypeStruct((TOP_K, rows), jnp.int32),
                   jax.ShapeDtypeStruct((E, LANE), F32)],
        scratch_shapes=[pltpu.VMEM((E, LANE), F32)],
        compiler_params=_cparams(1),
        name="router",
    )(x, g.reshape(1, D), mod_l, mod_l, whi, wlo, router_bias.reshape(E, 1), jnp.asarray(tri, BF16))


def _expert_kernel(vt_ref, ve_ref, nv_ref, gs_ref, xs_ref, wg_ref, wu_ref, wd_ref, *rest):
    o_ref, wgb_ref, wub_ref, wdb_ref = rest[-4:]
    v = pl.program_id(0)
    T = xs_ref.shape[0]

    @pl.when(v < nv_ref[0])
    def _():
        e = ve_ref[v]
        tile = vt_ref[v]
        prev = jnp.maximum(v - 1, 0)
        new_expert = jnp.logical_or(v == 0, ve_ref[prev] != e)
        first_visit = jnp.logical_or(v == 0, vt_ref[prev] != tile)

        @pl.when(new_expert)
        def _():
            wgb_ref[...] = wg_ref[...].astype(BF16)
            wub_ref[...] = wu_ref[...].astype(BF16)
            wdb_ref[...] = wd_ref[...].astype(BF16)

        xs = xs_ref[...]
        g = jnp.dot(xs, wgb_ref[...], preferred_element_type=F32)
        u = jnp.dot(xs, wub_ref[...], preferred_element_type=F32)
        a = (g * jax.nn.sigmoid(g) * u).astype(BF16)
        y = jnp.dot(a, wdb_ref[...], preferred_element_type=F32).astype(o_ref.dtype)

        @pl.when(first_visit)
        def _():
            o_ref[...] = y

        @pl.when(jnp.logical_not(first_visit))
        def _():
            row = tile * T + lax.broadcasted_iota(jnp.int32, (T, 1), 0)
            mine = jnp.logical_and(row >= gs_ref[e], row < gs_ref[e + 1])
            o_ref[...] = jnp.where(mine, y, o_ref[...])


def _experts(xs, tile0, n_rows_total, y_prev, visits, group_start, w_gate, w_up, w_down, layer):
    visit_tile, visit_expert, n_visits = visits
    D = xs.shape[1]
    DE = w_gate.shape[3]
    T = MOE_TILE
    in_specs = [pl.BlockSpec((T, D), lambda v, vt, ve, nv, gs: (vt[v] - tile0, 0)),
                pl.BlockSpec((None, None, D, DE), lambda v, vt, ve, nv, gs: (layer, ve[v], 0, 0)),
                pl.BlockSpec((None, None, D, DE), lambda v, vt, ve, nv, gs: (layer, ve[v], 0, 0)),
                pl.BlockSpec((None, None, DE, D), lambda v, vt, ve, nv, gs: (layer, ve[v], 0, 0))]
    args = [visit_tile, visit_expert, n_visits, group_start, xs, w_gate, w_up, w_down]
    aliases = {}
    if y_prev is not None:
        in_specs.append(pl.BlockSpec(memory_space=pl.ANY))
        args.append(y_prev)
        aliases = {len(args) - 1: 0}
    grid_spec = pltpu.PrefetchScalarGridSpec(
        num_scalar_prefetch=4,
        grid=(visit_tile.shape[0],),
        in_specs=in_specs,
        out_specs=pl.BlockSpec((T, D), lambda v, vt, ve, nv, gs: (vt[v], 0)),
        scratch_shapes=[pltpu.VMEM((D, DE), BF16), pltpu.VMEM((D, DE), BF16), pltpu.VMEM((DE, D), BF16)],
    )
    return pl.pallas_call(
        _expert_kernel,
        grid_spec=grid_spec,
        out_shape=jax.ShapeDtypeStruct((n_rows_total, D), BF16),
        input_output_aliases=aliases,
        compiler_params=_cparams(1),
        name="experts",
    )(*args)


def _dispatch_plan(eidx, rank, counts, tile_bounds):
    K, rows = eidx.shape
    E = counts.shape[0]
    T = MOE_TILE
    gend = jnp.cumsum(counts)
    gstart = gend - counts
    eids = jnp.arange(E, dtype=jnp.int32)
    dest = rank + jnp.sum(jnp.where(eidx[None] == eids[:, None, None], gstart[:, None, None], 0), axis=0)
    token = jnp.broadcast_to(jnp.arange(rows, dtype=jnp.int32)[None], (K, rows))
    _, row_token = lax.sort((dest.reshape(-1), token.reshape(-1)), num_keys=1)
    visits = []
    for t_lo, t_hi in zip(tile_bounds[:-1], tile_bounds[1:]):
        first_tile = jnp.maximum(gstart // T, t_lo)
        last_tile = jnp.minimum((gend - 1) // T, t_hi - 1)
        n_vis_e = jnp.where(counts > 0, jnp.maximum(last_tile - first_tile + 1, 0), 0)
        vend = jnp.cumsum(n_vis_e)
        voff = vend - n_vis_e
        n_visits = vend[-1]
        v = jnp.arange(t_hi - t_lo + E, dtype=jnp.int32)
        ve = jnp.minimum(jnp.sum((v[:, None] >= vend[None, :]).astype(jnp.int32), axis=1), E - 1)
        onehot = (ve[:, None] == eids[None, :]).astype(jnp.int32)
        vt = jnp.sum(onehot * (first_tile - voff)[None, :], axis=1) + v
        vt = jnp.where(v < n_visits, vt, t_hi - 1).astype(jnp.int32)
        visits.append((vt, ve.astype(jnp.int32), n_visits.reshape(1).astype(jnp.int32)))
    group_start = jnp.concatenate([gstart, gend[-1:]]).astype(jnp.int32)
    return dest, row_token, visits, group_start


def _shared_kernel(h_ref, wg_ref, wu_ref, wd_ref, y8_ref, gate_ref, x_ref, g2_ref, o_ref):
    h = h_ref[...]
    g = jnp.dot(h, wg_ref[...], preferred_element_type=F32)
    u = jnp.dot(h, wu_ref[...], preferred_element_type=F32)
    a = (g * jax.nn.sigmoid(g) * u).astype(BF16)
    y = jnp.dot(a, wd_ref[...], preferred_element_type=F32)
    for k in range(y8_ref.shape[0]):
        y = y + gate_ref[:, k:k + 1] * y8_ref[k].astype(F32)
    o_ref[...] = x_ref[...] + g2_ref[...] * y


def _shared_combine(h2, wsg, wsu, wsd, y8, gates, x, mod_l, row0, rows, n_lat, seq):
    D = h2.shape[1]
    DE = wsg.shape[1]
    K = y8.shape[0]
    tm = COMBINE_TILE
    b0 = row0 // tm
    mrow = functools.partial(_mod_row, tm=tm, n_lat=n_lat, seq=seq)
    return pl.pallas_call(
        _shared_kernel,
        grid=(rows // tm,),
        in_specs=[pl.BlockSpec((tm, D), lambda i: (i + b0, 0)),
                  pl.BlockSpec((D, DE), lambda i: (0, 0)),
                  pl.BlockSpec((D, DE), lambda i: (0, 0)),
                  pl.BlockSpec((DE, D), lambda i: (0, 0)),
                  pl.BlockSpec((K, tm, D), lambda i: (0, i, 0)),
                  pl.BlockSpec((tm, K), lambda i: (i, 0)),
                  pl.BlockSpec((tm, D), lambda i: (i + b0, 0)),
                  pl.BlockSpec((None, 1, D), lambda i: (mrow(i + b0) * 6 + 5, 0, 0))],
        out_specs=pl.BlockSpec((tm, D), lambda i: (i + b0, 0)),
        out_shape=jax.ShapeDtypeStruct(x.shape, F32),
        input_output_aliases={6: 0},
        compiler_params=_cparams(1),
        name="shared_combine",
    )(h2, wsg, wsu, wsd, y8, gates, x, mod_l)


def _rope_tables(seq, pad_rows):
    t = np.arange(seq)
    row = (t // GRID_W).astype(np.float32)
    col = (t % GRID_W).astype(np.float32)
    n_freq = MLA_ROPE // 4
    inv = (ROPE_BASE ** (-jnp.arange(n_freq, dtype=F32) / n_freq))
    ang = jnp.concatenate([jnp.asarray(row)[:, None] * inv, jnp.asarray(col)[:, None] * inv], axis=-1)
    cos, sin = jnp.cos(ang), jnp.sin(ang)
    half = MLA_ROPE // 2
    z = jnp.zeros((seq, LANE - MLA_ROPE), F32)
    zh = jnp.zeros((seq, half), F32)
    c = jnp.concatenate([cos, cos, z], axis=-1)
    s1 = jnp.concatenate([-sin, zh, z], axis=-1)
    s2 = jnp.concatenate([zh, sin, z], axis=-1)
    ident = jnp.concatenate([jnp.ones((pad_rows, MLA_ROPE), F32), jnp.zeros((pad_rows, LANE - MLA_ROPE), F32)], -1)
    zero = jnp.zeros((pad_rows, LANE), F32)
    return (jnp.concatenate([c, ident], 0), jnp.concatenate([s1, zero], 0), jnp.concatenate([s2, zero], 0))


def _pad_lanes(v, n):
    return jnp.concatenate([v, jnp.zeros((n - v.shape[0],), v.dtype)]).reshape(1, n)


def kernel(x, c, ctx, c_ctx, ada_w, ada_b, norm1_g, w_in, conv_w, na_q_norm, na_k_norm, na_rpb,
           mla_q_lat_norm, mla_kv_lat_norm, w_uq, w_ukv, mla_q_norm, mla_k_norm, out_norm_g, w_o,
           norm2_g, w_router, router_bias, w_gate, w_up, w_down, ws_gate, ws_up, ws_down):
    B, S, D = x.shape
    CTX = ctx.shape[1]
    L = ada_w.shape[0]
    n_lat = B * S
    R = n_lat + B * CTX
    W = D // 4
    kvr = w_ukv.shape[1]
    nh_mla = w_uq.shape[2] // (MLA_NOPE + MLA_ROPE)
    E = w_router.shape[2]
    assert B == 2 and S % ROW_TILE == 0 and (B * CTX) % ROW_TILE == 0 and CTX % CONV_TILE == 0
    assert S % GRID_W == 0 and S // GRID_W >= NA_WIN_H and W % HEAD_DIM == 0 and kvr % LANE == 0
    assert S % 1024 == 0 and (TOP_K * B * CTX) % MOE_TILE == 0 and ROW_TILE % COMBINE_TILE == 0
    assert (S // GRID_W) % NA_GROUP == 0 and S // GRID_W >= NA_KROWS

    xs = jnp.concatenate([x.reshape(n_lat, D), ctx.reshape(B * CTX, D)], axis=0)
    cvecs = jnp.concatenate([c, c_ctx[None]], axis=0)
    mod = _modulation(cvecs, ada_w, ada_b)
    tabs = _rope_tables(S, ROW_TILE)

    for l in range(L):
        last = l == L - 1
        rows = n_lat if last else R
        mod_l = mod[l].reshape(8 * 6, 1, D)

        w_tail = jnp.concatenate([w_in[l, :, 7 * W:], jnp.zeros((D, LANE - MLA_ROPE), F32)], axis=1)
        wq = w_uq[l].reshape(W, nh_mla, MLA_NOPE + MLA_ROPE)
        wq = jnp.concatenate([wq, jnp.zeros((W, nh_mla, MLA_QPAD - MLA_NOPE - MLA_ROPE), F32)], axis=2)
        wq = wq.reshape(W, nh_mla * MLA_QPAD).astype(BF16)
        wkv = w_ukv[l].reshape(kvr, nh_mla, MLA_NOPE + MLA_V)
        wk = wkv[:, :, :MLA_NOPE].reshape(kvr, nh_mla * MLA_NOPE).astype(BF16)
        wv = wkv[:, :, MLA_NOPE:].reshape(kvr, nh_mla * MLA_V).astype(BF16)
        gq_n = mla_q_norm[l][:MLA_NOPE].reshape(1, LANE)
        gq_r = _pad_lanes(mla_q_norm[l][MLA_NOPE:], LANE)
        gk_n = mla_k_norm[l][:MLA_NOPE].reshape(1, LANE)
        gk_r = _pad_lanes(mla_k_norm[l][MLA_NOPE:], LANE)
        og = out_norm_g[l]

        h = _norm_mod(xs, norm1_g[l], mod_l, 1, 0, n_lat, S)
        z_main = _matmul_ws(h, w_in, l, 7 * W, min(512, W), BF16)
        z_tail = _matmul_ws(h, w_tail, None, w_tail.shape[1], w_tail.shape[1], BF16)
        nq, nk, cq, ckv, kr = _post_in(z_main, z_tail, na_q_norm[l].reshape(1, LANE),
                                       na_k_norm[l].reshape(1, LANE), mla_q_lat_norm[l].reshape(1, W),
                                       mla_kv_lat_norm[l].reshape(1, kvr), gk_r, tabs, n_lat, S)
        qf = _uq_matmul(cq, wq, gq_n, gq_r, tabs, n_lat, S)
        kf = _uk_matmul(ckv, wk, gk_n, kr)
        vf = _matmul(ckv, wv, BF16, 512)

        y_conv = _conv_mixer(z_main, conv_w[l], og[:W].reshape(1, W), rows, n_lat, S, CTX)
        y_na = _na_attention(nq, nk, z_main, _na_bias_table(na_rpb[l], S // GRID_W), B, S, CTX, n_lat)
        y_mla = _mla_attention(qf, kf, vf, B, S, CTX, n_lat)
        if last:
            y_na_c, y_mla_c = y_na, y_mla
        else:
            nh_na = W // HEAD_DIM
            y_na_c = _ctx_attention(nq, nk, z_main, nh_na, HEAD_DIM, HEAD_DIM, 5 * nh_na, B, CTX, n_lat)
            y_mla_c = _ctx_attention(qf, kf, vf, nh_mla, MLA_QPAD, MLA_V, 0, B, CTX, n_lat)
        y = _merge(y_conv, y_na, y_mla, y_na_c, y_mla_c,
                   og[W:2 * W].reshape(1, W), og[2 * W:].reshape(1, D - 2 * W), rows)
        xs = _wo_matmul(y, w_o, l, xs, mod_l, rows, n_lat, S)

        wsg, wsu, wsd = ws_gate[l].astype(BF16), ws_up[l].astype(BF16), ws_down[l].astype(BF16)
        h2, eidx, wgt, rank, cnt = _router(xs, norm2_g[l], mod_l, w_router[l], router_bias[l],
                                           0, rows, n_lat, S)
        n_pairs = TOP_K * rows
        n_tiles = n_pairs // MOE_TILE
        tile_bounds = [n_tiles * p // MOE_PARTS for p in range(MOE_PARTS + 1)]
        dest, row_token, visits, gstart = _dispatch_plan(eidx, rank, cnt[:, 0].astype(jnp.int32), tile_bounds)
        y_rows = None
        for p in range(MOE_PARTS):
            part_tokens = row_token[tile_bounds[p] * MOE_TILE:tile_bounds[p + 1] * MOE_TILE]
            gathered = h2.at[part_tokens].get(mode="promise_in_bounds")
            y_rows = _experts(gathered, tile_bounds[p], n_pairs, y_rows, visits[p], gstart,
                              w_gate, w_up, w_down, l)
        gates = wgt.T
        n_blocks = rows // ROW_TILE
        row_bounds = [(n_blocks * p // MOE_PARTS) * ROW_TILE for p in range(MOE_PARTS + 1)]
        for row0, row1 in zip(row_bounds[:-1], row_bounds[1:]):
            y8 = y_rows.at[dest[:, row0:row1].reshape(-1)].get(mode="promise_in_bounds")
            xs = _shared_combine(h2, wsg, wsu, wsd, y8.reshape(TOP_K, row1 - row0, D), gates[row0:row1],
                                 xs, mod_l, row0, row1 - row0, n_lat, S)

    return xs[:n_lat].reshape(B, S, D)
```

```python
import functools

import numpy as np
import jax
import jax.numpy as jnp
from jax import lax
from jax.experimental import pallas as pl
from jax.experimental.pallas import tpu as pltpu

F32 = jnp.float32
BF16 = jnp.bfloat16

GRID_W = 64
HEAD_DIM = 128
MLA_NOPE = 128
MLA_ROPE = 64
MLA_V = 128
MLA_QPAD = 256
NA_WIN_H = 8
NA_WIN_W = 16
NA_GROUP = 8
NA_KROWS = NA_GROUP + NA_WIN_H
N_GROUPS = 8
TOPK_GROUPS = 4
TOP_K = 8
ROUTE_SCALE = 2.5
ROPE_BASE = 10000.0
EPS = 1e-6
NEG = -1e30
LOG2E = 1.4426950408889634

LANE = 128
ROW_TILE = 512
CONV_TILE = 256
MOE_TILE = 256
COMBINE_TILE = 128
MOE_PARTS = 4
VMEM_LIMIT = 56 * 1024 * 1024


def _cparams(n_axes):
    return pltpu.CompilerParams(dimension_semantics=("arbitrary",) * n_axes,
                                vmem_limit_bytes=VMEM_LIMIT)


def _rms(x, n):
    return x * lax.rsqrt(jnp.sum(x * x, axis=-1, keepdims=True) * (1.0 / n) + EPS)


def _mod_kernel(sb_ref, w_ref, b_ref, o_ref):
    tn = w_ref.shape[1]
    o_ref[...] = jnp.zeros(o_ref.shape, F32)
    for cblk in range(tn // LANE):
        wc = w_ref[:, cblk * LANE:(cblk + 1) * LANE]
        for r in range(3):
            acc = jnp.sum(wc * sb_ref[r], axis=0, keepdims=True)
            o_ref[r:r + 1, cblk * LANE:(cblk + 1) * LANE] = acc + b_ref[:, cblk * LANE:(cblk + 1) * LANE]


def _modulation(cvecs, ada_w, ada_b):
    L, D, D6 = ada_w.shape
    tn = 512
    s = cvecs * jax.nn.sigmoid(cvecs)
    sb = jnp.broadcast_to(s[:, :, None], (3, D, LANE))
    return pl.pallas_call(
        _mod_kernel,
        grid=(L, D6 // tn),
        in_specs=[pl.BlockSpec((3, D, LANE), lambda l, j: (0, 0, 0)),
                  pl.BlockSpec((None, D, tn), lambda l, j: (l, 0, j)),
                  pl.BlockSpec((None, 1, tn), lambda l, j: (l, 0, j))],
        out_specs=pl.BlockSpec((None, 8, tn), lambda l, j: (l, 0, j)),
        out_shape=jax.ShapeDtypeStruct((L, 8, D6), F32),
        compiler_params=_cparams(2),
        name="modulation",
    )(sb, ada_w, ada_b.reshape(L, 1, D6))


def _mod_row(i, tm, n_lat, seq):
    r0 = i * tm
    return jnp.where(r0 < n_lat, r0 // seq, n_lat // seq)


def _norm_mod_kernel(x_ref, g_ref, sc_ref, sh_ref, o_ref):
    x = x_ref[...]
    y = _rms(x, x.shape[-1]) * g_ref[...]
    o_ref[...] = (y * (1.0 + sc_ref[...]) + sh_ref[...]).astype(o_ref.dtype)


def _norm_mod(x, g, mod_l, which_sc, which_sh, n_lat, seq):
    R, D = x.shape
    tm = ROW_TILE
    mrow = functools.partial(_mod_row, tm=tm, n_lat=n_lat, seq=seq)
    return pl.pallas_call(
        _norm_mod_kernel,
        grid=(R // tm,),
        in_specs=[pl.BlockSpec((tm, D), lambda i: (i, 0)),
                  pl.BlockSpec((1, D), lambda i: (0, 0)),
                  pl.BlockSpec((None, 1, D), lambda i: (mrow(i) * 6 + which_sc, 0, 0)),
                  pl.BlockSpec((None, 1, D), lambda i: (mrow(i) * 6 + which_sh, 0, 0))],
        out_specs=pl.BlockSpec((tm, D), lambda i: (i, 0)),
        out_shape=jax.ShapeDtypeStruct((R, D), BF16),
        compiler_params=_cparams(1),
        name="norm_mod",
    )(x, g.reshape(1, D), mod_l, mod_l)


def _mm_kernel(x_ref, w_ref, o_ref):
    o_ref[...] = jnp.dot(x_ref[...], w_ref[...], preferred_element_type=F32).astype(o_ref.dtype)


def _matmul(x, w, out_dtype, tn, rows=None):
    R, K = x.shape
    rows = R if rows is None else rows
    N = w.shape[1]
    tm = ROW_TILE
    return pl.pallas_call(
        _mm_kernel,
        grid=(rows // tm, N // tn),
        in_specs=[pl.BlockSpec((tm, K), lambda i, j: (i, 0)),
                  pl.BlockSpec((K, tn), lambda i, j: (0, j))],
        out_specs=pl.BlockSpec((tm, tn), lambda i, j: (i, j)),
        out_shape=jax.ShapeDtypeStruct((rows, N), out_dtype),
        compiler_params=_cparams(2),
        name="matmul",
    )(x, w)


def _mm_ws_kernel(x_ref, w_ref, o_ref, wb_ref):
    @pl.when(pl.program_id(1) == 0)
    def _():
        wb_ref[...] = w_ref[...].astype(BF16)

    o_ref[...] = jnp.dot(x_ref[...], wb_ref[...], preferred_element_type=F32).astype(o_ref.dtype)


def _matmul_ws(x, w, layer, n_cols, tn, out_dtype):
    R, K = x.shape
    tm = ROW_TILE
    if layer is None:
        w_spec = pl.BlockSpec((K, tn), lambda j, i: (0, j))
    else:
        w_spec = pl.BlockSpec((None, K, tn), lambda j, i: (layer, 0, j))
    return pl.pallas_call(
        _mm_ws_kernel,
        grid=(n_cols // tn, R // tm),
        in_specs=[pl.BlockSpec((tm, K), lambda j, i: (i, 0)), w_spec],
        out_specs=pl.BlockSpec((tm, tn), lambda j, i: (i, j)),
        out_shape=jax.ShapeDtypeStruct((R, n_cols), out_dtype),
        scratch_shapes=[pltpu.VMEM((K, tn), BF16)],
        compiler_params=_cparams(2),
        name="matmul_ws",
    )(x, w)


def _rope_rotate(t, c_ref, s1_ref, s2_ref):
    return (t * c_ref[...] + pltpu.roll(t, LANE - MLA_ROPE // 2, axis=1) * s1_ref[...]
            + pltpu.roll(t, MLA_ROPE // 2, axis=1) * s2_ref[...])


def _uq_kernel(x_ref, w_ref, gn_ref, gr_ref, c_ref, s1_ref, s2_ref, o_ref):
    scale = (MLA_NOPE + MLA_ROPE) ** -0.5 * LOG2E
    tm = x_ref.shape[0]
    half = tm // 2
    for r0 in (0, half):
        rs = slice(r0, r0 + half)
        acc = jnp.dot(x_ref[rs, :], w_ref[...], preferred_element_type=F32)
        for h in range(acc.shape[1] // MLA_QPAD):
            o = h * MLA_QPAD
            a = _rms(acc[:, o:o + MLA_NOPE], MLA_NOPE) * gn_ref[...]
            r = _rms(acc[:, o + MLA_NOPE:o + MLA_QPAD], MLA_ROPE) * gr_ref[...]
            r = (r * c_ref[rs, :] + pltpu.roll(r, LANE - MLA_ROPE // 2, axis=1) * s1_ref[rs, :]
                 + pltpu.roll(r, MLA_ROPE // 2, axis=1) * s2_ref[rs, :])
            o_ref[rs, o:o + MLA_NOPE] = (a * scale).astype(o_ref.dtype)
            o_ref[rs, o + MLA_NOPE:o + MLA_QPAD] = (r * scale).astype(o_ref.dtype)


def _rope_block(i, tm, n_lat, seq):
    return jnp.where(i * tm < n_lat, i % (seq // tm), seq // tm)


def _uq_matmul(x, w, gn, gr, tabs, n_lat, seq):
    R, K = x.shape
    N = w.shape[1]
    tm, tn = ROW_TILE, 1024
    rb = functools.partial(_rope_block, tm=tm, n_lat=n_lat, seq=seq)
    tab_spec = pl.BlockSpec((tm, LANE), lambda i, j: (rb(i), 0))
    vec_spec = pl.BlockSpec((1, LANE), lambda i, j: (0, 0))
    return pl.pallas_call(
        _uq_kernel,
        grid=(R // tm, N // tn),
        in_specs=[pl.BlockSpec((tm, K), lambda i, j: (i, 0)),
                  pl.BlockSpec((K, tn), lambda i, j: (0, j)),
                  vec_spec, vec_spec, tab_spec, tab_spec, tab_spec],
        out_specs=pl.BlockSpec((tm, tn), lambda i, j: (i, j)),
        out_shape=jax.ShapeDtypeStruct((R, N), BF16),
        compiler_params=_cparams(2),
        name="uq_matmul",
    )(x, w, gn, gr, *tabs)


def _uk_kernel(x_ref, w_ref, gk_ref, kr_ref, o_ref):
    acc = jnp.dot(x_ref[...], w_ref[...], preferred_element_type=F32)
    for h in range(acc.shape[1] // MLA_NOPE):
        a = _rms(acc[:, h * MLA_NOPE:(h + 1) * MLA_NOPE], MLA_NOPE) * gk_ref[...]
        o_ref[:, h * MLA_QPAD:h * MLA_QPAD + MLA_NOPE] = a.astype(o_ref.dtype)
        o_ref[:, h * MLA_QPAD + MLA_NOPE:(h + 1) * MLA_QPAD] = kr_ref[...]


def _uk_matmul(x, w, gk, kr_pad):
    R, K = x.shape
    N = w.shape[1]
    tm, tn = ROW_TILE, 512
    return pl.pallas_call(
        _uk_kernel,
        grid=(R // tm, N // tn),
        in_specs=[pl.BlockSpec((tm, K), lambda i, j: (i, 0)),
                  pl.BlockSpec((K, tn), lambda i, j: (0, j)),
                  pl.BlockSpec((1, LANE), lambda i, j: (0, 0)),
                  pl.BlockSpec((tm, LANE), lambda i, j: (i, 0))],
        out_specs=pl.BlockSpec((tm, 2 * tn), lambda i, j: (i, j)),
        out_shape=jax.ShapeDtypeStruct((R, 2 * N), BF16),
        compiler_params=_cparams(2),
        name="uk_matmul",
    )(x, w, gk, kr_pad)


def _wo_kernel(y_ref, w_ref, x_ref, g_ref, o_ref, wb_ref):
    @pl.when(pl.program_id(1) == 0)
    def _():
        wb_ref[...] = w_ref[...].astype(BF16)

    acc = jnp.dot(y_ref[...], wb_ref[...], preferred_element_type=F32)
    o_ref[...] = x_ref[...] + g_ref[...] * acc


def _wo_matmul(y, w_o, layer, x, mod_l, rows, n_lat, seq):
    K = y.shape[1]
    N = w_o.shape[2]
    tm, tn = ROW_TILE, 512
    mrow = functools.partial(_mod_row, tm=tm, n_lat=n_lat, seq=seq)
    return pl.pallas_call(
        _wo_kernel,
        grid=(N // tn, rows // tm),
        in_specs=[pl.BlockSpec((tm, K), lambda j, i: (i, 0)),
                  pl.BlockSpec((None, K, tn), lambda j, i: (layer, 0, j)),
                  pl.BlockSpec((tm, tn), lambda j, i: (i, j)),
                  pl.BlockSpec((None, 1, tn), lambda j, i: (mrow(i) * 6 + 2, 0, j))],
        out_specs=pl.BlockSpec((tm, tn), lambda j, i: (i, j)),
        out_shape=jax.ShapeDtypeStruct((rows, N), F32),
        scratch_shapes=[pltpu.VMEM((K, tn), BF16)],
        compiler_params=_cparams(2),
        name="wo_matmul",
    )(y, w_o, x, mod_l)


def _post_in_kernel(nq_ref, nk_ref, cq_ref, tail_ref, gq_ref, gk_ref, gql_ref, gkvl_ref, gkr_ref,
                    c_ref, s1_ref, s2_ref, nq_o, nk_o, cq_o, ckv_o, kr_o):
    nh = nq_ref.shape[1] // HEAD_DIM
    scale = HEAD_DIM ** -0.5 * LOG2E
    for h in range(nh):
        sl = slice(h * HEAD_DIM, (h + 1) * HEAD_DIM)
        q = nq_ref[:, sl].astype(F32)
        nq_o[:, sl] = (_rms(q, HEAD_DIM) * gq_ref[...] * scale).astype(nq_o.dtype)
        k = nk_ref[:, sl].astype(F32)
        nk_o[:, sl] = (_rms(k, HEAD_DIM) * gk_ref[...]).astype(nk_o.dtype)
    cq = cq_ref[...].astype(F32)
    cq_o[...] = (_rms(cq, cq.shape[1]) * gql_ref[...]).astype(cq_o.dtype)
    kvr = ckv_o.shape[1]
    ckv = tail_ref[:, :kvr].astype(F32)
    ckv_o[...] = (_rms(ckv, kvr) * gkvl_ref[...]).astype(ckv_o.dtype)
    kr = tail_ref[:, kvr:kvr + LANE].astype(F32)
    kr = _rms(kr, MLA_ROPE) * gkr_ref[...]
    kr_o[...] = _rope_rotate(kr, c_ref, s1_ref, s2_ref).astype(kr_o.dtype)


def _post_in(z_main, z_tail, gq, gk, gql, gkvl, gkr, tabs, n_lat, seq):
    R = z_main.shape[0]
    W = z_main.shape[1] // 7
    kvr = z_tail.shape[1] - LANE
    tm = ROW_TILE
    rb = functools.partial(_rope_block, tm=tm, n_lat=n_lat, seq=seq)
    tab_spec = pl.BlockSpec((tm, LANE), lambda i: (rb(i), 0))

    def vec(n):
        return pl.BlockSpec((1, n), lambda i: (0, 0))

    def col(j):
        return pl.BlockSpec((tm, W), lambda i: (i, j))

    return pl.pallas_call(
        _post_in_kernel,
        grid=(R // tm,),
        in_specs=[col(3), col(4), col(6), pl.BlockSpec((tm, kvr + LANE), lambda i: (i, 0)),
                  vec(LANE), vec(LANE), vec(W), vec(kvr), vec(LANE), tab_spec, tab_spec, tab_spec],
        out_specs=[pl.BlockSpec((tm, W), lambda i: (i, 0)), pl.BlockSpec((tm, W), lambda i: (i, 0)),
                   pl.BlockSpec((tm, W), lambda i: (i, 0)), pl.BlockSpec((tm, kvr), lambda i: (i, 0)),
                   pl.BlockSpec((tm, LANE), lambda i: (i, 0))],
        out_shape=[jax.ShapeDtypeStruct((R, W), BF16), jax.ShapeDtypeStruct((R, W), BF16),
                   jax.ShapeDtypeStruct((R, W), BF16), jax.ShapeDtypeStruct((R, kvr), BF16),
                   jax.ShapeDtypeStruct((R, LANE), BF16)],
        compiler_params=_cparams(1),
        name="post_in",
    )(z_main, z_main, z_main, z_tail, gq, gk, gql, gkvl, gkr, *tabs)


def _conv_kernel(b_ref, c_ref, u_ref, cp_ref, up_ref, cn_ref, un_ref, w_ref, g_ref, o_ref,
                 *, n_lat, seq, ctx_len):
    tm, W = c_ref.shape
    row0 = pl.program_id(0) * tm
    lat = row0 < n_lat
    pos = jnp.where(lat, row0 % seq, (row0 - n_lat) % ctx_len)
    slen = jnp.where(lat, seq, ctx_len)
    has_prev = (pos > 0).astype(F32)
    has_next = (pos + tm < slen).astype(F32)
    v = c_ref[...].astype(F32) * u_ref[...].astype(F32)
    vprev = cp_ref[7:8, :].astype(F32) * up_ref[7:8, :].astype(F32) * has_prev
    vnext = cn_ref[0:1, :].astype(F32) * un_ref[0:1, :].astype(F32) * has_next
    rid = lax.broadcasted_iota(jnp.int32, (tm, W), 0)
    v_dn = jnp.where(rid == 0, vprev, pltpu.roll(v, 1, axis=0))
    v_up = jnp.where(rid == tm - 1, vnext, pltpu.roll(v, tm - 1, axis=0))
    y = w_ref[0:1, :] * v_dn + w_ref[1:2, :] * v + w_ref[2:3, :] * v_up
    y = b_ref[...].astype(F32) * y
    o_ref[...] = (_rms(y, W) * g_ref[...]).astype(o_ref.dtype)


def _conv_mixer(z_main, conv_w, g, rows, n_lat, seq, ctx_len):
    R = z_main.shape[0]
    W = z_main.shape[1] // 7
    tm = CONV_TILE
    nb8 = R // 8

    def cur(j):
        return pl.BlockSpec((tm, W), lambda i: (i, j))

    def prev(j):
        return pl.BlockSpec((8, W), lambda i: (jnp.maximum(i * (tm // 8) - 1, 0), j))

    def nxt(j):
        return pl.BlockSpec((8, W), lambda i: (jnp.minimum((i + 1) * (tm // 8), nb8 - 1), j))

    return pl.pallas_call(
        functools.partial(_conv_kernel, n_lat=n_lat, seq=seq, ctx_len=ctx_len),
        grid=(rows // tm,),
        in_specs=[cur(0), cur(1), cur(2), prev(1), prev(2), nxt(1), nxt(2),
                  pl.BlockSpec((3, W), lambda i: (0, 0)), pl.BlockSpec((1, W), lambda i: (0, 0))],
        out_specs=pl.BlockSpec((tm, W), lambda i: (i, 0)),
        out_shape=jax.ShapeDtypeStruct((rows, W), BF16),
        compiler_params=_cparams(1),
        name="conv_mixer",
    )(z_main, z_main, z_main, z_main, z_main, z_main, z_main, conv_w, g)


def _na_group_plan(rows):
    n_groups = rows // NA_GROUP
    kinds = [(0, 0), (NA_GROUP, NA_GROUP - NA_WIN_H // 2), ((n_groups - 1) * NA_GROUP, rows - NA_KROWS)]
    plane = np.full((3, NA_GROUP, NA_KROWS), 2 * NA_WIN_H - 1, np.int64)
    for t, (r_first, k_first) in enumerate(kinds):
        for qr in range(NA_GROUP):
            r = r_first + qr
            r0 = min(max(r - NA_WIN_H // 2, 0), rows - NA_WIN_H)
            for kr in range(NA_KROWS):
                kk = k_first + kr
                if r0 <= kk < r0 + NA_WIN_H:
                    plane[t, qr, kr] = kk - r + NA_WIN_H - 1
    return plane


def _na_bias_table(rpb, rows):
    qc = np.arange(GRID_W)[:, None]
    kc = np.arange(GRID_W)[None, :]
    cstart = np.clip(qc - NA_WIN_W // 2, 0, GRID_W - NA_WIN_W)
    ok = (kc >= cstart) & (kc < cstart + NA_WIN_W)
    dc = np.clip(kc - qc + NA_WIN_W - 1, 0, 2 * NA_WIN_W - 2)
    H = rpb.shape[0]
    onehot = (dc[None] == np.arange(2 * NA_WIN_W - 1)[:, None, None]).astype(np.float32)
    cols = jnp.einsum("hdc,cqk->hdqk", rpb.astype(F32), onehot, precision=lax.Precision.HIGHEST)
    cols = jnp.where(ok[None, None], cols * LOG2E, NEG)
    planes = jnp.concatenate([cols, jnp.full((H, 1, GRID_W, GRID_W), NEG, F32)], axis=1)
    pick = (_na_group_plan(rows)[..., None] == np.arange(2 * NA_WIN_H)).astype(np.float32)
    g = jnp.einsum("tqkp,hpxy->htqxky", pick, planes, precision=lax.Precision.HIGHEST)
    return g.reshape(H, 3, NA_GROUP * GRID_W, NA_KROWS * GRID_W)


def _na_kernel(q_ref, k_ref, v_ref, kc_ref, vc_ref, bias_ref, o_ref):
    rows = q_ref.shape[0] // GRID_W
    n_groups = rows // NA_GROUP
    nq = NA_GROUP * GRID_W
    nk = NA_KROWS * GRID_W
    dn = (((1,), (1,)), ((), ()))
    kc = kc_ref[...]
    vc = vc_ref[...]

    def body(g, carry):
        r_first = g * NA_GROUP
        k_first = jnp.clip(r_first - NA_WIN_H // 2, 0, rows - NA_KROWS)
        kind = jnp.where(g == 0, 0, jnp.where(g == n_groups - 1, 2, 1))
        qoff = pl.multiple_of(r_first * GRID_W, GRID_W)
        koff = pl.multiple_of(k_first * GRID_W, GRID_W)
        q = q_ref[pl.ds(qoff, nq), :]
        k = k_ref[pl.ds(koff, nk), :]
        v = v_ref[pl.ds(koff, nk), :]
        s = lax.dot_general(q, k, dn, preferred_element_type=F32) + bias_ref[kind]
        sc = lax.dot_general(q, kc, dn, preferred_element_type=F32)
        m = jnp.maximum(jnp.max(s, axis=-1, keepdims=True), jnp.max(sc, axis=-1, keepdims=True))
        p = jnp.exp2(s - m)
        pc = jnp.exp2(sc - m)
        l = jnp.sum(p, axis=-1, keepdims=True) + jnp.sum(pc, axis=-1, keepdims=True)
        o = (jnp.dot(p.astype(BF16), v, preferred_element_type=F32)
             + jnp.dot(pc.astype(BF16), vc, preferred_element_type=F32))
        o_ref[pl.ds(qoff, nq), :] = (o / l).astype(o_ref.dtype)
        return carry

    lax.fori_loop(0, n_groups, body, 0, unroll=2)


def _na_attention(nq, nk, z_main, bias_tab, B, seq, ctx_len, n_lat):
    W = nq.shape[1]
    nh = W // HEAD_DIM
    cb = n_lat // ctx_len
    voff = 5 * nh
    return pl.pallas_call(
        _na_kernel,
        grid=(nh, B),
        in_specs=[pl.BlockSpec((seq, HEAD_DIM), lambda h, b: (b, h)),
                  pl.BlockSpec((seq, HEAD_DIM), lambda h, b: (b, h)),
                  pl.BlockSpec((seq, HEAD_DIM), lambda h, b: (b, voff + h)),
                  pl.BlockSpec((ctx_len, HEAD_DIM), lambda h, b: (cb + b, h)),
                  pl.BlockSpec((ctx_len, HEAD_DIM), lambda h, b: (cb + b, voff + h)),
                  pl.BlockSpec((None, 3, NA_GROUP * GRID_W, NA_KROWS * GRID_W), lambda h, b: (h, 0, 0, 0))],
        out_specs=pl.BlockSpec((seq, HEAD_DIM), lambda h, b: (b, h)),
        out_shape=jax.ShapeDtypeStruct((n_lat, W), BF16),
        compiler_params=_cparams(2),
        name="na_attention",
    )(nq, nk, z_main, nk, z_main, bias_tab)


def _flash_kernel(q_ref, k_ref, v_ref, kc_ref, vc_ref, o_ref, *, tk):
    tq = q_ref.shape[0]
    dv = v_ref.shape[1]
    dn = (((1,), (1,)), ((), ()))
    q = q_ref[...]

    def step(k, v, m, l, acc):
        s = lax.dot_general(q, k, dn, preferred_element_type=F32)
        m_new = jnp.maximum(m, jnp.max(s, axis=-1, keepdims=True))
        alpha = jnp.exp2(m - m_new)
        p = jnp.exp2(s - m_new)
        l = alpha * l + jnp.sum(p, axis=-1, keepdims=True)
        acc = alpha * acc + jnp.dot(p.astype(BF16), v, preferred_element_type=F32)
        return m_new, l, acc

    def body(j, carry):
        off = pl.multiple_of(j * tk, tk)
        return step(k_ref[pl.ds(off, tk), :], v_ref[pl.ds(off, tk), :], *carry)

    init = (jnp.full((tq, 1), NEG, F32), jnp.zeros((tq, 1), F32), jnp.zeros((tq, dv), F32))
    n_chunks = k_ref.shape[0] // tk
    m, l, acc = lax.fori_loop(0, n_chunks, body, init, unroll=min(n_chunks, 8))
    m, l, acc = step(kc_ref[...], vc_ref[...], m, l, acc)
    o_ref[...] = (acc / l).astype(o_ref.dtype)


def _mla_attention(qf, kf, vf, B, seq, ctx_len, n_lat):
    R = qf.shape[0]
    nh = qf.shape[1] // MLA_QPAD
    tq, tk = 1024, 1024
    cb = n_lat // ctx_len
    nqb = seq // tq
    return pl.pallas_call(
        functools.partial(_flash_kernel, tk=tk),
        grid=(B, nh, nqb),
        in_specs=[pl.BlockSpec((tq, MLA_QPAD), lambda b, h, i: (b * nqb + i, h)),
                  pl.BlockSpec((seq, MLA_QPAD), lambda b, h, i: (b, h)),
                  pl.BlockSpec((seq, MLA_V), lambda b, h, i: (b, h)),
                  pl.BlockSpec((ctx_len, MLA_QPAD), lambda b, h, i: (cb + b, h)),
                  pl.BlockSpec((ctx_len, MLA_V), lambda b, h, i: (cb + b, h))],
        out_specs=pl.BlockSpec((tq, MLA_V), lambda b, h, i: (b * nqb + i, h)),
        out_shape=jax.ShapeDtypeStruct((n_lat, nh * MLA_V), BF16),
        compiler_params=_cparams(3),
        name="mla_attention",
    )(qf, kf, vf, kf, vf)


def _ctx_attn_kernel(q_ref, k_ref, v_ref, o_ref):
    s = lax.dot_general(q_ref[...], k_ref[...], (((1,), (1,)), ((), ())), preferred_element_type=F32)
    p = jnp.exp2(s - jnp.max(s, axis=-1, keepdims=True))
    o = jnp.dot(p.astype(BF16), v_ref[...], preferred_element_type=F32)
    o_ref[...] = (o / jnp.sum(p, axis=-1, keepdims=True)).astype(o_ref.dtype)


def _ctx_attention(q, k, v, nh, dq, dv, voff, B, ctx_len, n_lat):
    cb = n_lat // ctx_len
    return pl.pallas_call(
        _ctx_attn_kernel,
        grid=(B, nh),
        in_specs=[pl.BlockSpec((ctx_len, dq), lambda b, h: (cb + b, h)),
                  pl.BlockSpec((ctx_len, dq), lambda b, h: (cb + b, h)),
                  pl.BlockSpec((ctx_len, dv), lambda b, h: (cb + b, voff + h))],
        out_specs=pl.BlockSpec((ctx_len, dv), lambda b, h: (b, h)),
        out_shape=jax.ShapeDtypeStruct((B * ctx_len, nh * dv), BF16),
        compiler_params=_cparams(2),
        name="ctx_attention",
    )(q, k, v)


def _merge_kernel(yc_ref, yn_ref, ym_ref, ync_ref, ymc_ref, gn_ref, gm_ref, o_ref, *, lat_blocks):
    cw = yc_ref.shape[1]
    nw = yn_ref.shape[1]
    lat = pl.program_id(0) < lat_blocks
    o_ref[:, :cw] = yc_ref[...]
    yn = jnp.where(lat, yn_ref[...], ync_ref[...]).astype(F32)
    o_ref[:, cw:cw + nw] = (_rms(yn, nw) * gn_ref[...]).astype(o_ref.dtype)
    ym = jnp.where(lat, ym_ref[...], ymc_ref[...]).astype(F32)
    o_ref[:, cw + nw:] = (_rms(ym, ym.shape[1]) * gm_ref[...]).astype(o_ref.dtype)


def _merge(y_conv, y_na, y_mla, y_na_ctx, y_mla_ctx, gn, gm, rows):
    cw, nw, mw = y_conv.shape[1], y_na.shape[1], y_mla.shape[1]
    tm = ROW_TILE
    lat_blocks = y_na.shape[0] // tm
    ctx_last = y_na_ctx.shape[0] // tm - 1

    def lat_map(i):
        return (jnp.minimum(i, lat_blocks - 1), 0)

    def ctx_map(i):
        return (jnp.clip(i - lat_blocks, 0, ctx_last), 0)

    return pl.pallas_call(
        functools.partial(_merge_kernel, lat_blocks=lat_blocks),
        grid=(rows // tm,),
        in_specs=[pl.BlockSpec((tm, cw), lambda i: (i, 0)),
                  pl.BlockSpec((tm, nw), lat_map), pl.BlockSpec((tm, mw), lat_map),
                  pl.BlockSpec((tm, nw), ctx_map), pl.BlockSpec((tm, mw), ctx_map),
                  pl.BlockSpec((1, nw), lambda i: (0, 0)), pl.BlockSpec((1, mw), lambda i: (0, 0))],
        out_specs=pl.BlockSpec((tm, cw + nw + mw), lambda i: (i, 0)),
        out_shape=jax.ShapeDtypeStruct((rows, cw + nw + mw), BF16),
        compiler_params=_cparams(1),
        name="merge_norm",
    )(y_conv, y_na, y_mla, y_na_ctx, y_mla_ctx, gn, gm)


def _router_kernel(x_ref, g_ref, sc_ref, sh_ref, whi_ref, wlo_ref, rb_ref, tri_ref,
                   h_ref, idx_ref, wgt_ref, rank_ref, cnt_ref, base_ref):
    @pl.when(pl.program_id(0) == 0)
    def _():
        base_ref[...] = jnp.zeros(base_ref.shape, F32)

    x = x_ref[...]
    h = _rms(x, x.shape[-1]) * g_ref[...] * (1.0 + sc_ref[...]) + sh_ref[...]
    h_hi = h.astype(BF16)
    h_ref[...] = h_hi
    h_lo = (h - h_hi.astype(F32)).astype(BF16)
    dn = (((1,), (1,)), ((), ()))
    logits = (lax.dot_general(whi_ref[...], h_hi, dn, preferred_element_type=F32)
              + lax.dot_general(whi_ref[...], h_lo, dn, preferred_element_type=F32)
              + lax.dot_general(wlo_ref[...], h_hi, dn, preferred_element_type=F32))
    E, tm = logits.shape
    gs = E // N_GROUPS
    scores = jax.nn.sigmoid(logits)
    sel = scores + rb_ref[...]
    sel3 = sel.reshape(N_GROUPS, gs, tm)
    io3 = lax.broadcasted_iota(jnp.int32, sel3.shape, 1)
    m1 = jnp.max(sel3, axis=1, keepdims=True)
    i1 = jnp.min(jnp.where(sel3 == m1, io3, gs), axis=1, keepdims=True)
    m2 = jnp.max(jnp.where(io3 == i1, -jnp.inf, sel3), axis=1, keepdims=True)
    grp = m1 + m2
    iog = lax.broadcasted_iota(jnp.int32, grp.shape, 0)
    keep = jnp.zeros(grp.shape, jnp.bool_)
    for _ in range(TOPK_GROUPS):
        gm = jnp.max(grp, axis=0, keepdims=True)
        gi = jnp.min(jnp.where(grp == gm, iog, N_GROUPS), axis=0, keepdims=True)
        hit = iog == gi
        keep = jnp.logical_or(keep, hit)
        grp = jnp.where(hit, -jnp.inf, grp)
    selm = jnp.where(keep, sel3, -jnp.inf).reshape(E, tm)
    ioe = lax.broadcasted_iota(jnp.int32, (E, tm), 0)
    idxs, wgts, hits = [], [], []
    for _ in range(TOP_K):
        m = jnp.max(selm, axis=0, keepdims=True)
        ei = jnp.min(jnp.where(selm == m, ioe, E), axis=0, keepdims=True)
        hit = ioe == ei
        idxs.append(ei)
        hits.append(hit)
        wgts.append(jnp.sum(jnp.where(hit, scores, 0.0), axis=0, keepdims=True))
        selm = jnp.where(hit, -jnp.inf, selm)
    wsum = wgts[0]
    for w in wgts[1:]:
        wsum = wsum + w
    chosen = jnp.zeros((E, tm), F32)
    for hit in hits:
        chosen = chosen + jnp.where(hit, 1.0, 0.0)
    before = base_ref[:, 0:1] + jnp.dot(chosen.astype(BF16), tri_ref[...], preferred_element_type=F32)
    for k in range(TOP_K):
        idx_ref[k:k + 1, :] = idxs[k]
        wgt_ref[k:k + 1, :] = wgts[k] / wsum * ROUTE_SCALE
        rank_ref[k:k + 1, :] = jnp.sum(jnp.where(hits[k], before, 0.0), axis=0, keepdims=True).astype(jnp.int32)
    base_ref[...] = base_ref[...] + jnp.sum(chosen, axis=1, keepdims=True)
    cnt_ref[...] = base_ref[...]


def _router(x, g, mod_l, w_router, router_bias, row0, rows, n_lat, seq):
    D = x.shape[1]
    E = w_router.shape[1]
    tm = ROW_TILE
    b0 = row0 // tm
    mrow = functools.partial(_mod_row, tm=tm, n_lat=n_lat, seq=seq)
    wt = w_router.T
    whi = wt.astype(BF16)
    wlo = (wt - whi.astype(F32)).astype(BF16)
    tri = (np.arange(tm)[:, None] < np.arange(tm)[None, :]).astype(np.float32)
    return pl.pallas_call(
        _router_kernel,
        grid=(rows // tm,),
        in_specs=[pl.BlockSpec((tm, D), lambda i: (i + b0, 0)),
                  pl.BlockSpec((1, D), lambda i: (0, 0)),
                  pl.BlockSpec((None, 1, D), lambda i: (mrow(i + b0) * 6 + 4, 0, 0)),
                  pl.BlockSpec((None, 1, D), lambda i: (mrow(i + b0) * 6 + 3, 0, 0)),
                  pl.BlockSpec((E, D), lambda i: (0, 0)),
                  pl.BlockSpec((E, D), lambda i: (0, 0)),
                  pl.BlockSpec((E, 1), lambda i: (0, 0)),
                  pl.BlockSpec((tm, tm), lambda i: (0, 0))],
        out_specs=[pl.BlockSpec((tm, D), lambda i: (i, 0)),
                   pl.BlockSpec((TOP_K, tm), lambda i: (0, i)),
                   pl.BlockSpec((TOP_K, tm), lambda i: (0, i)),
                   pl.BlockSpec((TOP_K, tm), lambda i: (0, i)),
                   pl.BlockSpec((E, LANE), lambda i: (0, 0))],
        out_shape=[jax.ShapeDtypeStruct((rows, D), BF16),
                   jax.ShapeDtypeStruct((TOP_K, rows), jnp.int32),
                   jax.ShapeDtypeStruct((TOP_K, rows), F32),
                   jax.ShapeDtypeStruct((TOP_K, rows), jnp.int32),
                   jax.ShapeDtypeStruct((E, LANE), F32)],
        scratch_shapes=[pltpu.VMEM((E, LANE), F32)],
        compiler_params=_cparams(1),
        name="router",
    )(x, g.reshape(1, D), mod_l, mod_l, whi, wlo, router_bias.reshape(E, 1), jnp.asarray(tri, BF16))


def _expert_kernel(vt_ref, ve_ref, nv_ref, gs_ref, xs_ref, wg_ref, wu_ref, wd_ref, *rest):
    o_ref, wgb_ref, wub_ref, wdb_ref = rest[-4:]
    v = pl.program_id(0)
    T = xs_ref.shape[0]

    @pl.when(v < nv_ref[0])
    def _():
        e = ve_ref[v]
        tile = vt_ref[v]
        prev = jnp.maximum(v - 1, 0)
        new_expert = jnp.logical_or(v == 0, ve_ref[prev] != e)
        first_visit = jnp.logical_or(v == 0, vt_ref[prev] != tile)

        @pl.when(new_expert)
        def _():
            wgb_ref[...] = wg_ref[...].astype(BF16)
            wub_ref[...] = wu_ref[...].astype(BF16)
            wdb_ref[...] = wd_ref[...].astype(BF16)

        xs = xs_ref[...]
        g = jnp.dot(xs, wgb_ref[...], preferred_element_type=F32)
        u = jnp.dot(xs, wub_ref[...], preferred_element_type=F32)
        a = (g * jax.nn.sigmoid(g) * u).astype(BF16)
        y = jnp.dot(a, wdb_ref[...], preferred_element_type=F32).astype(o_ref.dtype)

        @pl.when(first_visit)
        def _():
            o_ref[...] = y

        @pl.when(jnp.logical_not(first_visit))
        def _():
            row = tile * T + lax.broadcasted_iota(jnp.int32, (T, 1), 0)
            mine = jnp.logical_and(row >= gs_ref[e], row < gs_ref[e + 1])
            o_ref[...] = jnp.where(mine, y, o_ref[...])


def _experts(xs, tile0, n_rows_total, y_prev, visits, group_start, w_gate, w_up, w_down, layer):
    visit_tile, visit_expert, n_visits = visits
    D = xs.shape[1]
    DE = w_gate.shape[3]
    T = MOE_TILE
    in_specs = [pl.BlockSpec((T, D), lambda v, vt, ve, nv, gs: (vt[v] - tile0, 0)),
                pl.BlockSpec((None, None, D, DE), lambda v, vt, ve, nv, gs: (layer, ve[v], 0, 0)),
                pl.BlockSpec((None, None, D, DE), lambda v, vt, ve, nv, gs: (layer, ve[v], 0, 0)),
                pl.BlockSpec((None, None, DE, D), lambda v, vt, ve, nv, gs: (layer, ve[v], 0, 0))]
    args = [visit_tile, visit_expert, n_visits, group_start, xs, w_gate, w_up, w_down]
    aliases = {}
    if y_prev is not None:
        in_specs.append(pl.BlockSpec(memory_space=pl.ANY))
        args.append(y_prev)
        aliases = {len(args) - 1: 0}
    grid_spec = pltpu.PrefetchScalarGridSpec(
        num_scalar_prefetch=4,
        grid=(visit_tile.shape[0],),
        in_specs=in_specs,
        out_specs=pl.BlockSpec((T, D), lambda v, vt, ve, nv, gs: (vt[v], 0)),
        scratch_shapes=[pltpu.VMEM((D, DE), BF16), pltpu.VMEM((D, DE), BF16), pltpu.VMEM((DE, D), BF16)],
    )
    return pl.pallas_call(
        _expert_kernel,
        grid_spec=grid_spec,
        out_shape=jax.ShapeDtypeStruct((n_rows_total, D), BF16),
        input_output_aliases=aliases,
        compiler_params=_cparams(1),
        name="experts",
    )(*args)


def _dispatch_plan(eidx, rank, counts, tile_bounds):
    K, rows = eidx.shape
    E = counts.shape[0]
    T = MOE_TILE
    gend = jnp.cumsum(counts)
    gstart = gend - counts
    eids = jnp.arange(E, dtype=jnp.int32)
    dest = rank + jnp.sum(jnp.where(eidx[None] == eids[:, None, None], gstart[:, None, None], 0), axis=0)
    token = jnp.broadcast_to(jnp.arange(rows, dtype=jnp.int32)[None], (K, rows))
    _, row_token = lax.sort((dest.reshape(-1), token.reshape(-1)), num_keys=1)
    visits = []
    for t_lo, t_hi in zip(tile_bounds[:-1], tile_bounds[1:]):
        first_tile = jnp.maximum(gstart // T, t_lo)
        last_tile = jnp.minimum((gend - 1) // T, t_hi - 1)
        n_vis_e = jnp.where(counts > 0, jnp.maximum(last_tile - first_tile + 1, 0), 0)
        vend = jnp.cumsum(n_vis_e)
        voff = vend - n_vis_e
        n_visits = vend[-1]
        v = jnp.arange(t_hi - t_lo + E, dtype=jnp.int32)
        ve = jnp.minimum(jnp.sum((v[:, None] >= vend[None, :]).astype(jnp.int32), axis=1), E - 1)
        onehot = (ve[:, None] == eids[None, :]).astype(jnp.int32)
        vt = jnp.sum(onehot * (first_tile - voff)[None, :], axis=1) + v
        vt = jnp.where(v < n_visits, vt, t_hi - 1).astype(jnp.int32)
        visits.append((vt, ve.astype(jnp.int32), n_visits.reshape(1).astype(jnp.int32)))
    group_start = jnp.concatenate([gstart, gend[-1:]]).astype(jnp.int32)
    return dest, row_token, visits, group_start


def _shared_kernel(h_ref, wg_ref, wu_ref, wd_ref, y8_ref, gate_ref, x_ref, g2_ref, o_ref):
    h = h_ref[...]
    g = jnp.dot(h, wg_ref[...], preferred_element_type=F32)
    u = jnp.dot(h, wu_ref[...], preferred_element_type=F32)
    a = (g * jax.nn.sigmoid(g) * u).astype(BF16)
    y = jnp.dot(a, wd_ref[...], preferred_element_type=F32)
    for k in range(y8_ref.shape[0]):
        y = y + gate_ref[:, k:k + 1] * y8_ref[k].astype(F32)
    o_ref[...] = x_ref[...] + g2_ref[...] * y


def _shared_combine(h2, wsg, wsu, wsd, y8, gates, x, mod_l, row0, rows, n_lat, seq):
    D = h2.shape[1]
    DE = wsg.shape[1]
    K = y8.shape[0]
    tm = COMBINE_TILE
    b0 = row0 // tm
    mrow = functools.partial(_mod_row, tm=tm, n_lat=n_lat, seq=seq)
    return pl.pallas_call(
        _shared_kernel,
        grid=(rows // tm,),
        in_specs=[pl.BlockSpec((tm, D), lambda i: (i + b0, 0)),
                  pl.BlockSpec((D, DE), lambda i: (0, 0)),
                  pl.BlockSpec((D, DE), lambda i: (0, 0)),
                  pl.BlockSpec((DE, D), lambda i: (0, 0)),
                  pl.BlockSpec((K, tm, D), lambda i: (0, i, 0)),
                  pl.BlockSpec((tm, K), lambda i: (i, 0)),
                  pl.BlockSpec((tm, D), lambda i: (i + b0, 0)),
                  pl.BlockSpec((None, 1, D), lambda i: (mrow(i + b0) * 6 + 5, 0, 0))],
        out_specs=pl.BlockSpec((tm, D), lambda i: (i + b0, 0)),
        out_shape=jax.ShapeDtypeStruct(x.shape, F32),
        input_output_aliases={6: 0},
        compiler_params=_cparams(1),
        name="shared_combine",
    )(h2, wsg, wsu, wsd, y8, gates, x, mod_l)


def _rope_tables(seq, pad_rows):
    t = np.arange(seq)
    row = (t // GRID_W).astype(np.float32)
    col = (t % GRID_W).astype(np.float32)
    n_freq = MLA_ROPE // 4
    inv = (ROPE_BASE ** (-jnp.arange(n_freq, dtype=F32) / n_freq))
    ang = jnp.concatenate([jnp.asarray(row)[:, None] * inv, jnp.asarray(col)[:, None] * inv], axis=-1)
    cos, sin = jnp.cos(ang), jnp.sin(ang)
    half = MLA_ROPE // 2
    z = jnp.zeros((seq, LANE - MLA_ROPE), F32)
    zh = jnp.zeros((seq, half), F32)
    c = jnp.concatenate([cos, cos, z], axis=-1)
    s1 = jnp.concatenate([-sin, zh, z], axis=-1)
    s2 = jnp.concatenate([zh, sin, z], axis=-1)
    ident = jnp.concatenate([jnp.ones((pad_rows, MLA_ROPE), F32), jnp.zeros((pad_rows, LANE - MLA_ROPE), F32)], -1)
    zero = jnp.zeros((pad_rows, LANE), F32)
    return (jnp.concatenate([c, ident], 0), jnp.concatenate([s1, zero], 0), jnp.concatenate([s2, zero], 0))


def _pad_lanes(v, n):
    return jnp.concatenate([v, jnp.zeros((n - v.shape[0],), v.dtype)]).reshape(1, n)


def kernel(x, c, ctx, c_ctx, ada_w, ada_b, norm1_g, w_in, conv_w, na_q_norm, na_k_norm, na_rpb,
           mla_q_lat_norm, mla_kv_lat_norm, w_uq, w_ukv, mla_q_norm, mla_k_norm, out_norm_g, w_o,
           norm2_g, w_router, router_bias, w_gate, w_up, w_down, ws_gate, ws_up, ws_down):
    B, S, D = x.shape
    CTX = ctx.shape[1]
    L = ada_w.shape[0]
    n_lat = B * S
    R = n_lat + B * CTX
    W = D // 4
    kvr = w_ukv.shape[1]
    nh_mla = w_uq.shape[2] // (MLA_NOPE + MLA_ROPE)
    E = w_router.shape[2]
    assert B == 2 and S % ROW_TILE == 0 and (B * CTX) % ROW_TILE == 0 and CTX % CONV_TILE == 0
    assert S % GRID_W == 0 and S // GRID_W >= NA_WIN_H and W % HEAD_DIM == 0 and kvr % LANE == 0
    assert S % 1024 == 0 and (TOP_K * B * CTX) % MOE_TILE == 0 and ROW_TILE % COMBINE_TILE == 0
    assert (S // GRID_W) % NA_GROUP == 0 and S // GRID_W >= NA_KROWS

    xs = jnp.concatenate([x.reshape(n_lat, D), ctx.reshape(B * CTX, D)], axis=0)
    cvecs = jnp.concatenate([c, c_ctx[None]], axis=0)
    mod = _modulation(cvecs, ada_w, ada_b)
    tabs = _rope_tables(S, ROW_TILE)

    for l in range(L):
        last = l == L - 1
        rows = n_lat if last else R
        mod_l = mod[l].reshape(8 * 6, 1, D)

        w_tail = jnp.concatenate([w_in[l, :, 7 * W:], jnp.zeros((D, LANE - MLA_ROPE), F32)], axis=1)
        wq = w_uq[l].reshape(W, nh_mla, MLA_NOPE + MLA_ROPE)
        wq = jnp.concatenate([wq, jnp.zeros((W, nh_mla, MLA_QPAD - MLA_NOPE - MLA_ROPE), F32)], axis=2)
        wq = wq.reshape(W, nh_mla * MLA_QPAD).astype(BF16)
        wkv = w_ukv[l].reshape(kvr, nh_mla, MLA_NOPE + MLA_V)
        wk = wkv[:, :, :MLA_NOPE].reshape(kvr, nh_mla * MLA_NOPE).astype(BF16)
        wv = wkv[:, :, MLA_NOPE:].reshape(kvr, nh_mla * MLA_V).astype(BF16)
        gq_n = mla_q_norm[l][:MLA_NOPE].reshape(1, LANE)
        gq_r = _pad_lanes(mla_q_norm[l][MLA_NOPE:], LANE)
        gk_n = mla_k_norm[l][:MLA_NOPE].reshape(1, LANE)
        gk_r = _pad_lanes(mla_k_norm[l][MLA_NOPE:], LANE)
        og = out_norm_g[l]

        h = _norm_mod(xs, norm1_g[l], mod_l, 1, 0, n_lat, S)
        z_main = _matmul_ws(h, w_in, l, 7 * W, min(512, W), BF16)
        z_tail = _matmul_ws(h, w_tail, None, w_tail.shape[1], w_tail.shape[1], BF16)
        nq, nk, cq, ckv, kr = _post_in(z_main, z_tail, na_q_norm[l].reshape(1, LANE),
                                       na_k_norm[l].reshape(1, LANE), mla_q_lat_norm[l].reshape(1, W),
                                       mla_kv_lat_norm[l].reshape(1, kvr), gk_r, tabs, n_lat, S)
        qf = _uq_matmul(cq, wq, gq_n, gq_r, tabs, n_lat, S)
        kf = _uk_matmul(ckv, wk, gk_n, kr)
        vf = _matmul(ckv, wv, BF16, 512)

        y_conv = _conv_mixer(z_main, conv_w[l], og[:W].reshape(1, W), rows, n_lat, S, CTX)
        y_na = _na_attention(nq, nk, z_main, _na_bias_table(na_rpb[l], S // GRID_W), B, S, CTX, n_lat)
        y_mla = _mla_attention(qf, kf, vf, B, S, CTX, n_lat)
        if last:
            y_na_c, y_mla_c = y_na, y_mla
        else:
            nh_na = W // HEAD_DIM
            y_na_c = _ctx_attention(nq, nk, z_main, nh_na, HEAD_DIM, HEAD_DIM, 5 * nh_na, B, CTX, n_lat)
            y_mla_c = _ctx_attention(qf, kf, vf, nh_mla, MLA_QPAD, MLA_V, 0, B, CTX, n_lat)
        y = _merge(y_conv, y_na, y_mla, y_na_c, y_mla_c,
                   og[W:2 * W].reshape(1, W), og[2 * W:].reshape(1, D - 2 * W), rows)
        xs = _wo_matmul(y, w_o, l, xs, mod_l, rows, n_lat, S)

        wsg, wsu, wsd = ws_gate[l].astype(BF16), ws_up[l].astype(BF16), ws_down[l].astype(BF16)
        h2, eidx, wgt, rank, cnt = _router(xs, norm2_g[l], mod_l, w_router[l], router_bias[l],
                                           0, rows, n_lat, S)
        n_pairs = TOP_K * rows
        n_tiles = n_pairs // MOE_TILE
        tile_bounds = [n_tiles * p // MOE_PARTS for p in range(MOE_PARTS + 1)]
        dest, row_token, visits, gstart = _dispatch_plan(eidx, rank, cnt[:, 0].astype(jnp.int32), tile_bounds)
        y_rows = None
        for p in range(MOE_PARTS):
            part_tokens = row_token[tile_bounds[p] * MOE_TILE:tile_bounds[p + 1] * MOE_TILE]
            gathered = h2.at[part_tokens].get(mode="promise_in_bounds")
            y_rows = _experts(gathered, tile_bounds[p], n_pairs, y_rows, visits[p], gstart,
                              w_gate, w_up, w_down, l)
        gates = wgt.T
        n_blocks = rows // ROW_TILE
        row_bounds = [(n_blocks * p // MOE_PARTS) * ROW_TILE for p in range(MOE_PARTS + 1)]
        for row0, row1 in zip(row_bounds[:-1], row_bounds[1:]):
            y8 = y_rows.at[dest[:, row0:row1].reshape(-1)].get(mode="promise_in_bounds")
            xs = _shared_combine(h2, wsg, wsu, wsd, y8.reshape(TOP_K, row1 - row0, D), gates[row0:row1],
                                 xs, mod_l, row0, row1 - row0, n_lat, S)

    return xs[:n_lat].reshape(B, S, D)
```

```python
import functools

import numpy as np
import jax
import jax.numpy as jnp
from jax import lax
from jax.experimental import pallas as pl
from jax.experimental.pallas import tpu as pltpu

F32 = jnp.float32
BF16 = jnp.bfloat16

GRID_W = 64
HEAD_DIM = 128
MLA_NOPE = 128
MLA_ROPE = 64
MLA_V = 128
MLA_QPAD = 256
NA_WIN_H = 8
NA_WIN_W = 16
NA_GROUP = 8
NA_KROWS = NA_GROUP + NA_WIN_H
N_GROUPS = 8
TOPK_GROUPS = 4
TOP_K = 8
ROUTE_SCALE = 2.5
ROPE_BASE = 10000.0
EPS = 1e-6
NEG = -1e30
LOG2E = 1.4426950408889634

LANE = 128
ROW_TILE = 512
CONV_TILE = 256
MOE_TILE = 256
COMBINE_TILE = 128
MOE_PARTS = 4
VMEM_LIMIT = 56 * 1024 * 1024


def _cparams(n_axes):
    return pltpu.CompilerParams(dimension_semantics=("arbitrary",) * n_axes,
                                vmem_limit_bytes=VMEM_LIMIT)


def _rms(x, n):
    return x * lax.rsqrt(jnp.sum(x * x, axis=-1, keepdims=True) * (1.0 / n) + EPS)


def _mod_kernel(sb_ref, w_ref, b_ref, o_ref):
    tn = w_ref.shape[1]
    o_ref[...] = jnp.zeros(o_ref.shape, F32)
    for cblk in range(tn // LANE):
        wc = w_ref[:, cblk * LANE:(cblk + 1) * LANE]
        for r in range(3):
            acc = jnp.sum(wc * sb_ref[r], axis=0, keepdims=True)
            o_ref[r:r + 1, cblk * LANE:(cblk + 1) * LANE] = acc + b_ref[:, cblk * LANE:(cblk + 1) * LANE]


def _modulation(cvecs, ada_w, ada_b):
    L, D, D6 = ada_w.shape
    tn = 512
    s = cvecs * jax.nn.sigmoid(cvecs)
    sb = jnp.broadcast_to(s[:, :, None], (3, D, LANE))
    return pl.pallas_call(
        _mod_kernel,
        grid=(L, D6 // tn),
        in_specs=[pl.BlockSpec((3, D, LANE), lambda l, j: (0, 0, 0)),
                  pl.BlockSpec((None, D, tn), lambda l, j: (l, 0, j)),
                  pl.BlockSpec((None, 1, tn), lambda l, j: (l, 0, j))],
        out_specs=pl.BlockSpec((None, 8, tn), lambda l, j: (l, 0, j)),
        out_shape=jax.ShapeDtypeStruct((L, 8, D6), F32),
        compiler_params=_cparams(2),
        name="modulation",
    )(sb, ada_w, ada_b.reshape(L, 1, D6))


def _mod_row(i, tm, n_lat, seq):
    r0 = i * tm
    return jnp.where(r0 < n_lat, r0 // seq, n_lat // seq)


def _split_rows(x_pair, tm):
    head, tail = x_pair
    head_blocks = head.shape[0] // tm
    tail_last = tail.shape[0] // tm - 1
    return (head_blocks, lambda i: jnp.minimum(i, head_blocks - 1),
            lambda i: jnp.clip(i - head_blocks, 0, tail_last))


def _norm_mod_kernel(xa_ref, xb_ref, g_ref, sc_ref, sh_ref, o_ref, *, head_blocks):
    x = jnp.where(pl.program_id(0) < head_blocks, xa_ref[...], xb_ref[...])
    y = _rms(x, x.shape[-1]) * g_ref[...]
    o_ref[...] = (y * (1.0 + sc_ref[...]) + sh_ref[...]).astype(o_ref.dtype)


def _norm_mod(x_pair, R, g, mod_l, which_sc, which_sh, n_lat, seq):
    D = x_pair[0].shape[1]
    tm = ROW_TILE
    mrow = functools.partial(_mod_row, tm=tm, n_lat=n_lat, seq=seq)
    head_blocks, head_map, tail_map = _split_rows(x_pair, tm)
    return pl.pallas_call(
        functools.partial(_norm_mod_kernel, head_blocks=head_blocks),
        grid=(R // tm,),
        in_specs=[pl.BlockSpec((tm, D), lambda i: (head_map(i), 0)),
                  pl.BlockSpec((tm, D), lambda i: (tail_map(i), 0)),
                  pl.BlockSpec((1, D), lambda i: (0, 0)),
                  pl.BlockSpec((None, 1, D), lambda i: (mrow(i) * 6 + which_sc, 0, 0)),
                  pl.BlockSpec((None, 1, D), lambda i: (mrow(i) * 6 + which_sh, 0, 0))],
        out_specs=pl.BlockSpec((tm, D), lambda i: (i, 0)),
        out_shape=jax.ShapeDtypeStruct((R, D), BF16),
        compiler_params=_cparams(1),
        name="norm_mod",
    )(x_pair[0], x_pair[1], g.reshape(1, D), mod_l, mod_l)


def _mm_kernel(x_ref, w_ref, o_ref):
    o_ref[...] = jnp.dot(x_ref[...], w_ref[...], preferred_element_type=F32).astype(o_ref.dtype)


def _matmul(x, w, out_dtype, tn, rows=None):
    R, K = x.shape
    rows = R if rows is None else rows
    N = w.shape[1]
    tm = ROW_TILE
    return pl.pallas_call(
        _mm_kernel,
        grid=(rows // tm, N // tn),
        in_specs=[pl.BlockSpec((tm, K), lambda i, j: (i, 0)),
                  pl.BlockSpec((K, tn), lambda i, j: (0, j))],
        out_specs=pl.BlockSpec((tm, tn), lambda i, j: (i, j)),
        out_shape=jax.ShapeDtypeStruct((rows, N), out_dtype),
        compiler_params=_cparams(2),
        name="matmul",
    )(x, w)


def _mm_ws_kernel(x_ref, w_ref, o_ref, wb_ref):
    @pl.when(pl.program_id(1) == 0)
    def _():
        wb_ref[...] = w_ref[...].astype(BF16)

    o_ref[...] = jnp.dot(x_ref[...], wb_ref[...], preferred_element_type=F32).astype(o_ref.dtype)


def _matmul_ws(x, w, layer, n_cols, tn, out_dtype):
    R, K = x.shape
    tm = ROW_TILE
    if layer is None:
        w_spec = pl.BlockSpec((K, tn), lambda j, i: (0, j))
    else:
        w_spec = pl.BlockSpec((None, K, tn), lambda j, i: (layer, 0, j))
    return pl.pallas_call(
        _mm_ws_kernel,
        grid=(n_cols // tn, R // tm),
        in_specs=[pl.BlockSpec((tm, K), lambda j, i: (i, 0)), w_spec],
        out_specs=pl.BlockSpec((tm, tn), lambda j, i: (i, j)),
        out_shape=jax.ShapeDtypeStruct((R, n_cols), out_dtype),
        scratch_shapes=[pltpu.VMEM((K, tn), BF16)],
        compiler_params=_cparams(2),
        name="matmul_ws",
    )(x, w)


def _rope_rotate(t, c_ref, s1_ref, s2_ref):
    return (t * c_ref[...] + pltpu.roll(t, LANE - MLA_ROPE // 2, axis=1) * s1_ref[...]
            + pltpu.roll(t, MLA_ROPE // 2, axis=1) * s2_ref[...])


def _uq_kernel(x_ref, w_ref, gn_ref, gr_ref, c_ref, s1_ref, s2_ref, o_ref):
    scale = (MLA_NOPE + MLA_ROPE) ** -0.5 * LOG2E
    tm = x_ref.shape[0]
    half = tm // 2
    for r0 in (0, half):
        rs = slice(r0, r0 + half)
        acc = jnp.dot(x_ref[rs, :], w_ref[...], preferred_element_type=F32)
        for h in range(acc.shape[1] // MLA_QPAD):
            o = h * MLA_QPAD
            a = _rms(acc[:, o:o + MLA_NOPE], MLA_NOPE) * gn_ref[...]
            r = _rms(acc[:, o + MLA_NOPE:o + MLA_QPAD], MLA_ROPE) * gr_ref[...]
            r = (r * c_ref[rs, :] + pltpu.roll(r, LANE - MLA_ROPE // 2, axis=1) * s1_ref[rs, :]
                 + pltpu.roll(r, MLA_ROPE // 2, axis=1) * s2_ref[rs, :])
            o_ref[rs, o:o + MLA_NOPE] = (a * scale).astype(o_ref.dtype)
            o_ref[rs, o + MLA_NOPE:o + MLA_QPAD] = (r * scale).astype(o_ref.dtype)


def _rope_block(i, tm, n_lat, seq):
    return jnp.where(i * tm < n_lat, i % (seq // tm), seq // tm)


def _uq_matmul(x, w, gn, gr, tabs, n_lat, seq):
    R, K = x.shape
    N = w.shape[1]
    tm, tn = ROW_TILE, 1024
    rb = functools.partial(_rope_block, tm=tm, n_lat=n_lat, seq=seq)
    tab_spec = pl.BlockSpec((tm, LANE), lambda i, j: (rb(i), 0))
    vec_spec = pl.BlockSpec((1, LANE), lambda i, j: (0, 0))
    return pl.pallas_call(
        _uq_kernel,
        grid=(R // tm, N // tn),
        in_specs=[pl.BlockSpec((tm, K), lambda i, j: (i, 0)),
                  pl.BlockSpec((K, tn), lambda i, j: (0, j)),
                  vec_spec, vec_spec, tab_spec, tab_spec, tab_spec],
        out_specs=pl.BlockSpec((tm, tn), lambda i, j: (i, j)),
        out_shape=jax.ShapeDtypeStruct((R, N), BF16),
        compiler_params=_cparams(2),
        name="uq_matmul",
    )(x, w, gn, gr, *tabs)


def _uk_kernel(x_ref, w_ref, gk_ref, kr_ref, o_ref):
    half = x_ref.shape[0] // 2
    for r0 in (0, half):
        rs = slice(r0, r0 + half)
        acc = jnp.dot(x_ref[rs, :], w_ref[...], preferred_element_type=F32)
        for h in range(acc.shape[1] // MLA_NOPE):
            a = _rms(acc[:, h * MLA_NOPE:(h + 1) * MLA_NOPE], MLA_NOPE) * gk_ref[...]
            o_ref[rs, h * MLA_QPAD:h * MLA_QPAD + MLA_NOPE] = a.astype(o_ref.dtype)
            o_ref[rs, h * MLA_QPAD + MLA_NOPE:(h + 1) * MLA_QPAD] = kr_ref[rs, :]


def _uk_matmul(x, w, gk, kr_pad):
    R, K = x.shape
    N = w.shape[1]
    tm, tn = ROW_TILE, 512
    return pl.pallas_call(
        _uk_kernel,
        grid=(R // tm, N // tn),
        in_specs=[pl.BlockSpec((tm, K), lambda i, j: (i, 0)),
                  pl.BlockSpec((K, tn), lambda i, j: (0, j)),
                  pl.BlockSpec((1, LANE), lambda i, j: (0, 0)),
                  pl.BlockSpec((tm, LANE), lambda i, j: (i, 0))],
        out_specs=pl.BlockSpec((tm, 2 * tn), lambda i, j: (i, j)),
        out_shape=jax.ShapeDtypeStruct((R, 2 * N), BF16),
        compiler_params=_cparams(2),
        name="uk_matmul",
    )(x, w, gk, kr_pad)


def _wo_kernel(y_ref, w_ref, xa_ref, xb_ref, g_ref, o_ref, wb_ref, *, head_blocks):
    @pl.when(pl.program_id(1) == 0)
    def _():
        wb_ref[...] = w_ref[...].astype(BF16)

    acc = jnp.dot(y_ref[...], wb_ref[...], preferred_element_type=F32)
    x = jnp.where(pl.program_id(1) < head_blocks, xa_ref[...], xb_ref[...])
    o_ref[...] = x + g_ref[...] * acc


def _wo_matmul(y, w_o, layer, x_pair, mod_l, rows, n_lat, seq):
    K = y.shape[1]
    N = w_o.shape[2]
    tm, tn = ROW_TILE, 512
    mrow = functools.partial(_mod_row, tm=tm, n_lat=n_lat, seq=seq)
    head_blocks, head_map, tail_map = _split_rows(x_pair, tm)
    return pl.pallas_call(
        functools.partial(_wo_kernel, head_blocks=head_blocks),
        grid=(N // tn, rows // tm),
        in_specs=[pl.BlockSpec((tm, K), lambda j, i: (i, 0)),
                  pl.BlockSpec((None, K, tn), lambda j, i: (layer, 0, j)),
                  pl.BlockSpec((tm, tn), lambda j, i: (head_map(i), j)),
                  pl.BlockSpec((tm, tn), lambda j, i: (tail_map(i), j)),
                  pl.BlockSpec((None, 1, tn), lambda j, i: (mrow(i) * 6 + 2, 0, j))],
        out_specs=pl.BlockSpec((tm, tn), lambda j, i: (i, j)),
        out_shape=jax.ShapeDtypeStruct((rows, N), F32),
        scratch_shapes=[pltpu.VMEM((K, tn), BF16)],
        compiler_params=_cparams(2),
        name="wo_matmul",
    )(y, w_o, x_pair[0], x_pair[1], mod_l)


def _post_in_kernel(nq_ref, nk_ref, cq_ref, tail_ref, gq_ref, gk_ref, gql_ref, gkvl_ref, gkr_ref,
                    c_ref, s1_ref, s2_ref, nq_o, nk_o, cq_o, ckv_o, kr_o):
    nh = nq_ref.shape[1] // HEAD_DIM
    scale = HEAD_DIM ** -0.5 * LOG2E
    for h in range(nh):
        sl = slice(h * HEAD_DIM, (h + 1) * HEAD_DIM)
        q = nq_ref[:, sl].astype(F32)
        nq_o[:, sl] = (_rms(q, HEAD_DIM) * gq_ref[...] * scale).astype(nq_o.dtype)
        k = nk_ref[:, sl].astype(F32)
        nk_o[:, sl] = (_rms(k, HEAD_DIM) * gk_ref[...]).astype(nk_o.dtype)
    cq = cq_ref[...].astype(F32)
    cq_o[...] = (_rms(cq, cq.shape[1]) * gql_ref[...]).astype(cq_o.dtype)
    kvr = ckv_o.shape[1]
    ckv = tail_ref[:, :kvr].astype(F32)
    ckv_o[...] = (_rms(ckv, kvr) * gkvl_ref[...]).astype(ckv_o.dtype)
    kr = tail_ref[:, kvr:kvr + LANE].astype(F32)
    kr = _rms(kr, MLA_ROPE) * gkr_ref[...]
    kr_o[...] = _rope_rotate(kr, c_ref, s1_ref, s2_ref).astype(kr_o.dtype)


def _post_in(z_main, z_tail, gq, gk, gql, gkvl, gkr, tabs, n_lat, seq):
    R = z_main.shape[0]
    W = z_main.shape[1] // 7
    kvr = z_tail.shape[1] - LANE
    tm = ROW_TILE
    rb = functools.partial(_rope_block, tm=tm, n_lat=n_lat, seq=seq)
    tab_spec = pl.BlockSpec((tm, LANE), lambda i: (rb(i), 0))

    def vec(n):
        return pl.BlockSpec((1, n), lambda i: (0, 0))

    def col(j):
        return pl.BlockSpec((tm, W), lambda i: (i, j))

    return pl.pallas_call(
        _post_in_kernel,
        grid=(R // tm,),
        in_specs=[col(3), col(4), col(6), pl.BlockSpec((tm, kvr + LANE), lambda i: (i, 0)),
                  vec(LANE), vec(LANE), vec(W), vec(kvr), vec(LANE), tab_spec, tab_spec, tab_spec],
        out_specs=[pl.BlockSpec((tm, W), lambda i: (i, 0)), pl.BlockSpec((tm, W), lambda i: (i, 0)),
                   pl.BlockSpec((tm, W), lambda i: (i, 0)), pl.BlockSpec((tm, kvr), lambda i: (i, 0)),
                   pl.BlockSpec((tm, LANE), lambda i: (i, 0))],
        out_shape=[jax.ShapeDtypeStruct((R, W), BF16), jax.ShapeDtypeStruct((R, W), BF16),
                   jax.ShapeDtypeStruct((R, W), BF16), jax.ShapeDtypeStruct((R, kvr), BF16),
                   jax.ShapeDtypeStruct((R, LANE), BF16)],
        compiler_params=_cparams(1),
        name="post_in",
    )(z_main, z_main, z_main, z_tail, gq, gk, gql, gkvl, gkr, *tabs)


def _conv_kernel(b_ref, c_ref, u_ref, cp_ref, up_ref, cn_ref, un_ref, w_ref, g_ref, o_ref,
                 *, n_lat, seq, ctx_len):
    tm, W = c_ref.shape
    row0 = pl.program_id(0) * tm
    lat = row0 < n_lat
    pos = jnp.where(lat, row0 % seq, (row0 - n_lat) % ctx_len)
    slen = jnp.where(lat, seq, ctx_len)
    has_prev = (pos > 0).astype(F32)
    has_next = (pos + tm < slen).astype(F32)
    v = c_ref[...].astype(F32) * u_ref[...].astype(F32)
    vprev = cp_ref[7:8, :].astype(F32) * up_ref[7:8, :].astype(F32) * has_prev
    vnext = cn_ref[0:1, :].astype(F32) * un_ref[0:1, :].astype(F32) * has_next
    rid = lax.broadcasted_iota(jnp.int32, (tm, W), 0)
    v_dn = jnp.where(rid == 0, vprev, pltpu.roll(v, 1, axis=0))
    v_up = jnp.where(rid == tm - 1, vnext, pltpu.roll(v, tm - 1, axis=0))
    y = w_ref[0:1, :] * v_dn + w_ref[1:2, :] * v + w_ref[2:3, :] * v_up
    y = b_ref[...].astype(F32) * y
    o_ref[...] = (_rms(y, W) * g_ref[...]).astype(o_ref.dtype)


def _conv_mixer(z_main, conv_w, g, rows, n_lat, seq, ctx_len):
    R = z_main.shape[0]
    W = z_main.shape[1] // 7
    tm = CONV_TILE
    nb8 = R // 8

    def cur(j):
        return pl.BlockSpec((tm, W), lambda i: (i, j))

    def prev(j):
        return pl.BlockSpec((8, W), lambda i: (jnp.maximum(i * (tm // 8) - 1, 0), j))

    def nxt(j):
        return pl.BlockSpec((8, W), lambda i: (jnp.minimum((i + 1) * (tm // 8), nb8 - 1), j))

    return pl.pallas_call(
        functools.partial(_conv_kernel, n_lat=n_lat, seq=seq, ctx_len=ctx_len),
        grid=(rows // tm,),
        in_specs=[cur(0), cur(1), cur(2), prev(1), prev(2), nxt(1), nxt(2),
                  pl.BlockSpec((3, W), lambda i: (0, 0)), pl.BlockSpec((1, W), lambda i: (0, 0))],
        out_specs=pl.BlockSpec((tm, W), lambda i: (i, 0)),
        out_shape=jax.ShapeDtypeStruct((rows, W), BF16),
        compiler_params=_cparams(1),
        name="conv_mixer",
    )(z_main, z_main, z_main, z_main, z_main, z_main, z_main, conv_w, g)


def _na_group_plan(rows):
    n_groups = rows // NA_GROUP
    kinds = [(0, 0), (NA_GROUP, NA_GROUP - NA_WIN_H // 2), ((n_groups - 1) * NA_GROUP, rows - NA_KROWS)]
    plane = np.full((3, NA_GROUP, NA_KROWS), 2 * NA_WIN_H - 1, np.int64)
    for t, (r_first, k_first) in enumerate(kinds):
        for qr in range(NA_GROUP):
            r = r_first + qr
            r0 = min(max(r - NA_WIN_H // 2, 0), rows - NA_WIN_H)
            for kr in range(NA_KROWS):
                kk = k_first + kr
                if r0 <= kk < r0 + NA_WIN_H:
                    plane[t, qr, kr] = kk - r + NA_WIN_H - 1
    return plane


def _na_bias_table(rpb, rows):
    qc = np.arange(GRID_W)[:, None]
    kc = np.arange(GRID_W)[None, :]
    cstart = np.clip(qc - NA_WIN_W // 2, 0, GRID_W - NA_WIN_W)
    ok = (kc >= cstart) & (kc < cstart + NA_WIN_W)
    dc = np.clip(kc - qc + NA_WIN_W - 1, 0, 2 * NA_WIN_W - 2)
    H = rpb.shape[0]
    onehot = (dc[None] == np.arange(2 * NA_WIN_W - 1)[:, None, None]).astype(np.float32)
    cols = jnp.einsum("hdc,cqk->hdqk", rpb.astype(F32), onehot, precision=lax.Precision.HIGHEST)
    cols = jnp.where(ok[None, None], cols * LOG2E, NEG)
    planes = jnp.concatenate([cols, jnp.full((H, 1, GRID_W, GRID_W), NEG, F32)], axis=1)
    pick = (_na_group_plan(rows)[..., None] == np.arange(2 * NA_WIN_H)).astype(np.float32)
    g = jnp.einsum("tqkp,hpxy->htqxky", pick, planes, precision=lax.Precision.HIGHEST)
    return g.reshape(H, 3, NA_GROUP * GRID_W, NA_KROWS * GRID_W)


def _na_kernel(q_ref, k_ref, v_ref, kc_ref, vc_ref, bias_ref, o_ref):
    rows = q_ref.shape[0] // GRID_W
    n_groups = rows // NA_GROUP
    nq = NA_GROUP * GRID_W
    nk = NA_KROWS * GRID_W
    dn = (((1,), (1,)), ((), ()))
    kc = kc_ref[...]
    vc = vc_ref[...]

    def body(g, carry):
        r_first = g * NA_GROUP
        k_first = jnp.clip(r_first - NA_WIN_H // 2, 0, rows - NA_KROWS)
        kind = jnp.where(g == 0, 0, jnp.where(g == n_groups - 1, 2, 1))
        qoff = pl.multiple_of(r_first * GRID_W, GRID_W)
        koff = pl.multiple_of(k_first * GRID_W, GRID_W)
        q = q_ref[pl.ds(qoff, nq), :]
        k = k_ref[pl.ds(koff, nk), :]
        v = v_ref[pl.ds(koff, nk), :]
        s = lax.dot_general(q, k, dn, preferred_element_type=F32) + bias_ref[kind]
        sc = lax.dot_general(q, kc, dn, preferred_element_type=F32)
        m = jnp.maximum(jnp.max(s, axis=-1, keepdims=True), jnp.max(sc, axis=-1, keepdims=True))
        p = jnp.exp2(s - m)
        pc = jnp.exp2(sc - m)
        l = jnp.sum(p, axis=-1, keepdims=True) + jnp.sum(pc, axis=-1, keepdims=True)
        o = (jnp.dot(p.astype(BF16), v, preferred_element_type=F32)
             + jnp.dot(pc.astype(BF16), vc, preferred_element_type=F32))
        o_ref[pl.ds(qoff, nq), :] = (o / l).astype(o_ref.dtype)
        return carry

    lax.fori_loop(0, n_groups, body, 0, unroll=2)


def _na_attention(nq, nk, z_main, bias_tab, B, seq, ctx_len, n_lat):
    W = nq.shape[1]
    nh = W // HEAD_DIM
    cb = n_lat // ctx_len
    voff = 5 * nh
    return pl.pallas_call(
        _na_kernel,
        grid=(nh, B),
        in_specs=[pl.BlockSpec((seq, HEAD_DIM), lambda h, b: (b, h)),
                  pl.BlockSpec((seq, HEAD_DIM), lambda h, b: (b, h)),
                  pl.BlockSpec((seq, HEAD_DIM), lambda h, b: (b, voff + h)),
                  pl.BlockSpec((ctx_len, HEAD_DIM), lambda h, b: (cb + b, h)),
                  pl.BlockSpec((ctx_len, HEAD_DIM), lambda h, b: (cb + b, voff + h)),
                  pl.BlockSpec((None, 3, NA_GROUP * GRID_W, NA_KROWS * GRID_W), lambda h, b: (h, 0, 0, 0))],
        out_specs=pl.BlockSpec((seq, HEAD_DIM), lambda h, b: (b, h)),
        out_shape=jax.ShapeDtypeStruct((n_lat, W), BF16),
        compiler_params=_cparams(2),
        name="na_attention",
    )(nq, nk, z_main, nk, z_main, bias_tab)


def _flash_kernel(q_ref, k_ref, v_ref, kc_ref, vc_ref, o_ref, *, tk):
    tq = q_ref.shape[0]
    dv = v_ref.shape[1]
    dn = (((1,), (1,)), ((), ()))
    q = q_ref[...]

    def step(k, v, m, l, acc):
        s = lax.dot_general(q, k, dn, preferred_element_type=F32)
        m_new = jnp.maximum(m, jnp.max(s, axis=-1, keepdims=True))
        alpha = jnp.exp2(m - m_new)
        p = jnp.exp2(s - m_new)
        l = alpha * l + jnp.sum(p, axis=-1, keepdims=True)
        acc = alpha * acc + jnp.dot(p.astype(BF16), v, preferred_element_type=F32)
        return m_new, l, acc

    def body(j, carry):
        off = pl.multiple_of(j * tk, tk)
        return step(k_ref[pl.ds(off, tk), :], v_ref[pl.ds(off, tk), :], *carry)

    init = (jnp.full((tq, 1), NEG, F32), jnp.zeros((tq, 1), F32), jnp.zeros((tq, dv), F32))
    n_chunks = k_ref.shape[0] // tk
    m, l, acc = lax.fori_loop(0, n_chunks, body, init, unroll=min(n_chunks, 8))
    m, l, acc = step(kc_ref[...], vc_ref[...], m, l, acc)
    o_ref[...] = (acc / l).astype(o_ref.dtype)


def _mla_attention(qf, kf, vf, B, seq, ctx_len, n_lat):
    R = qf.shape[0]
    nh = qf.shape[1] // MLA_QPAD
    tq, tk = 1024, 1024
    cb = n_lat // ctx_len
    nqb = seq // tq
    return pl.pallas_call(
        functools.partial(_flash_kernel, tk=tk),
        grid=(B, nh, nqb),
        in_specs=[pl.BlockSpec((tq, MLA_QPAD), lambda b, h, i: (b * nqb + i, h)),
                  pl.BlockSpec((seq, MLA_QPAD), lambda b, h, i: (b, h)),
                  pl.BlockSpec((seq, MLA_V), lambda b, h, i: (b, h)),
                  pl.BlockSpec((ctx_len, MLA_QPAD), lambda b, h, i: (cb + b, h)),
                  pl.BlockSpec((ctx_len, MLA_V), lambda b, h, i: (cb + b, h))],
        out_specs=pl.BlockSpec((tq, MLA_V), lambda b, h, i: (b * nqb + i, h)),
        out_shape=jax.ShapeDtypeStruct((n_lat, nh * MLA_V), BF16),
        compiler_params=_cparams(3),
        name="mla_attention",
    )(qf, kf, vf, kf, vf)


def _ctx_attn_kernel(q_ref, k_ref, v_ref, o_ref):
    s = lax.dot_general(q_ref[...], k_ref[...], (((1,), (1,)), ((), ())), preferred_element_type=F32)
    p = jnp.exp2(s - jnp.max(s, axis=-1, keepdims=True))
    o = jnp.dot(p.astype(BF16), v_ref[...], preferred_element_type=F32)
    o_ref[...] = (o / jnp.sum(p, axis=-1, keepdims=True)).astype(o_ref.dtype)


def _ctx_attention(q, k, v, nh, dq, dv, voff, B, ctx_len, n_lat):
    cb = n_lat // ctx_len
    return pl.pallas_call(
        _ctx_attn_kernel,
        grid=(B, nh),
        in_specs=[pl.BlockSpec((ctx_len, dq), lambda b, h: (cb + b, h)),
                  pl.BlockSpec((ctx_len, dq), lambda b, h: (cb + b, h)),
                  pl.BlockSpec((ctx_len, dv), lambda b, h: (cb + b, voff + h))],
        out_specs=pl.BlockSpec((ctx_len, dv), lambda b, h: (b, h)),
        out_shape=jax.ShapeDtypeStruct((B * ctx_len, nh * dv), BF16),
        compiler_params=_cparams(2),
        name="ctx_attention",
    )(q, k, v)


def _merge_kernel(yc_ref, yn_ref, ym_ref, ync_ref, ymc_ref, gn_ref, gm_ref, o_ref, *, lat_blocks):
    cw = yc_ref.shape[1]
    nw = yn_ref.shape[1]
    lat = pl.program_id(0) < lat_blocks
    o_ref[:, :cw] = yc_ref[...]
    yn = jnp.where(lat, yn_ref[...], ync_ref[...]).astype(F32)
    o_ref[:, cw:cw + nw] = (_rms(yn, nw) * gn_ref[...]).astype(o_ref.dtype)
    ym = jnp.where(lat, ym_ref[...], ymc_ref[...]).astype(F32)
    o_ref[:, cw + nw:] = (_rms(ym, ym.shape[1]) * gm_ref[...]).astype(o_ref.dtype)


def _merge(y_conv, y_na, y_mla, y_na_ctx, y_mla_ctx, gn, gm, rows):
    cw, nw, mw = y_conv.shape[1], y_na.shape[1], y_mla.shape[1]
    tm = ROW_TILE
    lat_blocks = y_na.shape[0] // tm
    ctx_last = y_na_ctx.shape[0] // tm - 1

    def lat_map(i):
        return (jnp.minimum(i, lat_blocks - 1), 0)

    def ctx_map(i):
        return (jnp.clip(i - lat_blocks, 0, ctx_last), 0)

    return pl.pallas_call(
        functools.partial(_merge_kernel, lat_blocks=lat_blocks),
        grid=(rows // tm,),
        in_specs=[pl.BlockSpec((tm, cw), lambda i: (i, 0)),
                  pl.BlockSpec((tm, nw), lat_map), pl.BlockSpec((tm, mw), lat_map),
                  pl.BlockSpec((tm, nw), ctx_map), pl.BlockSpec((tm, mw), ctx_map),
                  pl.BlockSpec((1, nw), lambda i: (0, 0)), pl.BlockSpec((1, mw), lambda i: (0, 0))],
        out_specs=pl.BlockSpec((tm, cw + nw + mw), lambda i: (i, 0)),
        out_shape=jax.ShapeDtypeStruct((rows, cw + nw + mw), BF16),
        compiler_params=_cparams(1),
        name="merge_norm",
    )(y_conv, y_na, y_mla, y_na_ctx, y_mla_ctx, gn, gm)


def _router_kernel(x_ref, g_ref, sc_ref, sh_ref, whi_ref, wlo_ref, rb_ref, tri_ref,
                   h_ref, idx_ref, wgt_ref, rank_ref, cnt_ref, base_ref):
    @pl.when(pl.program_id(0) == 0)
    def _():
        base_ref[...] = jnp.zeros(base_ref.shape, F32)

    x = x_ref[...]
    h = _rms(x, x.shape[-1]) * g_ref[...] * (1.0 + sc_ref[...]) + sh_ref[...]
    h_hi = h.astype(BF16)
    h_ref[...] = h_hi
    h_lo = (h - h_hi.astype(F32)).astype(BF16)
    dn = (((1,), (1,)), ((), ()))
    logits = (lax.dot_general(whi_ref[...], h_hi, dn, preferred_element_type=F32)
              + lax.dot_general(whi_ref[...], h_lo, dn, preferred_element_type=F32)
              + lax.dot_general(wlo_ref[...], h_hi, dn, preferred_element_type=F32))
    E, tm = logits.shape
    gs = E // N_GROUPS
    scores = jax.nn.sigmoid(logits)
    sel = scores + rb_ref[...]
    sel3 = sel.reshape(N_GROUPS, gs, tm)
    io3 = lax.broadcasted_iota(jnp.int32, sel3.shape, 1)
    m1 = jnp.max(sel3, axis=1, keepdims=True)
    i1 = jnp.min(jnp.where(sel3 == m1, io3, gs), axis=1, keepdims=True)
    m2 = jnp.max(jnp.where(io3 == i1, -jnp.inf, sel3), axis=1, keepdims=True)
    grp = m1 + m2
    iog = lax.broadcasted_iota(jnp.int32, grp.shape, 0)
    keep = jnp.zeros(grp.shape, jnp.bool_)
    for _ in range(TOPK_GROUPS):
        gm = jnp.max(grp, axis=0, keepdims=True)
        gi = jnp.min(jnp.where(grp == gm, iog, N_GROUPS), axis=0, keepdims=True)
        hit = iog == gi
        keep = jnp.logical_or(keep, hit)
        grp = jnp.where(hit, -jnp.inf, grp)
    selm = jnp.where(keep, sel3, -jnp.inf).reshape(E, tm)
    ioe = lax.broadcasted_iota(jnp.int32, (E, tm), 0)
    idxs, wgts, hits = [], [], []
    for _ in range(TOP_K):
        m = jnp.max(selm, axis=0, keepdims=True)
        ei = jnp.min(jnp.where(selm == m, ioe, E), axis=0, keepdims=True)
        hit = ioe == ei
        idxs.append(ei)
        hits.append(hit)
        wgts.append(jnp.sum(jnp.where(hit, scores, 0.0), axis=0, keepdims=True))
        selm = jnp.where(hit, -jnp.inf, selm)
    wsum = wgts[0]
    for w in wgts[1:]:
        wsum = wsum + w
    chosen = jnp.zeros((E, tm), F32)
    for hit in hits:
        chosen = chosen + jnp.where(hit, 1.0, 0.0)
    before = base_ref[:, 0:1] + jnp.dot(chosen.astype(BF16), tri_ref[...], preferred_element_type=F32)
    for k in range(TOP_K):
        idx_ref[k:k + 1, :] = idxs[k]
        wgt_ref[k:k + 1, :] = wgts[k] / wsum * ROUTE_SCALE
        rank_ref[k:k + 1, :] = jnp.sum(jnp.where(hits[k], before, 0.0), axis=0, keepdims=True).astype(jnp.int32)
    base_ref[...] = base_ref[...] + jnp.sum(chosen, axis=1, keepdims=True)
    cnt_ref[...] = base_ref[...]


def _router(x, g, mod_l, w_router, router_bias, row0, rows, n_lat, seq):
    D = x.shape[1]
    E = w_router.shape[1]
    tm = ROW_TILE
    b0 = row0 // tm
    mrow = functools.partial(_mod_row, tm=tm, n_lat=n_lat, seq=seq)
    wt = w_router.T
    whi = wt.astype(BF16)
    wlo = (wt - whi.astype(F32)).astype(BF16)
    tri = (np.arange(tm)[:, None] < np.arange(tm)[None, :]).astype(np.float32)
    return pl.pallas_call(
        _router_kernel,
        grid=(rows // tm,),
        in_specs=[pl.BlockSpec((tm, D), lambda i: (i + b0, 0)),
                  pl.BlockSpec((1, D), lambda i: (0, 0)),
                  pl.BlockSpec((None, 1, D), lambda i: (mrow(i + b0) * 6 + 4, 0, 0)),
                  pl.BlockSpec((None, 1, D), lambda i: (mrow(i + b0) * 6 + 3, 0, 0)),
                  pl.BlockSpec((E, D), lambda i: (0, 0)),
                  pl.BlockSpec((E, D), lambda i: (0, 0)),
                  pl.BlockSpec((E, 1), lambda i: (0, 0)),
                  pl.BlockSpec((tm, tm), lambda i: (0, 0))],
        out_specs=[pl.BlockSpec((tm, D), lambda i: (i, 0)),
                   pl.BlockSpec((TOP_K, tm), lambda i: (0, i)),
                   pl.BlockSpec((TOP_K, tm), lambda i: (0, i)),
                   pl.BlockSpec((TOP_K, tm), lambda i: (0, i)),
                   pl.BlockSpec((E, LANE), lambda i: (0, 0))],
        out_shape=[jax.ShapeDtypeStruct((rows, D), BF16),
                   jax.ShapeDtypeStruct((TOP_K, rows), jnp.int32),
                   jax.ShapeDtypeStruct((TOP_K, rows), F32),
                   jax.ShapeDtypeStruct((TOP_K, rows), jnp.int32),
                   jax.ShapeDtypeStruct((E, LANE), F32)],
        scratch_shapes=[pltpu.VMEM((E, LANE), F32)],
        compiler_params=_cparams(1),
        name="router",
    )(x, g.reshape(1, D), mod_l, mod_l, whi, wlo, router_bias.reshape(E, 1), jnp.asarray(tri, BF16))


def _expert_kernel(vt_ref, ve_ref, nv_ref, gs_ref, xs_ref, wg_ref, wu_ref, wd_ref, *rest):
    o_ref, wgb_ref, wub_ref, wdb_ref = rest[-4:]
    v = pl.program_id(0)
    T = xs_ref.shape[0]

    @pl.when(v < nv_ref[0])
    def _():
        e = ve_ref[v]
        tile = vt_ref[v]
        prev = jnp.maximum(v - 1, 0)
        new_expert = jnp.logical_or(v == 0, ve_ref[prev] != e)
        first_visit = jnp.logical_or(v == 0, vt_ref[prev] != tile)

        @pl.when(new_expert)
        def _():
            wgb_ref[...] = wg_ref[...].astype(BF16)
            wub_ref[...] = wu_ref[...].astype(BF16)
            wdb_ref[...] = wd_ref[...].astype(BF16)

        xs = xs_ref[...]
        g = jnp.dot(xs, wgb_ref[...], preferred_element_type=F32)
        u = jnp.dot(xs, wub_ref[...], preferred_element_type=F32)
        a = (g * jax.nn.sigmoid(g) * u).astype(BF16)
        y = jnp.dot(a, wdb_ref[...], preferred_element_type=F32).astype(o_ref.dtype)

        @pl.when(first_visit)
        def _():
            o_ref[...] = y

        @pl.when(jnp.logical_not(first_visit))
        def _():
            row = tile * T + lax.broadcasted_iota(jnp.int32, (T, 1), 0)
            mine = jnp.logical_and(row >= gs_ref[e], row < gs_ref[e + 1])
            o_ref[...] = jnp.where(mine, y, o_ref[...])


def _experts(xs, tile0, n_rows_total, y_prev, visits, group_start, w_gate, w_up, w_down, layer):
    visit_tile, visit_expert, n_visits = visits
    D = xs.shape[1]
    DE = w_gate.shape[3]
    T = MOE_TILE
    in_specs = [pl.BlockSpec((T, D), lambda v, vt, ve, nv, gs: (vt[v] - tile0, 0)),
                pl.BlockSpec((None, None, D, DE), lambda v, vt, ve, nv, gs: (layer, ve[v], 0, 0)),
                pl.BlockSpec((None, None, D, DE), lambda v, vt, ve, nv, gs: (layer, ve[v], 0, 0)),
                pl.BlockSpec((None, None, DE, D), lambda v, vt, ve, nv, gs: (layer, ve[v], 0, 0))]
    args = [visit_tile, visit_expert, n_visits, group_start, xs, w_gate, w_up, w_down]
    aliases = {}
    if y_prev is not None:
        in_specs.append(pl.BlockSpec(memory_space=pl.ANY))
        args.append(y_prev)
        aliases = {len(args) - 1: 0}
    grid_spec = pltpu.PrefetchScalarGridSpec(
        num_scalar_prefetch=4,
        grid=(visit_tile.shape[0],),
        in_specs=in_specs,
        out_specs=pl.BlockSpec((T, D), lambda v, vt, ve, nv, gs: (vt[v], 0)),
        scratch_shapes=[pltpu.VMEM((D, DE), BF16), pltpu.VMEM((D, DE), BF16), pltpu.VMEM((DE, D), BF16)],
    )
    return pl.pallas_call(
        _expert_kernel,
        grid_spec=grid_spec,
        out_shape=jax.ShapeDtypeStruct((n_rows_total, D), BF16),
        input_output_aliases=aliases,
        compiler_params=_cparams(1),
        name="experts",
    )(*args)


def _dispatch_plan(eidx, rank, counts, tile_bounds):
    K, rows = eidx.shape
    E = counts.shape[0]
    T = MOE_TILE
    gend = jnp.cumsum(counts)
    gstart = gend - counts
    eids = jnp.arange(E, dtype=jnp.int32)
    dest = rank + jnp.sum(jnp.where(eidx[None] == eids[:, None, None], gstart[:, None, None], 0), axis=0)
    token = jnp.broadcast_to(jnp.arange(rows, dtype=jnp.int32)[None], (K, rows))
    _, row_token = lax.sort((dest.reshape(-1), token.reshape(-1)), num_keys=1)
    visits = []
    for t_lo, t_hi in zip(tile_bounds[:-1], tile_bounds[1:]):
        first_tile = jnp.maximum(gstart // T, t_lo)
        last_tile = jnp.minimum((gend - 1) // T, t_hi - 1)
        n_vis_e = jnp.where(counts > 0, jnp.maximum(last_tile - first_tile + 1, 0), 0)
        vend = jnp.cumsum(n_vis_e)
        voff = vend - n_vis_e
        n_visits = vend[-1]
        v = jnp.arange(t_hi - t_lo + E, dtype=jnp.int32)
        ve = jnp.minimum(jnp.sum((v[:, None] >= vend[None, :]).astype(jnp.int32), axis=1), E - 1)
        onehot = (ve[:, None] == eids[None, :]).astype(jnp.int32)
        vt = jnp.sum(onehot * (first_tile - voff)[None, :], axis=1) + v
        vt = jnp.where(v < n_visits, vt, t_hi - 1).astype(jnp.int32)
        visits.append((vt, ve.astype(jnp.int32), n_visits.reshape(1).astype(jnp.int32)))
    group_start = jnp.concatenate([gstart, gend[-1:]]).astype(jnp.int32)
    return dest, row_token, visits, group_start


def _shared_kernel(h_ref, wg_ref, wu_ref, wd_ref, y8_ref, gate_ref, x_ref, g2_ref, o_ref):
    h = h_ref[...]
    g = jnp.dot(h, wg_ref[...], preferred_element_type=F32)
    u = jnp.dot(h, wu_ref[...], preferred_element_type=F32)
    a = (g * jax.nn.sigmoid(g) * u).astype(BF16)
    y = jnp.dot(a, wd_ref[...], preferred_element_type=F32)
    for k in range(y8_ref.shape[0]):
        y = y + gate_ref[:, k:k + 1] * y8_ref[k].astype(F32)
    o_ref[...] = x_ref[...] + g2_ref[...] * y


def _shared_combine(h2, wsg, wsu, wsd, y8, gates, x, mod_l, row0, rows, n_lat, seq):
    D = h2.shape[1]
    DE = wsg.shape[1]
    K = y8.shape[0]
    tm = COMBINE_TILE
    b0 = row0 // tm
    mrow = functools.partial(_mod_row, tm=tm, n_lat=n_lat, seq=seq)
    return pl.pallas_call(
        _shared_kernel,
        grid=(rows // tm,),
        in_specs=[pl.BlockSpec((tm, D), lambda i: (i + b0, 0)),
                  pl.BlockSpec((D, DE), lambda i: (0, 0)),
                  pl.BlockSpec((D, DE), lambda i: (0, 0)),
                  pl.BlockSpec((DE, D), lambda i: (0, 0)),
                  pl.BlockSpec((K, tm, D), lambda i: (0, i, 0)),
                  pl.BlockSpec((tm, K), lambda i: (i, 0)),
                  pl.BlockSpec((tm, D), lambda i: (i + b0, 0)),
                  pl.BlockSpec((None, 1, D), lambda i: (mrow(i + b0) * 6 + 5, 0, 0))],
        out_specs=pl.BlockSpec((tm, D), lambda i: (i + b0, 0)),
        out_shape=jax.ShapeDtypeStruct(x.shape, F32),
        input_output_aliases={6: 0},
        compiler_params=_cparams(1),
        name="shared_combine",
    )(h2, wsg, wsu, wsd, y8, gates, x, mod_l)


def _rope_tables(seq, pad_rows):
    t = np.arange(seq)
    row = (t // GRID_W).astype(np.float32)
    col = (t % GRID_W).astype(np.float32)
    n_freq = MLA_ROPE // 4
    inv = (ROPE_BASE ** (-jnp.arange(n_freq, dtype=F32) / n_freq))
    ang = jnp.concatenate([jnp.asarray(row)[:, None] * inv, jnp.asarray(col)[:, None] * inv], axis=-1)
    cos, sin = jnp.cos(ang), jnp.sin(ang)
    half = MLA_ROPE // 2
    z = jnp.zeros((seq, LANE - MLA_ROPE), F32)
    zh = jnp.zeros((seq, half), F32)
    c = jnp.concatenate([cos, cos, z], axis=-1)
    s1 = jnp.concatenate([-sin, zh, z], axis=-1)
    s2 = jnp.concatenate([zh, sin, z], axis=-1)
    ident = jnp.concatenate([jnp.ones((pad_rows, MLA_ROPE), F32), jnp.zeros((pad_rows, LANE - MLA_ROPE), F32)], -1)
    zero = jnp.zeros((pad_rows, LANE), F32)
    return (jnp.concatenate([c, ident], 0), jnp.concatenate([s1, zero], 0), jnp.concatenate([s2, zero], 0))


def _pad_lanes(v, n):
    return jnp.concatenate([v, jnp.zeros((n - v.shape[0],), v.dtype)]).reshape(1, n)


def kernel(x, c, ctx, c_ctx, ada_w, ada_b, norm1_g, w_in, conv_w, na_q_norm, na_k_norm, na_rpb,
           mla_q_lat_norm, mla_kv_lat_norm, w_uq, w_ukv, mla_q_norm, mla_k_norm, out_norm_g, w_o,
           norm2_g, w_router, router_bias, w_gate, w_up, w_down, ws_gate, ws_up, ws_down):
    B, S, D = x.shape
    CTX = ctx.shape[1]
    L = ada_w.shape[0]
    n_lat = B * S
    R = n_lat + B * CTX
    W = D // 4
    kvr = w_ukv.shape[1]
    nh_mla = w_uq.shape[2] // (MLA_NOPE + MLA_ROPE)
    E = w_router.shape[2]
    assert B == 2 and S % ROW_TILE == 0 and (B * CTX) % ROW_TILE == 0 and CTX % CONV_TILE == 0
    assert S % GRID_W == 0 and S // GRID_W >= NA_WIN_H and W % HEAD_DIM == 0 and kvr % LANE == 0
    assert S % 1024 == 0 and (TOP_K * B * CTX) % MOE_TILE == 0 and ROW_TILE % COMBINE_TILE == 0
    assert (S // GRID_W) % NA_GROUP == 0 and S // GRID_W >= NA_KROWS

    x_pair = (x.reshape(n_lat, D), ctx.reshape(B * CTX, D))
    cvecs = jnp.concatenate([c, c_ctx[None]], axis=0)
    mod = _modulation(cvecs, ada_w, ada_b)
    tabs = _rope_tables(S, ROW_TILE)

    for l in range(L):
        last = l == L - 1
        rows = n_lat if last else R
        mod_l = mod[l].reshape(8 * 6, 1, D)

        w_tail = jnp.concatenate([w_in[l, :, 7 * W:], jnp.zeros((D, LANE - MLA_ROPE), F32)], axis=1)
        wq = w_uq[l].reshape(W, nh_mla, MLA_NOPE + MLA_ROPE)
        wq = jnp.concatenate([wq, jnp.zeros((W, nh_mla, MLA_QPAD - MLA_NOPE - MLA_ROPE), F32)], axis=2)
        wq = wq.reshape(W, nh_mla * MLA_QPAD).astype(BF16)
        wkv = w_ukv[l].reshape(kvr, nh_mla, MLA_NOPE + MLA_V)
        wk = wkv[:, :, :MLA_NOPE].reshape(kvr, nh_mla * MLA_NOPE).astype(BF16)
        wv = wkv[:, :, MLA_NOPE:].reshape(kvr, nh_mla * MLA_V).astype(BF16)
        gq_n = mla_q_norm[l][:MLA_NOPE].reshape(1, LANE)
        gq_r = _pad_lanes(mla_q_norm[l][MLA_NOPE:], LANE)
        gk_n = mla_k_norm[l][:MLA_NOPE].reshape(1, LANE)
        gk_r = _pad_lanes(mla_k_norm[l][MLA_NOPE:], LANE)
        og = out_norm_g[l]

        h = _norm_mod(x_pair, R, norm1_g[l], mod_l, 1, 0, n_lat, S)
        z_main = _matmul_ws(h, w_in, l, 7 * W, min(512, W), BF16)
        z_tail = _matmul_ws(h, w_tail, None, w_tail.shape[1], w_tail.shape[1], BF16)
        nq, nk, cq, ckv, kr = _post_in(z_main, z_tail, na_q_norm[l].reshape(1, LANE),
                                       na_k_norm[l].reshape(1, LANE), mla_q_lat_norm[l].reshape(1, W),
                                       mla_kv_lat_norm[l].reshape(1, kvr), gk_r, tabs, n_lat, S)
        qf = _uq_matmul(cq, wq, gq_n, gq_r, tabs, n_lat, S)
        kf = _uk_matmul(ckv, wk, gk_n, kr)
        vf = _matmul(ckv, wv, BF16, 512)

        y_conv = _conv_mixer(z_main, conv_w[l], og[:W].reshape(1, W), rows, n_lat, S, CTX)
        y_na = _na_attention(nq, nk, z_main, _na_bias_table(na_rpb[l], S // GRID_W), B, S, CTX, n_lat)
        y_mla = _mla_attention(qf, kf, vf, B, S, CTX, n_lat)
        if last:
            y_na_c, y_mla_c = y_na, y_mla
        else:
            nh_na = W // HEAD_DIM
            y_na_c = _ctx_attention(nq, nk, z_main, nh_na, HEAD_DIM, HEAD_DIM, 5 * nh_na, B, CTX, n_lat)
            y_mla_c = _ctx_attention(qf, kf, vf, nh_mla, MLA_QPAD, MLA_V, 0, B, CTX, n_lat)
        y = _merge(y_conv, y_na, y_mla, y_na_c, y_mla_c,
                   og[W:2 * W].reshape(1, W), og[2 * W:].reshape(1, D - 2 * W), rows)
        xs = _wo_matmul(y, w_o, l, x_pair, mod_l, rows, n_lat, S)

        wsg, wsu, wsd = ws_gate[l].astype(BF16), ws_up[l].astype(BF16), ws_down[l].astype(BF16)
        h2, eidx, wgt, rank, cnt = _router(xs, norm2_g[l], mod_l, w_router[l], router_bias[l],
                                           0, rows, n_lat, S)
        n_pairs = TOP_K * rows
        n_tiles = n_pairs // MOE_TILE
        tile_bounds = [n_tiles * p // MOE_PARTS for p in range(MOE_PARTS + 1)]
        dest, row_token, visits, gstart = _dispatch_plan(eidx, rank, cnt[:, 0].astype(jnp.int32), tile_bounds)
        y_rows = None
        for p in range(MOE_PARTS):
            part_tokens = row_token[tile_bounds[p] * MOE_TILE:tile_bounds[p + 1] * MOE_TILE]
            gathered = h2.at[part_tokens].get(mode="promise_in_bounds")
            y_rows = _experts(gathered, tile_bounds[p], n_pairs, y_rows, visits[p], gstart,
                              w_gate, w_up, w_down, l)
        gates = wgt.T
        n_blocks = rows // ROW_TILE
        row_bounds = [(n_blocks * p // MOE_PARTS) * ROW_TILE for p in range(MOE_PARTS + 1)]
        for row0, row1 in zip(row_bounds[:-1], row_bounds[1:]):
            y8 = y_rows.at[dest[:, row0:row1].reshape(-1)].get(mode="promise_in_bounds")
            xs = _shared_combine(h2, wsg, wsu, wsd, y8.reshape(TOP_K, row1 - row0, D), gates[row0:row1],
                                 xs, mod_l, row0, row1 - row0, n_lat, S)
        x_pair = (xs, xs)

    return xs[:n_lat].reshape(B, S, D)
```

```python
import functools

import numpy as np
import jax
import jax.numpy as jnp
from jax import lax
from jax.experimental import pallas as pl
from jax.experimental.pallas import tpu as pltpu

F32 = jnp.float32
BF16 = jnp.bfloat16

GRID_W = 64
HEAD_DIM = 128
MLA_NOPE = 128
MLA_ROPE = 64
MLA_V = 128
MLA_QPAD = 256
NA_WIN_H = 8
NA_WIN_W = 16
NA_GROUP = 8
NA_KROWS = NA_GROUP + NA_WIN_H
N_GROUPS = 8
TOPK_GROUPS = 4
TOP_K = 8
ROUTE_SCALE = 2.5
ROPE_BASE = 10000.0
EPS = 1e-6
NEG = -1e30
LOG2E = 1.4426950408889634

LANE = 128
ROW_TILE = 512
CONV_TILE = 256
MOE_TILE = 256
COMBINE_TILE = 128
MOE_PARTS = 4
VMEM_LIMIT = 56 * 1024 * 1024


def _cparams(n_axes):
    return pltpu.CompilerParams(dimension_semantics=("arbitrary",) * n_axes,
                                vmem_limit_bytes=VMEM_LIMIT)


def _rms(x, n):
    return x * lax.rsqrt(jnp.sum(x * x, axis=-1, keepdims=True) * (1.0 / n) + EPS)


def _mod_kernel(sb_ref, w_ref, b_ref, o_ref):
    tn = w_ref.shape[1]
    o_ref[...] = jnp.zeros(o_ref.shape, F32)
    for cblk in range(tn // LANE):
        wc = w_ref[:, cblk * LANE:(cblk + 1) * LANE]
        for r in range(3):
            acc = jnp.sum(wc * sb_ref[r], axis=0, keepdims=True)
            o_ref[r:r + 1, cblk * LANE:(cblk + 1) * LANE] = acc + b_ref[:, cblk * LANE:(cblk + 1) * LANE]


def _modulation(cvecs, ada_w, ada_b):
    L, D, D6 = ada_w.shape
    tn = 512
    s = cvecs * jax.nn.sigmoid(cvecs)
    sb = jnp.broadcast_to(s[:, :, None], (3, D, LANE))
    return pl.pallas_call(
        _mod_kernel,
        grid=(L, D6 // tn),
        in_specs=[pl.BlockSpec((3, D, LANE), lambda l, j: (0, 0, 0)),
                  pl.BlockSpec((None, D, tn), lambda l, j: (l, 0, j)),
                  pl.BlockSpec((None, 1, tn), lambda l, j: (l, 0, j))],
        out_specs=pl.BlockSpec((None, 8, tn), lambda l, j: (l, 0, j)),
        out_shape=jax.ShapeDtypeStruct((L, 8, D6), F32),
        compiler_params=_cparams(2),
        name="modulation",
    )(sb, ada_w, ada_b.reshape(L, 1, D6))


def _mod_row(i, tm, n_lat, seq):
    r0 = i * tm
    return jnp.where(r0 < n_lat, r0 // seq, n_lat // seq)


def _split_rows(x_pair, tm):
    head, tail = x_pair
    head_blocks = head.shape[0] // tm
    tail_last = tail.shape[0] // tm - 1
    return (head_blocks, lambda i: jnp.minimum(i, head_blocks - 1),
            lambda i: jnp.clip(i - head_blocks, 0, tail_last))


def _norm_mod_kernel(xa_ref, xb_ref, g_ref, sc_ref, sh_ref, o_ref, *, head_blocks):
    x = jnp.where(pl.program_id(0) < head_blocks, xa_ref[...], xb_ref[...])
    y = _rms(x, x.shape[-1]) * g_ref[...]
    o_ref[...] = (y * (1.0 + sc_ref[...]) + sh_ref[...]).astype(o_ref.dtype)


def _norm_mod(x_pair, R, g, mod_l, which_sc, which_sh, n_lat, seq):
    D = x_pair[0].shape[1]
    tm = ROW_TILE
    mrow = functools.partial(_mod_row, tm=tm, n_lat=n_lat, seq=seq)
    head_blocks, head_map, tail_map = _split_rows(x_pair, tm)
    return pl.pallas_call(
        functools.partial(_norm_mod_kernel, head_blocks=head_blocks),
        grid=(R // tm,),
        in_specs=[pl.BlockSpec((tm, D), lambda i: (head_map(i), 0)),
                  pl.BlockSpec((tm, D), lambda i: (tail_map(i), 0)),
                  pl.BlockSpec((1, D), lambda i: (0, 0)),
                  pl.BlockSpec((None, 1, D), lambda i: (mrow(i) * 6 + which_sc, 0, 0)),
                  pl.BlockSpec((None, 1, D), lambda i: (mrow(i) * 6 + which_sh, 0, 0))],
        out_specs=pl.BlockSpec((tm, D), lambda i: (i, 0)),
        out_shape=jax.ShapeDtypeStruct((R, D), BF16),
        compiler_params=_cparams(1),
        name="norm_mod",
    )(x_pair[0], x_pair[1], g.reshape(1, D), mod_l, mod_l)


def _mm_kernel(x_ref, w_ref, o_ref):
    o_ref[...] = jnp.dot(x_ref[...], w_ref[...], preferred_element_type=F32).astype(o_ref.dtype)


def _matmul(x, w, out_dtype, tn, rows=None):
    R, K = x.shape
    rows = R if rows is None else rows
    N = w.shape[1]
    tm = ROW_TILE
    return pl.pallas_call(
        _mm_kernel,
        grid=(rows // tm, N // tn),
        in_specs=[pl.BlockSpec((tm, K), lambda i, j: (i, 0)),
                  pl.BlockSpec((K, tn), lambda i, j: (0, j))],
        out_specs=pl.BlockSpec((tm, tn), lambda i, j: (i, j)),
        out_shape=jax.ShapeDtypeStruct((rows, N), out_dtype),
        compiler_params=_cparams(2),
        name="matmul",
    )(x, w)


def _mm_ws_kernel(x_ref, w_ref, o_ref, wb_ref):
    @pl.when(pl.program_id(1) == 0)
    def _():
        wb_ref[...] = w_ref[...].astype(BF16)

    o_ref[...] = lax.dot_general(x_ref[...], wb_ref[...], (((1,), (1,)), ((), ())),
                                 preferred_element_type=F32).astype(o_ref.dtype)


def _matmul_ws(x, wt, layer, n_cols, tn, out_dtype):
    R, K = x.shape
    tm = ROW_TILE
    if layer is None:
        w_spec = pl.BlockSpec((tn, K), lambda j, i: (j, 0))
    else:
        w_spec = pl.BlockSpec((None, tn, K), lambda j, i: (layer, j, 0))
    return pl.pallas_call(
        _mm_ws_kernel,
        grid=(n_cols // tn, R // tm),
        in_specs=[pl.BlockSpec((tm, K), lambda j, i: (i, 0)), w_spec],
        out_specs=pl.BlockSpec((tm, tn), lambda j, i: (i, j)),
        out_shape=jax.ShapeDtypeStruct((R, n_cols), out_dtype),
        scratch_shapes=[pltpu.VMEM((tn, K), BF16)],
        compiler_params=_cparams(2),
        name="matmul_ws",
    )(x, wt)


def _rope_rotate(t, c_ref, s1_ref, s2_ref):
    return (t * c_ref[...] + pltpu.roll(t, LANE - MLA_ROPE // 2, axis=1) * s1_ref[...]
            + pltpu.roll(t, MLA_ROPE // 2, axis=1) * s2_ref[...])


def _uq_kernel(x_ref, w_ref, gn_ref, gr_ref, c_ref, s1_ref, s2_ref, o_ref):
    scale = (MLA_NOPE + MLA_ROPE) ** -0.5 * LOG2E
    tm = x_ref.shape[0]
    half = tm // 2
    for r0 in (0, half):
        rs = slice(r0, r0 + half)
        acc = jnp.dot(x_ref[rs, :], w_ref[...], preferred_element_type=F32)
        for h in range(acc.shape[1] // MLA_QPAD):
            o = h * MLA_QPAD
            a = _rms(acc[:, o:o + MLA_NOPE], MLA_NOPE) * gn_ref[...]
            r = _rms(acc[:, o + MLA_NOPE:o + MLA_QPAD], MLA_ROPE) * gr_ref[...]
            r = (r * c_ref[rs, :] + pltpu.roll(r, LANE - MLA_ROPE // 2, axis=1) * s1_ref[rs, :]
                 + pltpu.roll(r, MLA_ROPE // 2, axis=1) * s2_ref[rs, :])
            o_ref[rs, o:o + MLA_NOPE] = (a * scale).astype(o_ref.dtype)
            o_ref[rs, o + MLA_NOPE:o + MLA_QPAD] = (r * scale).astype(o_ref.dtype)


def _rope_block(i, tm, n_lat, seq):
    return jnp.where(i * tm < n_lat, i % (seq // tm), seq // tm)


def _uq_matmul(x, w, gn, gr, tabs, n_lat, seq):
    R, K = x.shape
    N = w.shape[1]
    tm, tn = ROW_TILE, 1024
    rb = functools.partial(_rope_block, tm=tm, n_lat=n_lat, seq=seq)
    tab_spec = pl.BlockSpec((tm, LANE), lambda i, j: (rb(i), 0))
    vec_spec = pl.BlockSpec((1, LANE), lambda i, j: (0, 0))
    return pl.pallas_call(
        _uq_kernel,
        grid=(R // tm, N // tn),
        in_specs=[pl.BlockSpec((tm, K), lambda i, j: (i, 0)),
                  pl.BlockSpec((K, tn), lambda i, j: (0, j)),
                  vec_spec, vec_spec, tab_spec, tab_spec, tab_spec],
        out_specs=pl.BlockSpec((tm, tn), lambda i, j: (i, j)),
        out_shape=jax.ShapeDtypeStruct((R, N), BF16),
        compiler_params=_cparams(2),
        name="uq_matmul",
    )(x, w, gn, gr, *tabs)


def _uk_kernel(x_ref, w_ref, gk_ref, kr_ref, o_ref):
    half = x_ref.shape[0] // 2
    for r0 in (0, half):
        rs = slice(r0, r0 + half)
        acc = jnp.dot(x_ref[rs, :], w_ref[...], preferred_element_type=F32)
        for h in range(acc.shape[1] // MLA_NOPE):
            a = _rms(acc[:, h * MLA_NOPE:(h + 1) * MLA_NOPE], MLA_NOPE) * gk_ref[...]
            o_ref[rs, h * MLA_QPAD:h * MLA_QPAD + MLA_NOPE] = a.astype(o_ref.dtype)
            o_ref[rs, h * MLA_QPAD + MLA_NOPE:(h + 1) * MLA_QPAD] = kr_ref[rs, :]


def _uk_matmul(x, w, gk, kr_pad):
    R, K = x.shape
    N = w.shape[1]
    tm, tn = ROW_TILE, 512
    return pl.pallas_call(
        _uk_kernel,
        grid=(R // tm, N // tn),
        in_specs=[pl.BlockSpec((tm, K), lambda i, j: (i, 0)),
                  pl.BlockSpec((K, tn), lambda i, j: (0, j)),
                  pl.BlockSpec((1, LANE), lambda i, j: (0, 0)),
                  pl.BlockSpec((tm, LANE), lambda i, j: (i, 0))],
        out_specs=pl.BlockSpec((tm, 2 * tn), lambda i, j: (i, j)),
        out_shape=jax.ShapeDtypeStruct((R, 2 * N), BF16),
        compiler_params=_cparams(2),
        name="uk_matmul",
    )(x, w, gk, kr_pad)


def _wo_kernel(y_ref, w_ref, xa_ref, xb_ref, g_ref, o_ref, wb_ref, *, head_blocks):
    @pl.when(pl.program_id(1) == 0)
    def _():
        wb_ref[...] = w_ref[...].astype(BF16)

    acc = jnp.dot(y_ref[...], wb_ref[...], preferred_element_type=F32)
    x = jnp.where(pl.program_id(1) < head_blocks, xa_ref[...], xb_ref[...])
    o_ref[...] = x + g_ref[...] * acc


def _wo_matmul(y, w_o, layer, x_pair, mod_l, rows, n_lat, seq):
    K = y.shape[1]
    N = w_o.shape[2]
    tm, tn = ROW_TILE, 512
    mrow = functools.partial(_mod_row, tm=tm, n_lat=n_lat, seq=seq)
    head_blocks, head_map, tail_map = _split_rows(x_pair, tm)
    return pl.pallas_call(
        functools.partial(_wo_kernel, head_blocks=head_blocks),
        grid=(N // tn, rows // tm),
        in_specs=[pl.BlockSpec((tm, K), lambda j, i: (i, 0)),
                  pl.BlockSpec((None, K, tn), lambda j, i: (layer, 0, j)),
                  pl.BlockSpec((tm, tn), lambda j, i: (head_map(i), j)),
                  pl.BlockSpec((tm, tn), lambda j, i: (tail_map(i), j)),
                  pl.BlockSpec((None, 1, tn), lambda j, i: (mrow(i) * 6 + 2, 0, j))],
        out_specs=pl.BlockSpec((tm, tn), lambda j, i: (i, j)),
        out_shape=jax.ShapeDtypeStruct((rows, N), F32),
        scratch_shapes=[pltpu.VMEM((K, tn), BF16)],
        compiler_params=_cparams(2),
        name="wo_matmul",
    )(y, w_o, x_pair[0], x_pair[1], mod_l)


def _post_in_kernel(nq_ref, nk_ref, cq_ref, tail_ref, gq_ref, gk_ref, gql_ref, gkvl_ref, gkr_ref,
                    c_ref, s1_ref, s2_ref, nq_o, nk_o, cq_o, ckv_o, kr_o):
    nh = nq_ref.shape[1] // HEAD_DIM
    scale = HEAD_DIM ** -0.5 * LOG2E
    for h in range(nh):
        sl = slice(h * HEAD_DIM, (h + 1) * HEAD_DIM)
        q = nq_ref[:, sl].astype(F32)
        nq_o[:, sl] = (_rms(q, HEAD_DIM) * gq_ref[...] * scale).astype(nq_o.dtype)
        k = nk_ref[:, sl].astype(F32)
        nk_o[:, sl] = (_rms(k, HEAD_DIM) * gk_ref[...]).astype(nk_o.dtype)
    cq = cq_ref[...].astype(F32)
    cq_o[...] = (_rms(cq, cq.shape[1]) * gql_ref[...]).astype(cq_o.dtype)
    kvr = ckv_o.shape[1]
    ckv = tail_ref[:, :kvr].astype(F32)
    ckv_o[...] = (_rms(ckv, kvr) * gkvl_ref[...]).astype(ckv_o.dtype)
    kr = tail_ref[:, kvr:kvr + LANE].astype(F32)
    kr = _rms(kr, MLA_ROPE) * gkr_ref[...]
    kr_o[...] = _rope_rotate(kr, c_ref, s1_ref, s2_ref).astype(kr_o.dtype)


def _post_in(z_main, z_tail, gq, gk, gql, gkvl, gkr, tabs, n_lat, seq):
    R = z_main.shape[0]
    W = z_main.shape[1] // 7
    kvr = z_tail.shape[1] - LANE
    tm = ROW_TILE
    rb = functools.partial(_rope_block, tm=tm, n_lat=n_lat, seq=seq)
    tab_spec = pl.BlockSpec((tm, LANE), lambda i: (rb(i), 0))

    def vec(n):
        return pl.BlockSpec((1, n), lambda i: (0, 0))

    def col(j):
        return pl.BlockSpec((tm, W), lambda i: (i, j))

    return pl.pallas_call(
        _post_in_kernel,
        grid=(R // tm,),
        in_specs=[col(3), col(4), col(6), pl.BlockSpec((tm, kvr + LANE), lambda i: (i, 0)),
                  vec(LANE), vec(LANE), vec(W), vec(kvr), vec(LANE), tab_spec, tab_spec, tab_spec],
        out_specs=[pl.BlockSpec((tm, W), lambda i: (i, 0)), pl.BlockSpec((tm, W), lambda i: (i, 0)),
                   pl.BlockSpec((tm, W), lambda i: (i, 0)), pl.BlockSpec((tm, kvr), lambda i: (i, 0)),
                   pl.BlockSpec((tm, LANE), lambda i: (i, 0))],
        out_shape=[jax.ShapeDtypeStruct((R, W), BF16), jax.ShapeDtypeStruct((R, W), BF16),
                   jax.ShapeDtypeStruct((R, W), BF16), jax.ShapeDtypeStruct((R, kvr), BF16),
                   jax.ShapeDtypeStruct((R, LANE), BF16)],
        compiler_params=_cparams(1),
        name="post_in",
    )(z_main, z_main, z_main, z_tail, gq, gk, gql, gkvl, gkr, *tabs)


def _conv_kernel(b_ref, c_ref, u_ref, cp_ref, up_ref, cn_ref, un_ref, w_ref, g_ref, o_ref,
                 *, n_lat, seq, ctx_len):
    tm, W = c_ref.shape
    row0 = pl.program_id(0) * tm
    lat = row0 < n_lat
    pos = jnp.where(lat, row0 % seq, (row0 - n_lat) % ctx_len)
    slen = jnp.where(lat, seq, ctx_len)
    has_prev = (pos > 0).astype(F32)
    has_next = (pos + tm < slen).astype(F32)
    v = c_ref[...].astype(F32) * u_ref[...].astype(F32)
    vprev = cp_ref[7:8, :].astype(F32) * up_ref[7:8, :].astype(F32) * has_prev
    vnext = cn_ref[0:1, :].astype(F32) * un_ref[0:1, :].astype(F32) * has_next
    rid = lax.broadcasted_iota(jnp.int32, (tm, W), 0)
    v_dn = jnp.where(rid == 0, vprev, pltpu.roll(v, 1, axis=0))
    v_up = jnp.where(rid == tm - 1, vnext, pltpu.roll(v, tm - 1, axis=0))
    y = w_ref[0:1, :] * v_dn + w_ref[1:2, :] * v + w_ref[2:3, :] * v_up
    y = b_ref[...].astype(F32) * y
    o_ref[...] = (_rms(y, W) * g_ref[...]).astype(o_ref.dtype)


def _conv_mixer(z_main, conv_w, g, rows, n_lat, seq, ctx_len):
    R = z_main.shape[0]
    W = z_main.shape[1] // 7
    tm = CONV_TILE
    nb8 = R // 8

    def cur(j):
        return pl.BlockSpec((tm, W), lambda i: (i, j))

    def prev(j):
        return pl.BlockSpec((8, W), lambda i: (jnp.maximum(i * (tm // 8) - 1, 0), j))

    def nxt(j):
        return pl.BlockSpec((8, W), lambda i: (jnp.minimum((i + 1) * (tm // 8), nb8 - 1), j))

    return pl.pallas_call(
        functools.partial(_conv_kernel, n_lat=n_lat, seq=seq, ctx_len=ctx_len),
        grid=(rows // tm,),
        in_specs=[cur(0), cur(1), cur(2), prev(1), prev(2), nxt(1), nxt(2),
                  pl.BlockSpec((3, W), lambda i: (0, 0)), pl.BlockSpec((1, W), lambda i: (0, 0))],
        out_specs=pl.BlockSpec((tm, W), lambda i: (i, 0)),
        out_shape=jax.ShapeDtypeStruct((rows, W), BF16),
        compiler_params=_cparams(1),
        name="conv_mixer",
    )(z_main, z_main, z_main, z_main, z_main, z_main, z_main, conv_w, g)


def _na_group_plan(rows):
    n_groups = rows // NA_GROUP
    kinds = [(0, 0), (NA_GROUP, NA_GROUP - NA_WIN_H // 2), ((n_groups - 1) * NA_GROUP, rows - NA_KROWS)]
    plane = np.full((3, NA_GROUP, NA_KROWS), 2 * NA_WIN_H - 1, np.int64)
    for t, (r_first, k_first) in enumerate(kinds):
        for qr in range(NA_GROUP):
            r = r_first + qr
            r0 = min(max(r - NA_WIN_H // 2, 0), rows - NA_WIN_H)
            for kr in range(NA_KROWS):
                kk = k_first + kr
                if r0 <= kk < r0 + NA_WIN_H:
                    plane[t, qr, kr] = kk - r + NA_WIN_H - 1
    return plane


def _na_bias_table(rpb, rows):
    qc = np.arange(GRID_W)[:, None]
    kc = np.arange(GRID_W)[None, :]
    cstart = np.clip(qc - NA_WIN_W // 2, 0, GRID_W - NA_WIN_W)
    ok = (kc >= cstart) & (kc < cstart + NA_WIN_W)
    dc = np.clip(kc - qc + NA_WIN_W - 1, 0, 2 * NA_WIN_W - 2)
    H = rpb.shape[0]
    onehot = (dc[None] == np.arange(2 * NA_WIN_W - 1)[:, None, None]).astype(np.float32)
    cols = jnp.einsum("hdc,cqk->hdqk", rpb.astype(F32), onehot, precision=lax.Precision.HIGHEST)
    cols = jnp.where(ok[None, None], cols * LOG2E, NEG)
    planes = jnp.concatenate([cols, jnp.full((H, 1, GRID_W, GRID_W), NEG, F32)], axis=1)
    pick = (_na_group_plan(rows)[..., None] == np.arange(2 * NA_WIN_H)).astype(np.float32)
    g = jnp.einsum("tqkp,hpxy->htqxky", pick, planes, precision=lax.Precision.HIGHEST)
    return g.reshape(H, 3, NA_GROUP * GRID_W, NA_KROWS * GRID_W)


def _na_kernel(q_ref, k_ref, v_ref, kc_ref, vc_ref, bias_ref, o_ref):
    rows = q_ref.shape[0] // GRID_W
    n_groups = rows // NA_GROUP
    nq = NA_GROUP * GRID_W
    nk = NA_KROWS * GRID_W
    dn = (((1,), (1,)), ((), ()))
    kc = kc_ref[...]
    vc = vc_ref[...]

    def body(g, carry):
        r_first = g * NA_GROUP
        k_first = jnp.clip(r_first - NA_WIN_H // 2, 0, rows - NA_KROWS)
        kind = jnp.where(g == 0, 0, jnp.where(g == n_groups - 1, 2, 1))
        qoff = pl.multiple_of(r_first * GRID_W, GRID_W)
        koff = pl.multiple_of(k_first * GRID_W, GRID_W)
        q = q_ref[pl.ds(qoff, nq), :]
        k = k_ref[pl.ds(koff, nk), :]
        v = v_ref[pl.ds(koff, nk), :]
        s = lax.dot_general(q, k, dn, preferred_element_type=F32) + bias_ref[kind]
        sc = lax.dot_general(q, kc, dn, preferred_element_type=F32)
        m = jnp.maximum(jnp.max(s, axis=-1, keepdims=True), jnp.max(sc, axis=-1, keepdims=True))
        p = jnp.exp2(s - m)
        pc = jnp.exp2(sc - m)
        l = jnp.sum(p, axis=-1, keepdims=True) + jnp.sum(pc, axis=-1, keepdims=True)
        o = (jnp.dot(p.astype(BF16), v, preferred_element_type=F32)
             + jnp.dot(pc.astype(BF16), vc, preferred_element_type=F32))
        o_ref[pl.ds(qoff, nq), :] = (o / l).astype(o_ref.dtype)
        return carry

    lax.fori_loop(0, n_groups, body, 0, unroll=2)


def _na_attention(nq, nk, z_main, bias_tab, B, seq, ctx_len, n_lat):
    W = nq.shape[1]
    nh = W // HEAD_DIM
    cb = n_lat // ctx_len
    voff = 5 * nh
    return pl.pallas_call(
        _na_kernel,
        grid=(nh, B),
        in_specs=[pl.BlockSpec((seq, HEAD_DIM), lambda h, b: (b, h)),
                  pl.BlockSpec((seq, HEAD_DIM), lambda h, b: (b, h)),
                  pl.BlockSpec((seq, HEAD_DIM), lambda h, b: (b, voff + h)),
                  pl.BlockSpec((ctx_len, HEAD_DIM), lambda h, b: (cb + b, h)),
                  pl.BlockSpec((ctx_len, HEAD_DIM), lambda h, b: (cb + b, voff + h)),
                  pl.BlockSpec((None, 3, NA_GROUP * GRID_W, NA_KROWS * GRID_W), lambda h, b: (h, 0, 0, 0))],
        out_specs=pl.BlockSpec((seq, HEAD_DIM), lambda h, b: (b, h)),
        out_shape=jax.ShapeDtypeStruct((n_lat, W), BF16),
        compiler_params=_cparams(2),
        name="na_attention",
    )(nq, nk, z_main, nk, z_main, bias_tab)


def _flash_kernel(q_ref, k_ref, v_ref, kc_ref, vc_ref, o_ref, *, tk):
    tq = q_ref.shape[0]
    dv = v_ref.shape[1]
    dn = (((1,), (1,)), ((), ()))
    q = q_ref[...]

    def step(k, v, m, l, acc):
        s = lax.dot_general(q, k, dn, preferred_element_type=F32)
        m_new = jnp.maximum(m, jnp.max(s, axis=-1, keepdims=True))
        alpha = jnp.exp2(m - m_new)
        p = jnp.exp2(s - m_new)
        l = alpha * l + jnp.sum(p, axis=-1, keepdims=True)
        acc = alpha * acc + jnp.dot(p.astype(BF16), v, preferred_element_type=F32)
        return m_new, l, acc

    def body(j, carry):
        off = pl.multiple_of(j * tk, tk)
        return step(k_ref[pl.ds(off, tk), :], v_ref[pl.ds(off, tk), :], *carry)

    init = (jnp.full((tq, 1), NEG, F32), jnp.zeros((tq, 1), F32), jnp.zeros((tq, dv), F32))
    n_chunks = k_ref.shape[0] // tk
    m, l, acc = lax.fori_loop(0, n_chunks, body, init, unroll=min(n_chunks, 8))
    m, l, acc = step(kc_ref[...], vc_ref[...], m, l, acc)
    o_ref[...] = (acc / l).astype(o_ref.dtype)


def _mla_attention(qf, kf, vf, B, seq, ctx_len, n_lat):
    R = qf.shape[0]
    nh = qf.shape[1] // MLA_QPAD
    tq, tk = 1024, 1024
    cb = n_lat // ctx_len
    nqb = seq // tq
    return pl.pallas_call(
        functools.partial(_flash_kernel, tk=tk),
        grid=(B, nh, nqb),
        in_specs=[pl.BlockSpec((tq, MLA_QPAD), lambda b, h, i: (b * nqb + i, h)),
                  pl.BlockSpec((seq, MLA_QPAD), lambda b, h, i: (b, h)),
                  pl.BlockSpec((seq, MLA_V), lambda b, h, i: (b, h)),
                  pl.BlockSpec((ctx_len, MLA_QPAD), lambda b, h, i: (cb + b, h)),
                  pl.BlockSpec((ctx_len, MLA_V), lambda b, h, i: (cb + b, h))],
        out_specs=pl.BlockSpec((tq, MLA_V), lambda b, h, i: (b * nqb + i, h)),
        out_shape=jax.ShapeDtypeStruct((n_lat, nh * MLA_V), BF16),
        compiler_params=_cparams(3),
        name="mla_attention",
    )(qf, kf, vf, kf, vf)


def _ctx_attn_kernel(q_ref, k_ref, v_ref, o_ref):
    s = lax.dot_general(q_ref[...], k_ref[...], (((1,), (1,)), ((), ())), preferred_element_type=F32)
    p = jnp.exp2(s - jnp.max(s, axis=-1, keepdims=True))
    o = jnp.dot(p.astype(BF16), v_ref[...], preferred_element_type=F32)
    o_ref[...] = (o / jnp.sum(p, axis=-1, keepdims=True)).astype(o_ref.dtype)


def _ctx_attention(q, k, v, nh, dq, dv, voff, B, ctx_len, n_lat):
    cb = n_lat // ctx_len
    return pl.pallas_call(
        _ctx_attn_kernel,
        grid=(B, nh),
        in_specs=[pl.BlockSpec((ctx_len, dq), lambda b, h: (cb + b, h)),
                  pl.BlockSpec((ctx_len, dq), lambda b, h: (cb + b, h)),
                  pl.BlockSpec((ctx_len, dv), lambda b, h: (cb + b, voff + h))],
        out_specs=pl.BlockSpec((ctx_len, dv), lambda b, h: (b, h)),
        out_shape=jax.ShapeDtypeStruct((B * ctx_len, nh * dv), BF16),
        compiler_params=_cparams(2),
        name="ctx_attention",
    )(q, k, v)


def _merge_kernel(yc_ref, yn_ref, ym_ref, ync_ref, ymc_ref, gn_ref, gm_ref, o_ref, *, lat_blocks):
    cw = yc_ref.shape[1]
    nw = yn_ref.shape[1]
    lat = pl.program_id(0) < lat_blocks
    o_ref[:, :cw] = yc_ref[...]
    yn = jnp.where(lat, yn_ref[...], ync_ref[...]).astype(F32)
    o_ref[:, cw:cw + nw] = (_rms(yn, nw) * gn_ref[...]).astype(o_ref.dtype)
    ym = jnp.where(lat, ym_ref[...], ymc_ref[...]).astype(F32)
    o_ref[:, cw + nw:] = (_rms(ym, ym.shape[1]) * gm_ref[...]).astype(o_ref.dtype)


def _merge(y_conv, y_na, y_mla, y_na_ctx, y_mla_ctx, gn, gm, rows):
    cw, nw, mw = y_conv.shape[1], y_na.shape[1], y_mla.shape[1]
    tm = ROW_TILE
    lat_blocks = y_na.shape[0] // tm
    ctx_last = y_na_ctx.shape[0] // tm - 1

    def lat_map(i):
        return (jnp.minimum(i, lat_blocks - 1), 0)

    def ctx_map(i):
        return (jnp.clip(i - lat_blocks, 0, ctx_last), 0)

    return pl.pallas_call(
        functools.partial(_merge_kernel, lat_blocks=lat_blocks),
        grid=(rows // tm,),
        in_specs=[pl.BlockSpec((tm, cw), lambda i: (i, 0)),
                  pl.BlockSpec((tm, nw), lat_map), pl.BlockSpec((tm, mw), lat_map),
                  pl.BlockSpec((tm, nw), ctx_map), pl.BlockSpec((tm, mw), ctx_map),
                  pl.BlockSpec((1, nw), lambda i: (0, 0)), pl.BlockSpec((1, mw), lambda i: (0, 0))],
        out_specs=pl.BlockSpec((tm, cw + nw + mw), lambda i: (i, 0)),
        out_shape=jax.ShapeDtypeStruct((rows, cw + nw + mw), BF16),
        compiler_params=_cparams(1),
        name="merge_norm",
    )(y_conv, y_na, y_mla, y_na_ctx, y_mla_ctx, gn, gm)


def _router_kernel(x_ref, g_ref, sc_ref, sh_ref, whi_ref, wlo_ref, rb_ref, tri_ref,
                   h_ref, idx_ref, wgt_ref, rank_ref, cnt_ref, base_ref):
    @pl.when(pl.program_id(0) == 0)
    def _():
        base_ref[...] = jnp.zeros(base_ref.shape, F32)

    x = x_ref[...]
    h = _rms(x, x.shape[-1]) * g_ref[...] * (1.0 + sc_ref[...]) + sh_ref[...]
    h_hi = h.astype(BF16)
    h_ref[...] = h_hi
    h_lo = (h - h_hi.astype(F32)).astype(BF16)
    dn = (((1,), (1,)), ((), ()))
    logits = (lax.dot_general(whi_ref[...], h_hi, dn, preferred_element_type=F32)
              + lax.dot_general(whi_ref[...], h_lo, dn, preferred_element_type=F32)
              + lax.dot_general(wlo_ref[...], h_hi, dn, preferred_element_type=F32))
    E, tm = logits.shape
    gs = E // N_GROUPS
    scores = jax.nn.sigmoid(logits)
    sel = scores + rb_ref[...]
    sel3 = sel.reshape(N_GROUPS, gs, tm)
    io3 = lax.broadcasted_iota(jnp.int32, sel3.shape, 1)
    m1 = jnp.max(sel3, axis=1, keepdims=True)
    i1 = jnp.min(jnp.where(sel3 == m1, io3, gs), axis=1, keepdims=True)
    m2 = jnp.max(jnp.where(io3 == i1, -jnp.inf, sel3), axis=1, keepdims=True)
    grp = m1 + m2
    iog = lax.broadcasted_iota(jnp.int32, grp.shape, 0)
    keep = jnp.zeros(grp.shape, jnp.bool_)
    for _ in range(TOPK_GROUPS):
        gm = jnp.max(grp, axis=0, keepdims=True)
        gi = jnp.min(jnp.where(grp == gm, iog, N_GROUPS), axis=0, keepdims=True)
        hit = iog == gi
        keep = jnp.logical_or(keep, hit)
        grp = jnp.where(hit, -jnp.inf, grp)
    selm = jnp.where(keep, sel3, -jnp.inf).reshape(E, tm)
    ioe = lax.broadcasted_iota(jnp.int32, (E, tm), 0)
    idxs, wgts, hits = [], [], []
    for _ in range(TOP_K):
        m = jnp.max(selm, axis=0, keepdims=True)
        ei = jnp.min(jnp.where(selm == m, ioe, E), axis=0, keepdims=True)
        hit = ioe == ei
        idxs.append(ei)
        hits.append(hit)
        wgts.append(jnp.sum(jnp.where(hit, scores, 0.0), axis=0, keepdims=True))
        selm = jnp.where(hit, -jnp.inf, selm)
    wsum = wgts[0]
    for w in wgts[1:]:
        wsum = wsum + w
    chosen = jnp.zeros((E, tm), F32)
    for hit in hits:
        chosen = chosen + jnp.where(hit, 1.0, 0.0)
    before = base_ref[:, 0:1] + jnp.dot(chosen.astype(BF16), tri_ref[...], preferred_element_type=F32)
    for k in range(TOP_K):
        idx_ref[k:k + 1, :] = idxs[k]
        wgt_ref[k:k + 1, :] = wgts[k] / wsum * ROUTE_SCALE
        rank_ref[k:k + 1, :] = jnp.sum(jnp.where(hits[k], before, 0.0), axis=0, keepdims=True).astype(jnp.int32)
    base_ref[...] = base_ref[...] + jnp.sum(chosen, axis=1, keepdims=True)
    cnt_ref[...] = base_ref[...]


def _router(x, g, mod_l, w_router, router_bias, row0, rows, n_lat, seq):
    D = x.shape[1]
    E = w_router.shape[1]
    tm = ROW_TILE
    b0 = row0 // tm
    mrow = functools.partial(_mod_row, tm=tm, n_lat=n_lat, seq=seq)
    wt = w_router.T
    whi = wt.astype(BF16)
    wlo = (wt - whi.astype(F32)).astype(BF16)
    tri = (np.arange(tm)[:, None] < np.arange(tm)[None, :]).astype(np.float32)
    return pl.pallas_call(
        _router_kernel,
        grid=(rows // tm,),
        in_specs=[pl.BlockSpec((tm, D), lambda i: (i + b0, 0)),
                  pl.BlockSpec((1, D), lambda i: (0, 0)),
                  pl.BlockSpec((None, 1, D), lambda i: (mrow(i + b0) * 6 + 4, 0, 0)),
                  pl.BlockSpec((None, 1, D), lambda i: (mrow(i + b0) * 6 + 3, 0, 0)),
                  pl.BlockSpec((E, D), lambda i: (0, 0)),
                  pl.BlockSpec((E, D), lambda i: (0, 0)),
                  pl.BlockSpec((E, 1), lambda i: (0, 0)),
                  pl.BlockSpec((tm, tm), lambda i: (0, 0))],
        out_specs=[pl.BlockSpec((tm, D), lambda i: (i, 0)),
                   pl.BlockSpec((TOP_K, tm), lambda i: (0, i)),
                   pl.BlockSpec((TOP_K, tm), lambda i: (0, i)),
                   pl.BlockSpec((TOP_K, tm), lambda i: (0, i)),
                   pl.BlockSpec((E, LANE), lambda i: (0, 0))],
        out_shape=[jax.ShapeDtypeStruct((rows, D), BF16),
                   jax.ShapeDtypeStruct((TOP_K, rows), jnp.int32),
                   jax.ShapeDtypeStruct((TOP_K, rows), F32),
                   jax.ShapeDtypeStruct((TOP_K, rows), jnp.int32),
                   jax.ShapeDtypeStruct((E, LANE), F32)],
        scratch_shapes=[pltpu.VMEM((E, LANE), F32)],
        compiler_params=_cparams(1),
        name="router",
    )(x, g.reshape(1, D), mod_l, mod_l, whi, wlo, router_bias.reshape(E, 1), jnp.asarray(tri, BF16))


def _expert_kernel(vt_ref, ve_ref, nv_ref, gs_ref, xs_ref, wg_ref, wu_ref, wd_ref, *rest):
    o_ref, wgb_ref, wub_ref, wdb_ref = rest[-4:]
    v = pl.program_id(0)
    T = xs_ref.shape[0]

    @pl.when(v < nv_ref[0])
    def _():
        e = ve_ref[v]
        tile = vt_ref[v]
        prev = jnp.maximum(v - 1, 0)
        new_expert = jnp.logical_or(v == 0, ve_ref[prev] != e)
        first_visit = jnp.logical_or(v == 0, vt_ref[prev] != tile)

        @pl.when(new_expert)
        def _():
            wgb_ref[...] = wg_ref[...].astype(BF16)
            wub_ref[...] = wu_ref[...].astype(BF16)
            wdb_ref[...] = wd_ref[...].astype(BF16)

        xs = xs_ref[...]
        g = jnp.dot(xs, wgb_ref[...], preferred_element_type=F32)
        u = jnp.dot(xs, wub_ref[...], preferred_element_type=F32)
        a = (g * jax.nn.sigmoid(g) * u).astype(BF16)
        y = jnp.dot(a, wdb_ref[...], preferred_element_type=F32).astype(o_ref.dtype)

        @pl.when(first_visit)
        def _():
            o_ref[...] = y

        @pl.when(jnp.logical_not(first_visit))
        def _():
            row = tile * T + lax.broadcasted_iota(jnp.int32, (T, 1), 0)
            mine = jnp.logical_and(row >= gs_ref[e], row < gs_ref[e + 1])
            o_ref[...] = jnp.where(mine, y, o_ref[...])


def _experts(xs, tile0, n_rows_total, y_prev, visits, group_start, w_gate, w_up, w_down, layer):
    visit_tile, visit_expert, n_visits = visits
    D = xs.shape[1]
    DE = w_gate.shape[3]
    T = MOE_TILE
    in_specs = [pl.BlockSpec((T, D), lambda v, vt, ve, nv, gs: (vt[v] - tile0, 0)),
                pl.BlockSpec((None, None, D, DE), lambda v, vt, ve, nv, gs: (layer, ve[v], 0, 0)),
                pl.BlockSpec((None, None, D, DE), lambda v, vt, ve, nv, gs: (layer, ve[v], 0, 0)),
                pl.BlockSpec((None, None, DE, D), lambda v, vt, ve, nv, gs: (layer, ve[v], 0, 0))]
    args = [visit_tile, visit_expert, n_visits, group_start, xs, w_gate, w_up, w_down]
    aliases = {}
    if y_prev is not None:
        in_specs.append(pl.BlockSpec(memory_space=pl.ANY))
        args.append(y_prev)
        aliases = {len(args) - 1: 0}
    grid_spec = pltpu.PrefetchScalarGridSpec(
        num_scalar_prefetch=4,
        grid=(visit_tile.shape[0],),
        in_specs=in_specs,
        out_specs=pl.BlockSpec((T, D), lambda v, vt, ve, nv, gs: (vt[v], 0)),
        scratch_shapes=[pltpu.VMEM((D, DE), BF16), pltpu.VMEM((D, DE), BF16), pltpu.VMEM((DE, D), BF16)],
    )
    return pl.pallas_call(
        _expert_kernel,
        grid_spec=grid_spec,
        out_shape=jax.ShapeDtypeStruct((n_rows_total, D), BF16),
        input_output_aliases=aliases,
        compiler_params=_cparams(1),
        name="experts",
    )(*args)


def _dispatch_plan(eidx, rank, counts, tile_bounds):
    K, rows = eidx.shape
    E = counts.shape[0]
    T = MOE_TILE
    gend = jnp.cumsum(counts)
    gstart = gend - counts
    eids = jnp.arange(E, dtype=jnp.int32)
    dest = rank + jnp.sum(jnp.where(eidx[None] == eids[:, None, None], gstart[:, None, None], 0), axis=0)
    token = jnp.broadcast_to(jnp.arange(rows, dtype=jnp.int32)[None], (K, rows))
    _, row_token = lax.sort((dest.reshape(-1), token.reshape(-1)), num_keys=1)
    visits = []
    for t_lo, t_hi in zip(tile_bounds[:-1], tile_bounds[1:]):
        first_tile = jnp.maximum(gstart // T, t_lo)
        last_tile = jnp.minimum((gend - 1) // T, t_hi - 1)
        n_vis_e = jnp.where(counts > 0, jnp.maximum(last_tile - first_tile + 1, 0), 0)
        vend = jnp.cumsum(n_vis_e)
        voff = vend - n_vis_e
        n_visits = vend[-1]
        v = jnp.arange(t_hi - t_lo + E, dtype=jnp.int32)
        ve = jnp.minimum(jnp.sum((v[:, None] >= vend[None, :]).astype(jnp.int32), axis=1), E - 1)
        onehot = (ve[:, None] == eids[None, :]).astype(jnp.int32)
        vt = jnp.sum(onehot * (first_tile - voff)[None, :], axis=1) + v
        vt = jnp.where(v < n_visits, vt, t_hi - 1).astype(jnp.int32)
        visits.append((vt, ve.astype(jnp.int32), n_visits.reshape(1).astype(jnp.int32)))
    group_start = jnp.concatenate([gstart, gend[-1:]]).astype(jnp.int32)
    return dest, row_token, visits, group_start


def _shared_kernel(h_ref, wg_ref, wu_ref, wd_ref, y8_ref, gate_ref, x_ref, g2_ref, o_ref):
    h = h_ref[...]
    g = jnp.dot(h, wg_ref[...], preferred_element_type=F32)
    u = jnp.dot(h, wu_ref[...], preferred_element_type=F32)
    a = (g * jax.nn.sigmoid(g) * u).astype(BF16)
    y = jnp.dot(a, wd_ref[...], preferred_element_type=F32)
    for k in range(y8_ref.shape[0]):
        y = y + gate_ref[:, k:k + 1] * y8_ref[k].astype(F32)
    o_ref[...] = x_ref[...] + g2_ref[...] * y


def _shared_combine(h2, wsg, wsu, wsd, y8, gates, x, mod_l, row0, rows, n_lat, seq):
    D = h2.shape[1]
    DE = wsg.shape[1]
    K = y8.shape[0]
    tm = COMBINE_TILE
    b0 = row0 // tm
    mrow = functools.partial(_mod_row, tm=tm, n_lat=n_lat, seq=seq)
    return pl.pallas_call(
        _shared_kernel,
        grid=(rows // tm,),
        in_specs=[pl.BlockSpec((tm, D), lambda i: (i + b0, 0)),
                  pl.BlockSpec((D, DE), lambda i: (0, 0)),
                  pl.BlockSpec((D, DE), lambda i: (0, 0)),
                  pl.BlockSpec((DE, D), lambda i: (0, 0)),
                  pl.BlockSpec((K, tm, D), lambda i: (0, i, 0)),
                  pl.BlockSpec((tm, K), lambda i: (i, 0)),
                  pl.BlockSpec((tm, D), lambda i: (i + b0, 0)),
                  pl.BlockSpec((None, 1, D), lambda i: (mrow(i + b0) * 6 + 5, 0, 0))],
        out_specs=pl.BlockSpec((tm, D), lambda i: (i + b0, 0)),
        out_shape=jax.ShapeDtypeStruct(x.shape, F32),
        input_output_aliases={6: 0},
        compiler_params=_cparams(1),
        name="shared_combine",
    )(h2, wsg, wsu, wsd, y8, gates, x, mod_l)


def _rope_tables(seq, pad_rows):
    t = np.arange(seq)
    row = (t // GRID_W).astype(np.float32)
    col = (t % GRID_W).astype(np.float32)
    n_freq = MLA_ROPE // 4
    inv = (ROPE_BASE ** (-jnp.arange(n_freq, dtype=F32) / n_freq))
    ang = jnp.concatenate([jnp.asarray(row)[:, None] * inv, jnp.asarray(col)[:, None] * inv], axis=-1)
    cos, sin = jnp.cos(ang), jnp.sin(ang)
    half = MLA_ROPE // 2
    z = jnp.zeros((seq, LANE - MLA_ROPE), F32)
    zh = jnp.zeros((seq, half), F32)
    c = jnp.concatenate([cos, cos, z], axis=-1)
    s1 = jnp.concatenate([-sin, zh, z], axis=-1)
    s2 = jnp.concatenate([zh, sin, z], axis=-1)
    ident = jnp.concatenate([jnp.ones((pad_rows, MLA_ROPE), F32), jnp.zeros((pad_rows, LANE - MLA_ROPE), F32)], -1)
    zero = jnp.zeros((pad_rows, LANE), F32)
    return (jnp.concatenate([c, ident], 0), jnp.concatenate([s1, zero], 0), jnp.concatenate([s2, zero], 0))


def _pad_lanes(v, n):
    return jnp.concatenate([v, jnp.zeros((n - v.shape[0],), v.dtype)]).reshape(1, n)


def kernel(x, c, ctx, c_ctx, ada_w, ada_b, norm1_g, w_in, conv_w, na_q_norm, na_k_norm, na_rpb,
           mla_q_lat_norm, mla_kv_lat_norm, w_uq, w_ukv, mla_q_norm, mla_k_norm, out_norm_g, w_o,
           norm2_g, w_router, router_bias, w_gate, w_up, w_down, ws_gate, ws_up, ws_down):
    B, S, D = x.shape
    CTX = ctx.shape[1]
    L = ada_w.shape[0]
    n_lat = B * S
    R = n_lat + B * CTX
    W = D // 4
    kvr = w_ukv.shape[1]
    nh_mla = w_uq.shape[2] // (MLA_NOPE + MLA_ROPE)
    E = w_router.shape[2]
    assert B == 2 and S % ROW_TILE == 0 and (B * CTX) % ROW_TILE == 0 and CTX % CONV_TILE == 0
    assert S % GRID_W == 0 and S // GRID_W >= NA_WIN_H and W % HEAD_DIM == 0 and kvr % LANE == 0
    assert S % 1024 == 0 and (TOP_K * B * CTX) % MOE_TILE == 0 and ROW_TILE % COMBINE_TILE == 0
    assert (S // GRID_W) % NA_GROUP == 0 and S // GRID_W >= NA_KROWS

    x_pair = (x.reshape(n_lat, D), ctx.reshape(B * CTX, D))
    w_in_t = jnp.swapaxes(w_in, 1, 2)
    cvecs = jnp.concatenate([c, c_ctx[None]], axis=0)
    mod = _modulation(cvecs, ada_w, ada_b)
    tabs = _rope_tables(S, ROW_TILE)

    for l in range(L):
        last = l == L - 1
        rows = n_lat if last else R
        mod_l = mod[l].reshape(8 * 6, 1, D)

        w_tail_t = jnp.concatenate([w_in_t[l, 7 * W:, :], jnp.zeros((LANE - MLA_ROPE, D), F32)], axis=0)
        wq = w_uq[l].reshape(W, nh_mla, MLA_NOPE + MLA_ROPE)
        wq = jnp.concatenate([wq, jnp.zeros((W, nh_mla, MLA_QPAD - MLA_NOPE - MLA_ROPE), F32)], axis=2)
        wq = wq.reshape(W, nh_mla * MLA_QPAD).astype(BF16)
        wkv = w_ukv[l].reshape(kvr, nh_mla, MLA_NOPE + MLA_V)
        wk = wkv[:, :, :MLA_NOPE].reshape(kvr, nh_mla * MLA_NOPE).astype(BF16)
        wv = wkv[:, :, MLA_NOPE:].reshape(kvr, nh_mla * MLA_V).astype(BF16)
        gq_n = mla_q_norm[l][:MLA_NOPE].reshape(1, LANE)
        gq_r = _pad_lanes(mla_q_norm[l][MLA_NOPE:], LANE)
        gk_n = mla_k_norm[l][:MLA_NOPE].reshape(1, LANE)
        gk_r = _pad_lanes(mla_k_norm[l][MLA_NOPE:], LANE)
        og = out_norm_g[l]

        h = _norm_mod(x_pair, R, norm1_g[l], mod_l, 1, 0, n_lat, S)
        z_main = _matmul_ws(h, w_in_t, l, 7 * W, min(512, W), BF16)
        z_tail = _matmul_ws(h, w_tail_t, None, w_tail_t.shape[0], w_tail_t.shape[0], BF16)
        nq, nk, cq, ckv, kr = _post_in(z_main, z_tail, na_q_norm[l].reshape(1, LANE),
                                       na_k_norm[l].reshape(1, LANE), mla_q_lat_norm[l].reshape(1, W),
                                       mla_kv_lat_norm[l].reshape(1, kvr), gk_r, tabs, n_lat, S)
        qf = _uq_matmul(cq, wq, gq_n, gq_r, tabs, n_lat, S)
        kf = _uk_matmul(ckv, wk, gk_n, kr)
        vf = _matmul(ckv, wv, BF16, 512)

        y_conv = _conv_mixer(z_main, conv_w[l], og[:W].reshape(1, W), rows, n_lat, S, CTX)
        y_na = _na_attention(nq, nk, z_main, _na_bias_table(na_rpb[l], S // GRID_W), B, S, CTX, n_lat)
        y_mla = _mla_attention(qf, kf, vf, B, S, CTX, n_lat)
        if last:
            y_na_c, y_mla_c = y_na, y_mla
        else:
            nh_na = W // HEAD_DIM
            y_na_c = _ctx_attention(nq, nk, z_main, nh_na, HEAD_DIM, HEAD_DIM, 5 * nh_na, B, CTX, n_lat)
            y_mla_c = _ctx_attention(qf, kf, vf, nh_mla, MLA_QPAD, MLA_V, 0, B, CTX, n_lat)
        y = _merge(y_conv, y_na, y_mla, y_na_c, y_mla_c,
                   og[W:2 * W].reshape(1, W), og[2 * W:].reshape(1, D - 2 * W), rows)
        xs = _wo_matmul(y, w_o, l, x_pair, mod_l, rows, n_lat, S)

        wsg, wsu, wsd = ws_gate[l].astype(BF16), ws_up[l].astype(BF16), ws_down[l].astype(BF16)
        h2, eidx, wgt, rank, cnt = _router(xs, norm2_g[l], mod_l, w_router[l], router_bias[l],
                                           0, rows, n_lat, S)
        n_pairs = TOP_K * rows
        n_tiles = n_pairs // MOE_TILE
        tile_bounds = [n_tiles * p // MOE_PARTS for p in range(MOE_PARTS + 1)]
        dest, row_token, visits, gstart = _dispatch_plan(eidx, rank, cnt[:, 0].astype(jnp.int32), tile_bounds)
        y_rows = None
        for p in range(MOE_PARTS):
            part_tokens = row_token[tile_bounds[p] * MOE_TILE:tile_bounds[p + 1] * MOE_TILE]
            gathered = h2.at[part_tokens].get(mode="promise_in_bounds")
            y_rows = _experts(gathered, tile_bounds[p], n_pairs, y_rows, visits[p], gstart,
                              w_gate, w_up, w_down, l)
        gates = wgt.T
        n_blocks = rows // ROW_TILE
        row_bounds = [(n_blocks * p // MOE_PARTS) * ROW_TILE for p in range(MOE_PARTS + 1)]
        for row0, row1 in zip(row_bounds[:-1], row_bounds[1:]):
            y8 = y_rows.at[dest[:, row0:row1].reshape(-1)].get(mode="promise_in_bounds")
            xs = _shared_combine(h2, wsg, wsu, wsd, y8.reshape(TOP_K, row1 - row0, D), gates[row0:row1],
                                 xs, mod_l, row0, row1 - row0, n_lat, S)
        x_pair = (xs, xs)

    return xs[:n_lat].reshape(B, S, D)
```

```python
import functools

import numpy as np
import jax
import jax.numpy as jnp
from jax import lax
from jax.experimental import pallas as pl
from jax.experimental.pallas import tpu as pltpu

F32 = jnp.float32
BF16 = jnp.bfloat16

GRID_W = 64
HEAD_DIM = 128
MLA_NOPE = 128
MLA_ROPE = 64
MLA_V = 128
MLA_QPAD = 256
NA_WIN_H = 8
NA_WIN_W = 16
NA_GROUP = 8
NA_KROWS = NA_GROUP + NA_WIN_H
N_GROUPS = 8
TOPK_GROUPS = 4
TOP_K = 8
ROUTE_SCALE = 2.5
ROPE_BASE = 10000.0
EPS = 1e-6
NEG = -1e30
LOG2E = 1.4426950408889634

LANE = 128
ROW_TILE = 512
CONV_TILE = 256
MOE_TILE = 256
COMBINE_TILE = 128
MOE_PARTS = 8
VMEM_LIMIT = 56 * 1024 * 1024


def _cparams(n_axes):
    return pltpu.CompilerParams(dimension_semantics=("arbitrary",) * n_axes,
                                vmem_limit_bytes=VMEM_LIMIT)


def _rms(x, n):
    return x * lax.rsqrt(jnp.sum(x * x, axis=-1, keepdims=True) * (1.0 / n) + EPS)


def _mod_kernel(sb_ref, w_ref, b_ref, o_ref):
    tn = w_ref.shape[1]
    o_ref[...] = jnp.zeros(o_ref.shape, F32)
    for cblk in range(tn // LANE):
        wc = w_ref[:, cblk * LANE:(cblk + 1) * LANE]
        for r in range(3):
            acc = jnp.sum(wc * sb_ref[r], axis=0, keepdims=True)
            o_ref[r:r + 1, cblk * LANE:(cblk + 1) * LANE] = acc + b_ref[:, cblk * LANE:(cblk + 1) * LANE]


def _modulation(cvecs, ada_w, ada_b):
    L, D, D6 = ada_w.shape
    tn = 512
    s = cvecs * jax.nn.sigmoid(cvecs)
    sb = jnp.broadcast_to(s[:, :, None], (3, D, LANE))
    return pl.pallas_call(
        _mod_kernel,
        grid=(L, D6 // tn),
        in_specs=[pl.BlockSpec((3, D, LANE), lambda l, j: (0, 0, 0)),
                  pl.BlockSpec((None, D, tn), lambda l, j: (l, 0, j)),
                  pl.BlockSpec((None, 1, tn), lambda l, j: (l, 0, j))],
        out_specs=pl.BlockSpec((None, 8, tn), lambda l, j: (l, 0, j)),
        out_shape=jax.ShapeDtypeStruct((L, 8, D6), F32),
        compiler_params=_cparams(2),
        name="modulation",
    )(sb, ada_w, ada_b.reshape(L, 1, D6))


def _mod_row(i, tm, n_lat, seq):
    r0 = i * tm
    return jnp.where(r0 < n_lat, r0 // seq, n_lat // seq)


def _split_rows(x_pair, tm):
    head, tail = x_pair
    head_blocks = head.shape[0] // tm
    tail_last = tail.shape[0] // tm - 1
    return (head_blocks, lambda i: jnp.minimum(i, head_blocks - 1),
            lambda i: jnp.clip(i - head_blocks, 0, tail_last))


def _norm_mod_kernel(xa_ref, xb_ref, g_ref, sc_ref, sh_ref, o_ref, *, head_blocks):
    x = jnp.where(pl.program_id(0) < head_blocks, xa_ref[...], xb_ref[...])
    y = _rms(x, x.shape[-1]) * g_ref[...]
    o_ref[...] = (y * (1.0 + sc_ref[...]) + sh_ref[...]).astype(o_ref.dtype)


def _norm_mod(x_pair, R, g, mod_l, which_sc, which_sh, n_lat, seq):
    D = x_pair[0].shape[1]
    tm = ROW_TILE
    mrow = functools.partial(_mod_row, tm=tm, n_lat=n_lat, seq=seq)
    head_blocks, head_map, tail_map = _split_rows(x_pair, tm)
    return pl.pallas_call(
        functools.partial(_norm_mod_kernel, head_blocks=head_blocks),
        grid=(R // tm,),
        in_specs=[pl.BlockSpec((tm, D), lambda i: (head_map(i), 0)),
                  pl.BlockSpec((tm, D), lambda i: (tail_map(i), 0)),
                  pl.BlockSpec((1, D), lambda i: (0, 0)),
                  pl.BlockSpec((None, 1, D), lambda i: (mrow(i) * 6 + which_sc, 0, 0)),
                  pl.BlockSpec((None, 1, D), lambda i: (mrow(i) * 6 + which_sh, 0, 0))],
        out_specs=pl.BlockSpec((tm, D), lambda i: (i, 0)),
        out_shape=jax.ShapeDtypeStruct((R, D), BF16),
        compiler_params=_cparams(1),
        name="norm_mod",
    )(x_pair[0], x_pair[1], g.reshape(1, D), mod_l, mod_l)


def _mm_kernel(x_ref, w_ref, o_ref):
    o_ref[...] = jnp.dot(x_ref[...], w_ref[...], preferred_element_type=F32).astype(o_ref.dtype)


def _matmul(x, w, out_dtype, tn, rows=None):
    R, K = x.shape
    rows = R if rows is None else rows
    N = w.shape[1]
    tm = ROW_TILE
    return pl.pallas_call(
        _mm_kernel,
        grid=(rows // tm, N // tn),
        in_specs=[pl.BlockSpec((tm, K), lambda i, j: (i, 0)),
                  pl.BlockSpec((K, tn), lambda i, j: (0, j))],
        out_specs=pl.BlockSpec((tm, tn), lambda i, j: (i, j)),
        out_shape=jax.ShapeDtypeStruct((rows, N), out_dtype),
        compiler_params=_cparams(2),
        name="matmul",
    )(x, w)


def _mm_ws_kernel(x_ref, w_ref, o_ref, wb_ref):
    @pl.when(pl.program_id(1) == 0)
    def _():
        wb_ref[...] = w_ref[...].astype(BF16)

    o_ref[...] = lax.dot_general(x_ref[...], wb_ref[...], (((1,), (1,)), ((), ())),
                                 preferred_element_type=F32).astype(o_ref.dtype)


def _matmul_ws(x, wt, layer, n_cols, tn, out_dtype):
    R, K = x.shape
    tm = ROW_TILE
    if layer is None:
        w_spec = pl.BlockSpec((tn, K), lambda j, i: (j, 0))
    else:
        w_spec = pl.BlockSpec((None, tn, K), lambda j, i: (layer, j, 0))
    return pl.pallas_call(
        _mm_ws_kernel,
        grid=(n_cols // tn, R // tm),
        in_specs=[pl.BlockSpec((tm, K), lambda j, i: (i, 0)), w_spec],
        out_specs=pl.BlockSpec((tm, tn), lambda j, i: (i, j)),
        out_shape=jax.ShapeDtypeStruct((R, n_cols), out_dtype),
        scratch_shapes=[pltpu.VMEM((tn, K), BF16)],
        compiler_params=_cparams(2),
        name="matmul_ws",
    )(x, wt)


def _rope_rotate(t, c_ref, s1_ref, s2_ref):
    return (t * c_ref[...] + pltpu.roll(t, LANE - MLA_ROPE // 2, axis=1) * s1_ref[...]
            + pltpu.roll(t, MLA_ROPE // 2, axis=1) * s2_ref[...])


def _uq_kernel(x_ref, w_ref, gn_ref, gr_ref, c_ref, s1_ref, s2_ref, o_ref):
    scale = (MLA_NOPE + MLA_ROPE) ** -0.5 * LOG2E
    tm = x_ref.shape[0]
    half = tm // 2
    for r0 in (0, half):
        rs = slice(r0, r0 + half)
        acc = jnp.dot(x_ref[rs, :], w_ref[...], preferred_element_type=F32)
        for h in range(acc.shape[1] // MLA_QPAD):
            o = h * MLA_QPAD
            a = _rms(acc[:, o:o + MLA_NOPE], MLA_NOPE) * gn_ref[...]
            r = _rms(acc[:, o + MLA_NOPE:o + MLA_QPAD], MLA_ROPE) * gr_ref[...]
            r = (r * c_ref[rs, :] + pltpu.roll(r, LANE - MLA_ROPE // 2, axis=1) * s1_ref[rs, :]
                 + pltpu.roll(r, MLA_ROPE // 2, axis=1) * s2_ref[rs, :])
            o_ref[rs, o:o + MLA_NOPE] = (a * scale).astype(o_ref.dtype)
            o_ref[rs, o + MLA_NOPE:o + MLA_QPAD] = (r * scale).astype(o_ref.dtype)


def _rope_block(i, tm, n_lat, seq):
    return jnp.where(i * tm < n_lat, i % (seq // tm), seq // tm)


def _uq_matmul(x, w, gn, gr, tabs, n_lat, seq):
    R, K = x.shape
    N = w.shape[1]
    tm, tn = ROW_TILE, 1024
    rb = functools.partial(_rope_block, tm=tm, n_lat=n_lat, seq=seq)
    tab_spec = pl.BlockSpec((tm, LANE), lambda i, j: (rb(i), 0))
    vec_spec = pl.BlockSpec((1, LANE), lambda i, j: (0, 0))
    return pl.pallas_call(
        _uq_kernel,
        grid=(R // tm, N // tn),
        in_specs=[pl.BlockSpec((tm, K), lambda i, j: (i, 0)),
                  pl.BlockSpec((K, tn), lambda i, j: (0, j)),
                  vec_spec, vec_spec, tab_spec, tab_spec, tab_spec],
        out_specs=pl.BlockSpec((tm, tn), lambda i, j: (i, j)),
        out_shape=jax.ShapeDtypeStruct((R, N), BF16),
        compiler_params=_cparams(2),
        name="uq_matmul",
    )(x, w, gn, gr, *tabs)


def _uk_kernel(x_ref, w_ref, gk_ref, kr_ref, o_ref):
    half = x_ref.shape[0] // 2
    for r0 in (0, half):
        rs = slice(r0, r0 + half)
        acc = jnp.dot(x_ref[rs, :], w_ref[...], preferred_element_type=F32)
        for h in range(acc.shape[1] // MLA_NOPE):
            a = _rms(acc[:, h * MLA_NOPE:(h + 1) * MLA_NOPE], MLA_NOPE) * gk_ref[...]
            o_ref[rs, h * MLA_QPAD:h * MLA_QPAD + MLA_NOPE] = a.astype(o_ref.dtype)
            o_ref[rs, h * MLA_QPAD + MLA_NOPE:(h + 1) * MLA_QPAD] = kr_ref[rs, :]


def _uk_matmul(x, w, gk, kr_pad):
    R, K = x.shape
    N = w.shape[1]
    tm, tn = ROW_TILE, 512
    return pl.pallas_call(
        _uk_kernel,
        grid=(R // tm, N // tn),
        in_specs=[pl.BlockSpec((tm, K), lambda i, j: (i, 0)),
                  pl.BlockSpec((K, tn), lambda i, j: (0, j)),
                  pl.BlockSpec((1, LANE), lambda i, j: (0, 0)),
                  pl.BlockSpec((tm, LANE), lambda i, j: (i, 0))],
        out_specs=pl.BlockSpec((tm, 2 * tn), lambda i, j: (i, j)),
        out_shape=jax.ShapeDtypeStruct((R, 2 * N), BF16),
        compiler_params=_cparams(2),
        name="uk_matmul",
    )(x, w, gk, kr_pad)


def _wo_kernel(y_ref, w_ref, xa_ref, xb_ref, g_ref, o_ref, wb_ref, *, head_blocks):
    @pl.when(pl.program_id(1) == 0)
    def _():
        wb_ref[...] = w_ref[...].astype(BF16)

    acc = jnp.dot(y_ref[...], wb_ref[...], preferred_element_type=F32)
    x = jnp.where(pl.program_id(1) < head_blocks, xa_ref[...], xb_ref[...])
    o_ref[...] = x + g_ref[...] * acc


def _wo_matmul(y, w_o, layer, x_pair, mod_l, rows, n_lat, seq):
    K = y.shape[1]
    N = w_o.shape[2]
    tm, tn = ROW_TILE, 512
    mrow = functools.partial(_mod_row, tm=tm, n_lat=n_lat, seq=seq)
    head_blocks, head_map, tail_map = _split_rows(x_pair, tm)
    return pl.pallas_call(
        functools.partial(_wo_kernel, head_blocks=head_blocks),
        grid=(N // tn, rows // tm),
        in_specs=[pl.BlockSpec((tm, K), lambda j, i: (i, 0)),
                  pl.BlockSpec((None, K, tn), lambda j, i: (layer, 0, j)),
                  pl.BlockSpec((tm, tn), lambda j, i: (head_map(i), j)),
                  pl.BlockSpec((tm, tn), lambda j, i: (tail_map(i), j)),
                  pl.BlockSpec((None, 1, tn), lambda j, i: (mrow(i) * 6 + 2, 0, j))],
        out_specs=pl.BlockSpec((tm, tn), lambda j, i: (i, j)),
        out_shape=jax.ShapeDtypeStruct((rows, N), F32),
        scratch_shapes=[pltpu.VMEM((K, tn), BF16)],
        compiler_params=_cparams(2),
        name="wo_matmul",
    )(y, w_o, x_pair[0], x_pair[1], mod_l)


def _post_in_kernel(nq_ref, nk_ref, cq_ref, tail_ref, gq_ref, gk_ref, gql_ref, gkvl_ref, gkr_ref,
                    c_ref, s1_ref, s2_ref, nq_o, nk_o, cq_o, ckv_o, kr_o):
    nh = nq_ref.shape[1] // HEAD_DIM
    scale = HEAD_DIM ** -0.5 * LOG2E
    for h in range(nh):
        sl = slice(h * HEAD_DIM, (h + 1) * HEAD_DIM)
        q = nq_ref[:, sl].astype(F32)
        nq_o[:, sl] = (_rms(q, HEAD_DIM) * gq_ref[...] * scale).astype(nq_o.dtype)
        k = nk_ref[:, sl].astype(F32)
        nk_o[:, sl] = (_rms(k, HEAD_DIM) * gk_ref[...]).astype(nk_o.dtype)
    cq = cq_ref[...].astype(F32)
    cq_o[...] = (_rms(cq, cq.shape[1]) * gql_ref[...]).astype(cq_o.dtype)
    kvr = ckv_o.shape[1]
    ckv = tail_ref[:, :kvr].astype(F32)
    ckv_o[...] = (_rms(ckv, kvr) * gkvl_ref[...]).astype(ckv_o.dtype)
    kr = tail_ref[:, kvr:kvr + LANE].astype(F32)
    kr = _rms(kr, MLA_ROPE) * gkr_ref[...]
    kr_o[...] = _rope_rotate(kr, c_ref, s1_ref, s2_ref).astype(kr_o.dtype)


def _post_in(z_main, z_tail, gq, gk, gql, gkvl, gkr, tabs, n_lat, seq):
    R = z_main.shape[0]
    W = z_main.shape[1] // 7
    kvr = z_tail.shape[1] - LANE
    tm = ROW_TILE
    rb = functools.partial(_rope_block, tm=tm, n_lat=n_lat, seq=seq)
    tab_spec = pl.BlockSpec((tm, LANE), lambda i: (rb(i), 0))

    def vec(n):
        return pl.BlockSpec((1, n), lambda i: (0, 0))

    def col(j):
        return pl.BlockSpec((tm, W), lambda i: (i, j))

    return pl.pallas_call(
        _post_in_kernel,
        grid=(R // tm,),
        in_specs=[col(3), col(4), col(6), pl.BlockSpec((tm, kvr + LANE), lambda i: (i, 0)),
                  vec(LANE), vec(LANE), vec(W), vec(kvr), vec(LANE), tab_spec, tab_spec, tab_spec],
        out_specs=[pl.BlockSpec((tm, W), lambda i: (i, 0)), pl.BlockSpec((tm, W), lambda i: (i, 0)),
                   pl.BlockSpec((tm, W), lambda i: (i, 0)), pl.BlockSpec((tm, kvr), lambda i: (i, 0)),
                   pl.BlockSpec((tm, LANE), lambda i: (i, 0))],
        out_shape=[jax.ShapeDtypeStruct((R, W), BF16), jax.ShapeDtypeStruct((R, W), BF16),
                   jax.ShapeDtypeStruct((R, W), BF16), jax.ShapeDtypeStruct((R, kvr), BF16),
                   jax.ShapeDtypeStruct((R, LANE), BF16)],
        compiler_params=_cparams(1),
        name="post_in",
    )(z_main, z_main, z_main, z_tail, gq, gk, gql, gkvl, gkr, *tabs)


def _conv_kernel(b_ref, c_ref, u_ref, cp_ref, up_ref, cn_ref, un_ref, w_ref, g_ref, o_ref,
                 *, n_lat, seq, ctx_len):
    tm, W = c_ref.shape
    row0 = pl.program_id(0) * tm
    lat = row0 < n_lat
    pos = jnp.where(lat, row0 % seq, (row0 - n_lat) % ctx_len)
    slen = jnp.where(lat, seq, ctx_len)
    has_prev = (pos > 0).astype(F32)
    has_next = (pos + tm < slen).astype(F32)
    v = c_ref[...].astype(F32) * u_ref[...].astype(F32)
    vprev = cp_ref[7:8, :].astype(F32) * up_ref[7:8, :].astype(F32) * has_prev
    vnext = cn_ref[0:1, :].astype(F32) * un_ref[0:1, :].astype(F32) * has_next
    rid = lax.broadcasted_iota(jnp.int32, (tm, W), 0)
    v_dn = jnp.where(rid == 0, vprev, pltpu.roll(v, 1, axis=0))
    v_up = jnp.where(rid == tm - 1, vnext, pltpu.roll(v, tm - 1, axis=0))
    y = w_ref[0:1, :] * v_dn + w_ref[1:2, :] * v + w_ref[2:3, :] * v_up
    y = b_ref[...].astype(F32) * y
    o_ref[...] = (_rms(y, W) * g_ref[...]).astype(o_ref.dtype)


def _conv_mixer(z_main, conv_w, g, rows, n_lat, seq, ctx_len):
    R = z_main.shape[0]
    W = z_main.shape[1] // 7
    tm = CONV_TILE
    nb8 = R // 8

    def cur(j):
        return pl.BlockSpec((tm, W), lambda i: (i, j))

    def prev(j):
        return pl.BlockSpec((8, W), lambda i: (jnp.maximum(i * (tm // 8) - 1, 0), j))

    def nxt(j):
        return pl.BlockSpec((8, W), lambda i: (jnp.minimum((i + 1) * (tm // 8), nb8 - 1), j))

    return pl.pallas_call(
        functools.partial(_conv_kernel, n_lat=n_lat, seq=seq, ctx_len=ctx_len),
        grid=(rows // tm,),
        in_specs=[cur(0), cur(1), cur(2), prev(1), prev(2), nxt(1), nxt(2),
                  pl.BlockSpec((3, W), lambda i: (0, 0)), pl.BlockSpec((1, W), lambda i: (0, 0))],
        out_specs=pl.BlockSpec((tm, W), lambda i: (i, 0)),
        out_shape=jax.ShapeDtypeStruct((rows, W), BF16),
        compiler_params=_cparams(1),
        name="conv_mixer",
    )(z_main, z_main, z_main, z_main, z_main, z_main, z_main, conv_w, g)


def _na_group_plan(rows):
    n_groups = rows // NA_GROUP
    kinds = [(0, 0), (NA_GROUP, NA_GROUP - NA_WIN_H // 2), ((n_groups - 1) * NA_GROUP, rows - NA_KROWS)]
    plane = np.full((3, NA_GROUP, NA_KROWS), 2 * NA_WIN_H - 1, np.int64)
    for t, (r_first, k_first) in enumerate(kinds):
        for qr in range(NA_GROUP):
            r = r_first + qr
            r0 = min(max(r - NA_WIN_H // 2, 0), rows - NA_WIN_H)
            for kr in range(NA_KROWS):
                kk = k_first + kr
                if r0 <= kk < r0 + NA_WIN_H:
                    plane[t, qr, kr] = kk - r + NA_WIN_H - 1
    return plane


def _na_bias_table(rpb, rows):
    qc = np.arange(GRID_W)[:, None]
    kc = np.arange(GRID_W)[None, :]
    cstart = np.clip(qc - NA_WIN_W // 2, 0, GRID_W - NA_WIN_W)
    ok = (kc >= cstart) & (kc < cstart + NA_WIN_W)
    dc = np.clip(kc - qc + NA_WIN_W - 1, 0, 2 * NA_WIN_W - 2)
    H = rpb.shape[0]
    onehot = (dc[None] == np.arange(2 * NA_WIN_W - 1)[:, None, None]).astype(np.float32)
    cols = jnp.einsum("hdc,cqk->hdqk", rpb.astype(F32), onehot, precision=lax.Precision.HIGHEST)
    cols = jnp.where(ok[None, None], cols * LOG2E, NEG)
    planes = jnp.concatenate([cols, jnp.full((H, 1, GRID_W, GRID_W), NEG, F32)], axis=1)
    pick = (_na_group_plan(rows)[..., None] == np.arange(2 * NA_WIN_H)).astype(np.float32)
    g = jnp.einsum("tqkp,hpxy->htqxky", pick, planes, precision=lax.Precision.HIGHEST)
    return g.reshape(H, 3, NA_GROUP * GRID_W, NA_KROWS * GRID_W)


def _na_kernel(q_ref, k_ref, v_ref, kc_ref, vc_ref, bias_ref, o_ref):
    rows = q_ref.shape[0] // GRID_W
    n_groups = rows // NA_GROUP
    nq = NA_GROUP * GRID_W
    nk = NA_KROWS * GRID_W
    dn = (((1,), (1,)), ((), ()))
    kc = kc_ref[...]
    vc = vc_ref[...]

    def body(g, carry):
        r_first = g * NA_GROUP
        k_first = jnp.clip(r_first - NA_WIN_H // 2, 0, rows - NA_KROWS)
        kind = jnp.where(g == 0, 0, jnp.where(g == n_groups - 1, 2, 1))
        qoff = pl.multiple_of(r_first * GRID_W, GRID_W)
        koff = pl.multiple_of(k_first * GRID_W, GRID_W)
        q = q_ref[pl.ds(qoff, nq), :]
        k = k_ref[pl.ds(koff, nk), :]
        v = v_ref[pl.ds(koff, nk), :]
        s = lax.dot_general(q, k, dn, preferred_element_type=F32) + bias_ref[kind]
        sc = lax.dot_general(q, kc, dn, preferred_element_type=F32)
        m = jnp.maximum(jnp.max(s, axis=-1, keepdims=True), jnp.max(sc, axis=-1, keepdims=True))
        p = jnp.exp2(s - m)
        pc = jnp.exp2(sc - m)
        l = jnp.sum(p, axis=-1, keepdims=True) + jnp.sum(pc, axis=-1, keepdims=True)
        o = (jnp.dot(p.astype(BF16), v, preferred_element_type=F32)
             + jnp.dot(pc.astype(BF16), vc, preferred_element_type=F32))
        o_ref[pl.ds(qoff, nq), :] = (o / l).astype(o_ref.dtype)
        return carry

    lax.fori_loop(0, n_groups, body, 0, unroll=2)


def _na_attention(nq, nk, z_main, bias_tab, B, seq, ctx_len, n_lat):
    W = nq.shape[1]
    nh = W // HEAD_DIM
    cb = n_lat // ctx_len
    voff = 5 * nh
    return pl.pallas_call(
        _na_kernel,
        grid=(nh, B),
        in_specs=[pl.BlockSpec((seq, HEAD_DIM), lambda h, b: (b, h)),
                  pl.BlockSpec((seq, HEAD_DIM), lambda h, b: (b, h)),
                  pl.BlockSpec((seq, HEAD_DIM), lambda h, b: (b, voff + h)),
                  pl.BlockSpec((ctx_len, HEAD_DIM), lambda h, b: (cb + b, h)),
                  pl.BlockSpec((ctx_len, HEAD_DIM), lambda h, b: (cb + b, voff + h)),
                  pl.BlockSpec((None, 3, NA_GROUP * GRID_W, NA_KROWS * GRID_W), lambda h, b: (h, 0, 0, 0))],
        out_specs=pl.BlockSpec((seq, HEAD_DIM), lambda h, b: (b, h)),
        out_shape=jax.ShapeDtypeStruct((n_lat, W), BF16),
        compiler_params=_cparams(2),
        name="na_attention",
    )(nq, nk, z_main, nk, z_main, bias_tab)


def _flash_kernel(q_ref, k_ref, v_ref, kc_ref, vc_ref, o_ref, *, tk):
    tq = q_ref.shape[0]
    dv = v_ref.shape[1]
    dn = (((1,), (1,)), ((), ()))
    q = q_ref[...]

    def step(k, v, m, l, acc):
        s = lax.dot_general(q, k, dn, preferred_element_type=F32)
        m_new = jnp.maximum(m, jnp.max(s, axis=-1, keepdims=True))
        alpha = jnp.exp2(m - m_new)
        p = jnp.exp2(s - m_new)
        l = alpha * l + jnp.sum(p, axis=-1, keepdims=True)
        acc = alpha * acc + jnp.dot(p.astype(BF16), v, preferred_element_type=F32)
        return m_new, l, acc

    def body(j, carry):
        off = pl.multiple_of(j * tk, tk)
        return step(k_ref[pl.ds(off, tk), :], v_ref[pl.ds(off, tk), :], *carry)

    init = (jnp.full((tq, 1), NEG, F32), jnp.zeros((tq, 1), F32), jnp.zeros((tq, dv), F32))
    n_chunks = k_ref.shape[0] // tk
    m, l, acc = lax.fori_loop(0, n_chunks, body, init, unroll=min(n_chunks, 8))
    m, l, acc = step(kc_ref[...], vc_ref[...], m, l, acc)
    o_ref[...] = (acc / l).astype(o_ref.dtype)


def _mla_attention(qf, kf, vf, B, seq, ctx_len, n_lat):
    R = qf.shape[0]
    nh = qf.shape[1] // MLA_QPAD
    tq, tk = 1024, 1024
    cb = n_lat // ctx_len
    nqb = seq // tq
    return pl.pallas_call(
        functools.partial(_flash_kernel, tk=tk),
        grid=(B, nh, nqb),
        in_specs=[pl.BlockSpec((tq, MLA_QPAD), lambda b, h, i: (b * nqb + i, h)),
                  pl.BlockSpec((seq, MLA_QPAD), lambda b, h, i: (b, h)),
                  pl.BlockSpec((seq, MLA_V), lambda b, h, i: (b, h)),
                  pl.BlockSpec((ctx_len, MLA_QPAD), lambda b, h, i: (cb + b, h)),
                  pl.BlockSpec((ctx_len, MLA_V), lambda b, h, i: (cb + b, h))],
        out_specs=pl.BlockSpec((tq, MLA_V), lambda b, h, i: (b * nqb + i, h)),
        out_shape=jax.ShapeDtypeStruct((n_lat, nh * MLA_V), BF16),
        compiler_params=_cparams(3),
        name="mla_attention",
    )(qf, kf, vf, kf, vf)


def _ctx_attn_kernel(q_ref, k_ref, v_ref, o_ref):
    s = lax.dot_general(q_ref[...], k_ref[...], (((1,), (1,)), ((), ())), preferred_element_type=F32)
    p = jnp.exp2(s - jnp.max(s, axis=-1, keepdims=True))
    o = jnp.dot(p.astype(BF16), v_ref[...], preferred_element_type=F32)
    o_ref[...] = (o / jnp.sum(p, axis=-1, keepdims=True)).astype(o_ref.dtype)


def _ctx_attention(q, k, v, nh, dq, dv, voff, B, ctx_len, n_lat):
    cb = n_lat // ctx_len
    return pl.pallas_call(
        _ctx_attn_kernel,
        grid=(B, nh),
        in_specs=[pl.BlockSpec((ctx_len, dq), lambda b, h: (cb + b, h)),
                  pl.BlockSpec((ctx_len, dq), lambda b, h: (cb + b, h)),
                  pl.BlockSpec((ctx_len, dv), lambda b, h: (cb + b, voff + h))],
        out_specs=pl.BlockSpec((ctx_len, dv), lambda b, h: (b, h)),
        out_shape=jax.ShapeDtypeStruct((B * ctx_len, nh * dv), BF16),
        compiler_params=_cparams(2),
        name="ctx_attention",
    )(q, k, v)


def _merge_kernel(yc_ref, yn_ref, ym_ref, ync_ref, ymc_ref, gn_ref, gm_ref, o_ref, *, lat_blocks):
    cw = yc_ref.shape[1]
    nw = yn_ref.shape[1]
    lat = pl.program_id(0) < lat_blocks
    o_ref[:, :cw] = yc_ref[...]
    yn = jnp.where(lat, yn_ref[...], ync_ref[...]).astype(F32)
    o_ref[:, cw:cw + nw] = (_rms(yn, nw) * gn_ref[...]).astype(o_ref.dtype)
    ym = jnp.where(lat, ym_ref[...], ymc_ref[...]).astype(F32)
    o_ref[:, cw + nw:] = (_rms(ym, ym.shape[1]) * gm_ref[...]).astype(o_ref.dtype)


def _merge(y_conv, y_na, y_mla, y_na_ctx, y_mla_ctx, gn, gm, rows):
    cw, nw, mw = y_conv.shape[1], y_na.shape[1], y_mla.shape[1]
    tm = ROW_TILE
    lat_blocks = y_na.shape[0] // tm
    ctx_last = y_na_ctx.shape[0] // tm - 1

    def lat_map(i):
        return (jnp.minimum(i, lat_blocks - 1), 0)

    def ctx_map(i):
        return (jnp.clip(i - lat_blocks, 0, ctx_last), 0)

    return pl.pallas_call(
        functools.partial(_merge_kernel, lat_blocks=lat_blocks),
        grid=(rows // tm,),
        in_specs=[pl.BlockSpec((tm, cw), lambda i: (i, 0)),
                  pl.BlockSpec((tm, nw), lat_map), pl.BlockSpec((tm, mw), lat_map),
                  pl.BlockSpec((tm, nw), ctx_map), pl.BlockSpec((tm, mw), ctx_map),
                  pl.BlockSpec((1, nw), lambda i: (0, 0)), pl.BlockSpec((1, mw), lambda i: (0, 0))],
        out_specs=pl.BlockSpec((tm, cw + nw + mw), lambda i: (i, 0)),
        out_shape=jax.ShapeDtypeStruct((rows, cw + nw + mw), BF16),
        compiler_params=_cparams(1),
        name="merge_norm",
    )(y_conv, y_na, y_mla, y_na_ctx, y_mla_ctx, gn, gm)


def _router_kernel(x_ref, g_ref, sc_ref, sh_ref, whi_ref, wlo_ref, rb_ref, tri_ref,
                   h_ref, idx_ref, wgt_ref, rank_ref, cnt_ref, base_ref):
    @pl.when(pl.program_id(0) == 0)
    def _():
        base_ref[...] = jnp.zeros(base_ref.shape, F32)

    x = x_ref[...]
    h = _rms(x, x.shape[-1]) * g_ref[...] * (1.0 + sc_ref[...]) + sh_ref[...]
    h_hi = h.astype(BF16)
    h_ref[...] = h_hi
    h_lo = (h - h_hi.astype(F32)).astype(BF16)
    dn = (((1,), (1,)), ((), ()))
    logits = (lax.dot_general(whi_ref[...], h_hi, dn, preferred_element_type=F32)
              + lax.dot_general(whi_ref[...], h_lo, dn, preferred_element_type=F32)
              + lax.dot_general(wlo_ref[...], h_hi, dn, preferred_element_type=F32))
    E, tm = logits.shape
    gs = E // N_GROUPS
    scores = jax.nn.sigmoid(logits)
    sel = scores + rb_ref[...]
    sel3 = sel.reshape(N_GROUPS, gs, tm)
    io3 = lax.broadcasted_iota(jnp.int32, sel3.shape, 1)
    m1 = jnp.max(sel3, axis=1, keepdims=True)
    i1 = jnp.min(jnp.where(sel3 == m1, io3, gs), axis=1, keepdims=True)
    m2 = jnp.max(jnp.where(io3 == i1, -jnp.inf, sel3), axis=1, keepdims=True)
    grp = m1 + m2
    iog = lax.broadcasted_iota(jnp.int32, grp.shape, 0)
    keep = jnp.zeros(grp.shape, jnp.bool_)
    for _ in range(TOPK_GROUPS):
        gm = jnp.max(grp, axis=0, keepdims=True)
        gi = jnp.min(jnp.where(grp == gm, iog, N_GROUPS), axis=0, keepdims=True)
        hit = iog == gi
        keep = jnp.logical_or(keep, hit)
        grp = jnp.where(hit, -jnp.inf, grp)
    selm = jnp.where(keep, sel3, -jnp.inf).reshape(E, tm)
    ioe = lax.broadcasted_iota(jnp.int32, (E, tm), 0)
    idxs, wgts, hits = [], [], []
    for _ in range(TOP_K):
        m = jnp.max(selm, axis=0, keepdims=True)
        ei = jnp.min(jnp.where(selm == m, ioe, E), axis=0, keepdims=True)
        hit = ioe == ei
        idxs.append(ei)
        hits.append(hit)
        wgts.append(jnp.sum(jnp.where(hit, scores, 0.0), axis=0, keepdims=True))
        selm = jnp.where(hit, -jnp.inf, selm)
    wsum = wgts[0]
    for w in wgts[1:]:
        wsum = wsum + w
    chosen = jnp.zeros((E, tm), F32)
    for hit in hits:
        chosen = chosen + jnp.where(hit, 1.0, 0.0)
    before = base_ref[:, 0:1] + jnp.dot(chosen.astype(BF16), tri_ref[...], preferred_element_type=F32)
    for k in range(TOP_K):
        idx_ref[k:k + 1, :] = idxs[k]
        wgt_ref[k:k + 1, :] = wgts[k] / wsum * ROUTE_SCALE
        rank_ref[k:k + 1, :] = jnp.sum(jnp.where(hits[k], before, 0.0), axis=0, keepdims=True).astype(jnp.int32)
    base_ref[...] = base_ref[...] + jnp.sum(chosen, axis=1, keepdims=True)
    cnt_ref[...] = base_ref[...]


def _router(x, g, mod_l, w_router, router_bias, row0, rows, n_lat, seq):
    D = x.shape[1]
    E = w_router.shape[1]
    tm = ROW_TILE
    b0 = row0 // tm
    mrow = functools.partial(_mod_row, tm=tm, n_lat=n_lat, seq=seq)
    wt = w_router.T
    whi = wt.astype(BF16)
    wlo = (wt - whi.astype(F32)).astype(BF16)
    tri = (np.arange(tm)[:, None] < np.arange(tm)[None, :]).astype(np.float32)
    return pl.pallas_call(
        _router_kernel,
        grid=(rows // tm,),
        in_specs=[pl.BlockSpec((tm, D), lambda i: (i + b0, 0)),
                  pl.BlockSpec((1, D), lambda i: (0, 0)),
                  pl.BlockSpec((None, 1, D), lambda i: (mrow(i + b0) * 6 + 4, 0, 0)),
                  pl.BlockSpec((None, 1, D), lambda i: (mrow(i + b0) * 6 + 3, 0, 0)),
                  pl.BlockSpec((E, D), lambda i: (0, 0)),
                  pl.BlockSpec((E, D), lambda i: (0, 0)),
                  pl.BlockSpec((E, 1), lambda i: (0, 0)),
                  pl.BlockSpec((tm, tm), lambda i: (0, 0))],
        out_specs=[pl.BlockSpec((tm, D), lambda i: (i, 0)),
                   pl.BlockSpec((TOP_K, tm), lambda i: (0, i)),
                   pl.BlockSpec((TOP_K, tm), lambda i: (0, i)),
                   pl.BlockSpec((TOP_K, tm), lambda i: (0, i)),
                   pl.BlockSpec((E, LANE), lambda i: (0, 0))],
        out_shape=[jax.ShapeDtypeStruct((rows, D), BF16),
                   jax.ShapeDtypeStruct((TOP_K, rows), jnp.int32),
                   jax.ShapeDtypeStruct((TOP_K, rows), F32),
                   jax.ShapeDtypeStruct((TOP_K, rows), jnp.int32),
                   jax.ShapeDtypeStruct((E, LANE), F32)],
        scratch_shapes=[pltpu.VMEM((E, LANE), F32)],
        compiler_params=_cparams(1),
        name="router",
    )(x, g.reshape(1, D), mod_l, mod_l, whi, wlo, router_bias.reshape(E, 1), jnp.asarray(tri, BF16))


def _expert_kernel(vt_ref, ve_ref, nv_ref, gs_ref, xs_ref, wg_ref, wu_ref, wd_ref, *rest):
    o_ref, wgb_ref, wub_ref, wdb_ref = rest[-4:]
    v = pl.program_id(0)
    T = xs_ref.shape[0]

    @pl.when(v < nv_ref[0])
    def _():
        e = ve_ref[v]
        tile = vt_ref[v]
        prev = jnp.maximum(v - 1, 0)
        new_expert = jnp.logical_or(v == 0, ve_ref[prev] != e)
        first_visit = jnp.logical_or(v == 0, vt_ref[prev] != tile)

        @pl.when(new_expert)
        def _():
            wgb_ref[...] = wg_ref[...].astype(BF16)
            wub_ref[...] = wu_ref[...].astype(BF16)
            wdb_ref[...] = wd_ref[...].astype(BF16)

        xs = xs_ref[...]
        g = jnp.dot(xs, wgb_ref[...], preferred_element_type=F32)
        u = jnp.dot(xs, wub_ref[...], preferred_element_type=F32)
        a = (g * jax.nn.sigmoid(g) * u).astype(BF16)
        y = jnp.dot(a, wdb_ref[...], preferred_element_type=F32).astype(o_ref.dtype)

        @pl.when(first_visit)
        def _():
            o_ref[...] = y

        @pl.when(jnp.logical_not(first_visit))
        def _():
            row = tile * T + lax.broadcasted_iota(jnp.int32, (T, 1), 0)
            mine = jnp.logical_and(row >= gs_ref[e], row < gs_ref[e + 1])
            o_ref[...] = jnp.where(mine, y, o_ref[...])


def _experts(xs, tile0, n_rows_total, y_prev, visits, group_start, w_gate, w_up, w_down, layer):
    visit_tile, visit_expert, n_visits = visits
    D = xs.shape[1]
    DE = w_gate.shape[3]
    T = MOE_TILE
    in_specs = [pl.BlockSpec((T, D), lambda v, vt, ve, nv, gs: (vt[v] - tile0, 0)),
                pl.BlockSpec((None, None, D, DE), lambda v, vt, ve, nv, gs: (layer, ve[v], 0, 0)),
                pl.BlockSpec((None, None, D, DE), lambda v, vt, ve, nv, gs: (layer, ve[v], 0, 0)),
                pl.BlockSpec((None, None, DE, D), lambda v, vt, ve, nv, gs: (layer, ve[v], 0, 0))]
    args = [visit_tile, visit_expert, n_visits, group_start, xs, w_gate, w_up, w_down]
    aliases = {}
    if y_prev is not None:
        in_specs.append(pl.BlockSpec(memory_space=pl.ANY))
        args.append(y_prev)
        aliases = {len(args) - 1: 0}
    grid_spec = pltpu.PrefetchScalarGridSpec(
        num_scalar_prefetch=4,
        grid=(visit_tile.shape[0],),
        in_specs=in_specs,
        out_specs=pl.BlockSpec((T, D), lambda v, vt, ve, nv, gs: (vt[v], 0)),
        scratch_shapes=[pltpu.VMEM((D, DE), BF16), pltpu.VMEM((D, DE), BF16), pltpu.VMEM((DE, D), BF16)],
    )
    return pl.pallas_call(
        _expert_kernel,
        grid_spec=grid_spec,
        out_shape=jax.ShapeDtypeStruct((n_rows_total, D), BF16),
        input_output_aliases=aliases,
        compiler_params=_cparams(1),
        name="experts",
    )(*args)


def _dispatch_plan(eidx, rank, counts, tile_bounds):
    K, rows = eidx.shape
    E = counts.shape[0]
    T = MOE_TILE
    gend = jnp.cumsum(counts)
    gstart = gend - counts
    eids = jnp.arange(E, dtype=jnp.int32)
    dest = rank + jnp.sum(jnp.where(eidx[None] == eids[:, None, None], gstart[:, None, None], 0), axis=0)
    token = jnp.broadcast_to(jnp.arange(rows, dtype=jnp.int32)[None], (K, rows))
    _, row_token = lax.sort((dest.reshape(-1), token.reshape(-1)), num_keys=1)
    visits = []
    for t_lo, t_hi in zip(tile_bounds[:-1], tile_bounds[1:]):
        first_tile = jnp.maximum(gstart // T, t_lo)
        last_tile = jnp.minimum((gend - 1) // T, t_hi - 1)
        n_vis_e = jnp.where(counts > 0, jnp.maximum(last_tile - first_tile + 1, 0), 0)
        vend = jnp.cumsum(n_vis_e)
        voff = vend - n_vis_e
        n_visits = vend[-1]
        v = jnp.arange(t_hi - t_lo + E, dtype=jnp.int32)
        ve = jnp.minimum(jnp.sum((v[:, None] >= vend[None, :]).astype(jnp.int32), axis=1), E - 1)
        onehot = (ve[:, None] == eids[None, :]).astype(jnp.int32)
        vt = jnp.sum(onehot * (first_tile - voff)[None, :], axis=1) + v
        vt = jnp.where(v < n_visits, vt, t_hi - 1).astype(jnp.int32)
        visits.append((vt, ve.astype(jnp.int32), n_visits.reshape(1).astype(jnp.int32)))
    group_start = jnp.concatenate([gstart, gend[-1:]]).astype(jnp.int32)
    return dest, row_token, visits, group_start


def _shared_kernel(h_ref, wg_ref, wu_ref, wd_ref, y8_ref, gate_ref, x_ref, g2_ref, o_ref):
    h = h_ref[...]
    g = jnp.dot(h, wg_ref[...], preferred_element_type=F32)
    u = jnp.dot(h, wu_ref[...], preferred_element_type=F32)
    a = (g * jax.nn.sigmoid(g) * u).astype(BF16)
    y = jnp.dot(a, wd_ref[...], preferred_element_type=F32)
    for k in range(y8_ref.shape[0]):
        y = y + gate_ref[:, k:k + 1] * y8_ref[k].astype(F32)
    o_ref[...] = x_ref[...] + g2_ref[...] * y


def _shared_combine(h2, wsg, wsu, wsd, y8, gates, x, mod_l, row0, rows, n_lat, seq):
    D = h2.shape[1]
    DE = wsg.shape[1]
    K = y8.shape[0]
    tm = COMBINE_TILE
    b0 = row0 // tm
    mrow = functools.partial(_mod_row, tm=tm, n_lat=n_lat, seq=seq)
    return pl.pallas_call(
        _shared_kernel,
        grid=(rows // tm,),
        in_specs=[pl.BlockSpec((tm, D), lambda i: (i + b0, 0)),
                  pl.BlockSpec((D, DE), lambda i: (0, 0)),
                  pl.BlockSpec((D, DE), lambda i: (0, 0)),
                  pl.BlockSpec((DE, D), lambda i: (0, 0)),
                  pl.BlockSpec((K, tm, D), lambda i: (0, i, 0)),
                  pl.BlockSpec((tm, K), lambda i: (i, 0)),
                  pl.BlockSpec((tm, D), lambda i: (i + b0, 0)),
                  pl.BlockSpec((None, 1, D), lambda i: (mrow(i + b0) * 6 + 5, 0, 0))],
        out_specs=pl.BlockSpec((tm, D), lambda i: (i + b0, 0)),
        out_shape=jax.ShapeDtypeStruct(x.shape, F32),
        input_output_aliases={6: 0},
        compiler_params=_cparams(1),
        name="shared_combine",
    )(h2, wsg, wsu, wsd, y8, gates, x, mod_l)


def _rope_tables(seq, pad_rows):
    t = np.arange(seq)
    row = (t // GRID_W).astype(np.float32)
    col = (t % GRID_W).astype(np.float32)
    n_freq = MLA_ROPE // 4
    inv = (ROPE_BASE ** (-jnp.arange(n_freq, dtype=F32) / n_freq))
    ang = jnp.concatenate([jnp.asarray(row)[:, None] * inv, jnp.asarray(col)[:, None] * inv], axis=-1)
    cos, sin = jnp.cos(ang), jnp.sin(ang)
    half = MLA_ROPE // 2
    z = jnp.zeros((seq, LANE - MLA_ROPE), F32)
    zh = jnp.zeros((seq, half), F32)
    c = jnp.concatenate([cos, cos, z], axis=-1)
    s1 = jnp.concatenate([-sin, zh, z], axis=-1)
    s2 = jnp.concatenate([zh, sin, z], axis=-1)
    ident = jnp.concatenate([jnp.ones((pad_rows, MLA_ROPE), F32), jnp.zeros((pad_rows, LANE - MLA_ROPE), F32)], -1)
    zero = jnp.zeros((pad_rows, LANE), F32)
    return (jnp.concatenate([c, ident], 0), jnp.concatenate([s1, zero], 0), jnp.concatenate([s2, zero], 0))


def _pad_lanes(v, n):
    return jnp.concatenate([v, jnp.zeros((n - v.shape[0],), v.dtype)]).reshape(1, n)


def kernel(x, c, ctx, c_ctx, ada_w, ada_b, norm1_g, w_in, conv_w, na_q_norm, na_k_norm, na_rpb,
           mla_q_lat_norm, mla_kv_lat_norm, w_uq, w_ukv, mla_q_norm, mla_k_norm, out_norm_g, w_o,
           norm2_g, w_router, router_bias, w_gate, w_up, w_down, ws_gate, ws_up, ws_down):
    B, S, D = x.shape
    CTX = ctx.shape[1]
    L = ada_w.shape[0]
    n_lat = B * S
    R = n_lat + B * CTX
    W = D // 4
    kvr = w_ukv.shape[1]
    nh_mla = w_uq.shape[2] // (MLA_NOPE + MLA_ROPE)
    E = w_router.shape[2]
    assert B == 2 and S % ROW_TILE == 0 and (B * CTX) % ROW_TILE == 0 and CTX % CONV_TILE == 0
    assert S % GRID_W == 0 and S // GRID_W >= NA_WIN_H and W % HEAD_DIM == 0 and kvr % LANE == 0
    assert S % 1024 == 0 and (TOP_K * B * CTX) % MOE_TILE == 0 and ROW_TILE % COMBINE_TILE == 0
    assert (S // GRID_W) % NA_GROUP == 0 and S // GRID_W >= NA_KROWS

    x_pair = (x.reshape(n_lat, D), ctx.reshape(B * CTX, D))
    w_in_t = jnp.swapaxes(w_in, 1, 2)
    cvecs = jnp.concatenate([c, c_ctx[None]], axis=0)
    mod = _modulation(cvecs, ada_w, ada_b)
    tabs = _rope_tables(S, ROW_TILE)

    for l in range(L):
        last = l == L - 1
        rows = n_lat if last else R
        mod_l = mod[l].reshape(8 * 6, 1, D)

        w_tail_t = jnp.concatenate([w_in_t[l, 7 * W:, :], jnp.zeros((LANE - MLA_ROPE, D), F32)], axis=0)
        wq = w_uq[l].reshape(W, nh_mla, MLA_NOPE + MLA_ROPE)
        wq = jnp.concatenate([wq, jnp.zeros((W, nh_mla, MLA_QPAD - MLA_NOPE - MLA_ROPE), F32)], axis=2)
        wq = wq.reshape(W, nh_mla * MLA_QPAD).astype(BF16)
        wkv = w_ukv[l].reshape(kvr, nh_mla, MLA_NOPE + MLA_V)
        wk = wkv[:, :, :MLA_NOPE].reshape(kvr, nh_mla * MLA_NOPE).astype(BF16)
        wv = wkv[:, :, MLA_NOPE:].reshape(kvr, nh_mla * MLA_V).astype(BF16)
        gq_n = mla_q_norm[l][:MLA_NOPE].reshape(1, LANE)
        gq_r = _pad_lanes(mla_q_norm[l][MLA_NOPE:], LANE)
        gk_n = mla_k_norm[l][:MLA_NOPE].reshape(1, LANE)
        gk_r = _pad_lanes(mla_k_norm[l][MLA_NOPE:], LANE)
        og = out_norm_g[l]

        h = _norm_mod(x_pair, R, norm1_g[l], mod_l, 1, 0, n_lat, S)
        z_main = _matmul_ws(h, w_in_t, l, 7 * W, min(512, W), BF16)
        z_tail = _matmul_ws(h, w_tail_t, None, w_tail_t.shape[0], w_tail_t.shape[0], BF16)
        nq, nk, cq, ckv, kr = _post_in(z_main, z_tail, na_q_norm[l].reshape(1, LANE),
                                       na_k_norm[l].reshape(1, LANE), mla_q_lat_norm[l].reshape(1, W),
                                       mla_kv_lat_norm[l].reshape(1, kvr), gk_r, tabs, n_lat, S)
        qf = _uq_matmul(cq, wq, gq_n, gq_r, tabs, n_lat, S)
        kf = _uk_matmul(ckv, wk, gk_n, kr)
        vf = _matmul(ckv, wv, BF16, 512)

        y_conv = _conv_mixer(z_main, conv_w[l], og[:W].reshape(1, W), rows, n_lat, S, CTX)
        y_na = _na_attention(nq, nk, z_main, _na_bias_table(na_rpb[l], S // GRID_W), B, S, CTX, n_lat)
        y_mla = _mla_attention(qf, kf, vf, B, S, CTX, n_lat)
        if last:
            y_na_c, y_mla_c = y_na, y_mla
        else:
            nh_na = W // HEAD_DIM
            y_na_c = _ctx_attention(nq, nk, z_main, nh_na, HEAD_DIM, HEAD_DIM, 5 * nh_na, B, CTX, n_lat)
            y_mla_c = _ctx_attention(qf, kf, vf, nh_mla, MLA_QPAD, MLA_V, 0, B, CTX, n_lat)
        y = _merge(y_conv, y_na, y_mla, y_na_c, y_mla_c,
                   og[W:2 * W].reshape(1, W), og[2 * W:].reshape(1, D - 2 * W), rows)
        xs = _wo_matmul(y, w_o, l, x_pair, mod_l, rows, n_lat, S)

        wsg, wsu, wsd = ws_gate[l].astype(BF16), ws_up[l].astype(BF16), ws_down[l].astype(BF16)
        h2, eidx, wgt, rank, cnt = _router(xs, norm2_g[l], mod_l, w_router[l], router_bias[l],
                                           0, rows, n_lat, S)
        n_pairs = TOP_K * rows
        n_tiles = n_pairs // MOE_TILE
        tile_bounds = [n_tiles * p // MOE_PARTS for p in range(MOE_PARTS + 1)]
        dest, row_token, visits, gstart = _dispatch_plan(eidx, rank, cnt[:, 0].astype(jnp.int32), tile_bounds)
        y_rows = None
        for p in range(MOE_PARTS):
            part_tokens = row_token[tile_bounds[p] * MOE_TILE:tile_bounds[p + 1] * MOE_TILE]
            gathered = h2.at[part_tokens].get(mode="promise_in_bounds")
            y_rows = _experts(gathered, tile_bounds[p], n_pairs, y_rows, visits[p], gstart,
                              w_gate, w_up, w_down, l)
        gates = wgt.T
        n_blocks = rows // ROW_TILE
        row_bounds = [(n_blocks * p // MOE_PARTS) * ROW_TILE for p in range(MOE_PARTS + 1)]
        for row0, row1 in zip(row_bounds[:-1], row_bounds[1:]):
            y8 = y_rows.at[dest[:, row0:row1].reshape(-1)].get(mode="promise_in_bounds")
            xs = _shared_combine(h2, wsg, wsu, wsd, y8.reshape(TOP_K, row1 - row0, D), gates[row0:row1],
                                 xs, mod_l, row0, row1 - row0, n_lat, S)
        x_pair = (xs, xs)

    return xs[:n_lat].reshape(B, S, D)
```

```python
import functools

import numpy as np
import jax
import jax.numpy as jnp
from jax import lax
from jax.experimental import pallas as pl
from jax.experimental.pallas import tpu as pltpu

F32 = jnp.float32
BF16 = jnp.bfloat16

GRID_W = 64
HEAD_DIM = 128
MLA_NOPE = 128
MLA_ROPE = 64
MLA_V = 128
MLA_QPAD = 256
NA_WIN_H = 8
NA_WIN_W = 16
NA_GROUP = 8
NA_KROWS = NA_GROUP + NA_WIN_H
N_GROUPS = 8
TOPK_GROUPS = 4
TOP_K = 8
ROUTE_SCALE = 2.5
ROPE_BASE = 10000.0
EPS = 1e-6
NEG = -1e30
LOG2E = 1.4426950408889634

LANE = 128
ROW_TILE = 512
CONV_TILE = 256
MOE_TILE = 256
COMBINE_TILE = 128
MOE_EXPERT_SPLIT = (0.1, 0.4, 0.7, 1.0)
MOE_COMBINE_SPLIT = (0.1, 0.45, 0.85, 1.0)
VMEM_LIMIT = 56 * 1024 * 1024


def _cparams(n_axes):
    return pltpu.CompilerParams(dimension_semantics=("arbitrary",) * n_axes,
                                vmem_limit_bytes=VMEM_LIMIT)


def _rms(x, n):
    return x * lax.rsqrt(jnp.sum(x * x, axis=-1, keepdims=True) * (1.0 / n) + EPS)


def _mod_kernel(sb_ref, w_ref, b_ref, o_ref):
    tn = w_ref.shape[1]
    o_ref[...] = jnp.zeros(o_ref.shape, F32)
    for cblk in range(tn // LANE):
        wc = w_ref[:, cblk * LANE:(cblk + 1) * LANE]
        for r in range(3):
            acc = jnp.sum(wc * sb_ref[r], axis=0, keepdims=True)
            o_ref[r:r + 1, cblk * LANE:(cblk + 1) * LANE] = acc + b_ref[:, cblk * LANE:(cblk + 1) * LANE]


def _modulation(cvecs, ada_w, ada_b):
    L, D, D6 = ada_w.shape
    tn = 512
    s = cvecs * jax.nn.sigmoid(cvecs)
    sb = jnp.broadcast_to(s[:, :, None], (3, D, LANE))
    return pl.pallas_call(
        _mod_kernel,
        grid=(L, D6 // tn),
        in_specs=[pl.BlockSpec((3, D, LANE), lambda l, j: (0, 0, 0)),
                  pl.BlockSpec((None, D, tn), lambda l, j: (l, 0, j)),
                  pl.BlockSpec((None, 1, tn), lambda l, j: (l, 0, j))],
        out_specs=pl.BlockSpec((None, 8, tn), lambda l, j: (l, 0, j)),
        out_shape=jax.ShapeDtypeStruct((L, 8, D6), F32),
        compiler_params=_cparams(2),
        name="modulation",
    )(sb, ada_w, ada_b.reshape(L, 1, D6))


def _mod_row(i, tm, n_lat, seq):
    r0 = i * tm
    return jnp.where(r0 < n_lat, r0 // seq, n_lat // seq)


def _split_rows(x_pair, tm):
    head, tail = x_pair
    head_blocks = head.shape[0] // tm
    tail_last = tail.shape[0] // tm - 1
    return (head_blocks, lambda i: jnp.minimum(i, head_blocks - 1),
            lambda i: jnp.clip(i - head_blocks, 0, tail_last))


def _norm_mod_kernel(xa_ref, xb_ref, g_ref, sc_ref, sh_ref, o_ref, *, head_blocks):
    x = jnp.where(pl.program_id(0) < head_blocks, xa_ref[...], xb_ref[...])
    y = _rms(x, x.shape[-1]) * g_ref[...]
    o_ref[...] = (y * (1.0 + sc_ref[...]) + sh_ref[...]).astype(o_ref.dtype)


def _norm_mod(x_pair, R, g, mod_l, which_sc, which_sh, n_lat, seq):
    D = x_pair[0].shape[1]
    tm = ROW_TILE
    mrow = functools.partial(_mod_row, tm=tm, n_lat=n_lat, seq=seq)
    head_blocks, head_map, tail_map = _split_rows(x_pair, tm)
    return pl.pallas_call(
        functools.partial(_norm_mod_kernel, head_blocks=head_blocks),
        grid=(R // tm,),
        in_specs=[pl.BlockSpec((tm, D), lambda i: (head_map(i), 0)),
                  pl.BlockSpec((tm, D), lambda i: (tail_map(i), 0)),
                  pl.BlockSpec((1, D), lambda i: (0, 0)),
                  pl.BlockSpec((None, 1, D), lambda i: (mrow(i) * 6 + which_sc, 0, 0)),
                  pl.BlockSpec((None, 1, D), lambda i: (mrow(i) * 6 + which_sh, 0, 0))],
        out_specs=pl.BlockSpec((tm, D), lambda i: (i, 0)),
        out_shape=jax.ShapeDtypeStruct((R, D), BF16),
        compiler_params=_cparams(1),
        name="norm_mod",
    )(x_pair[0], x_pair[1], g.reshape(1, D), mod_l, mod_l)


def _mm_kernel(x_ref, w_ref, o_ref):
    o_ref[...] = jnp.dot(x_ref[...], w_ref[...], preferred_element_type=F32).astype(o_ref.dtype)


def _matmul(x, w, out_dtype, tn, rows=None):
    R, K = x.shape
    rows = R if rows is None else rows
    N = w.shape[1]
    tm = ROW_TILE
    return pl.pallas_call(
        _mm_kernel,
        grid=(rows // tm, N // tn),
        in_specs=[pl.BlockSpec((tm, K), lambda i, j: (i, 0)),
                  pl.BlockSpec((K, tn), lambda i, j: (0, j))],
        out_specs=pl.BlockSpec((tm, tn), lambda i, j: (i, j)),
        out_shape=jax.ShapeDtypeStruct((rows, N), out_dtype),
        compiler_params=_cparams(2),
        name="matmul",
    )(x, w)


def _mm_ws_kernel(x_ref, w_ref, o_ref, wb_ref):
    @pl.when(pl.program_id(1) == 0)
    def _():
        wb_ref[...] = w_ref[...].astype(BF16)

    o_ref[...] = lax.dot_general(x_ref[...], wb_ref[...], (((1,), (1,)), ((), ())),
                                 preferred_element_type=F32).astype(o_ref.dtype)


def _matmul_ws(x, wt, layer, n_cols, tn, out_dtype):
    R, K = x.shape
    tm = ROW_TILE
    if layer is None:
        w_spec = pl.BlockSpec((tn, K), lambda j, i: (j, 0))
    else:
        w_spec = pl.BlockSpec((None, tn, K), lambda j, i: (layer, j, 0))
    return pl.pallas_call(
        _mm_ws_kernel,
        grid=(n_cols // tn, R // tm),
        in_specs=[pl.BlockSpec((tm, K), lambda j, i: (i, 0)), w_spec],
        out_specs=pl.BlockSpec((tm, tn), lambda j, i: (i, j)),
        out_shape=jax.ShapeDtypeStruct((R, n_cols), out_dtype),
        scratch_shapes=[pltpu.VMEM((tn, K), BF16)],
        compiler_params=_cparams(2),
        name="matmul_ws",
    )(x, wt)


def _rope_rotate(t, c_ref, s1_ref, s2_ref):
    return (t * c_ref[...] + pltpu.roll(t, LANE - MLA_ROPE // 2, axis=1) * s1_ref[...]
            + pltpu.roll(t, MLA_ROPE // 2, axis=1) * s2_ref[...])


def _uq_kernel(x_ref, w_ref, gn_ref, gr_ref, c_ref, s1_ref, s2_ref, o_ref):
    scale = (MLA_NOPE + MLA_ROPE) ** -0.5 * LOG2E
    tm = x_ref.shape[0]
    half = tm // 2
    for r0 in (0, half):
        rs = slice(r0, r0 + half)
        acc = jnp.dot(x_ref[rs, :], w_ref[...], preferred_element_type=F32)
        for h in range(acc.shape[1] // MLA_QPAD):
            o = h * MLA_QPAD
            a = _rms(acc[:, o:o + MLA_NOPE], MLA_NOPE) * gn_ref[...]
            r = _rms(acc[:, o + MLA_NOPE:o + MLA_QPAD], MLA_ROPE) * gr_ref[...]
            r = (r * c_ref[rs, :] + pltpu.roll(r, LANE - MLA_ROPE // 2, axis=1) * s1_ref[rs, :]
                 + pltpu.roll(r, MLA_ROPE // 2, axis=1) * s2_ref[rs, :])
            o_ref[rs, o:o + MLA_NOPE] = (a * scale).astype(o_ref.dtype)
            o_ref[rs, o + MLA_NOPE:o + MLA_QPAD] = (r * scale).astype(o_ref.dtype)


def _rope_block(i, tm, n_lat, seq):
    return jnp.where(i * tm < n_lat, i % (seq // tm), seq // tm)


def _uq_matmul(x, w, gn, gr, tabs, n_lat, seq):
    R, K = x.shape
    N = w.shape[1]
    tm, tn = ROW_TILE, 1024
    rb = functools.partial(_rope_block, tm=tm, n_lat=n_lat, seq=seq)
    tab_spec = pl.BlockSpec((tm, LANE), lambda i, j: (rb(i), 0))
    vec_spec = pl.BlockSpec((1, LANE), lambda i, j: (0, 0))
    return pl.pallas_call(
        _uq_kernel,
        grid=(R // tm, N // tn),
        in_specs=[pl.BlockSpec((tm, K), lambda i, j: (i, 0)),
                  pl.BlockSpec((K, tn), lambda i, j: (0, j)),
                  vec_spec, vec_spec, tab_spec, tab_spec, tab_spec],
        out_specs=pl.BlockSpec((tm, tn), lambda i, j: (i, j)),
        out_shape=jax.ShapeDtypeStruct((R, N), BF16),
        compiler_params=_cparams(2),
        name="uq_matmul",
    )(x, w, gn, gr, *tabs)


def _uk_kernel(x_ref, w_ref, gk_ref, kr_ref, o_ref):
    half = x_ref.shape[0] // 2
    for r0 in (0, half):
        rs = slice(r0, r0 + half)
        acc = jnp.dot(x_ref[rs, :], w_ref[...], preferred_element_type=F32)
        for h in range(acc.shape[1] // MLA_NOPE):
            a = _rms(acc[:, h * MLA_NOPE:(h + 1) * MLA_NOPE], MLA_NOPE) * gk_ref[...]
            o_ref[rs, h * MLA_QPAD:h * MLA_QPAD + MLA_NOPE] = a.astype(o_ref.dtype)
            o_ref[rs, h * MLA_QPAD + MLA_NOPE:(h + 1) * MLA_QPAD] = kr_ref[rs, :]


def _uk_matmul(x, w, gk, kr_pad):
    R, K = x.shape
    N = w.shape[1]
    tm, tn = ROW_TILE, 512
    return pl.pallas_call(
        _uk_kernel,
        grid=(R // tm, N // tn),
        in_specs=[pl.BlockSpec((tm, K), lambda i, j: (i, 0)),
                  pl.BlockSpec((K, tn), lambda i, j: (0, j)),
                  pl.BlockSpec((1, LANE), lambda i, j: (0, 0)),
                  pl.BlockSpec((tm, LANE), lambda i, j: (i, 0))],
        out_specs=pl.BlockSpec((tm, 2 * tn), lambda i, j: (i, j)),
        out_shape=jax.ShapeDtypeStruct((R, 2 * N), BF16),
        compiler_params=_cparams(2),
        name="uk_matmul",
    )(x, w, gk, kr_pad)


def _wo_kernel(y_ref, w_ref, xa_ref, xb_ref, g_ref, o_ref, wb_ref, *, head_blocks):
    @pl.when(pl.program_id(1) == 0)
    def _():
        wb_ref[...] = w_ref[...].astype(BF16)

    acc = jnp.dot(y_ref[...], wb_ref[...], preferred_element_type=F32)
    x = jnp.where(pl.program_id(1) < head_blocks, xa_ref[...], xb_ref[...])
    o_ref[...] = x + g_ref[...] * acc


def _wo_matmul(y, w_o, layer, x_pair, mod_l, rows, n_lat, seq):
    K = y.shape[1]
    N = w_o.shape[2]
    tm, tn = ROW_TILE, 512
    mrow = functools.partial(_mod_row, tm=tm, n_lat=n_lat, seq=seq)
    head_blocks, head_map, tail_map = _split_rows(x_pair, tm)
    return pl.pallas_call(
        functools.partial(_wo_kernel, head_blocks=head_blocks),
        grid=(N // tn, rows // tm),
        in_specs=[pl.BlockSpec((tm, K), lambda j, i: (i, 0)),
                  pl.BlockSpec((None, K, tn), lambda j, i: (layer, 0, j)),
                  pl.BlockSpec((tm, tn), lambda j, i: (head_map(i), j)),
                  pl.BlockSpec((tm, tn), lambda j, i: (tail_map(i), j)),
                  pl.BlockSpec((None, 1, tn), lambda j, i: (mrow(i) * 6 + 2, 0, j))],
        out_specs=pl.BlockSpec((tm, tn), lambda j, i: (i, j)),
        out_shape=jax.ShapeDtypeStruct((rows, N), F32),
        scratch_shapes=[pltpu.VMEM((K, tn), BF16)],
        compiler_params=_cparams(2),
        name="wo_matmul",
    )(y, w_o, x_pair[0], x_pair[1], mod_l)


def _post_in_kernel(nq_ref, nk_ref, cq_ref, tail_ref, gq_ref, gk_ref, gql_ref, gkvl_ref, gkr_ref,
                    c_ref, s1_ref, s2_ref, nq_o, nk_o, cq_o, ckv_o, kr_o):
    nh = nq_ref.shape[1] // HEAD_DIM
    scale = HEAD_DIM ** -0.5 * LOG2E
    for h in range(nh):
        sl = slice(h * HEAD_DIM, (h + 1) * HEAD_DIM)
        q = nq_ref[:, sl].astype(F32)
        nq_o[:, sl] = (_rms(q, HEAD_DIM) * gq_ref[...] * scale).astype(nq_o.dtype)
        k = nk_ref[:, sl].astype(F32)
        nk_o[:, sl] = (_rms(k, HEAD_DIM) * gk_ref[...]).astype(nk_o.dtype)
    cq = cq_ref[...].astype(F32)
    cq_o[...] = (_rms(cq, cq.shape[1]) * gql_ref[...]).astype(cq_o.dtype)
    kvr = ckv_o.shape[1]
    ckv = tail_ref[:, :kvr].astype(F32)
    ckv_o[...] = (_rms(ckv, kvr) * gkvl_ref[...]).astype(ckv_o.dtype)
    kr = tail_ref[:, kvr:kvr + LANE].astype(F32)
    kr = _rms(kr, MLA_ROPE) * gkr_ref[...]
    kr_o[...] = _rope_rotate(kr, c_ref, s1_ref, s2_ref).astype(kr_o.dtype)


def _post_in(z_main, z_tail, gq, gk, gql, gkvl, gkr, tabs, n_lat, seq):
    R = z_main.shape[0]
    W = z_main.shape[1] // 7
    kvr = z_tail.shape[1] - LANE
    tm = ROW_TILE
    rb = functools.partial(_rope_block, tm=tm, n_lat=n_lat, seq=seq)
    tab_spec = pl.BlockSpec((tm, LANE), lambda i: (rb(i), 0))

    def vec(n):
        return pl.BlockSpec((1, n), lambda i: (0, 0))

    def col(j):
        return pl.BlockSpec((tm, W), lambda i: (i, j))

    return pl.pallas_call(
        _post_in_kernel,
        grid=(R // tm,),
        in_specs=[col(3), col(4), col(6), pl.BlockSpec((tm, kvr + LANE), lambda i: (i, 0)),
                  vec(LANE), vec(LANE), vec(W), vec(kvr), vec(LANE), tab_spec, tab_spec, tab_spec],
        out_specs=[pl.BlockSpec((tm, W), lambda i: (i, 0)), pl.BlockSpec((tm, W), lambda i: (i, 0)),
                   pl.BlockSpec((tm, W), lambda i: (i, 0)), pl.BlockSpec((tm, kvr), lambda i: (i, 0)),
                   pl.BlockSpec((tm, LANE), lambda i: (i, 0))],
        out_shape=[jax.ShapeDtypeStruct((R, W), BF16), jax.ShapeDtypeStruct((R, W), BF16),
                   jax.ShapeDtypeStruct((R, W), BF16), jax.ShapeDtypeStruct((R, kvr), BF16),
                   jax.ShapeDtypeStruct((R, LANE), BF16)],
        compiler_params=_cparams(1),
        name="post_in",
    )(z_main, z_main, z_main, z_tail, gq, gk, gql, gkvl, gkr, *tabs)


def _conv_kernel(b_ref, c_ref, u_ref, cp_ref, up_ref, cn_ref, un_ref, w_ref, g_ref, o_ref,
                 *, n_lat, seq, ctx_len):
    tm, W = c_ref.shape
    row0 = pl.program_id(0) * tm
    lat = row0 < n_lat
    pos = jnp.where(lat, row0 % seq, (row0 - n_lat) % ctx_len)
    slen = jnp.where(lat, seq, ctx_len)
    has_prev = (pos > 0).astype(F32)
    has_next = (pos + tm < slen).astype(F32)
    v = c_ref[...].astype(F32) * u_ref[...].astype(F32)
    vprev = cp_ref[7:8, :].astype(F32) * up_ref[7:8, :].astype(F32) * has_prev
    vnext = cn_ref[0:1, :].astype(F32) * un_ref[0:1, :].astype(F32) * has_next
    rid = lax.broadcasted_iota(jnp.int32, (tm, W), 0)
    v_dn = jnp.where(rid == 0, vprev, pltpu.roll(v, 1, axis=0))
    v_up = jnp.where(rid == tm - 1, vnext, pltpu.roll(v, tm - 1, axis=0))
    y = w_ref[0:1, :] * v_dn + w_ref[1:2, :] * v + w_ref[2:3, :] * v_up
    y = b_ref[...].astype(F32) * y
    o_ref[...] = (_rms(y, W) * g_ref[...]).astype(o_ref.dtype)


def _conv_mixer(z_main, conv_w, g, rows, n_lat, seq, ctx_len):
    R = z_main.shape[0]
    W = z_main.shape[1] // 7
    tm = CONV_TILE
    nb8 = R // 8

    def cur(j):
        return pl.BlockSpec((tm, W), lambda i: (i, j))

    def prev(j):
        return pl.BlockSpec((8, W), lambda i: (jnp.maximum(i * (tm // 8) - 1, 0), j))

    def nxt(j):
        return pl.BlockSpec((8, W), lambda i: (jnp.minimum((i + 1) * (tm // 8), nb8 - 1), j))

    return pl.pallas_call(
        functools.partial(_conv_kernel, n_lat=n_lat, seq=seq, ctx_len=ctx_len),
        grid=(rows // tm,),
        in_specs=[cur(0), cur(1), cur(2), prev(1), prev(2), nxt(1), nxt(2),
                  pl.BlockSpec((3, W), lambda i: (0, 0)), pl.BlockSpec((1, W), lambda i: (0, 0))],
        out_specs=pl.BlockSpec((tm, W), lambda i: (i, 0)),
        out_shape=jax.ShapeDtypeStruct((rows, W), BF16),
        compiler_params=_cparams(1),
        name="conv_mixer",
    )(z_main, z_main, z_main, z_main, z_main, z_main, z_main, conv_w, g)


def _na_group_plan(rows):
    n_groups = rows // NA_GROUP
    kinds = [(0, 0), (NA_GROUP, NA_GROUP - NA_WIN_H // 2), ((n_groups - 1) * NA_GROUP, rows - NA_KROWS)]
    plane = np.full((3, NA_GROUP, NA_KROWS), 2 * NA_WIN_H - 1, np.int64)
    for t, (r_first, k_first) in enumerate(kinds):
        for qr in range(NA_GROUP):
            r = r_first + qr
            r0 = min(max(r - NA_WIN_H // 2, 0), rows - NA_WIN_H)
            for kr in range(NA_KROWS):
                kk = k_first + kr
                if r0 <= kk < r0 + NA_WIN_H:
                    plane[t, qr, kr] = kk - r + NA_WIN_H - 1
    return plane


def _na_bias_table(rpb, rows):
    qc = np.arange(GRID_W)[:, None]
    kc = np.arange(GRID_W)[None, :]
    cstart = np.clip(qc - NA_WIN_W // 2, 0, GRID_W - NA_WIN_W)
    ok = (kc >= cstart) & (kc < cstart + NA_WIN_W)
    dc = np.clip(kc - qc + NA_WIN_W - 1, 0, 2 * NA_WIN_W - 2)
    H = rpb.shape[0]
    onehot = (dc[None] == np.arange(2 * NA_WIN_W - 1)[:, None, None]).astype(np.float32)
    cols = jnp.einsum("hdc,cqk->hdqk", rpb.astype(F32), onehot, precision=lax.Precision.HIGHEST)
    cols = jnp.where(ok[None, None], cols * LOG2E, NEG)
    planes = jnp.concatenate([cols, jnp.full((H, 1, GRID_W, GRID_W), NEG, F32)], axis=1)
    pick = (_na_group_plan(rows)[..., None] == np.arange(2 * NA_WIN_H)).astype(np.float32)
    g = jnp.einsum("tqkp,hpxy->htqxky", pick, planes, precision=lax.Precision.HIGHEST)
    return g.reshape(H, 3, NA_GROUP * GRID_W, NA_KROWS * GRID_W)


def _na_kernel(q_ref, k_ref, v_ref, kc_ref, vc_ref, bias_ref, o_ref):
    rows = q_ref.shape[0] // GRID_W
    n_groups = rows // NA_GROUP
    nq = NA_GROUP * GRID_W
    nk = NA_KROWS * GRID_W
    dn = (((1,), (1,)), ((), ()))
    kc = kc_ref[...]
    vc = vc_ref[...]

    def body(g, carry):
        r_first = g * NA_GROUP
        k_first = jnp.clip(r_first - NA_WIN_H // 2, 0, rows - NA_KROWS)
        kind = jnp.where(g == 0, 0, jnp.where(g == n_groups - 1, 2, 1))
        qoff = pl.multiple_of(r_first * GRID_W, GRID_W)
        koff = pl.multiple_of(k_first * GRID_W, GRID_W)
        q = q_ref[pl.ds(qoff, nq), :]
        k = k_ref[pl.ds(koff, nk), :]
        v = v_ref[pl.ds(koff, nk), :]
        s = lax.dot_general(q, k, dn, preferred_element_type=F32) + bias_ref[kind]
        sc = lax.dot_general(q, kc, dn, preferred_element_type=F32)
        m = jnp.maximum(jnp.max(s, axis=-1, keepdims=True), jnp.max(sc, axis=-1, keepdims=True))
        p = jnp.exp2(s - m)
        pc = jnp.exp2(sc - m)
        l = jnp.sum(p, axis=-1, keepdims=True) + jnp.sum(pc, axis=-1, keepdims=True)
        o = (jnp.dot(p.astype(BF16), v, preferred_element_type=F32)
             + jnp.dot(pc.astype(BF16), vc, preferred_element_type=F32))
        o_ref[pl.ds(qoff, nq), :] = (o / l).astype(o_ref.dtype)
        return carry

    lax.fori_loop(0, n_groups, body, 0, unroll=2)


def _na_attention(nq, nk, z_main, bias_tab, B, seq, ctx_len, n_lat):
    W = nq.shape[1]
    nh = W // HEAD_DIM
    cb = n_lat // ctx_len
    voff = 5 * nh
    return pl.pallas_call(
        _na_kernel,
        grid=(nh, B),
        in_specs=[pl.BlockSpec((seq, HEAD_DIM), lambda h, b: (b, h)),
                  pl.BlockSpec((seq, HEAD_DIM), lambda h, b: (b, h)),
                  pl.BlockSpec((seq, HEAD_DIM), lambda h, b: (b, voff + h)),
                  pl.BlockSpec((ctx_len, HEAD_DIM), lambda h, b: (cb + b, h)),
                  pl.BlockSpec((ctx_len, HEAD_DIM), lambda h, b: (cb + b, voff + h)),
                  pl.BlockSpec((None, 3, NA_GROUP * GRID_W, NA_KROWS * GRID_W), lambda h, b: (h, 0, 0, 0))],
        out_specs=pl.BlockSpec((seq, HEAD_DIM), lambda h, b: (b, h)),
        out_shape=jax.ShapeDtypeStruct((n_lat, W), BF16),
        compiler_params=_cparams(2),
        name="na_attention",
    )(nq, nk, z_main, nk, z_main, bias_tab)


def _flash_kernel(q_ref, k_ref, v_ref, kc_ref, vc_ref, o_ref, *, tk):
    tq = q_ref.shape[0]
    dv = v_ref.shape[1]
    dn = (((1,), (1,)), ((), ()))
    q = q_ref[...]

    def step(k, v, m, l, acc):
        s = lax.dot_general(q, k, dn, preferred_element_type=F32)
        m_new = jnp.maximum(m, jnp.max(s, axis=-1, keepdims=True))
        alpha = jnp.exp2(m - m_new)
        p = jnp.exp2(s - m_new)
        l = alpha * l + jnp.sum(p, axis=-1, keepdims=True)
        acc = alpha * acc + jnp.dot(p.astype(BF16), v, preferred_element_type=F32)
        return m_new, l, acc

    def body(j, carry):
        off = pl.multiple_of(j * tk, tk)
        return step(k_ref[pl.ds(off, tk), :], v_ref[pl.ds(off, tk), :], *carry)

    init = (jnp.full((tq, 1), NEG, F32), jnp.zeros((tq, 1), F32), jnp.zeros((tq, dv), F32))
    n_chunks = k_ref.shape[0] // tk
    m, l, acc = lax.fori_loop(0, n_chunks, body, init, unroll=min(n_chunks, 8))
    m, l, acc = step(kc_ref[...], vc_ref[...], m, l, acc)
    o_ref[...] = (acc / l).astype(o_ref.dtype)


def _mla_attention(qf, kf, vf, B, seq, ctx_len, n_lat):
    R = qf.shape[0]
    nh = qf.shape[1] // MLA_QPAD
    tq, tk = 1024, 1024
    cb = n_lat // ctx_len
    nqb = seq // tq
    return pl.pallas_call(
        functools.partial(_flash_kernel, tk=tk),
        grid=(B, nh, nqb),
        in_specs=[pl.BlockSpec((tq, MLA_QPAD), lambda b, h, i: (b * nqb + i, h)),
                  pl.BlockSpec((seq, MLA_QPAD), lambda b, h, i: (b, h)),
                  pl.BlockSpec((seq, MLA_V), lambda b, h, i: (b, h)),
                  pl.BlockSpec((ctx_len, MLA_QPAD), lambda b, h, i: (cb + b, h)),
                  pl.BlockSpec((ctx_len, MLA_V), lambda b, h, i: (cb + b, h))],
        out_specs=pl.BlockSpec((tq, MLA_V), lambda b, h, i: (b * nqb + i, h)),
        out_shape=jax.ShapeDtypeStruct((n_lat, nh * MLA_V), BF16),
        compiler_params=_cparams(3),
        name="mla_attention",
    )(qf, kf, vf, kf, vf)


def _ctx_attn_kernel(q_ref, k_ref, v_ref, o_ref):
    s = lax.dot_general(q_ref[...], k_ref[...], (((1,), (1,)), ((), ())), preferred_element_type=F32)
    p = jnp.exp2(s - jnp.max(s, axis=-1, keepdims=True))
    o = jnp.dot(p.astype(BF16), v_ref[...], preferred_element_type=F32)
    o_ref[...] = (o / jnp.sum(p, axis=-1, keepdims=True)).astype(o_ref.dtype)


def _ctx_attention(q, k, v, nh, dq, dv, voff, B, ctx_len, n_lat):
    cb = n_lat // ctx_len
    return pl.pallas_call(
        _ctx_attn_kernel,
        grid=(B, nh),
        in_specs=[pl.BlockSpec((ctx_len, dq), lambda b, h: (cb + b, h)),
                  pl.BlockSpec((ctx_len, dq), lambda b, h: (cb + b, h)),
                  pl.BlockSpec((ctx_len, dv), lambda b, h: (cb + b, voff + h))],
        out_specs=pl.BlockSpec((ctx_len, dv), lambda b, h: (b, h)),
        out_shape=jax.ShapeDtypeStruct((B * ctx_len, nh * dv), BF16),
        compiler_params=_cparams(2),
        name="ctx_attention",
    )(q, k, v)


def _merge_kernel(yc_ref, yn_ref, ym_ref, ync_ref, ymc_ref, gn_ref, gm_ref, o_ref, *, lat_blocks):
    cw = yc_ref.shape[1]
    nw = yn_ref.shape[1]
    lat = pl.program_id(0) < lat_blocks
    o_ref[:, :cw] = yc_ref[...]
    yn = jnp.where(lat, yn_ref[...], ync_ref[...]).astype(F32)
    o_ref[:, cw:cw + nw] = (_rms(yn, nw) * gn_ref[...]).astype(o_ref.dtype)
    ym = jnp.where(lat, ym_ref[...], ymc_ref[...]).astype(F32)
    o_ref[:, cw + nw:] = (_rms(ym, ym.shape[1]) * gm_ref[...]).astype(o_ref.dtype)


def _merge(y_conv, y_na, y_mla, y_na_ctx, y_mla_ctx, gn, gm, rows):
    cw, nw, mw = y_conv.shape[1], y_na.shape[1], y_mla.shape[1]
    tm = ROW_TILE
    lat_blocks = y_na.shape[0] // tm
    ctx_last = y_na_ctx.shape[0] // tm - 1

    def lat_map(i):
        return (jnp.minimum(i, lat_blocks - 1), 0)

    def ctx_map(i):
        return (jnp.clip(i - lat_blocks, 0, ctx_last), 0)

    return pl.pallas_call(
        functools.partial(_merge_kernel, lat_blocks=lat_blocks),
        grid=(rows // tm,),
        in_specs=[pl.BlockSpec((tm, cw), lambda i: (i, 0)),
                  pl.BlockSpec((tm, nw), lat_map), pl.BlockSpec((tm, mw), lat_map),
                  pl.BlockSpec((tm, nw), ctx_map), pl.BlockSpec((tm, mw), ctx_map),
                  pl.BlockSpec((1, nw), lambda i: (0, 0)), pl.BlockSpec((1, mw), lambda i: (0, 0))],
        out_specs=pl.BlockSpec((tm, cw + nw + mw), lambda i: (i, 0)),
        out_shape=jax.ShapeDtypeStruct((rows, cw + nw + mw), BF16),
        compiler_params=_cparams(1),
        name="merge_norm",
    )(y_conv, y_na, y_mla, y_na_ctx, y_mla_ctx, gn, gm)


def _router_kernel(x_ref, g_ref, sc_ref, sh_ref, whi_ref, wlo_ref, rb_ref, tri_ref,
                   h_ref, idx_ref, wgt_ref, rank_ref, cnt_ref, base_ref):
    @pl.when(pl.program_id(0) == 0)
    def _():
        base_ref[...] = jnp.zeros(base_ref.shape, F32)

    x = x_ref[...]
    h = _rms(x, x.shape[-1]) * g_ref[...] * (1.0 + sc_ref[...]) + sh_ref[...]
    h_hi = h.astype(BF16)
    h_ref[...] = h_hi
    h_lo = (h - h_hi.astype(F32)).astype(BF16)
    dn = (((1,), (1,)), ((), ()))
    logits = (lax.dot_general(whi_ref[...], h_hi, dn, preferred_element_type=F32)
              + lax.dot_general(whi_ref[...], h_lo, dn, preferred_element_type=F32)
              + lax.dot_general(wlo_ref[...], h_hi, dn, preferred_element_type=F32))
    E, tm = logits.shape
    gs = E // N_GROUPS
    scores = jax.nn.sigmoid(logits)
    sel = scores + rb_ref[...]
    sel3 = sel.reshape(N_GROUPS, gs, tm)
    io3 = lax.broadcasted_iota(jnp.int32, sel3.shape, 1)
    m1 = jnp.max(sel3, axis=1, keepdims=True)
    i1 = jnp.min(jnp.where(sel3 == m1, io3, gs), axis=1, keepdims=True)
    m2 = jnp.max(jnp.where(io3 == i1, -jnp.inf, sel3), axis=1, keepdims=True)
    grp = m1 + m2
    iog = lax.broadcasted_iota(jnp.int32, grp.shape, 0)
    keep = jnp.zeros(grp.shape, jnp.bool_)
    for _ in range(TOPK_GROUPS):
        gm = jnp.max(grp, axis=0, keepdims=True)
        gi = jnp.min(jnp.where(grp == gm, iog, N_GROUPS), axis=0, keepdims=True)
        hit = iog == gi
        keep = jnp.logical_or(keep, hit)
        grp = jnp.where(hit, -jnp.inf, grp)
    selm = jnp.where(keep, sel3, -jnp.inf).reshape(E, tm)
    ioe = lax.broadcasted_iota(jnp.int32, (E, tm), 0)
    idxs, wgts, hits = [], [], []
    for _ in range(TOP_K):
        m = jnp.max(selm, axis=0, keepdims=True)
        ei = jnp.min(jnp.where(selm == m, ioe, E), axis=0, keepdims=True)
        hit = ioe == ei
        idxs.append(ei)
        hits.append(hit)
        wgts.append(jnp.sum(jnp.where(hit, scores, 0.0), axis=0, keepdims=True))
        selm = jnp.where(hit, -jnp.inf, selm)
    wsum = wgts[0]
    for w in wgts[1:]:
        wsum = wsum + w
    chosen = jnp.zeros((E, tm), F32)
    for hit in hits:
        chosen = chosen + jnp.where(hit, 1.0, 0.0)
    before = base_ref[:, 0:1] + jnp.dot(chosen.astype(BF16), tri_ref[...], preferred_element_type=F32)
    for k in range(TOP_K):
        idx_ref[k:k + 1, :] = idxs[k]
        wgt_ref[k:k + 1, :] = wgts[k] / wsum * ROUTE_SCALE
        rank_ref[k:k + 1, :] = jnp.sum(jnp.where(hits[k], before, 0.0), axis=0, keepdims=True).astype(jnp.int32)
    base_ref[...] = base_ref[...] + jnp.sum(chosen, axis=1, keepdims=True)
    cnt_ref[...] = base_ref[...]


def _router(x, g, mod_l, w_router, router_bias, row0, rows, n_lat, seq):
    D = x.shape[1]
    E = w_router.shape[1]
    tm = ROW_TILE
    b0 = row0 // tm
    mrow = functools.partial(_mod_row, tm=tm, n_lat=n_lat, seq=seq)
    wt = w_router.T
    whi = wt.astype(BF16)
    wlo = (wt - whi.astype(F32)).astype(BF16)
    tri = (np.arange(tm)[:, None] < np.arange(tm)[None, :]).astype(np.float32)
    return pl.pallas_call(
        _router_kernel,
        grid=(rows // tm,),
        in_specs=[pl.BlockSpec((tm, D), lambda i: (i + b0, 0)),
                  pl.BlockSpec((1, D), lambda i: (0, 0)),
                  pl.BlockSpec((None, 1, D), lambda i: (mrow(i + b0) * 6 + 4, 0, 0)),
                  pl.BlockSpec((None, 1, D), lambda i: (mrow(i + b0) * 6 + 3, 0, 0)),
                  pl.BlockSpec((E, D), lambda i: (0, 0)),
                  pl.BlockSpec((E, D), lambda i: (0, 0)),
                  pl.BlockSpec((E, 1), lambda i: (0, 0)),
                  pl.BlockSpec((tm, tm), lambda i: (0, 0))],
        out_specs=[pl.BlockSpec((tm, D), lambda i: (i, 0)),
                   pl.BlockSpec((TOP_K, tm), lambda i: (0, i)),
                   pl.BlockSpec((TOP_K, tm), lambda i: (0, i)),
                   pl.BlockSpec((TOP_K, tm), lambda i: (0, i)),
                   pl.BlockSpec((E, LANE), lambda i: (0, 0))],
        out_shape=[jax.ShapeDtypeStruct((rows, D), BF16),
                   jax.ShapeDtypeStruct((TOP_K, rows), jnp.int32),
                   jax.ShapeDtypeStruct((TOP_K, rows), F32),
                   jax.ShapeDtypeStruct((TOP_K, rows), jnp.int32),
                   jax.ShapeDtypeStruct((E, LANE), F32)],
        scratch_shapes=[pltpu.VMEM((E, LANE), F32)],
        compiler_params=_cparams(1),
        name="router",
    )(x, g.reshape(1, D), mod_l, mod_l, whi, wlo, router_bias.reshape(E, 1), jnp.asarray(tri, BF16))


def _expert_kernel(vt_ref, ve_ref, nv_ref, gs_ref, xs_ref, wg_ref, wu_ref, wd_ref, *rest):
    o_ref, wgb_ref, wub_ref, wdb_ref = rest[-4:]
    v = pl.program_id(0)
    T = xs_ref.shape[0]

    @pl.when(v < nv_ref[0])
    def _():
        e = ve_ref[v]
        tile = vt_ref[v]
        prev = jnp.maximum(v - 1, 0)
        new_expert = jnp.logical_or(v == 0, ve_ref[prev] != e)
        first_visit = jnp.logical_or(v == 0, vt_ref[prev] != tile)

        @pl.when(new_expert)
        def _():
            wgb_ref[...] = wg_ref[...].astype(BF16)
            wub_ref[...] = wu_ref[...].astype(BF16)
            wdb_ref[...] = wd_ref[...].astype(BF16)

        xs = xs_ref[...]
        g = jnp.dot(xs, wgb_ref[...], preferred_element_type=F32)
        u = jnp.dot(xs, wub_ref[...], preferred_element_type=F32)
        a = (g * jax.nn.sigmoid(g) * u).astype(BF16)
        y = jnp.dot(a, wdb_ref[...], preferred_element_type=F32).astype(o_ref.dtype)

        @pl.when(first_visit)
        def _():
            o_ref[...] = y

        @pl.when(jnp.logical_not(first_visit))
        def _():
            row = tile * T + lax.broadcasted_iota(jnp.int32, (T, 1), 0)
            mine = jnp.logical_and(row >= gs_ref[e], row < gs_ref[e + 1])
            o_ref[...] = jnp.where(mine, y, o_ref[...])


def _experts(xs, tile0, n_rows_total, y_prev, visits, group_start, w_gate, w_up, w_down, layer):
    visit_tile, visit_expert, n_visits = visits
    D = xs.shape[1]
    DE = w_gate.shape[3]
    T = MOE_TILE
    in_specs = [pl.BlockSpec((T, D), lambda v, vt, ve, nv, gs: (vt[v] - tile0, 0)),
                pl.BlockSpec((None, None, D, DE), lambda v, vt, ve, nv, gs: (layer, ve[v], 0, 0)),
                pl.BlockSpec((None, None, D, DE), lambda v, vt, ve, nv, gs: (layer, ve[v], 0, 0)),
                pl.BlockSpec((None, None, DE, D), lambda v, vt, ve, nv, gs: (layer, ve[v], 0, 0))]
    args = [visit_tile, visit_expert, n_visits, group_start, xs, w_gate, w_up, w_down]
    aliases = {}
    if y_prev is not None:
        in_specs.append(pl.BlockSpec(memory_space=pl.ANY))
        args.append(y_prev)
        aliases = {len(args) - 1: 0}
    grid_spec = pltpu.PrefetchScalarGridSpec(
        num_scalar_prefetch=4,
        grid=(visit_tile.shape[0],),
        in_specs=in_specs,
        out_specs=pl.BlockSpec((T, D), lambda v, vt, ve, nv, gs: (vt[v], 0)),
        scratch_shapes=[pltpu.VMEM((D, DE), BF16), pltpu.VMEM((D, DE), BF16), pltpu.VMEM((DE, D), BF16)],
    )
    return pl.pallas_call(
        _expert_kernel,
        grid_spec=grid_spec,
        out_shape=jax.ShapeDtypeStruct((n_rows_total, D), BF16),
        input_output_aliases=aliases,
        compiler_params=_cparams(1),
        name="experts",
    )(*args)


def _dispatch_plan(eidx, rank, counts, tile_bounds):
    K, rows = eidx.shape
    E = counts.shape[0]
    T = MOE_TILE
    gend = jnp.cumsum(counts)
    gstart = gend - counts
    eids = jnp.arange(E, dtype=jnp.int32)
    dest = rank + jnp.sum(jnp.where(eidx[None] == eids[:, None, None], gstart[:, None, None], 0), axis=0)
    token = jnp.broadcast_to(jnp.arange(rows, dtype=jnp.int32)[None], (K, rows))
    _, row_token = lax.sort((dest.reshape(-1), token.reshape(-1)), num_keys=1)
    visits = []
    for t_lo, t_hi in zip(tile_bounds[:-1], tile_bounds[1:]):
        first_tile = jnp.maximum(gstart // T, t_lo)
        last_tile = jnp.minimum((gend - 1) // T, t_hi - 1)
        n_vis_e = jnp.where(counts > 0, jnp.maximum(last_tile - first_tile + 1, 0), 0)
        vend = jnp.cumsum(n_vis_e)
        voff = vend - n_vis_e
        n_visits = vend[-1]
        v = jnp.arange(t_hi - t_lo + E, dtype=jnp.int32)
        ve = jnp.minimum(jnp.sum((v[:, None] >= vend[None, :]).astype(jnp.int32), axis=1), E - 1)
        onehot = (ve[:, None] == eids[None, :]).astype(jnp.int32)
        vt = jnp.sum(onehot * (first_tile - voff)[None, :], axis=1) + v
        vt = jnp.where(v < n_visits, vt, t_hi - 1).astype(jnp.int32)
        visits.append((vt, ve.astype(jnp.int32), n_visits.reshape(1).astype(jnp.int32)))
    group_start = jnp.concatenate([gstart, gend[-1:]]).astype(jnp.int32)
    return dest, row_token, visits, group_start


def _shared_kernel(h_ref, wg_ref, wu_ref, wd_ref, y8_ref, gate_ref, x_ref, g2_ref, o_ref):
    h = h_ref[...]
    g = jnp.dot(h, wg_ref[...], preferred_element_type=F32)
    u = jnp.dot(h, wu_ref[...], preferred_element_type=F32)
    a = (g * jax.nn.sigmoid(g) * u).astype(BF16)
    y = jnp.dot(a, wd_ref[...], preferred_element_type=F32)
    for k in range(y8_ref.shape[0]):
        y = y + gate_ref[:, k:k + 1] * y8_ref[k].astype(F32)
    o_ref[...] = x_ref[...] + g2_ref[...] * y


def _shared_combine(h2, wsg, wsu, wsd, y8, gates, x, mod_l, row0, rows, n_lat, seq):
    D = h2.shape[1]
    DE = wsg.shape[1]
    K = y8.shape[0]
    tm = COMBINE_TILE
    b0 = row0 // tm
    mrow = functools.partial(_mod_row, tm=tm, n_lat=n_lat, seq=seq)
    return pl.pallas_call(
        _shared_kernel,
        grid=(rows // tm,),
        in_specs=[pl.BlockSpec((tm, D), lambda i: (i + b0, 0)),
                  pl.BlockSpec((D, DE), lambda i: (0, 0)),
                  pl.BlockSpec((D, DE), lambda i: (0, 0)),
                  pl.BlockSpec((DE, D), lambda i: (0, 0)),
                  pl.BlockSpec((K, tm, D), lambda i: (0, i, 0)),
                  pl.BlockSpec((tm, K), lambda i: (i, 0)),
                  pl.BlockSpec((tm, D), lambda i: (i + b0, 0)),
                  pl.BlockSpec((None, 1, D), lambda i: (mrow(i + b0) * 6 + 5, 0, 0))],
        out_specs=pl.BlockSpec((tm, D), lambda i: (i + b0, 0)),
        out_shape=jax.ShapeDtypeStruct(x.shape, F32),
        input_output_aliases={6: 0},
        compiler_params=_cparams(1),
        name="shared_combine",
    )(h2, wsg, wsu, wsd, y8, gates, x, mod_l)


def _rope_tables(seq, pad_rows):
    t = np.arange(seq)
    row = (t // GRID_W).astype(np.float32)
    col = (t % GRID_W).astype(np.float32)
    n_freq = MLA_ROPE // 4
    inv = (ROPE_BASE ** (-jnp.arange(n_freq, dtype=F32) / n_freq))
    ang = jnp.concatenate([jnp.asarray(row)[:, None] * inv, jnp.asarray(col)[:, None] * inv], axis=-1)
    cos, sin = jnp.cos(ang), jnp.sin(ang)
    half = MLA_ROPE // 2
    z = jnp.zeros((seq, LANE - MLA_ROPE), F32)
    zh = jnp.zeros((seq, half), F32)
    c = jnp.concatenate([cos, cos, z], axis=-1)
    s1 = jnp.concatenate([-sin, zh, z], axis=-1)
    s2 = jnp.concatenate([zh, sin, z], axis=-1)
    ident = jnp.concatenate([jnp.ones((pad_rows, MLA_ROPE), F32), jnp.zeros((pad_rows, LANE - MLA_ROPE), F32)], -1)
    zero = jnp.zeros((pad_rows, LANE), F32)
    return (jnp.concatenate([c, ident], 0), jnp.concatenate([s1, zero], 0), jnp.concatenate([s2, zero], 0))


def _pad_lanes(v, n):
    return jnp.concatenate([v, jnp.zeros((n - v.shape[0],), v.dtype)]).reshape(1, n)


def kernel(x, c, ctx, c_ctx, ada_w, ada_b, norm1_g, w_in, conv_w, na_q_norm, na_k_norm, na_rpb,
           mla_q_lat_norm, mla_kv_lat_norm, w_uq, w_ukv, mla_q_norm, mla_k_norm, out_norm_g, w_o,
           norm2_g, w_router, router_bias, w_gate, w_up, w_down, ws_gate, ws_up, ws_down):
    B, S, D = x.shape
    CTX = ctx.shape[1]
    L = ada_w.shape[0]
    n_lat = B * S
    R = n_lat + B * CTX
    W = D // 4
    kvr = w_ukv.shape[1]
    nh_mla = w_uq.shape[2] // (MLA_NOPE + MLA_ROPE)
    E = w_router.shape[2]
    assert B == 2 and S % ROW_TILE == 0 and (B * CTX) % ROW_TILE == 0 and CTX % CONV_TILE == 0
    assert S % GRID_W == 0 and S // GRID_W >= NA_WIN_H and W % HEAD_DIM == 0 and kvr % LANE == 0
    assert S % 1024 == 0 and (TOP_K * B * CTX) % MOE_TILE == 0 and ROW_TILE % COMBINE_TILE == 0
    assert (S // GRID_W) % NA_GROUP == 0 and S // GRID_W >= NA_KROWS

    x_pair = (x.reshape(n_lat, D), ctx.reshape(B * CTX, D))
    w_in_t = jnp.swapaxes(w_in, 1, 2)
    cvecs = jnp.concatenate([c, c_ctx[None]], axis=0)
    mod = _modulation(cvecs, ada_w, ada_b)
    tabs = _rope_tables(S, ROW_TILE)

    for l in range(L):
        last = l == L - 1
        rows = n_lat if last else R
        mod_l = mod[l].reshape(8 * 6, 1, D)

        w_tail_t = jnp.concatenate([w_in_t[l, 7 * W:, :], jnp.zeros((LANE - MLA_ROPE, D), F32)], axis=0)
        wq = w_uq[l].reshape(W, nh_mla, MLA_NOPE + MLA_ROPE)
        wq = jnp.concatenate([wq, jnp.zeros((W, nh_mla, MLA_QPAD - MLA_NOPE - MLA_ROPE), F32)], axis=2)
        wq = wq.reshape(W, nh_mla * MLA_QPAD).astype(BF16)
        wkv = w_ukv[l].reshape(kvr, nh_mla, MLA_NOPE + MLA_V)
        wk = wkv[:, :, :MLA_NOPE].reshape(kvr, nh_mla * MLA_NOPE).astype(BF16)
        wv = wkv[:, :, MLA_NOPE:].reshape(kvr, nh_mla * MLA_V).astype(BF16)
        gq_n = mla_q_norm[l][:MLA_NOPE].reshape(1, LANE)
        gq_r = _pad_lanes(mla_q_norm[l][MLA_NOPE:], LANE)
        gk_n = mla_k_norm[l][:MLA_NOPE].reshape(1, LANE)
        gk_r = _pad_lanes(mla_k_norm[l][MLA_NOPE:], LANE)
        og = out_norm_g[l]

        h = _norm_mod(x_pair, R, norm1_g[l], mod_l, 1, 0, n_lat, S)
        z_main = _matmul_ws(h, w_in_t, l, 7 * W, min(512, W), BF16)
        z_tail = _matmul_ws(h, w_tail_t, None, w_tail_t.shape[0], w_tail_t.shape[0], BF16)
        nq, nk, cq, ckv, kr = _post_in(z_main, z_tail, na_q_norm[l].reshape(1, LANE),
                                       na_k_norm[l].reshape(1, LANE), mla_q_lat_norm[l].reshape(1, W),
                                       mla_kv_lat_norm[l].reshape(1, kvr), gk_r, tabs, n_lat, S)
        qf = _uq_matmul(cq, wq, gq_n, gq_r, tabs, n_lat, S)
        kf = _uk_matmul(ckv, wk, gk_n, kr)
        vf = _matmul(ckv, wv, BF16, 512)

        y_conv = _conv_mixer(z_main, conv_w[l], og[:W].reshape(1, W), rows, n_lat, S, CTX)
        y_na = _na_attention(nq, nk, z_main, _na_bias_table(na_rpb[l], S // GRID_W), B, S, CTX, n_lat)
        y_mla = _mla_attention(qf, kf, vf, B, S, CTX, n_lat)
        if last:
            y_na_c, y_mla_c = y_na, y_mla
        else:
            nh_na = W // HEAD_DIM
            y_na_c = _ctx_attention(nq, nk, z_main, nh_na, HEAD_DIM, HEAD_DIM, 5 * nh_na, B, CTX, n_lat)
            y_mla_c = _ctx_attention(qf, kf, vf, nh_mla, MLA_QPAD, MLA_V, 0, B, CTX, n_lat)
        y = _merge(y_conv, y_na, y_mla, y_na_c, y_mla_c,
                   og[W:2 * W].reshape(1, W), og[2 * W:].reshape(1, D - 2 * W), rows)
        xs = _wo_matmul(y, w_o, l, x_pair, mod_l, rows, n_lat, S)

        wsg, wsu, wsd = ws_gate[l].astype(BF16), ws_up[l].astype(BF16), ws_down[l].astype(BF16)
        h2, eidx, wgt, rank, cnt = _router(xs, norm2_g[l], mod_l, w_router[l], router_bias[l],
                                           0, rows, n_lat, S)
        n_pairs = TOP_K * rows
        n_tiles = n_pairs // MOE_TILE
        tile_bounds = sorted({0} | {round(n_tiles * f) for f in MOE_EXPERT_SPLIT})
        dest, row_token, visits, gstart = _dispatch_plan(eidx, rank, cnt[:, 0].astype(jnp.int32), tile_bounds)
        y_rows = None
        for p in range(len(tile_bounds) - 1):
            part_tokens = row_token[tile_bounds[p] * MOE_TILE:tile_bounds[p + 1] * MOE_TILE]
            gathered = h2.at[part_tokens].get(mode="promise_in_bounds")
            y_rows = _experts(gathered, tile_bounds[p], n_pairs, y_rows, visits[p], gstart,
                              w_gate, w_up, w_down, l)
        gates = wgt.T
        n_blocks = rows // ROW_TILE
        row_bounds = sorted({0} | {round(n_blocks * f) * ROW_TILE for f in MOE_COMBINE_SPLIT})
        for row0, row1 in zip(row_bounds[:-1], row_bounds[1:]):
            y8 = y_rows.at[dest[:, row0:row1].reshape(-1)].get(mode="promise_in_bounds")
            xs = _shared_combine(h2, wsg, wsu, wsd, y8.reshape(TOP_K, row1 - row0, D), gates[row0:row1],
                                 xs, mod_l, row0, row1 - row0, n_lat, S)
        x_pair = (xs, xs)

    return xs[:n_lat].reshape(B, S, D)
```

```python
import functools

import numpy as np
import jax
import jax.numpy as jnp
from jax import lax
from jax.experimental import pallas as pl
from jax.experimental.pallas import tpu as pltpu

F32 = jnp.float32
BF16 = jnp.bfloat16

GRID_W = 64
HEAD_DIM = 128
MLA_NOPE = 128
MLA_ROPE = 64
MLA_V = 128
MLA_QPAD = 256
NA_WIN_H = 8
NA_WIN_W = 16
NA_GROUP = 8
NA_KROWS = NA_GROUP + NA_WIN_H
N_GROUPS = 8
TOPK_GROUPS = 4
TOP_K = 8
ROUTE_SCALE = 2.5
ROPE_BASE = 10000.0
EPS = 1e-6
NEG = -1e30
LOG2E = 1.4426950408889634

LANE = 128
ROW_TILE = 512
CONV_TILE = 256
IN_COL_TILE = 1024
WO_COL_TILE = 1024
MOE_TILE = 256
COMBINE_TILE = 128
MOE_PARTS = 4
VMEM_LIMIT = 56 * 1024 * 1024


def _cparams(n_axes):
    return pltpu.CompilerParams(dimension_semantics=("arbitrary",) * n_axes,
                                vmem_limit_bytes=VMEM_LIMIT)


def _rms(x, n):
    return x * lax.rsqrt(jnp.sum(x * x, axis=-1, keepdims=True) * (1.0 / n) + EPS)


def _mod_kernel(sb_ref, w_ref, b_ref, o_ref):
    tn = w_ref.shape[1]
    o_ref[...] = jnp.zeros(o_ref.shape, F32)
    for cblk in range(tn // LANE):
        wc = w_ref[:, cblk * LANE:(cblk + 1) * LANE]
        for r in range(3):
            acc = jnp.sum(wc * sb_ref[r], axis=0, keepdims=True)
            o_ref[r:r + 1, cblk * LANE:(cblk + 1) * LANE] = acc + b_ref[:, cblk * LANE:(cblk + 1) * LANE]


def _modulation(cvecs, ada_w, ada_b):
    L, D, D6 = ada_w.shape
    tn = 512
    s = cvecs * jax.nn.sigmoid(cvecs)
    sb = jnp.broadcast_to(s[:, :, None], (3, D, LANE))
    return pl.pallas_call(
        _mod_kernel,
        grid=(L, D6 // tn),
        in_specs=[pl.BlockSpec((3, D, LANE), lambda l, j: (0, 0, 0)),
                  pl.BlockSpec((None, D, tn), lambda l, j: (l, 0, j)),
                  pl.BlockSpec((None, 1, tn), lambda l, j: (l, 0, j))],
        out_specs=pl.BlockSpec((None, 8, tn), lambda l, j: (l, 0, j)),
        out_shape=jax.ShapeDtypeStruct((L, 8, D6), F32),
        compiler_params=_cparams(2),
        name="modulation",
    )(sb, ada_w, ada_b.reshape(L, 1, D6))


def _mod_row(i, tm, n_lat, seq):
    r0 = i * tm
    return jnp.where(r0 < n_lat, r0 // seq, n_lat // seq)


def _split_rows(x_pair, tm):
    head, tail = x_pair
    head_blocks = head.shape[0] // tm
    tail_last = tail.shape[0] // tm - 1
    return (head_blocks, lambda i: jnp.minimum(i, head_blocks - 1),
            lambda i: jnp.clip(i - head_blocks, 0, tail_last))


def _norm_mod_kernel(xa_ref, xb_ref, g_ref, sc_ref, sh_ref, o_ref, *, head_blocks):
    x = jnp.where(pl.program_id(0) < head_blocks, xa_ref[...], xb_ref[...])
    y = _rms(x, x.shape[-1]) * g_ref[...]
    o_ref[...] = (y * (1.0 + sc_ref[...]) + sh_ref[...]).astype(o_ref.dtype)


def _norm_mod(x_pair, R, g, mod_l, which_sc, which_sh, n_lat, seq):
    D = x_pair[0].shape[1]
    tm = ROW_TILE
    mrow = functools.partial(_mod_row, tm=tm, n_lat=n_lat, seq=seq)
    head_blocks, head_map, tail_map = _split_rows(x_pair, tm)
    return pl.pallas_call(
        functools.partial(_norm_mod_kernel, head_blocks=head_blocks),
        grid=(R // tm,),
        in_specs=[pl.BlockSpec((tm, D), lambda i: (head_map(i), 0)),
                  pl.BlockSpec((tm, D), lambda i: (tail_map(i), 0)),
                  pl.BlockSpec((1, D), lambda i: (0, 0)),
                  pl.BlockSpec((None, 1, D), lambda i: (mrow(i) * 6 + which_sc, 0, 0)),
                  pl.BlockSpec((None, 1, D), lambda i: (mrow(i) * 6 + which_sh, 0, 0))],
        out_specs=pl.BlockSpec((tm, D), lambda i: (i, 0)),
        out_shape=jax.ShapeDtypeStruct((R, D), BF16),
        compiler_params=_cparams(1),
        name="norm_mod",
    )(x_pair[0], x_pair[1], g.reshape(1, D), mod_l, mod_l)


def _mm_kernel(x_ref, w_ref, o_ref):
    o_ref[...] = jnp.dot(x_ref[...], w_ref[...], preferred_element_type=F32).astype(o_ref.dtype)


def _matmul(x, w, out_dtype, tn, rows=None):
    R, K = x.shape
    rows = R if rows is None else rows
    N = w.shape[1]
    tm = ROW_TILE
    return pl.pallas_call(
        _mm_kernel,
        grid=(rows // tm, N // tn),
        in_specs=[pl.BlockSpec((tm, K), lambda i, j: (i, 0)),
                  pl.BlockSpec((K, tn), lambda i, j: (0, j))],
        out_specs=pl.BlockSpec((tm, tn), lambda i, j: (i, j)),
        out_shape=jax.ShapeDtypeStruct((rows, N), out_dtype),
        compiler_params=_cparams(2),
        name="matmul",
    )(x, w)


def _mm_ws_kernel(x_ref, w_ref, o_ref, wb_ref):
    @pl.when(pl.program_id(1) == 0)
    def _():
        wb_ref[...] = w_ref[...].astype(BF16)

    o_ref[...] = lax.dot_general(x_ref[...], wb_ref[...], (((1,), (1,)), ((), ())),
                                 preferred_element_type=F32).astype(o_ref.dtype)


def _matmul_ws(x, wt, layer, n_cols, tn, out_dtype):
    R, K = x.shape
    tm = ROW_TILE
    if layer is None:
        w_spec = pl.BlockSpec((tn, K), lambda j, i: (j, 0))
    else:
        w_spec = pl.BlockSpec((None, tn, K), lambda j, i: (layer, j, 0))
    return pl.pallas_call(
        _mm_ws_kernel,
        grid=(n_cols // tn, R // tm),
        in_specs=[pl.BlockSpec((tm, K), lambda j, i: (i, 0)), w_spec],
        out_specs=pl.BlockSpec((tm, tn), lambda j, i: (i, j)),
        out_shape=jax.ShapeDtypeStruct((R, n_cols), out_dtype),
        scratch_shapes=[pltpu.VMEM((tn, K), BF16)],
        compiler_params=_cparams(2),
        name="matmul_ws",
    )(x, wt)


def _rope_rotate(t, c_ref, s1_ref, s2_ref):
    return (t * c_ref[...] + pltpu.roll(t, LANE - MLA_ROPE // 2, axis=1) * s1_ref[...]
            + pltpu.roll(t, MLA_ROPE // 2, axis=1) * s2_ref[...])


def _uq_kernel(x_ref, w_ref, gn_ref, gr_ref, c_ref, s1_ref, s2_ref, o_ref):
    scale = (MLA_NOPE + MLA_ROPE) ** -0.5 * LOG2E
    tm = x_ref.shape[0]
    half = tm // 2
    for r0 in (0, half):
        rs = slice(r0, r0 + half)
        acc = jnp.dot(x_ref[rs, :], w_ref[...], preferred_element_type=F32)
        for h in range(acc.shape[1] // MLA_QPAD):
            o = h * MLA_QPAD
            a = _rms(acc[:, o:o + MLA_NOPE], MLA_NOPE) * gn_ref[...]
            r = _rms(acc[:, o + MLA_NOPE:o + MLA_QPAD], MLA_ROPE) * gr_ref[...]
            r = (r * c_ref[rs, :] + pltpu.roll(r, LANE - MLA_ROPE // 2, axis=1) * s1_ref[rs, :]
                 + pltpu.roll(r, MLA_ROPE // 2, axis=1) * s2_ref[rs, :])
            o_ref[rs, o:o + MLA_NOPE] = (a * scale).astype(o_ref.dtype)
            o_ref[rs, o + MLA_NOPE:o + MLA_QPAD] = (r * scale).astype(o_ref.dtype)


def _rope_block(i, tm, n_lat, seq):
    return jnp.where(i * tm < n_lat, i % (seq // tm), seq // tm)


def _uq_matmul(x, w, gn, gr, tabs, n_lat, seq):
    R, K = x.shape
    N = w.shape[1]
    tm, tn = ROW_TILE, 1024
    rb = functools.partial(_rope_block, tm=tm, n_lat=n_lat, seq=seq)
    tab_spec = pl.BlockSpec((tm, LANE), lambda i, j: (rb(i), 0))
    vec_spec = pl.BlockSpec((1, LANE), lambda i, j: (0, 0))
    return pl.pallas_call(
        _uq_kernel,
        grid=(R // tm, N // tn),
        in_specs=[pl.BlockSpec((tm, K), lambda i, j: (i, 0)),
                  pl.BlockSpec((K, tn), lambda i, j: (0, j)),
                  vec_spec, vec_spec, tab_spec, tab_spec, tab_spec],
        out_specs=pl.BlockSpec((tm, tn), lambda i, j: (i, j)),
        out_shape=jax.ShapeDtypeStruct((R, N), BF16),
        compiler_params=_cparams(2),
        name="uq_matmul",
    )(x, w, gn, gr, *tabs)


def _uk_kernel(x_ref, w_ref, gk_ref, kr_ref, o_ref):
    half = x_ref.shape[0] // 2
    for r0 in (0, half):
        rs = slice(r0, r0 + half)
        acc = jnp.dot(x_ref[rs, :], w_ref[...], preferred_element_type=F32)
        for h in range(acc.shape[1] // MLA_NOPE):
            a = _rms(acc[:, h * MLA_NOPE:(h + 1) * MLA_NOPE], MLA_NOPE) * gk_ref[...]
            o_ref[rs, h * MLA_QPAD:h * MLA_QPAD + MLA_NOPE] = a.astype(o_ref.dtype)
            o_ref[rs, h * MLA_QPAD + MLA_NOPE:(h + 1) * MLA_QPAD] = kr_ref[rs, :]


def _uk_matmul(x, w, gk, kr_pad):
    R, K = x.shape
    N = w.shape[1]
    tm, tn = ROW_TILE, 512
    return pl.pallas_call(
        _uk_kernel,
        grid=(R // tm, N // tn),
        in_specs=[pl.BlockSpec((tm, K), lambda i, j: (i, 0)),
                  pl.BlockSpec((K, tn), lambda i, j: (0, j)),
                  pl.BlockSpec((1, LANE), lambda i, j: (0, 0)),
                  pl.BlockSpec((tm, LANE), lambda i, j: (i, 0))],
        out_specs=pl.BlockSpec((tm, 2 * tn), lambda i, j: (i, j)),
        out_shape=jax.ShapeDtypeStruct((R, 2 * N), BF16),
        compiler_params=_cparams(2),
        name="uk_matmul",
    )(x, w, gk, kr_pad)


def _wo_kernel(y_ref, w_ref, xa_ref, xb_ref, g_ref, o_ref, *, head_blocks):
    acc = jnp.dot(y_ref[...], w_ref[...], preferred_element_type=F32)
    x = jnp.where(pl.program_id(1) < head_blocks, xa_ref[...], xb_ref[...])
    o_ref[...] = x + g_ref[...] * acc


def _wo_matmul(y, w_o, layer, x_pair, mod_l, rows, n_lat, seq):
    K = y.shape[1]
    N = w_o.shape[2]
    tm, tn = ROW_TILE, WO_COL_TILE
    mrow = functools.partial(_mod_row, tm=tm, n_lat=n_lat, seq=seq)
    head_blocks, head_map, tail_map = _split_rows(x_pair, tm)
    return pl.pallas_call(
        functools.partial(_wo_kernel, head_blocks=head_blocks),
        grid=(N // tn, rows // tm),
        in_specs=[pl.BlockSpec((tm, K), lambda j, i: (i, 0)),
                  pl.BlockSpec((None, K, tn), lambda j, i: (layer, 0, j)),
                  pl.BlockSpec((tm, tn), lambda j, i: (head_map(i), j)),
                  pl.BlockSpec((tm, tn), lambda j, i: (tail_map(i), j)),
                  pl.BlockSpec((None, 1, tn), lambda j, i: (mrow(i) * 6 + 2, 0, j))],
        out_specs=pl.BlockSpec((tm, tn), lambda j, i: (i, j)),
        out_shape=jax.ShapeDtypeStruct((rows, N), F32),
        compiler_params=_cparams(2),
        name="wo_matmul",
    )(y, w_o, x_pair[0], x_pair[1], mod_l)


def _post_in_kernel(nq_ref, nk_ref, cq_ref, tail_ref, gq_ref, gk_ref, gql_ref, gkvl_ref, gkr_ref,
                    c_ref, s1_ref, s2_ref, nq_o, nk_o, cq_o, ckv_o, kr_o):
    nh = nq_ref.shape[1] // HEAD_DIM
    scale = HEAD_DIM ** -0.5 * LOG2E
    for h in range(nh):
        sl = slice(h * HEAD_DIM, (h + 1) * HEAD_DIM)
        q = nq_ref[:, sl].astype(F32)
        nq_o[:, sl] = (_rms(q, HEAD_DIM) * gq_ref[...] * scale).astype(nq_o.dtype)
        k = nk_ref[:, sl].astype(F32)
        nk_o[:, sl] = (_rms(k, HEAD_DIM) * gk_ref[...]).astype(nk_o.dtype)
    cq = cq_ref[...].astype(F32)
    cq_o[...] = (_rms(cq, cq.shape[1]) * gql_ref[...]).astype(cq_o.dtype)
    kvr = ckv_o.shape[1]
    ckv = tail_ref[:, :kvr].astype(F32)
    ckv_o[...] = (_rms(ckv, kvr) * gkvl_ref[...]).astype(ckv_o.dtype)
    kr = tail_ref[:, kvr:kvr + LANE].astype(F32)
    kr = _rms(kr, MLA_ROPE) * gkr_ref[...]
    kr_o[...] = _rope_rotate(kr, c_ref, s1_ref, s2_ref).astype(kr_o.dtype)


def _post_in(z_main, z_tail, gq, gk, gql, gkvl, gkr, tabs, n_lat, seq):
    R = z_main.shape[0]
    W = z_main.shape[1] // 7
    kvr = z_tail.shape[1] - LANE
    tm = ROW_TILE
    rb = functools.partial(_rope_block, tm=tm, n_lat=n_lat, seq=seq)
    tab_spec = pl.BlockSpec((tm, LANE), lambda i: (rb(i), 0))

    def vec(n):
        return pl.BlockSpec((1, n), lambda i: (0, 0))

    def col(j):
        return pl.BlockSpec((tm, W), lambda i: (i, j))

    return pl.pallas_call(
        _post_in_kernel,
        grid=(R // tm,),
        in_specs=[col(3), col(4), col(6), pl.BlockSpec((tm, kvr + LANE), lambda i: (i, 0)),
                  vec(LANE), vec(LANE), vec(W), vec(kvr), vec(LANE), tab_spec, tab_spec, tab_spec],
        out_specs=[pl.BlockSpec((tm, W), lambda i: (i, 0)), pl.BlockSpec((tm, W), lambda i: (i, 0)),
                   pl.BlockSpec((tm, W), lambda i: (i, 0)), pl.BlockSpec((tm, kvr), lambda i: (i, 0)),
                   pl.BlockSpec((tm, LANE), lambda i: (i, 0))],
        out_shape=[jax.ShapeDtypeStruct((R, W), BF16), jax.ShapeDtypeStruct((R, W), BF16),
                   jax.ShapeDtypeStruct((R, W), BF16), jax.ShapeDtypeStruct((R, kvr), BF16),
                   jax.ShapeDtypeStruct((R, LANE), BF16)],
        compiler_params=_cparams(1),
        name="post_in",
    )(z_main, z_main, z_main, z_tail, gq, gk, gql, gkvl, gkr, *tabs)


def _conv_kernel(b_ref, c_ref, u_ref, cp_ref, up_ref, cn_ref, un_ref, w_ref, g_ref, o_ref,
                 *, n_lat, seq, ctx_len):
    tm, W = c_ref.shape
    row0 = pl.program_id(0) * tm
    lat = row0 < n_lat
    pos = jnp.where(lat, row0 % seq, (row0 - n_lat) % ctx_len)
    slen = jnp.where(lat, seq, ctx_len)
    has_prev = (pos > 0).astype(F32)
    has_next = (pos + tm < slen).astype(F32)
    v = c_ref[...].astype(F32) * u_ref[...].astype(F32)
    vprev = cp_ref[7:8, :].astype(F32) * up_ref[7:8, :].astype(F32) * has_prev
    vnext = cn_ref[0:1, :].astype(F32) * un_ref[0:1, :].astype(F32) * has_next
    rid = lax.broadcasted_iota(jnp.int32, (tm, W), 0)
    v_dn = jnp.where(rid == 0, vprev, pltpu.roll(v, 1, axis=0))
    v_up = jnp.where(rid == tm - 1, vnext, pltpu.roll(v, tm - 1, axis=0))
    y = w_ref[0:1, :] * v_dn + w_ref[1:2, :] * v + w_ref[2:3, :] * v_up
    y = b_ref[...].astype(F32) * y
    o_ref[...] = (_rms(y, W) * g_ref[...]).astype(o_ref.dtype)


def _conv_mixer(z_main, conv_w, g, rows, n_lat, seq, ctx_len):
    R = z_main.shape[0]
    W = z_main.shape[1] // 7
    tm = CONV_TILE
    nb8 = R // 8

    def cur(j):
        return pl.BlockSpec((tm, W), lambda i: (i, j))

    def prev(j):
        return pl.BlockSpec((8, W), lambda i: (jnp.maximum(i * (tm // 8) - 1, 0), j))

    def nxt(j):
        return pl.BlockSpec((8, W), lambda i: (jnp.minimum((i + 1) * (tm // 8), nb8 - 1), j))

    return pl.pallas_call(
        functools.partial(_conv_kernel, n_lat=n_lat, seq=seq, ctx_len=ctx_len),
        grid=(rows // tm,),
        in_specs=[cur(0), cur(1), cur(2), prev(1), prev(2), nxt(1), nxt(2),
                  pl.BlockSpec((3, W), lambda i: (0, 0)), pl.BlockSpec((1, W), lambda i: (0, 0))],
        out_specs=pl.BlockSpec((tm, W), lambda i: (i, 0)),
        out_shape=jax.ShapeDtypeStruct((rows, W), BF16),
        compiler_params=_cparams(1),
        name="conv_mixer",
    )(z_main, z_main, z_main, z_main, z_main, z_main, z_main, conv_w, g)


def _na_group_plan(rows):
    n_groups = rows // NA_GROUP
    kinds = [(0, 0), (NA_GROUP, NA_GROUP - NA_WIN_H // 2), ((n_groups - 1) * NA_GROUP, rows - NA_KROWS)]
    plane = np.full((3, NA_GROUP, NA_KROWS), 2 * NA_WIN_H - 1, np.int64)
    for t, (r_first, k_first) in enumerate(kinds):
        for qr in range(NA_GROUP):
            r = r_first + qr
            r0 = min(max(r - NA_WIN_H // 2, 0), rows - NA_WIN_H)
            for kr in range(NA_KROWS):
                kk = k_first + kr
                if r0 <= kk < r0 + NA_WIN_H:
                    plane[t, qr, kr] = kk - r + NA_WIN_H - 1
    return plane


def _na_bias_table(rpb, rows):
    qc = np.arange(GRID_W)[:, None]
    kc = np.arange(GRID_W)[None, :]
    cstart = np.clip(qc - NA_WIN_W // 2, 0, GRID_W - NA_WIN_W)
    ok = (kc >= cstart) & (kc < cstart + NA_WIN_W)
    dc = np.clip(kc - qc + NA_WIN_W - 1, 0, 2 * NA_WIN_W - 2)
    H = rpb.shape[0]
    onehot = (dc[None] == np.arange(2 * NA_WIN_W - 1)[:, None, None]).astype(np.float32)
    cols = jnp.einsum("hdc,cqk->hdqk", rpb.astype(F32), onehot, precision=lax.Precision.HIGHEST)
    cols = jnp.where(ok[None, None], cols * LOG2E, NEG)
    planes = jnp.concatenate([cols, jnp.full((H, 1, GRID_W, GRID_W), NEG, F32)], axis=1)
    pick = (_na_group_plan(rows)[..., None] == np.arange(2 * NA_WIN_H)).astype(np.float32)
    g = jnp.einsum("tqkp,hpxy->htqxky", pick, planes, precision=lax.Precision.HIGHEST)
    return g.reshape(H, 3, NA_GROUP * GRID_W, NA_KROWS * GRID_W)


def _na_kernel(q_ref, k_ref, v_ref, kc_ref, vc_ref, bias_ref, o_ref):
    rows = q_ref.shape[0] // GRID_W
    n_groups = rows // NA_GROUP
    nq = NA_GROUP * GRID_W
    nk = NA_KROWS * GRID_W
    dn = (((1,), (1,)), ((), ()))
    kc = kc_ref[...]
    vc = vc_ref[...]

    def body(g, carry):
        r_first = g * NA_GROUP
        k_first = jnp.clip(r_first - NA_WIN_H // 2, 0, rows - NA_KROWS)
        kind = jnp.where(g == 0, 0, jnp.where(g == n_groups - 1, 2, 1))
        qoff = pl.multiple_of(r_first * GRID_W, GRID_W)
        koff = pl.multiple_of(k_first * GRID_W, GRID_W)
        q = q_ref[pl.ds(qoff, nq), :]
        k = k_ref[pl.ds(koff, nk), :]
        v = v_ref[pl.ds(koff, nk), :]
        s = lax.dot_general(q, k, dn, preferred_element_type=F32) + bias_ref[kind]
        sc = lax.dot_general(q, kc, dn, preferred_element_type=F32)
        m = jnp.maximum(jnp.max(s, axis=-1, keepdims=True), jnp.max(sc, axis=-1, keepdims=True))
        p = jnp.exp2(s - m)
        pc = jnp.exp2(sc - m)
        l = jnp.sum(p, axis=-1, keepdims=True) + jnp.sum(pc, axis=-1, keepdims=True)
        o = (jnp.dot(p.astype(BF16), v, preferred_element_type=F32)
             + jnp.dot(pc.astype(BF16), vc, preferred_element_type=F32))
        o_ref[pl.ds(qoff, nq), :] = (o / l).astype(o_ref.dtype)
        return carry

    lax.fori_loop(0, n_groups, body, 0, unroll=2)


def _na_attention(nq, nk, z_main, bias_tab, B, seq, ctx_len, n_lat):
    W = nq.shape[1]
    nh = W // HEAD_DIM
    cb = n_lat // ctx_len
    voff = 5 * nh
    return pl.pallas_call(
        _na_kernel,
        grid=(nh, B),
        in_specs=[pl.BlockSpec((seq, HEAD_DIM), lambda h, b: (b, h)),
                  pl.BlockSpec((seq, HEAD_DIM), lambda h, b: (b, h)),
                  pl.BlockSpec((seq, HEAD_DIM), lambda h, b: (b, voff + h)),
                  pl.BlockSpec((ctx_len, HEAD_DIM), lambda h, b: (cb + b, h)),
                  pl.BlockSpec((ctx_len, HEAD_DIM), lambda h, b: (cb + b, voff + h)),
                  pl.BlockSpec((None, 3, NA_GROUP * GRID_W, NA_KROWS * GRID_W), lambda h, b: (h, 0, 0, 0))],
        out_specs=pl.BlockSpec((seq, HEAD_DIM), lambda h, b: (b, h)),
        out_shape=jax.ShapeDtypeStruct((n_lat, W), BF16),
        compiler_params=_cparams(2),
        name="na_attention",
    )(nq, nk, z_main, nk, z_main, bias_tab)


def _flash_kernel(q_ref, k_ref, v_ref, kc_ref, vc_ref, o_ref, *, tk):
    tq = q_ref.shape[0]
    dv = v_ref.shape[1]
    dn = (((1,), (1,)), ((), ()))
    q = q_ref[...]

    def step(k, v, m, l, acc):
        s = lax.dot_general(q, k, dn, preferred_element_type=F32)
        m_new = jnp.maximum(m, jnp.max(s, axis=-1, keepdims=True))
        alpha = jnp.exp2(m - m_new)
        p = jnp.exp2(s - m_new)
        l = alpha * l + jnp.sum(p, axis=-1, keepdims=True)
        acc = alpha * acc + jnp.dot(p.astype(BF16), v, preferred_element_type=F32)
        return m_new, l, acc

    def body(j, carry):
        off = pl.multiple_of(j * tk, tk)
        return step(k_ref[pl.ds(off, tk), :], v_ref[pl.ds(off, tk), :], *carry)

    init = (jnp.full((tq, 1), NEG, F32), jnp.zeros((tq, 1), F32), jnp.zeros((tq, dv), F32))
    n_chunks = k_ref.shape[0] // tk
    m, l, acc = lax.fori_loop(0, n_chunks, body, init, unroll=min(n_chunks, 8))
    m, l, acc = step(kc_ref[...], vc_ref[...], m, l, acc)
    o_ref[...] = (acc / l).astype(o_ref.dtype)


def _mla_attention(qf, kf, vf, B, seq, ctx_len, n_lat):
    R = qf.shape[0]
    nh = qf.shape[1] // MLA_QPAD
    tq, tk = 1024, 1024
    cb = n_lat // ctx_len
    nqb = seq // tq
    return pl.pallas_call(
        functools.partial(_flash_kernel, tk=tk),
        grid=(B, nh, nqb),
        in_specs=[pl.BlockSpec((tq, MLA_QPAD), lambda b, h, i: (b * nqb + i, h)),
                  pl.BlockSpec((seq, MLA_QPAD), lambda b, h, i: (b, h)),
                  pl.BlockSpec((seq, MLA_V), lambda b, h, i: (b, h)),
                  pl.BlockSpec((ctx_len, MLA_QPAD), lambda b, h, i: (cb + b, h)),
                  pl.BlockSpec((ctx_len, MLA_V), lambda b, h, i: (cb + b, h))],
        out_specs=pl.BlockSpec((tq, MLA_V), lambda b, h, i: (b * nqb + i, h)),
        out_shape=jax.ShapeDtypeStruct((n_lat, nh * MLA_V), BF16),
        compiler_params=_cparams(3),
        name="mla_attention",
    )(qf, kf, vf, kf, vf)


def _ctx_attn_kernel(q_ref, k_ref, v_ref, o_ref):
    s = lax.dot_general(q_ref[...], k_ref[...], (((1,), (1,)), ((), ())), preferred_element_type=F32)
    p = jnp.exp2(s - jnp.max(s, axis=-1, keepdims=True))
    o = jnp.dot(p.astype(BF16), v_ref[...], preferred_element_type=F32)
    o_ref[...] = (o / jnp.sum(p, axis=-1, keepdims=True)).astype(o_ref.dtype)


def _ctx_attention(q, k, v, nh, dq, dv, voff, B, ctx_len, n_lat):
    cb = n_lat // ctx_len
    return pl.pallas_call(
        _ctx_attn_kernel,
        grid=(B, nh),
        in_specs=[pl.BlockSpec((ctx_len, dq), lambda b, h: (cb + b, h)),
                  pl.BlockSpec((ctx_len, dq), lambda b, h: (cb + b, h)),
                  pl.BlockSpec((ctx_len, dv), lambda b, h: (cb + b, voff + h))],
        out_specs=pl.BlockSpec((ctx_len, dv), lambda b, h: (b, h)),
        out_shape=jax.ShapeDtypeStruct((B * ctx_len, nh * dv), BF16),
        compiler_params=_cparams(2),
        name="ctx_attention",
    )(q, k, v)


def _merge_kernel(yc_ref, yn_ref, ym_ref, ync_ref, ymc_ref, gn_ref, gm_ref, o_ref, *, lat_blocks):
    cw = yc_ref.shape[1]
    nw = yn_ref.shape[1]
    lat = pl.program_id(0) < lat_blocks
    o_ref[:, :cw] = yc_ref[...]
    yn = jnp.where(lat, yn_ref[...], ync_ref[...]).astype(F32)
    o_ref[:, cw:cw + nw] = (_rms(yn, nw) * gn_ref[...]).astype(o_ref.dtype)
    ym = jnp.where(lat, ym_ref[...], ymc_ref[...]).astype(F32)
    o_ref[:, cw + nw:] = (_rms(ym, ym.shape[1]) * gm_ref[...]).astype(o_ref.dtype)


def _merge(y_conv, y_na, y_mla, y_na_ctx, y_mla_ctx, gn, gm, rows):
    cw, nw, mw = y_conv.shape[1], y_na.shape[1], y_mla.shape[1]
    tm = ROW_TILE
    lat_blocks = y_na.shape[0] // tm
    ctx_last = y_na_ctx.shape[0] // tm - 1

    def lat_map(i):
        return (jnp.minimum(i, lat_blocks - 1), 0)

    def ctx_map(i):
        return (jnp.clip(i - lat_blocks, 0, ctx_last), 0)

    return pl.pallas_call(
        functools.partial(_merge_kernel, lat_blocks=lat_blocks),
        grid=(rows // tm,),
        in_specs=[pl.BlockSpec((tm, cw), lambda i: (i, 0)),
                  pl.BlockSpec((tm, nw), lat_map), pl.BlockSpec((tm, mw), lat_map),
                  pl.BlockSpec((tm, nw), ctx_map), pl.BlockSpec((tm, mw), ctx_map),
                  pl.BlockSpec((1, nw), lambda i: (0, 0)), pl.BlockSpec((1, mw), lambda i: (0, 0))],
        out_specs=pl.BlockSpec((tm, cw + nw + mw), lambda i: (i, 0)),
        out_shape=jax.ShapeDtypeStruct((rows, cw + nw + mw), BF16),
        compiler_params=_cparams(1),
        name="merge_norm",
    )(y_conv, y_na, y_mla, y_na_ctx, y_mla_ctx, gn, gm)


def _router_kernel(x_ref, g_ref, sc_ref, sh_ref, whi_ref, wlo_ref, rb_ref, tri_ref,
                   h_ref, idx_ref, wgt_ref, rank_ref, cnt_ref, base_ref):
    @pl.when(pl.program_id(0) == 0)
    def _():
        base_ref[...] = jnp.zeros(base_ref.shape, F32)

    x = x_ref[...]
    h = _rms(x, x.shape[-1]) * g_ref[...] * (1.0 + sc_ref[...]) + sh_ref[...]
    h_hi = h.astype(BF16)
    h_ref[...] = h_hi
    h_lo = (h - h_hi.astype(F32)).astype(BF16)
    dn = (((1,), (1,)), ((), ()))
    logits = (lax.dot_general(whi_ref[...], h_hi, dn, preferred_element_type=F32)
              + lax.dot_general(whi_ref[...], h_lo, dn, preferred_element_type=F32)
              + lax.dot_general(wlo_ref[...], h_hi, dn, preferred_element_type=F32))
    E, tm = logits.shape
    gs = E // N_GROUPS
    scores = jax.nn.sigmoid(logits)
    sel = scores + rb_ref[...]
    sel3 = sel.reshape(N_GROUPS, gs, tm)
    io3 = lax.broadcasted_iota(jnp.int32, sel3.shape, 1)
    m1 = jnp.max(sel3, axis=1, keepdims=True)
    i1 = jnp.min(jnp.where(sel3 == m1, io3, gs), axis=1, keepdims=True)
    m2 = jnp.max(jnp.where(io3 == i1, -jnp.inf, sel3), axis=1, keepdims=True)
    grp = m1 + m2
    iog = lax.broadcasted_iota(jnp.int32, grp.shape, 0)
    keep = jnp.zeros(grp.shape, jnp.bool_)
    for _ in range(TOPK_GROUPS):
        gm = jnp.max(grp, axis=0, keepdims=True)
        gi = jnp.min(jnp.where(grp == gm, iog, N_GROUPS), axis=0, keepdims=True)
        hit = iog == gi
        keep = jnp.logical_or(keep, hit)
        grp = jnp.where(hit, -jnp.inf, grp)
    selm = jnp.where(keep, sel3, -jnp.inf).reshape(E, tm)
    ioe = lax.broadcasted_iota(jnp.int32, (E, tm), 0)
    idxs, wgts, hits = [], [], []
    for _ in range(TOP_K):
        m = jnp.max(selm, axis=0, keepdims=True)
        ei = jnp.min(jnp.where(selm == m, ioe, E), axis=0, keepdims=True)
        hit = ioe == ei
        idxs.append(ei)
        hits.append(hit)
        wgts.append(jnp.sum(jnp.where(hit, scores, 0.0), axis=0, keepdims=True))
        selm = jnp.where(hit, -jnp.inf, selm)
    wsum = wgts[0]
    for w in wgts[1:]:
        wsum = wsum + w
    chosen = jnp.zeros((E, tm), F32)
    for hit in hits:
        chosen = chosen + jnp.where(hit, 1.0, 0.0)
    before = base_ref[:, 0:1] + jnp.dot(chosen.astype(BF16), tri_ref[...], preferred_element_type=F32)
    for k in range(TOP_K):
        idx_ref[k:k + 1, :] = idxs[k]
        wgt_ref[k:k + 1, :] = wgts[k] / wsum * ROUTE_SCALE
        rank_ref[k:k + 1, :] = jnp.sum(jnp.where(hits[k], before, 0.0), axis=0, keepdims=True).astype(jnp.int32)
    base_ref[...] = base_ref[...] + jnp.sum(chosen, axis=1, keepdims=True)
    cnt_ref[...] = base_ref[...]


def _router(x, g, mod_l, w_router, router_bias, row0, rows, n_lat, seq):
    D = x.shape[1]
    E = w_router.shape[1]
    tm = ROW_TILE
    b0 = row0 // tm
    mrow = functools.partial(_mod_row, tm=tm, n_lat=n_lat, seq=seq)
    wt = w_router.T
    whi = wt.astype(BF16)
    wlo = (wt - whi.astype(F32)).astype(BF16)
    tri = (np.arange(tm)[:, None] < np.arange(tm)[None, :]).astype(np.float32)
    return pl.pallas_call(
        _router_kernel,
        grid=(rows // tm,),
        in_specs=[pl.BlockSpec((tm, D), lambda i: (i + b0, 0)),
                  pl.BlockSpec((1, D), lambda i: (0, 0)),
                  pl.BlockSpec((None, 1, D), lambda i: (mrow(i + b0) * 6 + 4, 0, 0)),
                  pl.BlockSpec((None, 1, D), lambda i: (mrow(i + b0) * 6 + 3, 0, 0)),
                  pl.BlockSpec((E, D), lambda i: (0, 0)),
                  pl.BlockSpec((E, D), lambda i: (0, 0)),
                  pl.BlockSpec((E, 1), lambda i: (0, 0)),
                  pl.BlockSpec((tm, tm), lambda i: (0, 0))],
        out_specs=[pl.BlockSpec((tm, D), lambda i: (i, 0)),
                   pl.BlockSpec((TOP_K, tm), lambda i: (0, i)),
                   pl.BlockSpec((TOP_K, tm), lambda i: (0, i)),
                   pl.BlockSpec((TOP_K, tm), lambda i: (0, i)),
                   pl.BlockSpec((E, LANE), lambda i: (0, 0))],
        out_shape=[jax.ShapeDtypeStruct((rows, D), BF16),
                   jax.ShapeDtypeStruct((TOP_K, rows), jnp.int32),
                   jax.ShapeDtypeStruct((TOP_K, rows), F32),
                   jax.ShapeDtypeStruct((TOP_K, rows), jnp.int32),
                   jax.ShapeDtypeStruct((E, LANE), F32)],
        scratch_shapes=[pltpu.VMEM((E, LANE), F32)],
        compiler_params=_cparams(1),
        name="router",
    )(x, g.reshape(1, D), mod_l, mod_l, whi, wlo, router_bias.reshape(E, 1), jnp.asarray(tri, BF16))


def _expert_kernel(vt_ref, ve_ref, nv_ref, gs_ref, xs_ref, wg_ref, wu_ref, wd_ref, *rest):
    o_ref, wgb_ref, wub_ref, wdb_ref = rest[-4:]
    v = pl.program_id(0)
    T = xs_ref.shape[0]

    @pl.when(v < nv_ref[0])
    def _():
        e = ve_ref[v]
        tile = vt_ref[v]
        prev = jnp.maximum(v - 1, 0)
        new_expert = jnp.logical_or(v == 0, ve_ref[prev] != e)
        first_visit = jnp.logical_or(v == 0, vt_ref[prev] != tile)

        @pl.when(new_expert)
        def _():
            wgb_ref[...] = wg_ref[...].astype(BF16)
            wub_ref[...] = wu_ref[...].astype(BF16)
            wdb_ref[...] = wd_ref[...].astype(BF16)

        xs = xs_ref[...]
        g = jnp.dot(xs, wgb_ref[...], preferred_element_type=F32)
        u = jnp.dot(xs, wub_ref[...], preferred_element_type=F32)
        a = (g * jax.nn.sigmoid(g) * u).astype(BF16)
        y = jnp.dot(a, wdb_ref[...], preferred_element_type=F32).astype(o_ref.dtype)

        @pl.when(first_visit)
        def _():
            o_ref[...] = y

        @pl.when(jnp.logical_not(first_visit))
        def _():
            row = tile * T + lax.broadcasted_iota(jnp.int32, (T, 1), 0)
            mine = jnp.logical_and(row >= gs_ref[e], row < gs_ref[e + 1])
            o_ref[...] = jnp.where(mine, y, o_ref[...])


def _experts(xs, tile0, n_rows_total, y_prev, visits, group_start, w_gate, w_up, w_down, layer):
    visit_tile, visit_expert, n_visits = visits
    D = xs.shape[1]
    DE = w_gate.shape[3]
    T = MOE_TILE
    in_specs = [pl.BlockSpec((T, D), lambda v, vt, ve, nv, gs: (vt[v] - tile0, 0)),
                pl.BlockSpec((None, None, D, DE), lambda v, vt, ve, nv, gs: (layer, ve[v], 0, 0)),
                pl.BlockSpec((None, None, D, DE), lambda v, vt, ve, nv, gs: (layer, ve[v], 0, 0)),
                pl.BlockSpec((None, None, DE, D), lambda v, vt, ve, nv, gs: (layer, ve[v], 0, 0))]
    args = [visit_tile, visit_expert, n_visits, group_start, xs, w_gate, w_up, w_down]
    aliases = {}
    if y_prev is not None:
        in_specs.append(pl.BlockSpec(memory_space=pl.ANY))
        args.append(y_prev)
        aliases = {len(args) - 1: 0}
    grid_spec = pltpu.PrefetchScalarGridSpec(
        num_scalar_prefetch=4,
        grid=(visit_tile.shape[0],),
        in_specs=in_specs,
        out_specs=pl.BlockSpec((T, D), lambda v, vt, ve, nv, gs: (vt[v], 0)),
        scratch_shapes=[pltpu.VMEM((D, DE), BF16), pltpu.VMEM((D, DE), BF16), pltpu.VMEM((DE, D), BF16)],
    )
    return pl.pallas_call(
        _expert_kernel,
        grid_spec=grid_spec,
        out_shape=jax.ShapeDtypeStruct((n_rows_total, D), BF16),
        input_output_aliases=aliases,
        compiler_params=_cparams(1),
        name="experts",
    )(*args)


def _dispatch_plan(eidx, rank, counts, tile_bounds):
    K, rows = eidx.shape
    E = counts.shape[0]
    T = MOE_TILE
    gend = jnp.cumsum(counts)
    gstart = gend - counts
    eids = jnp.arange(E, dtype=jnp.int32)
    dest = rank + jnp.sum(jnp.where(eidx[None] == eids[:, None, None], gstart[:, None, None], 0), axis=0)
    token = jnp.broadcast_to(jnp.arange(rows, dtype=jnp.int32)[None], (K, rows))
    _, row_token = lax.sort((dest.reshape(-1), token.reshape(-1)), num_keys=1)
    visits = []
    for t_lo, t_hi in zip(tile_bounds[:-1], tile_bounds[1:]):
        first_tile = jnp.maximum(gstart // T, t_lo)
        last_tile = jnp.minimum((gend - 1) // T, t_hi - 1)
        n_vis_e = jnp.where(counts > 0, jnp.maximum(last_tile - first_tile + 1, 0), 0)
        vend = jnp.cumsum(n_vis_e)
        voff = vend - n_vis_e
        n_visits = vend[-1]
        v = jnp.arange(t_hi - t_lo + E, dtype=jnp.int32)
        ve = jnp.minimum(jnp.sum((v[:, None] >= vend[None, :]).astype(jnp.int32), axis=1), E - 1)
        onehot = (ve[:, None] == eids[None, :]).astype(jnp.int32)
        vt = jnp.sum(onehot * (first_tile - voff)[None, :], axis=1) + v
        vt = jnp.where(v < n_visits, vt, t_hi - 1).astype(jnp.int32)
        visits.append((vt, ve.astype(jnp.int32), n_visits.reshape(1).astype(jnp.int32)))
    group_start = jnp.concatenate([gstart, gend[-1:]]).astype(jnp.int32)
    return dest, row_token, visits, group_start


def _shared_kernel(h_ref, wg_ref, wu_ref, wd_ref, y8_ref, gate_ref, x_ref, g2_ref, o_ref):
    h = h_ref[...]
    g = jnp.dot(h, wg_ref[...], preferred_element_type=F32)
    u = jnp.dot(h, wu_ref[...], preferred_element_type=F32)
    a = (g * jax.nn.sigmoid(g) * u).astype(BF16)
    y = jnp.dot(a, wd_ref[...], preferred_element_type=F32)
    for k in range(y8_ref.shape[0]):
        y = y + gate_ref[:, k:k + 1] * y8_ref[k].astype(F32)
    o_ref[...] = x_ref[...] + g2_ref[...] * y


def _shared_combine(h2, wsg, wsu, wsd, y8, gates, x, mod_l, row0, rows, n_lat, seq):
    D = h2.shape[1]
    DE = wsg.shape[1]
    K = y8.shape[0]
    tm = COMBINE_TILE
    b0 = row0 // tm
    mrow = functools.partial(_mod_row, tm=tm, n_lat=n_lat, seq=seq)
    return pl.pallas_call(
        _shared_kernel,
        grid=(rows // tm,),
        in_specs=[pl.BlockSpec((tm, D), lambda i: (i + b0, 0)),
                  pl.BlockSpec((D, DE), lambda i: (0, 0)),
                  pl.BlockSpec((D, DE), lambda i: (0, 0)),
                  pl.BlockSpec((DE, D), lambda i: (0, 0)),
                  pl.BlockSpec((K, tm, D), lambda i: (0, i, 0)),
                  pl.BlockSpec((tm, K), lambda i: (i, 0)),
                  pl.BlockSpec((tm, D), lambda i: (i + b0, 0)),
                  pl.BlockSpec((None, 1, D), lambda i: (mrow(i + b0) * 6 + 5, 0, 0))],
        out_specs=pl.BlockSpec((tm, D), lambda i: (i + b0, 0)),
        out_shape=jax.ShapeDtypeStruct(x.shape, F32),
        input_output_aliases={6: 0},
        compiler_params=_cparams(1),
        name="shared_combine",
    )(h2, wsg, wsu, wsd, y8, gates, x, mod_l)


def _rope_tables(seq, pad_rows):
    t = np.arange(seq)
    row = (t // GRID_W).astype(np.float32)
    col = (t % GRID_W).astype(np.float32)
    n_freq = MLA_ROPE // 4
    inv = (ROPE_BASE ** (-jnp.arange(n_freq, dtype=F32) / n_freq))
    ang = jnp.concatenate([jnp.asarray(row)[:, None] * inv, jnp.asarray(col)[:, None] * inv], axis=-1)
    cos, sin = jnp.cos(ang), jnp.sin(ang)
    half = MLA_ROPE // 2
    z = jnp.zeros((seq, LANE - MLA_ROPE), F32)
    zh = jnp.zeros((seq, half), F32)
    c = jnp.concatenate([cos, cos, z], axis=-1)
    s1 = jnp.concatenate([-sin, zh, z], axis=-1)
    s2 = jnp.concatenate([zh, sin, z], axis=-1)
    ident = jnp.concatenate([jnp.ones((pad_rows, MLA_ROPE), F32), jnp.zeros((pad_rows, LANE - MLA_ROPE), F32)], -1)
    zero = jnp.zeros((pad_rows, LANE), F32)
    return (jnp.concatenate([c, ident], 0), jnp.concatenate([s1, zero], 0), jnp.concatenate([s2, zero], 0))


def _pad_lanes(v, n):
    return jnp.concatenate([v, jnp.zeros((n - v.shape[0],), v.dtype)]).reshape(1, n)


def kernel(x, c, ctx, c_ctx, ada_w, ada_b, norm1_g, w_in, conv_w, na_q_norm, na_k_norm, na_rpb,
           mla_q_lat_norm, mla_kv_lat_norm, w_uq, w_ukv, mla_q_norm, mla_k_norm, out_norm_g, w_o,
           norm2_g, w_router, router_bias, w_gate, w_up, w_down, ws_gate, ws_up, ws_down):
    B, S, D = x.shape
    CTX = ctx.shape[1]
    L = ada_w.shape[0]
    n_lat = B * S
    R = n_lat + B * CTX
    W = D // 4
    kvr = w_ukv.shape[1]
    nh_mla = w_uq.shape[2] // (MLA_NOPE + MLA_ROPE)
    E = w_router.shape[2]
    assert B == 2 and S % ROW_TILE == 0 and (B * CTX) % ROW_TILE == 0 and CTX % CONV_TILE == 0
    assert S % GRID_W == 0 and S // GRID_W >= NA_WIN_H and W % HEAD_DIM == 0 and kvr % LANE == 0
    assert S % 1024 == 0 and (TOP_K * B * CTX) % MOE_TILE == 0 and ROW_TILE % COMBINE_TILE == 0
    assert (S // GRID_W) % NA_GROUP == 0 and S // GRID_W >= NA_KROWS

    x_pair = (x.reshape(n_lat, D), ctx.reshape(B * CTX, D))
    w_in_t = jnp.swapaxes(w_in, 1, 2)
    w_o_b = w_o.astype(BF16)
    cvecs = jnp.concatenate([c, c_ctx[None]], axis=0)
    mod = _modulation(cvecs, ada_w, ada_b)
    tabs = _rope_tables(S, ROW_TILE)

    for l in range(L):
        last = l == L - 1
        rows = n_lat if last else R
        mod_l = mod[l].reshape(8 * 6, 1, D)

        w_tail_t = jnp.concatenate([w_in_t[l, 7 * W:, :], jnp.zeros((LANE - MLA_ROPE, D), F32)], axis=0)
        wq = w_uq[l].reshape(W, nh_mla, MLA_NOPE + MLA_ROPE)
        wq = jnp.concatenate([wq, jnp.zeros((W, nh_mla, MLA_QPAD - MLA_NOPE - MLA_ROPE), F32)], axis=2)
        wq = wq.reshape(W, nh_mla * MLA_QPAD).astype(BF16)
        wkv = w_ukv[l].reshape(kvr, nh_mla, MLA_NOPE + MLA_V)
        wk = wkv[:, :, :MLA_NOPE].reshape(kvr, nh_mla * MLA_NOPE).astype(BF16)
        wv = wkv[:, :, MLA_NOPE:].reshape(kvr, nh_mla * MLA_V).astype(BF16)
        gq_n = mla_q_norm[l][:MLA_NOPE].reshape(1, LANE)
        gq_r = _pad_lanes(mla_q_norm[l][MLA_NOPE:], LANE)
        gk_n = mla_k_norm[l][:MLA_NOPE].reshape(1, LANE)
        gk_r = _pad_lanes(mla_k_norm[l][MLA_NOPE:], LANE)
        og = out_norm_g[l]

        h = _norm_mod(x_pair, R, norm1_g[l], mod_l, 1, 0, n_lat, S)
        z_main = _matmul_ws(h, w_in_t, l, 7 * W, min(IN_COL_TILE, W), BF16)
        z_tail = _matmul_ws(h, w_tail_t, None, w_tail_t.shape[0], w_tail_t.shape[0], BF16)
        nq, nk, cq, ckv, kr = _post_in(z_main, z_tail, na_q_norm[l].reshape(1, LANE),
                                       na_k_norm[l].reshape(1, LANE), mla_q_lat_norm[l].reshape(1, W),
                                       mla_kv_lat_norm[l].reshape(1, kvr), gk_r, tabs, n_lat, S)
        qf = _uq_matmul(cq, wq, gq_n, gq_r, tabs, n_lat, S)
        kf = _uk_matmul(ckv, wk, gk_n, kr)
        vf = _matmul(ckv, wv, BF16, 512)

        y_conv = _conv_mixer(z_main, conv_w[l], og[:W].reshape(1, W), rows, n_lat, S, CTX)
        y_na = _na_attention(nq, nk, z_main, _na_bias_table(na_rpb[l], S // GRID_W), B, S, CTX, n_lat)
        y_mla = _mla_attention(qf, kf, vf, B, S, CTX, n_lat)
        if last:
            y_na_c, y_mla_c = y_na, y_mla
        else:
            nh_na = W // HEAD_DIM
            y_na_c = _ctx_attention(nq, nk, z_main, nh_na, HEAD_DIM, HEAD_DIM, 5 * nh_na, B, CTX, n_lat)
            y_mla_c = _ctx_attention(qf, kf, vf, nh_mla, MLA_QPAD, MLA_V, 0, B, CTX, n_lat)
        y = _merge(y_conv, y_na, y_mla, y_na_c, y_mla_c,
                   og[W:2 * W].reshape(1, W), og[2 * W:].reshape(1, D - 2 * W), rows)
        xs = _wo_matmul(y, w_o_b, l, x_pair, mod_l, rows, n_lat, S)

        wsg, wsu, wsd = ws_gate[l].astype(BF16), ws_up[l].astype(BF16), ws_down[l].astype(BF16)
        h2, eidx, wgt, rank, cnt = _router(xs, norm2_g[l], mod_l, w_router[l], router_bias[l],
                                           0, rows, n_lat, S)
        n_pairs = TOP_K * rows
        n_tiles = n_pairs // MOE_TILE
        tile_bounds = [n_tiles * p // MOE_PARTS for p in range(MOE_PARTS + 1)]
        dest, row_token, visits, gstart = _dispatch_plan(eidx, rank, cnt[:, 0].astype(jnp.int32), tile_bounds)
        y_rows = None
        for p in range(MOE_PARTS):
            part_tokens = row_token[tile_bounds[p] * MOE_TILE:tile_bounds[p + 1] * MOE_TILE]
            gathered = h2.at[part_tokens].get(mode="promise_in_bounds")
            y_rows = _experts(gathered, tile_bounds[p], n_pairs, y_rows, visits[p], gstart,
                              w_gate, w_up, w_down, l)
        gates = wgt.T
        n_blocks = rows // ROW_TILE
        row_bounds = [(n_blocks * p // MOE_PARTS) * ROW_TILE for p in range(MOE_PARTS + 1)]
        for row0, row1 in zip(row_bounds[:-1], row_bounds[1:]):
            y8 = y_rows.at[dest[:, row0:row1].reshape(-1)].get(mode="promise_in_bounds")
            xs = _shared_combine(h2, wsg, wsu, wsd, y8.reshape(TOP_K, row1 - row0, D), gates[row0:row1],
                                 xs, mod_l, row0, row1 - row0, n_lat, S)
        x_pair = (xs, xs)

    return xs[:n_lat].reshape(B, S, D)
```
